```python
import jax, jax.numpy as jnp
from jax import lax
import numpy as np

D_MODEL = 1024
BATCH = 32
SEQ = 2048
DEPTH = 1

HEAD_DIM = 64
N_HEADS_SWA = 8
N_KV_SWA = 2
GROUP_SWA = N_HEADS_SWA // N_KV_SWA
WINDOW = 128
N_HEADS_MOBA = 8
MOBA_BLOCK = 256
MOBA_TOPK = 3
MOBA_QCHUNK = 16
ROPE_THETA = 10000.0
N_EXPERTS = 32
TOP_K = 4
D_FF = D_MODEL
SWIGLU_LIMIT = 7.0
SWIGLU_ALPHA = 1.702
EXPERT_BLOCK = 256
LN_EPS = 1e-5
DEEPNORM_ALPHA = (2 * DEPTH) ** 0.25
DEEPNORM_BETA = (8 * DEPTH) ** -0.25

W_Q_SWA = N_HEADS_SWA * HEAD_DIM
W_KV_SWA = N_KV_SWA * HEAD_DIM
W_MOBA = N_HEADS_MOBA * HEAD_DIM
MIX_WIDTH = W_Q_SWA + W_MOBA
IN_WIDTH = W_Q_SWA + 2 * W_KV_SWA + 3 * W_MOBA
SPLITS = (W_Q_SWA, W_Q_SWA + W_KV_SWA, W_Q_SWA + 2 * W_KV_SWA,
          W_Q_SWA + 2 * W_KV_SWA + W_MOBA, W_Q_SWA + 2 * W_KV_SWA + 2 * W_MOBA)

kernel_name = 'hymba_style_swa_sink_moba_moe_deepnorm'


def layer_norm(x, g, b):
    xf = x.astype(jnp.float32)
    mu = jnp.mean(xf, axis=-1, keepdims=True)
    var = jnp.mean(jnp.square(xf - mu), axis=-1, keepdims=True)
    y = (xf - mu) * lax.rsqrt(var + LN_EPS) * g.astype(jnp.float32) + b.astype(jnp.float32)
    return y.astype(x.dtype)


def rope_tables(seq_len, dtype):
    inv_freq = 1.0 / (ROPE_THETA ** (jnp.arange(0, HEAD_DIM, 2, dtype=jnp.float32) / HEAD_DIM))
    ang = jnp.arange(seq_len, dtype=jnp.float32)[:, None] * inv_freq[None, :]
    return jnp.cos(ang)[None, :, None, :].astype(dtype), jnp.sin(ang)[None, :, None, :].astype(dtype)


def apply_rope(t, cos, sin):
    t1, t2 = jnp.split(t, 2, axis=-1)
    return jnp.concatenate([t1 * cos - t2 * sin, t2 * cos + t1 * sin], axis=-1)


def sliding_window_sink_attention(q, k, v, sinks):
    B, S = q.shape[0], q.shape[1]
    nb = S // WINDOW
    qb = q.reshape(B, nb, WINDOW, N_KV_SWA, GROUP_SWA, HEAD_DIM)

    def band(t):
        cur = t.reshape(B, nb, WINDOW, N_KV_SWA, HEAD_DIM)
        prev = jnp.pad(cur, ((0, 0), (1, 0), (0, 0), (0, 0), (0, 0)))[:, :-1]
        return jnp.concatenate([prev, cur], axis=2)

    kb, vb = band(k), band(v)
    scale = HEAD_DIM ** -0.5
    s = jnp.einsum('bnqkgd,bnckd->bkgnqc', qb, kb).astype(jnp.float32) * scale
    r = jnp.arange(WINDOW)[:, None]
    c = jnp.arange(2 * WINDOW)[None, :]
    rel = WINDOW + r - c
    band_mask = (rel >= 0) & (rel < WINDOW)
    key_pos = jnp.arange(nb)[:, None, None] * WINDOW + c[None] - WINDOW
    mask = band_mask[None] & (key_pos >= 0)
    s = jnp.where(mask, s, -jnp.inf)
    sink = sinks.astype(jnp.float32).reshape(1, N_KV_SWA, GROUP_SWA, 1, 1, 1)
    m = jnp.maximum(jnp.max(s, axis=-1, keepdims=True), sink)
    p = jnp.exp(s - m)
    denom = jnp.sum(p, axis=-1, keepdims=True) + jnp.exp(sink - m)
    p = (p / denom).astype(v.dtype)
    o = jnp.einsum('bkgnqc,bnckd->bnqkgd', p, vb)
    return o.reshape(B, S, W_Q_SWA)


def moba_attention(q, k, v):
    B, S = q.shape[0], q.shape[1]
    s_pad = -(-S // MOBA_BLOCK) * MOBA_BLOCK
    pad = ((0, 0), (0, s_pad - S), (0, 0), (0, 0))
    q, k, v = [jnp.pad(t, pad).transpose(0, 2, 1, 3) for t in (q, k, v)]
    nblk = s_pad // MOBA_BLOCK
    ksel = min(MOBA_TOPK, nblk)
    kblk = k.reshape(B, N_HEADS_MOBA, nblk, MOBA_BLOCK, HEAD_DIM)
    vblk = v.reshape(B, N_HEADS_MOBA, nblk, MOBA_BLOCK, HEAD_DIM)
    kmean = jnp.mean(kblk.astype(jnp.float32), axis=3)
    gate = jnp.einsum('bhsd,bhnd->bhsn', q.astype(jnp.float32), kmean)
    q_block = jnp.arange(s_pad) // MOBA_BLOCK
    fully_past = jnp.arange(nblk)[None, :] < q_block[:, None]
    gate = jnp.where(fully_past, gate, -jnp.inf)
    gate_vals, sel_idx = lax.top_k(gate, ksel)
    sel_valid = jnp.isfinite(gate_vals)

    nc = s_pad // MOBA_QCHUNK

    def chunked(t):
        return jnp.moveaxis(t.reshape(B, N_HEADS_MOBA, nc, MOBA_QCHUNK, *t.shape[3:]), 2, 0)

    bi = jnp.arange(B)[:, None, None, None]
    hi = jnp.arange(N_HEADS_MOBA)[None, :, None, None]
    scale = HEAD_DIM ** -0.5

    def chunk_attend(args):
        qc, ic, vc, cidx = args
        k_sel = kblk[bi, hi, ic]
        v_sel = vblk[bi, hi, ic]
        own = cidx * MOBA_QCHUNK // MOBA_BLOCK
        k_own = lax.dynamic_index_in_dim(kblk, own, axis=2, keepdims=False)
        v_own = lax.dynamic_index_in_dim(vblk, own, axis=2, keepdims=False)
        s_sel = jnp.einsum('bhqd,bhqnkd->bhqnk', qc, k_sel).astype(jnp.float32) * scale
        s_sel = jnp.where(vc[..., None], s_sel, -jnp.inf)
        s_own = jnp.einsum('bhqd,bhkd->bhqk', qc, k_own).astype(jnp.float32) * scale
        q_pos = cidx * MOBA_QCHUNK + jnp.arange(MOBA_QCHUNK)
        k_pos = own * MOBA_BLOCK + jnp.arange(MOBA_BLOCK)
        s_own = jnp.where(k_pos[None, :] <= q_pos[:, None], s_own, -jnp.inf)
        s_all = jnp.concatenate([s_sel.reshape(B, N_HEADS_MOBA, MOBA_QCHUNK, ksel * MOBA_BLOCK), s_own], axis=-1)
        p = jax.nn.softmax(s_all, axis=-1).astype(v.dtype)
        p_sel = p[..., :ksel * MOBA_BLOCK].reshape(B, N_HEADS_MOBA, MOBA_QCHUNK, ksel, MOBA_BLOCK)
        p_own = p[..., ksel * MOBA_BLOCK:]
        return (jnp.einsum('bhqnk,bhqnkd->bhqd', p_sel, v_sel)
                + jnp.einsum('bhqk,bhkd->bhqd', p_own, v_own))

    out = lax.map(chunk_attend, (chunked(q), chunked(sel_idx), chunked(sel_valid),
                                 jnp.arange(nc, dtype=jnp.int32)))
    out = jnp.moveaxis(out, 0, 2).reshape(B, N_HEADS_MOBA, s_pad, HEAD_DIM)
    return out.transpose(0, 2, 1, 3)[:, :S].reshape(B, S, W_MOBA)


def clamped_swiglu(h):
    glu = jnp.minimum(h[..., 0::2], SWIGLU_LIMIT)
    lin = jnp.clip(h[..., 1::2], -SWIGLU_LIMIT, SWIGLU_LIMIT)
    return glu * jax.nn.sigmoid(SWIGLU_ALPHA * glu) * (lin + 1.0)


def moe_ffn(h, w_router, b_router, w1, b1, w2, b2):
    T, D = h.shape
    logits = (h @ w_router + b_router).astype(jnp.float32)
    top_vals, top_idx = lax.top_k(logits, TOP_K)
    gates = jax.nn.softmax(top_vals, axis=-1)
    n_assign = T * TOP_K
    e_flat = top_idx.reshape(n_assign)
    order = jnp.argsort(e_flat)
    e_sorted = e_flat[order]
    tok_sorted = order // TOP_K
    gate_sorted = gates.reshape(n_assign)[order]
    counts = jnp.bincount(e_flat, length=N_EXPERTS)
    offsets = jnp.cumsum(counts) - counts
    padded = (counts + EXPERT_BLOCK - 1) // EXPERT_BLOCK * EXPERT_BLOCK
    padded_end = jnp.cumsum(padded)
    padded_off = padded_end - padded
    dest = padded_off[e_sorted] + (jnp.arange(n_assign) - offsets[e_sorted])
    n_blocks = -(-n_assign // EXPERT_BLOCK) + N_EXPERTS
    rows = n_blocks * EXPERT_BLOCK
    xbuf = jnp.zeros((rows, D), h.dtype).at[dest].set(h[tok_sorted])
    block_start = jnp.arange(n_blocks) * EXPERT_BLOCK
    block_expert = jnp.minimum(jnp.searchsorted(padded_end, block_start, side='right'), N_EXPERTS - 1)

    def expert_block(args):
        xb, e = args
        act = clamped_swiglu(xb @ w1[e] + b1[e])
        return act @ w2[e] + b2[e]

    ybuf = lax.map(expert_block, (xbuf.reshape(n_blocks, EXPERT_BLOCK, D), block_expert)).reshape(rows, D)
    contrib = gate_sorted.astype(h.dtype)[:, None] * ybuf[dest]
    return jnp.zeros((T, D), h.dtype).at[tok_sorted].add(contrib)


def setup_inputs(seed: int = 0) -> dict:
    key = jax.random.key(seed)
    ks = jax.random.split(key, 16)
    f32 = jnp.float32

    def nrm(k, shape, scale):
        return scale * jax.random.normal(k, shape, f32)

    x = jax.random.normal(ks[0], (BATCH, SEQ, D_MODEL), f32)
    col_scale = jnp.concatenate([
        jnp.ones((W_Q_SWA + W_KV_SWA,), f32), jnp.full((W_KV_SWA,), DEEPNORM_BETA, f32),
        jnp.ones((2 * W_MOBA,), f32), jnp.full((W_MOBA,), DEEPNORM_BETA, f32)])
    w_in = nrm(ks[1], (DEPTH, D_MODEL, IN_WIDTH), D_MODEL ** -0.5) * col_scale
    b_in = nrm(ks[2], (DEPTH, IN_WIDTH), 0.02)
    sinks = nrm(ks[3], (DEPTH, N_HEADS_SWA), 1.0)
    w_out = nrm(ks[4], (DEPTH, MIX_WIDTH, D_MODEL), MIX_WIDTH ** -0.5 * DEEPNORM_BETA)
    b_out = nrm(ks[5], (DEPTH, D_MODEL), 0.02)
    ln1_g = 1.0 + nrm(ks[6], (DEPTH, D_MODEL), 0.02)
    ln1_b = nrm(ks[7], (DEPTH, D_MODEL), 0.02)
    w_router = nrm(ks[8], (DEPTH, D_MODEL, N_EXPERTS), D_MODEL ** -0.5)
    b_router = nrm(ks[9], (DEPTH, N_EXPERTS), 0.01)
    w1 = nrm(ks[10], (DEPTH, N_EXPERTS, D_MODEL, 2 * D_FF), D_MODEL ** -0.5 * DEEPNORM_BETA)
    b1 = nrm(ks[11], (DEPTH, N_EXPERTS, 2 * D_FF), 0.02)
    w2 = nrm(ks[12], (DEPTH, N_EXPERTS, D_FF, D_MODEL), D_FF ** -0.5 * DEEPNORM_BETA)
    b2 = nrm(ks[13], (DEPTH, N_EXPERTS, D_MODEL), 0.02)
    ln2_g = 1.0 + nrm(ks[14], (DEPTH, D_MODEL), 0.02)
    ln2_b = nrm(ks[15], (DEPTH, D_MODEL), 0.02)
    return {'x': x, 'w_in': w_in, 'b_in': b_in, 'sinks': sinks, 'w_out': w_out, 'b_out': b_out,
            'ln1_g': ln1_g, 'ln1_b': ln1_b, 'w_router': w_router, 'b_router': b_router,
            'w1': w1, 'b1': b1, 'w2': w2, 'b2': b2, 'ln2_g': ln2_g, 'ln2_b': ln2_b}


def reference(x, w_in, b_in, sinks, w_out, b_out, ln1_g, ln1_b, w_router, b_router,
              w1, b1, w2, b2, ln2_g, ln2_b):
    B, S, D = x.shape
    cos, sin = rope_tables(S, x.dtype)
    for l in range(DEPTH):
        proj = x @ w_in[l] + b_in[l]
        q_a, k_a, v_a, q_b, k_b, v_b = jnp.split(proj, SPLITS, axis=-1)
        q_a = apply_rope(q_a.reshape(B, S, N_HEADS_SWA, HEAD_DIM), cos, sin)
        k_a = apply_rope(k_a.reshape(B, S, N_KV_SWA, HEAD_DIM), cos, sin)
        v_a = v_a.reshape(B, S, N_KV_SWA, HEAD_DIM)
        q_b = apply_rope(q_b.reshape(B, S, N_HEADS_MOBA, HEAD_DIM), cos, sin)
        k_b = apply_rope(k_b.reshape(B, S, N_HEADS_MOBA, HEAD_DIM), cos, sin)
        v_b = v_b.reshape(B, S, N_HEADS_MOBA, HEAD_DIM)
        o_a = sliding_window_sink_attention(q_a, k_a, v_a, sinks[l])
        o_b = moba_attention(q_b, k_b, v_b)
        mix = jnp.concatenate([o_a, o_b], axis=-1) @ w_out[l] + b_out[l]
        x = layer_norm(DEEPNORM_ALPHA * x + mix, ln1_g[l], ln1_b[l])
        moe = moe_ffn(x.reshape(B * S, D), w_router[l], b_router[l], w1[l], b1[l], w2[l], b2[l]).reshape(B, S, D)
        x = layer_norm(DEEPNORM_ALPHA * x + moe, ln2_g[l], ln2_b[l])
    return x
```

```python
import functools

import jax
import jax.numpy as jnp
from jax import lax
from jax.experimental import pallas as pl
from jax.experimental.pallas import tpu as pltpu

D_MODEL = 1024
HEAD_DIM = 64
N_HEADS_SWA = 8
N_KV_SWA = 2
WINDOW = 128
N_HEADS_MOBA = 8
MOBA_BLOCK = 256
MOBA_TOPK = 3
ROPE_THETA = 10000.0
N_EXPERTS = 32
TOP_K = 4
D_FF = 1024
SWIGLU_LIMIT = 7.0
SWIGLU_ALPHA = 1.702
LN_EPS = 1e-5
DEPTH = 1
DEEPNORM_ALPHA = (2 * DEPTH) ** 0.25

W_Q_SWA = N_HEADS_SWA * HEAD_DIM
W_KV_SWA = N_KV_SWA * HEAD_DIM
W_MOBA = N_HEADS_MOBA * HEAD_DIM
IN_WIDTH = W_Q_SWA + 2 * W_KV_SWA + 3 * W_MOBA
LANES = 128
COL_K_SWA = W_Q_SWA // LANES
COL_V_SWA = COL_K_SWA + 1
COL_Q_MOBA = COL_V_SWA + 1
COL_K_MOBA = COL_Q_MOBA + W_MOBA // LANES
COL_V_MOBA = COL_K_MOBA + W_MOBA // LANES

ROW_TILE = 512
EXPERT_ROWS = 512
NEG_BIG = -1e30
VMEM_LIMIT = 48 * 1024 * 1024

_PROJ_CHUNKS = (
    (0, 256, True, True), (256, 256, True, True),
    (512, 128, True, False), (640, 128, False, False),
    (768, 256, True, True), (1024, 256, True, True),
    (1280, 256, True, False), (1536, 256, True, False),
    (1792, 256, False, False), (2048, 256, False, False),
)


def _params(*sem):
    return pltpu.CompilerParams(dimension_semantics=sem, vmem_limit_bytes=VMEM_LIMIT)


def _inproj_kernel(x_ref, w_ref, b_ref, cos_ref, sin_ref, o_ref):
    xb = x_ref[...].astype(jnp.bfloat16)
    for start, width, rope, scaled in _PROJ_CHUNKS:
        t = jnp.dot(xb, w_ref[:, start:start + width], preferred_element_type=jnp.float32)
        t = t + b_ref[:, start:start + width]
        if rope:
            lane = lax.broadcasted_iota(jnp.int32, t.shape, 1)
            first_half = (lane % HEAD_DIM) < (HEAD_DIM // 2)
            rot = jnp.where(first_half,
                            pltpu.roll(t, width - HEAD_DIM // 2, 1),
                            pltpu.roll(t, HEAD_DIM // 2, 1))
            t = t * cos_ref[:, :width] + rot * sin_ref[:, :width]
        if scaled:
            t = t * (HEAD_DIM ** -0.5)
        o_ref[:, start:start + width] = t.astype(o_ref.dtype)


def _inproj(x2, w_in, b_in, cos_t, sin_t, seq):
    n_tok = x2.shape[0]
    per_seq = seq // ROW_TILE
    return pl.pallas_call(
        _inproj_kernel,
        grid=(n_tok // ROW_TILE,),
        in_specs=[
            pl.BlockSpec((ROW_TILE, D_MODEL), lambda i: (i, 0)),
            pl.BlockSpec((D_MODEL, IN_WIDTH), lambda i: (0, 0)),
            pl.BlockSpec((1, IN_WIDTH), lambda i: (0, 0)),
            pl.BlockSpec((ROW_TILE, 256), lambda i: (i % per_seq, 0)),
            pl.BlockSpec((ROW_TILE, 256), lambda i: (i % per_seq, 0)),
        ],
        out_specs=pl.BlockSpec((ROW_TILE, IN_WIDTH), lambda i: (i, 0)),
        out_shape=jax.ShapeDtypeStruct((n_tok, IN_WIDTH), jnp.bfloat16),
        compiler_params=_params("parallel"),
        name="inproj_rope",
    )(x2, w_in, b_in, cos_t, sin_t)


def _swa_kernel(sink_ref, q_ref, k_ref, v_ref, o_ref, kd_ref, vd_ref, *, seq):
    lane = lax.broadcasted_iota(jnp.int32, (seq, LANES), 1)
    low = lane < HEAD_DIM
    for src, dst in ((k_ref, kd_ref), (v_ref, vd_ref)):
        t = src[...].astype(jnp.float32)
        tr = pltpu.roll(t, HEAD_DIM, 1)
        dst[0, WINDOW:, :] = jnp.where(low, t, tr).astype(dst.dtype)
        dst[1, WINDOW:, :] = jnp.where(low, tr, t).astype(dst.dtype)
        dst[:, :WINDOW, :] = jnp.zeros((2, WINDOW, LANES), dst.dtype)

    group = N_HEADS_SWA // N_KV_SWA
    rows = group * WINDOW
    r_id = lax.broadcasted_iota(jnp.int32, (rows, 2 * WINDOW), 0)
    c_id = lax.broadcasted_iota(jnp.int32, (rows, 2 * WINDOW), 1)
    r_in = r_id % WINDOW
    band = (c_id > r_in) & (c_id <= r_in + WINDOW)
    head_in_group = lax.broadcasted_iota(jnp.int32, (rows, 1), 0) // WINDOW
    qlane_low = lax.broadcasted_iota(jnp.int32, (WINDOW, LANES), 1) < HEAD_DIM

    def block(n, carry):
        r0 = pl.multiple_of(n * WINDOW, WINDOW)
        mask = band & ((c_id >= WINDOW) | (n > 0))
        for g in range(N_KV_SWA):
            parts = []
            for c in (2 * g, 2 * g + 1):
                qc = q_ref[pl.ds(r0, WINDOW), c * LANES:(c + 1) * LANES]
                zero = jnp.zeros_like(qc)
                parts.append(jnp.where(qlane_low, qc, zero))
                parts.append(jnp.where(qlane_low, zero, qc))
            qcat = jnp.concatenate(parts, axis=0)
            kd = kd_ref[g, pl.ds(r0, 2 * WINDOW), :]
            vd = vd_ref[g, pl.ds(r0, 2 * WINDOW), :]
            s = lax.dot_general(qcat, kd, (((1,), (1,)), ((), ())),
                                preferred_element_type=jnp.float32)
            s = jnp.where(mask, s, -jnp.inf)
            sink = jnp.zeros((rows, 1), jnp.float32)
            for j in range(group):
                sink = jnp.where(head_in_group == j, sink_ref[g * group + j], sink)
            m = jnp.maximum(jnp.max(s, axis=-1, keepdims=True), sink)
            p = jnp.exp(s - m)
            denom = jnp.sum(p, axis=-1, keepdims=True) + jnp.exp(sink - m)
            o = jnp.dot(p.astype(vd.dtype), vd, preferred_element_type=jnp.float32) / denom
            for ci, c in enumerate((2 * g, 2 * g + 1)):
                lo = o[(2 * ci) * WINDOW:(2 * ci + 1) * WINDOW]
                hi = o[(2 * ci + 1) * WINDOW:(2 * ci + 2) * WINDOW]
                o_ref[pl.ds(r0, WINDOW), c * LANES:(c + 1) * LANES] = (
                    jnp.where(qlane_low, lo, hi).astype(o_ref.dtype))
        return carry

    lax.fori_loop(0, seq // WINDOW, block, 0)


def _swa(proj, sinks, batch, seq):
    grid_spec = pltpu.PrefetchScalarGridSpec(
        num_scalar_prefetch=0,
        grid=(batch,),
        in_specs=[
            pl.BlockSpec(memory_space=pltpu.SMEM),
            pl.BlockSpec((seq, W_Q_SWA), lambda b: (b, 0)),
            pl.BlockSpec((seq, LANES), lambda b: (b, COL_K_SWA)),
            pl.BlockSpec((seq, LANES), lambda b: (b, COL_V_SWA)),
        ],
        out_specs=pl.BlockSpec((seq, W_Q_SWA), lambda b: (b, 0)),
        scratch_shapes=[pltpu.VMEM((N_KV_SWA, WINDOW + seq, LANES), jnp.bfloat16),
                        pltpu.VMEM((N_KV_SWA, WINDOW + seq, LANES), jnp.bfloat16)],
    )
    return pl.pallas_call(
        functools.partial(_swa_kernel, seq=seq),
        grid_spec=grid_spec,
        out_shape=jax.ShapeDtypeStruct((batch * seq, W_Q_SWA), jnp.bfloat16),
        compiler_params=_params("parallel"),
        name="swa_sink_attention",
    )(sinks, proj, proj, proj)


def _moba_kernel(q_ref, k_ref, v_ref, o_ref, *, seq):
    nblk = seq // MOBA_BLOCK
    pad_rows = 16
    k_all = k_ref[...]
    kmean = jnp.sum(k_all.astype(jnp.float32).reshape(nblk, MOBA_BLOCK, LANES), axis=1) / MOBA_BLOCK
    kmean = jnp.concatenate([kmean, jnp.zeros((pad_rows - nblk, LANES), jnp.float32)], axis=0)
    klane_low = lax.broadcasted_iota(jnp.int32, (pad_rows, LANES), 1) < HEAD_DIM

    j_id = lax.broadcasted_iota(jnp.int32, (pad_rows, seq), 0)
    q_blk = lax.broadcasted_iota(jnp.int32, (pad_rows, seq), 1) // MOBA_BLOCK
    eligible = j_id < q_blk
    qlane_low = lax.broadcasted_iota(jnp.int32, (MOBA_BLOCK, LANES), 1) < HEAD_DIM
    rr = lax.broadcasted_iota(jnp.int32, (MOBA_BLOCK, MOBA_BLOCK), 0)
    cc = lax.broadcasted_iota(jnp.int32, (MOBA_BLOCK, MOBA_BLOCK), 1)
    causal = cc <= rr

    outs = []
    for half in range(2):
        km = jnp.where(klane_low if half == 0 else ~klane_low, kmean, 0.0).astype(jnp.bfloat16)
        gate = lax.dot_general(km, q_ref[...], (((1,), (1,)), ((), ())),
                               preferred_element_type=jnp.float32)
        gate = jnp.where(eligible, gate, -jnp.inf)
        beaten = jnp.zeros((pad_rows, seq), jnp.int32)
        for jp in range(nblk):
            row = gate[jp:jp + 1, :]
            wins = (row > gate) | ((row == gate) & (jp < j_id))
            beaten = beaten + wins.astype(jnp.int32)
        selected = eligible & (beaten < MOBA_TOPK)
        bias = jnp.where(selected, 0.0, NEG_BIG)
        bias = jnp.concatenate([bias, jnp.zeros((LANES - pad_rows, seq), jnp.float32)], axis=0)
        bias_t = bias.T

        blocks = []
        for i in range(nblk):
            r0 = i * MOBA_BLOCK
            qc = q_ref[r0:r0 + MOBA_BLOCK, :]
            zero = jnp.zeros_like(qc)
            qm = jnp.where(qlane_low if half == 0 else ~qlane_low, qc, zero)
            keys = k_ref[0:r0 + MOBA_BLOCK, :]
            s = lax.dot_general(qm, keys, (((1,), (1,)), ((), ())),
                                preferred_element_type=jnp.float32)
            pieces = []
            for j in range(i):
                pieces.append(s[:, j * MOBA_BLOCK:(j + 1) * MOBA_BLOCK] + bias_t[r0:r0 + MOBA_BLOCK, j:j + 1])
            pieces.append(jnp.where(causal, s[:, r0:r0 + MOBA_BLOCK], NEG_BIG))
            s = jnp.concatenate(pieces, axis=1) if len(pieces) > 1 else pieces[0]
            m = jnp.max(s, axis=-1, keepdims=True)
            p = jnp.exp(s - m)
            denom = jnp.sum(p, axis=-1, keepdims=True)
            vals = v_ref[0:r0 + MOBA_BLOCK, :]
            blocks.append(jnp.dot(p.astype(vals.dtype), vals, preferred_element_type=jnp.float32) / denom)
        outs.append(blocks)

    for i in range(nblk):
        r0 = i * MOBA_BLOCK
        o_ref[r0:r0 + MOBA_BLOCK, :] = jnp.where(qlane_low, outs[0][i], outs[1][i]).astype(o_ref.dtype)


def _moba(proj, batch, seq):
    pairs = W_MOBA // LANES
    return pl.pallas_call(
        functools.partial(_moba_kernel, seq=seq),
        grid=(batch, pairs),
        in_specs=[
            pl.BlockSpec((seq, LANES), lambda b, p: (b, COL_Q_MOBA + p)),
            pl.BlockSpec((seq, LANES), lambda b, p: (b, COL_K_MOBA + p)),
            pl.BlockSpec((seq, LANES), lambda b, p: (b, COL_V_MOBA + p)),
        ],
        out_specs=pl.BlockSpec((seq, LANES), lambda b, p: (b, p)),
        out_shape=jax.ShapeDtypeStruct((batch * seq, W_MOBA), jnp.bfloat16),
        compiler_params=_params("parallel", "parallel"),
        name="moba_attention",
    )(proj, proj, proj)


def _layer_norm(h, g, b):
    mu = jnp.mean(h, axis=-1, keepdims=True)
    d = h - mu
    var = jnp.mean(d * d, axis=-1, keepdims=True)
    return d * lax.rsqrt(var + LN_EPS) * g + b


def _outproj_kernel(oa_ref, ob_ref, wa_ref, wb_ref, bo_ref, x_ref, g_ref, b_ref, wr_ref, br_ref,
                    x1_ref, x1b_ref, idx_ref, gate_ref):
    mix = jnp.dot(oa_ref[...], wa_ref[...], preferred_element_type=jnp.float32)
    mix = mix + jnp.dot(ob_ref[...], wb_ref[...], preferred_element_type=jnp.float32)
    mix = mix + bo_ref[...]
    x1 = _layer_norm(DEEPNORM_ALPHA * x_ref[...] + mix, g_ref[...], b_ref[...])
    x1_ref[...] = x1
    x1b = x1.astype(jnp.bfloat16)
    x1b_ref[...] = x1b
    logits = jnp.dot(x1b, wr_ref[...], preferred_element_type=jnp.float32) + br_ref[...]
    lane = lax.broadcasted_iota(jnp.int32, logits.shape, 1)
    idx_out = jnp.zeros(logits.shape, jnp.int32)
    val_out = jnp.zeros(logits.shape, jnp.float32)
    top = None
    total = None
    for k in range(TOP_K):
        m = jnp.max(logits, axis=-1, keepdims=True)
        idx = jnp.min(jnp.where(logits == m, lane, LANES), axis=-1, keepdims=True)
        logits = jnp.where(lane == idx, -jnp.inf, logits)
        if k == 0:
            top = m
        e = jnp.exp(m - top)
        total = e if k == 0 else total + e
        idx_out = jnp.where(lane == k, idx, idx_out)
        val_out = jnp.where(lane == k, e, val_out)
    idx_ref[...] = idx_out[:, :TOP_K]
    gate_ref[...] = (val_out / total)[:, :TOP_K]


def _outproj(o_a, o_b, w_a, w_b, b_out, x2, g, b, w_r, b_r):
    n_tok = x2.shape[0]
    row = lambda i: (i, 0)
    fixed = lambda i: (0, 0)
    return pl.pallas_call(
        _outproj_kernel,
        grid=(n_tok // ROW_TILE,),
        in_specs=[
            pl.BlockSpec((ROW_TILE, W_Q_SWA), row),
            pl.BlockSpec((ROW_TILE, W_MOBA), row),
            pl.BlockSpec((W_Q_SWA, D_MODEL), fixed),
            pl.BlockSpec((W_MOBA, D_MODEL), fixed),
            pl.BlockSpec((1, D_MODEL), fixed),
            pl.BlockSpec((ROW_TILE, D_MODEL), row),
            pl.BlockSpec((1, D_MODEL), fixed),
            pl.BlockSpec((1, D_MODEL), fixed),
            pl.BlockSpec((D_MODEL, LANES), fixed),
            pl.BlockSpec((1, LANES), fixed),
        ],
        out_specs=[
            pl.BlockSpec((ROW_TILE, D_MODEL), row),
            pl.BlockSpec((ROW_TILE, D_MODEL), row),
            pl.BlockSpec((ROW_TILE, TOP_K), row),
            pl.BlockSpec((ROW_TILE, TOP_K), row),
        ],
        out_shape=[
            jax.ShapeDtypeStruct((n_tok, D_MODEL), jnp.float32),
            jax.ShapeDtypeStruct((n_tok, D_MODEL), jnp.bfloat16),
            jax.ShapeDtypeStruct((n_tok, TOP_K), jnp.int32),
            jax.ShapeDtypeStruct((n_tok, TOP_K), jnp.float32),
        ],
        compiler_params=_params("parallel"),
        name="outproj_ln_router",
    )(o_a, o_b, w_a, w_b, b_out, x2, g, b, w_r, b_r)


def _expert_kernel(be_ref, nv_ref, x_ref, wg_ref, wl_ref, bg_ref, bl_ref, w2_ref, b2_ref, y_ref):
    i = pl.program_id(0)

    @pl.when(i < nv_ref[0])
    def _():
        xb = x_ref[...]
        hg = jnp.dot(xb, wg_ref[0], preferred_element_type=jnp.float32) + bg_ref[0]
        hl = jnp.dot(xb, wl_ref[0], preferred_element_type=jnp.float32) + bl_ref[0]
        glu = jnp.minimum(hg, SWIGLU_LIMIT)
        lin = jnp.clip(hl, -SWIGLU_LIMIT, SWIGLU_LIMIT)
        act = glu * jax.nn.sigmoid(SWIGLU_ALPHA * glu) * (lin + 1.0)
        y = jnp.dot(act.astype(jnp.bfloat16), w2_ref[0], preferred_element_type=jnp.float32) + b2_ref[0]
        y_ref[...] = y.astype(y_ref.dtype)

    @pl.when(i >= nv_ref[0])
    def _():
        y_ref[...] = jnp.zeros(y_ref.shape, y_ref.dtype)


def _experts(block_expert, n_valid, xg, w1g, w1l, b1g, b1l, w2, b2):
    n_rows = xg.shape[0]
    n_blocks = n_rows // EXPERT_ROWS

    def row(i, be, nv):
        return (jnp.minimum(i, nv[0] - 1), 0)

    def per_expert(i, be, nv):
        return (be[i], 0, 0)

    grid_spec = pltpu.PrefetchScalarGridSpec(
        num_scalar_prefetch=2,
        grid=(n_blocks,),
        in_specs=[
            pl.BlockSpec((EXPERT_ROWS, D_MODEL), row),
            pl.BlockSpec((1, D_MODEL, D_FF), per_expert),
            pl.BlockSpec((1, D_MODEL, D_FF), per_expert),
            pl.BlockSpec((1, 1, D_FF), per_expert),
            pl.BlockSpec((1, 1, D_FF), per_expert),
            pl.BlockSpec((1, D_FF, D_MODEL), per_expert),
            pl.BlockSpec((1, 1, D_MODEL), per_expert),
        ],
        out_specs=pl.BlockSpec((EXPERT_ROWS, D_MODEL), lambda i, be, nv: (i, 0)),
    )
    return pl.pallas_call(
        _expert_kernel,
        grid_spec=grid_spec,
        out_shape=jax.ShapeDtypeStruct((n_rows, D_MODEL), jnp.bfloat16),
        compiler_params=_params("arbitrary"),
        name="grouped_experts",
    )(block_expert, n_valid, xg, w1g, w1l, b1g, b1l, w2, b2)


def _combine_kernel(y_ref, gate_ref, x1_ref, g_ref, b_ref, o_ref):
    gates = gate_ref[...]
    moe = gates[:, 0:1] * y_ref[0].astype(jnp.float32)
    for k in range(1, TOP_K):
        moe = moe + gates[:, k:k + 1] * y_ref[k].astype(jnp.float32)
    o_ref[...] = _layer_norm(DEEPNORM_ALPHA * x1_ref[...] + moe, g_ref[...], b_ref[...])


def _combine(yg, gates, x1, g, b):
    n_tok = x1.shape[0]
    row = lambda i: (i, 0)
    fixed = lambda i: (0, 0)
    return pl.pallas_call(
        _combine_kernel,
        grid=(n_tok // ROW_TILE,),
        in_specs=[
            pl.BlockSpec((TOP_K, ROW_TILE, D_MODEL), lambda i: (0, i, 0)),
            pl.BlockSpec((ROW_TILE, TOP_K), row),
            pl.BlockSpec((ROW_TILE, D_MODEL), row),
            pl.BlockSpec((1, D_MODEL), fixed),
            pl.BlockSpec((1, D_MODEL), fixed),
        ],
        out_specs=pl.BlockSpec((ROW_TILE, D_MODEL), row),
        out_shape=jax.ShapeDtypeStruct((n_tok, D_MODEL), jnp.float32),
        compiler_params=_params("parallel"),
        name="combine_ln",
    )(yg, gates, x1, g, b)


def _route(top_idx):
    n_assign = top_idx.size
    e_flat = top_idx.reshape(n_assign)
    onehot = (e_flat[:, None] == jnp.arange(N_EXPERTS, dtype=jnp.int32)[None, :]).astype(jnp.int32)
    csum = jnp.cumsum(onehot, axis=0)
    rank = jnp.sum(csum * onehot, axis=1) - 1
    counts = csum[-1]
    padded = (counts + EXPERT_ROWS - 1) // EXPERT_ROWS * EXPERT_ROWS
    padded_end = jnp.cumsum(padded)
    padded_off = padded_end - padded
    pos = padded_off[e_flat] + rank
    n_blocks = n_assign // EXPERT_ROWS + N_EXPERTS
    src_tok = jnp.zeros((n_blocks * EXPERT_ROWS,), jnp.int32).at[pos].set(
        jnp.arange(n_assign, dtype=jnp.int32) // TOP_K)
    block_start = jnp.arange(n_blocks, dtype=jnp.int32) * EXPERT_ROWS
    block_expert = jnp.minimum(jnp.searchsorted(padded_end, block_start, side='right'),
                               N_EXPERTS - 1).astype(jnp.int32)
    n_valid = (padded_end[-1] // EXPERT_ROWS).astype(jnp.int32).reshape(1)
    return pos, src_tok, block_expert, n_valid


def _rope_tables(seq):
    inv_freq = 1.0 / (ROPE_THETA ** (jnp.arange(0, HEAD_DIM, 2, dtype=jnp.float32) / HEAD_DIM))
    ang = jnp.arange(seq, dtype=jnp.float32)[:, None] * inv_freq[None, :]
    cos, sin = jnp.cos(ang), jnp.sin(ang)
    cos_t = jnp.tile(jnp.concatenate([cos, cos], axis=1), (1, 256 // HEAD_DIM))
    sin_t = jnp.tile(jnp.concatenate([-sin, sin], axis=1), (1, 256 // HEAD_DIM))
    return cos_t, sin_t


def kernel(x, w_in, b_in, sinks, w_out, b_out, ln1_g, ln1_b, w_router, b_router, w1, b1, w2, b2, ln2_g, ln2_b):
    batch, seq, d = x.shape
    assert d == D_MODEL and seq % ROW_TILE == 0 and seq % MOBA_BLOCK == 0 and w_in.shape[0] == DEPTH == 1
    n_tok = batch * seq
    bf16 = jnp.bfloat16
    x2 = x.reshape(n_tok, d)
    cos_t, sin_t = _rope_tables(seq)

    proj = _inproj(x2, w_in[0].astype(bf16), b_in[0].reshape(1, IN_WIDTH), cos_t, sin_t, seq)
    o_a = _swa(proj, sinks[0], batch, seq)
    o_b = _moba(proj, batch, seq)

    w_o = w_out[0].astype(bf16)
    w_r = jnp.pad(w_router[0], ((0, 0), (0, LANES - N_EXPERTS))).astype(bf16)
    b_r = jnp.pad(b_router[0], (0, LANES - N_EXPERTS), constant_values=NEG_BIG).reshape(1, LANES)
    x1, x1b, top_idx, gates = _outproj(
        o_a, o_b, w_o[:W_Q_SWA], w_o[W_Q_SWA:], b_out[0].reshape(1, d), x2,
        ln1_g[0].reshape(1, d), ln1_b[0].reshape(1, d), w_r, b_r)

    pos, src_tok, block_expert, n_valid = _route(top_idx)
    xg = jnp.take(x1b, src_tok, axis=0)

    w1r = w1[0].reshape(N_EXPERTS, d, D_FF, 2)
    b1r = b1[0].reshape(N_EXPERTS, 1, D_FF, 2)
    y = _experts(block_expert, n_valid, xg,
                 w1r[..., 0].astype(bf16), w1r[..., 1].astype(bf16), b1r[..., 0], b1r[..., 1],
                 w2[0].astype(bf16), b2[0].reshape(N_EXPERTS, 1, d))

    yg = jnp.take(y, pos.reshape(n_tok, TOP_K).T, axis=0)
    out = _combine(yg, gates, x1, ln2_g[0].reshape(1, d), ln2_b[0].reshape(1, d))
    return out.reshape(batch, seq, d)
```

```python
import functools

import jax
import jax.numpy as jnp
from jax import lax
from jax.experimental import pallas as pl
from jax.experimental.pallas import tpu as pltpu
from jax.experimental.pallas import tpu_sc as plsc

D_MODEL = 1024
HEAD_DIM = 64
N_HEADS_SWA = 8
N_KV_SWA = 2
WINDOW = 128
N_HEADS_MOBA = 8
MOBA_BLOCK = 256
MOBA_TOPK = 3
ROPE_THETA = 10000.0
N_EXPERTS = 32
TOP_K = 4
D_FF = 1024
SWIGLU_LIMIT = 7.0
SWIGLU_ALPHA = 1.702
LN_EPS = 1e-5
DEPTH = 1
DEEPNORM_ALPHA = (2 * DEPTH) ** 0.25

W_Q_SWA = N_HEADS_SWA * HEAD_DIM
W_KV_SWA = N_KV_SWA * HEAD_DIM
W_MOBA = N_HEADS_MOBA * HEAD_DIM
IN_WIDTH = W_Q_SWA + 2 * W_KV_SWA + 3 * W_MOBA
LANES = 128
COL_K_SWA = W_Q_SWA // LANES
COL_V_SWA = COL_K_SWA + 1
COL_Q_MOBA = COL_V_SWA + 1
COL_K_MOBA = COL_Q_MOBA + W_MOBA // LANES
COL_V_MOBA = COL_K_MOBA + W_MOBA // LANES

ROW_TILE = 512
EXPERT_ROWS = 512
PACKED = D_MODEL // 2
NEG_BIG = -1e30
VMEM_LIMIT = 48 * 1024 * 1024

SC_CORES = 2
SC_SUBCORES = 16
SC_WORKERS = SC_CORES * SC_SUBCORES
SC_ROWS = 64

_PROJ_CHUNKS = (
    (0, 256, True, True), (256, 256, True, True),
    (512, 128, True, False), (640, 128, False, False),
    (768, 256, True, True), (1024, 256, True, True),
    (1280, 256, True, False), (1536, 256, True, False),
    (1792, 256, False, False), (2048, 256, False, False),
)


def _params(*sem):
    return pltpu.CompilerParams(dimension_semantics=sem, vmem_limit_bytes=VMEM_LIMIT)


def _pack_bf16_pairs(v):
    n = v.shape[1] // 2
    bits = lax.bitcast_convert_type(v.astype(jnp.bfloat16).astype(jnp.float32), jnp.uint32)
    word = (bits[:, :n] >> 16) | (bits[:, n:] & jnp.uint32(0xFFFF0000))
    return lax.bitcast_convert_type(word, jnp.int32)


def _unpack_bf16_pairs(word):
    bits = lax.bitcast_convert_type(word, jnp.uint32)
    lo = lax.bitcast_convert_type(bits << 16, jnp.float32)
    hi = lax.bitcast_convert_type(bits & jnp.uint32(0xFFFF0000), jnp.float32)
    return lo, hi


def _inproj_kernel(x_ref, w_ref, b_ref, cos_ref, sin_ref, o_ref):
    xb = x_ref[...].astype(jnp.bfloat16)
    for start, width, rope, scaled in _PROJ_CHUNKS:
        t = jnp.dot(xb, w_ref[:, start:start + width], preferred_element_type=jnp.float32)
        t = t + b_ref[:, start:start + width]
        if rope:
            lane = lax.broadcasted_iota(jnp.int32, t.shape, 1)
            first_half = (lane % HEAD_DIM) < (HEAD_DIM // 2)
            rot = jnp.where(first_half,
                            pltpu.roll(t, width - HEAD_DIM // 2, 1),
                            pltpu.roll(t, HEAD_DIM // 2, 1))
            t = t * cos_ref[:, :width] + rot * sin_ref[:, :width]
        if scaled:
            t = t * (HEAD_DIM ** -0.5)
        o_ref[:, start:start + width] = t.astype(o_ref.dtype)


def _inproj(x2, w_in, b_in, cos_t, sin_t, seq):
    n_tok = x2.shape[0]
    per_seq = seq // ROW_TILE
    return pl.pallas_call(
        _inproj_kernel,
        grid=(n_tok // ROW_TILE,),
        in_specs=[
            pl.BlockSpec((ROW_TILE, D_MODEL), lambda i: (i, 0)),
            pl.BlockSpec((D_MODEL, IN_WIDTH), lambda i: (0, 0)),
            pl.BlockSpec((1, IN_WIDTH), lambda i: (0, 0)),
            pl.BlockSpec((ROW_TILE, 256), lambda i: (i % per_seq, 0)),
            pl.BlockSpec((ROW_TILE, 256), lambda i: (i % per_seq, 0)),
        ],
        out_specs=pl.BlockSpec((ROW_TILE, IN_WIDTH), lambda i: (i, 0)),
        out_shape=jax.ShapeDtypeStruct((n_tok, IN_WIDTH), jnp.bfloat16),
        compiler_params=_params("parallel"),
        name="inproj_rope",
    )(x2, w_in, b_in, cos_t, sin_t)


def _swa_kernel(sink_ref, q_ref, k_ref, v_ref, o_ref, kd_ref, vd_ref, *, seq):
    lane = lax.broadcasted_iota(jnp.int32, (seq, LANES), 1)
    low = lane < HEAD_DIM
    for src, dst in ((k_ref, kd_ref), (v_ref, vd_ref)):
        t = src[...].astype(jnp.float32)
        tr = pltpu.roll(t, HEAD_DIM, 1)
        dst[0, WINDOW:, :] = jnp.where(low, t, tr).astype(dst.dtype)
        dst[1, WINDOW:, :] = jnp.where(low, tr, t).astype(dst.dtype)
        dst[:, :WINDOW, :] = jnp.zeros((2, WINDOW, LANES), dst.dtype)

    group = N_HEADS_SWA // N_KV_SWA
    rows = group * WINDOW
    r_id = lax.broadcasted_iota(jnp.int32, (rows, 2 * WINDOW), 0)
    c_id = lax.broadcasted_iota(jnp.int32, (rows, 2 * WINDOW), 1)
    r_in = r_id % WINDOW
    band = (c_id > r_in) & (c_id <= r_in + WINDOW)
    head_in_group = lax.broadcasted_iota(jnp.int32, (rows, 1), 0) // WINDOW
    qlane_low = lax.broadcasted_iota(jnp.int32, (WINDOW, LANES), 1) < HEAD_DIM

    def block(n, carry):
        r0 = pl.multiple_of(n * WINDOW, WINDOW)
        mask = band & ((c_id >= WINDOW) | (n > 0))
        for g in range(N_KV_SWA):
            parts = []
            for c in (2 * g, 2 * g + 1):
                qc = q_ref[pl.ds(r0, WINDOW), c * LANES:(c + 1) * LANES]
                zero = jnp.zeros_like(qc)
                parts.append(jnp.where(qlane_low, qc, zero))
                parts.append(jnp.where(qlane_low, zero, qc))
            qcat = jnp.concatenate(parts, axis=0)
            kd = kd_ref[g, pl.ds(r0, 2 * WINDOW), :]
            vd = vd_ref[g, pl.ds(r0, 2 * WINDOW), :]
            s = lax.dot_general(qcat, kd, (((1,), (1,)), ((), ())),
                                preferred_element_type=jnp.float32)
            s = jnp.where(mask, s, -jnp.inf)
            sink = jnp.zeros((rows, 1), jnp.float32)
            for j in range(group):
                sink = jnp.where(head_in_group == j, sink_ref[g * group + j], sink)
            m = jnp.maximum(jnp.max(s, axis=-1, keepdims=True), sink)
            p = jnp.exp(s - m)
            denom = jnp.sum(p, axis=-1, keepdims=True) + jnp.exp(sink - m)
            o = jnp.dot(p.astype(vd.dtype), vd, preferred_element_type=jnp.float32) / denom
            for ci, c in enumerate((2 * g, 2 * g + 1)):
                lo = o[(2 * ci) * WINDOW:(2 * ci + 1) * WINDOW]
                hi = o[(2 * ci + 1) * WINDOW:(2 * ci + 2) * WINDOW]
                o_ref[pl.ds(r0, WINDOW), c * LANES:(c + 1) * LANES] = (
                    jnp.where(qlane_low, lo, hi).astype(o_ref.dtype))
        return carry

    lax.fori_loop(0, seq // WINDOW, block, 0)


def _swa(proj, sinks, batch, seq):
    grid_spec = pltpu.PrefetchScalarGridSpec(
        num_scalar_prefetch=0,
        grid=(batch,),
        in_specs=[
            pl.BlockSpec(memory_space=pltpu.SMEM),
            pl.BlockSpec((seq, W_Q_SWA), lambda b: (b, 0)),
            pl.BlockSpec((seq, LANES), lambda b: (b, COL_K_SWA)),
            pl.BlockSpec((seq, LANES), lambda b: (b, COL_V_SWA)),
        ],
        out_specs=pl.BlockSpec((seq, W_Q_SWA), lambda b: (b, 0)),
        scratch_shapes=[pltpu.VMEM((N_KV_SWA, WINDOW + seq, LANES), jnp.bfloat16),
                        pltpu.VMEM((N_KV_SWA, WINDOW + seq, LANES), jnp.bfloat16)],
    )
    return pl.pallas_call(
        functools.partial(_swa_kernel, seq=seq),
        grid_spec=grid_spec,
        out_shape=jax.ShapeDtypeStruct((batch * seq, W_Q_SWA), jnp.bfloat16),
        compiler_params=_params("parallel"),
        name="swa_sink_attention",
    )(sinks, proj, proj, proj)


def _moba_kernel(q_ref, k_ref, v_ref, o_ref, *, seq):
    nblk = seq // MOBA_BLOCK
    pad_rows = 16
    k_all = k_ref[...]
    kmean = jnp.sum(k_all.astype(jnp.float32).reshape(nblk, MOBA_BLOCK, LANES), axis=1) / MOBA_BLOCK
    kmean = jnp.concatenate([kmean, jnp.zeros((pad_rows - nblk, LANES), jnp.float32)], axis=0)
    klane_low = lax.broadcasted_iota(jnp.int32, (pad_rows, LANES), 1) < HEAD_DIM

    j_id = lax.broadcasted_iota(jnp.int32, (pad_rows, seq), 0)
    q_blk = lax.broadcasted_iota(jnp.int32, (pad_rows, seq), 1) // MOBA_BLOCK
    eligible = j_id < q_blk
    qlane_low = lax.broadcasted_iota(jnp.int32, (MOBA_BLOCK, LANES), 1) < HEAD_DIM
    rr = lax.broadcasted_iota(jnp.int32, (MOBA_BLOCK, MOBA_BLOCK), 0)
    cc = lax.broadcasted_iota(jnp.int32, (MOBA_BLOCK, MOBA_BLOCK), 1)
    causal = cc <= rr

    outs = []
    for half in range(2):
        km = jnp.where(klane_low if half == 0 else ~klane_low, kmean, 0.0).astype(jnp.bfloat16)
        gate = lax.dot_general(km, q_ref[...], (((1,), (1,)), ((), ())),
                               preferred_element_type=jnp.float32)
        gate = jnp.where(eligible, gate, -jnp.inf)
        beaten = jnp.zeros((pad_rows, seq), jnp.int32)
        for jp in range(nblk):
            row = gate[jp:jp + 1, :]
            wins = (row > gate) | ((row == gate) & (jp < j_id))
            beaten = beaten + wins.astype(jnp.int32)
        selected = eligible & (beaten < MOBA_TOPK)
        bias = jnp.where(selected, 0.0, NEG_BIG)
        bias = jnp.concatenate([bias, jnp.zeros((LANES - pad_rows, seq), jnp.float32)], axis=0)
        bias_t = bias.T

        blocks = []
        for i in range(nblk):
            r0 = i * MOBA_BLOCK
            qc = q_ref[r0:r0 + MOBA_BLOCK, :]
            zero = jnp.zeros_like(qc)
            qm = jnp.where(qlane_low if half == 0 else ~qlane_low, qc, zero)
            keys = k_ref[0:r0 + MOBA_BLOCK, :]
            s = lax.dot_general(qm, keys, (((1,), (1,)), ((), ())),
                                preferred_element_type=jnp.float32)
            pieces = []
            for j in range(i):
                pieces.append(s[:, j * MOBA_BLOCK:(j + 1) * MOBA_BLOCK] + bias_t[r0:r0 + MOBA_BLOCK, j:j + 1])
            pieces.append(jnp.where(causal, s[:, r0:r0 + MOBA_BLOCK], NEG_BIG))
            s = jnp.concatenate(pieces, axis=1) if len(pieces) > 1 else pieces[0]
            m = jnp.max(s, axis=-1, keepdims=True)
            p = jnp.exp(s - m)
            denom = jnp.sum(p, axis=-1, keepdims=True)
            vals = v_ref[0:r0 + MOBA_BLOCK, :]
            blocks.append(jnp.dot(p.astype(vals.dtype), vals, preferred_element_type=jnp.float32) / denom)
        outs.append(blocks)

    for i in range(nblk):
        r0 = i * MOBA_BLOCK
        o_ref[r0:r0 + MOBA_BLOCK, :] = jnp.where(qlane_low, outs[0][i], outs[1][i]).astype(o_ref.dtype)


def _moba(proj, batch, seq):
    pairs = W_MOBA // LANES
    return pl.pallas_call(
        functools.partial(_moba_kernel, seq=seq),
        grid=(batch, pairs),
        in_specs=[
            pl.BlockSpec((seq, LANES), lambda b, p: (b, COL_Q_MOBA + p)),
            pl.BlockSpec((seq, LANES), lambda b, p: (b, COL_K_MOBA + p)),
            pl.BlockSpec((seq, LANES), lambda b, p: (b, COL_V_MOBA + p)),
        ],
        out_specs=pl.BlockSpec((seq, LANES), lambda b, p: (b, p)),
        out_shape=jax.ShapeDtypeStruct((batch * seq, W_MOBA), jnp.bfloat16),
        compiler_params=_params("parallel", "parallel"),
        name="moba_attention",
    )(proj, proj, proj)


def _layer_norm(h, g, b):
    mu = jnp.mean(h, axis=-1, keepdims=True)
    d = h - mu
    var = jnp.mean(d * d, axis=-1, keepdims=True)
    return d * lax.rsqrt(var + LN_EPS) * g + b


def _outproj_kernel(oa_ref, ob_ref, wa_ref, wb_ref, bo_ref, x_ref, g_ref, b_ref, wr_ref, br_ref, tri_ref,
                    x1_ref, x1p_ref, idx_ref, gate_ref, rank_ref, count_ref, running_ref):
    @pl.when(pl.program_id(0) == 0)
    def _():
        running_ref[...] = jnp.zeros(running_ref.shape, running_ref.dtype)

    mix = jnp.dot(oa_ref[...], wa_ref[...], preferred_element_type=jnp.float32)
    mix = mix + jnp.dot(ob_ref[...], wb_ref[...], preferred_element_type=jnp.float32)
    mix = mix + bo_ref[...]
    x1 = _layer_norm(DEEPNORM_ALPHA * x_ref[...] + mix, g_ref[...], b_ref[...])
    x1_ref[...] = x1
    x1p_ref[...] = _pack_bf16_pairs(x1)
    logits = jnp.dot(x1.astype(jnp.bfloat16), wr_ref[...], preferred_element_type=jnp.float32) + br_ref[...]
    lane = lax.broadcasted_iota(jnp.int32, logits.shape, 1)
    idx_out = jnp.zeros(logits.shape, jnp.int32)
    val_out = jnp.zeros(logits.shape, jnp.float32)
    top = None
    total = None
    onehots = []
    for k in range(TOP_K):
        m = jnp.max(logits, axis=-1, keepdims=True)
        idx = jnp.min(jnp.where(logits == m, lane, LANES), axis=-1, keepdims=True)
        picked = lane == idx
        onehots.append(picked)
        logits = jnp.where(picked, -jnp.inf, logits)
        if k == 0:
            top = m
        e = jnp.exp(m - top)
        total = e if k == 0 else total + e
        idx_out = jnp.where(lane == k, idx, idx_out)
        val_out = jnp.where(lane == k, e, val_out)
    idx_ref[...] = idx_out[:, :TOP_K]
    gate_ref[...] = (val_out / total)[:, :TOP_K]

    picks = jnp.concatenate([p.astype(jnp.bfloat16) for p in onehots], axis=1)
    before = jnp.dot(tri_ref[...], picks, preferred_element_type=jnp.float32)
    base = running_ref[...]
    rank_out = jnp.zeros(logits.shape, jnp.float32)
    for k in range(TOP_K):
        pk = onehots[k].astype(jnp.float32)
        here = before[:, k * LANES:(k + 1) * LANES] + base
        rank_k = jnp.sum(pk * here, axis=-1, keepdims=True)
        rank_out = jnp.where(lane == k, rank_k, rank_out)
        base = base + jnp.sum(pk, axis=0, keepdims=True)
    running_ref[...] = base
    rank_ref[...] = rank_out[:, :TOP_K].astype(jnp.int32)
    count_ref[...] = base.astype(jnp.int32)


def _outproj(o_a, o_b, w_a, w_b, b_out, x2, g, b, w_r, b_r):
    n_tok = x2.shape[0]
    row = lambda i: (i, 0)
    fixed = lambda i: (0, 0)
    r = lax.broadcasted_iota(jnp.int32, (ROW_TILE, ROW_TILE), 0)
    c = lax.broadcasted_iota(jnp.int32, (ROW_TILE, ROW_TILE), 1)
    tri = (c < r).astype(jnp.bfloat16)
    return pl.pallas_call(
        _outproj_kernel,
        grid=(n_tok // ROW_TILE,),
        in_specs=[
            pl.BlockSpec((ROW_TILE, W_Q_SWA), row),
            pl.BlockSpec((ROW_TILE, W_MOBA), row),
            pl.BlockSpec((W_Q_SWA, D_MODEL), fixed),
            pl.BlockSpec((W_MOBA, D_MODEL), fixed),
            pl.BlockSpec((1, D_MODEL), fixed),
            pl.BlockSpec((ROW_TILE, D_MODEL), row),
            pl.BlockSpec((1, D_MODEL), fixed),
            pl.BlockSpec((1, D_MODEL), fixed),
            pl.BlockSpec((D_MODEL, LANES), fixed),
            pl.BlockSpec((1, LANES), fixed),
            pl.BlockSpec((ROW_TILE, ROW_TILE), fixed),
        ],
        out_specs=[
            pl.BlockSpec((ROW_TILE, D_MODEL), row),
            pl.BlockSpec((ROW_TILE, PACKED), row),
            pl.BlockSpec((ROW_TILE, TOP_K), row),
            pl.BlockSpec((ROW_TILE, TOP_K), row),
            pl.BlockSpec((ROW_TILE, TOP_K), row),
            pl.BlockSpec((1, LANES), fixed),
        ],
        out_shape=[
            jax.ShapeDtypeStruct((n_tok, D_MODEL), jnp.float32),
            jax.ShapeDtypeStruct((n_tok, PACKED), jnp.int32),
            jax.ShapeDtypeStruct((n_tok, TOP_K), jnp.int32),
            jax.ShapeDtypeStruct((n_tok, TOP_K), jnp.float32),
            jax.ShapeDtypeStruct((n_tok, TOP_K), jnp.int32),
            jax.ShapeDtypeStruct((1, LANES), jnp.int32),
        ],
        scratch_shapes=[pltpu.VMEM((1, LANES), jnp.float32)],
        compiler_params=_params("arbitrary"),
        name="outproj_ln_router",
    )(o_a, o_b, w_a, w_b, b_out, x2, g, b, w_r, b_r, tri)


def _sc_worker_id():
    return lax.axis_index("s") * SC_CORES + lax.axis_index("c")


def _sc_scatter_rows(rows, pos3, n_out):
    n_tok = rows.shape[0]
    steps = n_tok // SC_ROWS // SC_WORKERS
    assert steps * SC_ROWS * SC_WORKERS == n_tok and steps % 2 == 0
    mesh = plsc.VectorSubcoreMesh(core_axis_name="c", subcore_axis_name="s")

    @functools.partial(
        pl.kernel, mesh=mesh,
        out_type=jax.ShapeDtypeStruct((n_out, PACKED), jnp.int32),
        scratch_types=[pltpu.VMEM((2, TOP_K, SC_ROWS), jnp.int32), pltpu.VMEM((2, SC_ROWS, PACKED), jnp.int32),
                       pltpu.SemaphoreType.DMA((2,)), pltpu.SemaphoreType.DMA((2,))],
        name="sc_dispatch_scatter")
    def scatter(x_hbm, pos_hbm, out_hbm, idx_v, rows_v, sem_ld, sem_st):
        base = _sc_worker_id() * steps

        def loads(s, b):
            return (pltpu.make_async_copy(pos_hbm.at[base + s], idx_v.at[b], sem_ld.at[b]),
                    pltpu.make_async_copy(x_hbm.at[pl.ds((base + s) * SC_ROWS, SC_ROWS)], rows_v.at[b],
                                          sem_ld.at[b]))

        def stores(b):
            return [pltpu.make_async_copy(rows_v.at[b], out_hbm.at[idx_v.at[b, k]], sem_st.at[b])
                    for k in range(TOP_K)]

        for c in loads(0, 0):
            c.start()

        @pl.loop(0, steps, step=2)
        def _(s0):
            for b in range(2):
                s = s0 + b
                for c in loads(s, b):
                    c.wait()

                @pl.when(s >= 1)
                def _():
                    for c in stores(1 - b):
                        c.wait()

                @pl.when(s + 1 < steps)
                def _():
                    for c in loads(s + 1, 1 - b):
                        c.start()

                for c in stores(b):
                    c.start()

        for c in stores(1):
            c.wait()

    return scatter(rows, pos3)


def _sc_gather_rows(table, idx2):
    n_blk = idx2.shape[0]
    steps = n_blk // SC_WORKERS
    assert steps * SC_WORKERS == n_blk and steps % 2 == 0 and idx2.shape[1] == SC_ROWS
    mesh = plsc.VectorSubcoreMesh(core_axis_name="c", subcore_axis_name="s")

    @functools.partial(
        pl.kernel, mesh=mesh,
        out_type=jax.ShapeDtypeStruct((n_blk * SC_ROWS, PACKED), jnp.int32),
        scratch_types=[pltpu.VMEM((steps, SC_ROWS), jnp.int32), pltpu.VMEM((2, SC_ROWS, PACKED), jnp.int32),
                       pltpu.SemaphoreType.DMA((2,)), pltpu.SemaphoreType.DMA((2,))],
        name="sc_combine_gather")
    def gather(y_hbm, idx_hbm, out_hbm, idx_v, rows_v, sem_ld, sem_st):
        base = _sc_worker_id() * steps
        pltpu.sync_copy(idx_hbm.at[pl.ds(base, steps)], idx_v)

        def fetch(s, b):
            return pltpu.make_async_copy(y_hbm.at[idx_v.at[s]], rows_v.at[b], sem_ld.at[b])

        def store(s, b):
            return pltpu.make_async_copy(rows_v.at[b], out_hbm.at[pl.ds((base + s) * SC_ROWS, SC_ROWS)],
                                         sem_st.at[b])

        fetch(0, 0).start()

        @pl.loop(0, steps, step=2)
        def _(s0):
            for b in range(2):
                s = s0 + b
                fetch(s, b).wait()

                @pl.when(s >= 1)
                def _():
                    store(s - 1, 1 - b).wait()

                @pl.when(s + 1 < steps)
                def _():
                    fetch(s + 1, 1 - b).start()

                store(s, b).start()

        store(steps - 1, 1).wait()

    return gather(table, idx2)


def _deinterleave_kernel(w_ref, perm_ref, g_ref, l_ref):
    for c in range(w_ref.shape[2] // 256):
        t = w_ref[0, :, c * 256:(c + 1) * 256].astype(jnp.bfloat16)
        r = jnp.dot(t, perm_ref[...], preferred_element_type=jnp.float32)
        g_ref[0, :, c * LANES:(c + 1) * LANES] = r[:, :LANES].astype(g_ref.dtype)
        l_ref[0, :, c * LANES:(c + 1) * LANES] = r[:, LANES:].astype(l_ref.dtype)


def _deinterleave_w1(w1):
    n_exp, d, two_f = w1.shape
    rows = 512
    i = lax.broadcasted_iota(jnp.int32, (256, 256), 0)
    o = lax.broadcasted_iota(jnp.int32, (256, 256), 1)
    perm = (i == jnp.where(o < LANES, 2 * o, 2 * (o - LANES) + 1)).astype(jnp.bfloat16)
    return pl.pallas_call(
        _deinterleave_kernel,
        grid=(n_exp, d // rows),
        in_specs=[pl.BlockSpec((1, rows, two_f), lambda e, r: (e, r, 0)),
                  pl.BlockSpec((256, 256), lambda e, r: (0, 0))],
        out_specs=[pl.BlockSpec((1, rows, two_f // 2), lambda e, r: (e, r, 0)),
                   pl.BlockSpec((1, rows, two_f // 2), lambda e, r: (e, r, 0))],
        out_shape=[jax.ShapeDtypeStruct((n_exp, d, two_f // 2), jnp.bfloat16),
                   jax.ShapeDtypeStruct((n_exp, d, two_f // 2), jnp.bfloat16)],
        compiler_params=_params("parallel", "parallel"),
        name="w1_deinterleave",
    )(w1, perm)


def _expert_kernel(be_ref, br_ref, nv_ref, x_ref, wg_ref, wl_ref, bg_ref, bl_ref, w2_ref, b2_ref, y_ref):
    i = pl.program_id(0)

    @pl.when(i < nv_ref[0])
    def _():
        valid = lax.broadcasted_iota(jnp.int32, x_ref.shape, 0) < br_ref[i]
        lo, hi = _unpack_bf16_pairs(jnp.where(valid, x_ref[...], 0))
        xb = jnp.concatenate([lo.astype(jnp.bfloat16), hi.astype(jnp.bfloat16)], axis=1)
        hg = jnp.dot(xb, wg_ref[0], preferred_element_type=jnp.float32) + bg_ref[0]
        hl = jnp.dot(xb, wl_ref[0], preferred_element_type=jnp.float32) + bl_ref[0]
        glu = jnp.minimum(hg, SWIGLU_LIMIT)
        lin = jnp.clip(hl, -SWIGLU_LIMIT, SWIGLU_LIMIT)
        act = glu * jax.nn.sigmoid(SWIGLU_ALPHA * glu) * (lin + 1.0)
        y = jnp.dot(act.astype(jnp.bfloat16), w2_ref[0], preferred_element_type=jnp.float32) + b2_ref[0]
        y_ref[...] = _pack_bf16_pairs(y)

    @pl.when(i >= nv_ref[0])
    def _():
        y_ref[...] = jnp.zeros(y_ref.shape, y_ref.dtype)


def _experts(block_expert, block_rows, n_valid, xg, w1g, w1l, b1g, b1l, w2, b2):
    n_rows = xg.shape[0]
    n_blocks = n_rows // EXPERT_ROWS

    def row(i, be, br, nv):
        return (jnp.minimum(i, nv[0] - 1), 0)

    def per_expert(i, be, br, nv):
        return (be[i], 0, 0)

    grid_spec = pltpu.PrefetchScalarGridSpec(
        num_scalar_prefetch=3,
        grid=(n_blocks,),
        in_specs=[
            pl.BlockSpec((EXPERT_ROWS, PACKED), row),
            pl.BlockSpec((1, D_MODEL, D_FF), per_expert),
            pl.BlockSpec((1, D_MODEL, D_FF), per_expert),
            pl.BlockSpec((1, 1, D_FF), per_expert),
            pl.BlockSpec((1, 1, D_FF), per_expert),
            pl.BlockSpec((1, D_FF, D_MODEL), per_expert),
            pl.BlockSpec((1, 1, D_MODEL), per_expert),
        ],
        out_specs=pl.BlockSpec((EXPERT_ROWS, PACKED), lambda i, be, br, nv: (i, 0)),
    )
    return pl.pallas_call(
        _expert_kernel,
        grid_spec=grid_spec,
        out_shape=jax.ShapeDtypeStruct((n_rows, PACKED), jnp.int32),
        compiler_params=_params("arbitrary"),
        name="grouped_experts",
    )(block_expert, block_rows, n_valid, xg, w1g, w1l, b1g, b1l, w2, b2)


def _combine_kernel(y_ref, gate_ref, x1_ref, g_ref, b_ref, o_ref):
    gates = gate_ref[...]
    lo_sum = None
    hi_sum = None
    for k in range(TOP_K):
        lo, hi = _unpack_bf16_pairs(y_ref[k])
        gk = gates[:, k:k + 1]
        lo_sum = gk * lo if k == 0 else lo_sum + gk * lo
        hi_sum = gk * hi if k == 0 else hi_sum + gk * hi
    moe = jnp.concatenate([lo_sum, hi_sum], axis=1)
    o_ref[...] = _layer_norm(DEEPNORM_ALPHA * x1_ref[...] + moe, g_ref[...], b_ref[...])


def _combine(yg, gates, x1, g, b):
    n_tok = x1.shape[0]
    row = lambda i: (i, 0)
    fixed = lambda i: (0, 0)
    return pl.pallas_call(
        _combine_kernel,
        grid=(n_tok // ROW_TILE,),
        in_specs=[
            pl.BlockSpec((TOP_K, ROW_TILE, PACKED), lambda i: (0, i, 0)),
            pl.BlockSpec((ROW_TILE, TOP_K), row),
            pl.BlockSpec((ROW_TILE, D_MODEL), row),
            pl.BlockSpec((1, D_MODEL), fixed),
            pl.BlockSpec((1, D_MODEL), fixed),
        ],
        out_specs=pl.BlockSpec((ROW_TILE, D_MODEL), row),
        out_shape=jax.ShapeDtypeStruct((n_tok, D_MODEL), jnp.float32),
        compiler_params=_params("parallel"),
        name="combine_ln",
    )(yg, gates, x1, g, b)


def _route(top_idx, rank, counts, n_blocks):
    counts = counts[0, :N_EXPERTS]
    blocks_per = (counts + EXPERT_ROWS - 1) // EXPERT_ROWS
    blk_end = jnp.cumsum(blocks_per)
    blk_start = blk_end - blocks_per
    pos = (blk_start * EXPERT_ROWS)[top_idx] + rank
    blk = jnp.arange(n_blocks, dtype=jnp.int32)
    block_expert = jnp.minimum(jnp.sum((blk[:, None] >= blk_end[None, :]).astype(jnp.int32), axis=1),
                               N_EXPERTS - 1)
    inside = blk - blk_start[block_expert]
    block_rows = jnp.clip(counts[block_expert] - inside * EXPERT_ROWS, 0, EXPERT_ROWS).astype(jnp.int32)
    n_valid = blk_end[-1:].astype(jnp.int32)
    return pos, block_expert.astype(jnp.int32), block_rows, n_valid


def _rope_tables(seq):
    inv_freq = 1.0 / (ROPE_THETA ** (jnp.arange(0, HEAD_DIM, 2, dtype=jnp.float32) / HEAD_DIM))
    ang = jnp.arange(seq, dtype=jnp.float32)[:, None] * inv_freq[None, :]
    cos, sin = jnp.cos(ang), jnp.sin(ang)
    cos_t = jnp.tile(jnp.concatenate([cos, cos], axis=1), (1, 256 // HEAD_DIM))
    sin_t = jnp.tile(jnp.concatenate([-sin, sin], axis=1), (1, 256 // HEAD_DIM))
    return cos_t, sin_t


def kernel(x, w_in, b_in, sinks, w_out, b_out, ln1_g, ln1_b, w_router, b_router, w1, b1, w2, b2, ln2_g, ln2_b):
    batch, seq, d = x.shape
    assert d == D_MODEL and seq % ROW_TILE == 0 and seq % MOBA_BLOCK == 0 and w_in.shape[0] == DEPTH == 1
    n_tok = batch * seq
    bf16 = jnp.bfloat16
    x2 = x.reshape(n_tok, d)
    cos_t, sin_t = _rope_tables(seq)

    proj = _inproj(x2, w_in[0].astype(bf16), b_in[0].reshape(1, IN_WIDTH), cos_t, sin_t, seq)
    o_a = _swa(proj, sinks[0], batch, seq)
    o_b = _moba(proj, batch, seq)

    w_o = w_out[0].astype(bf16)
    w_r = jnp.pad(w_router[0], ((0, 0), (0, LANES - N_EXPERTS))).astype(bf16)
    b_r = jnp.pad(b_router[0], (0, LANES - N_EXPERTS), constant_values=NEG_BIG).reshape(1, LANES)
    x1, x1p, top_idx, gates, rank, counts = _outproj(
        o_a, o_b, w_o[:W_Q_SWA], w_o[W_Q_SWA:], b_out[0].reshape(1, d), x2,
        ln1_g[0].reshape(1, d), ln1_b[0].reshape(1, d), w_r, b_r)

    n_blocks = n_tok * TOP_K // EXPERT_ROWS + N_EXPERTS
    pos, block_expert, block_rows, n_valid = _route(top_idx, rank, counts, n_blocks)
    pos3 = pos.reshape(n_tok // SC_ROWS, SC_ROWS, TOP_K).transpose(0, 2, 1)
    xg = _sc_scatter_rows(x1p, pos3, n_blocks * EXPERT_ROWS)

    w1g, w1l = _deinterleave_w1(w1[0])
    b1r = b1[0].reshape(N_EXPERTS, 1, D_FF, 2)
    y = _experts(block_expert, block_rows, n_valid, xg, w1g, w1l, b1r[..., 0], b1r[..., 1],
                 w2[0].astype(bf16), b2[0].reshape(N_EXPERTS, 1, d))

    yg = _sc_gather_rows(y, pos.T.reshape(n_tok * TOP_K // SC_ROWS, SC_ROWS))
    out = _combine(yg.reshape(TOP_K, n_tok, PACKED), gates, x1, ln2_g[0].reshape(1, d), ln2_b[0].reshape(1, d))
    return out.reshape(batch, seq, d)
```

```python
import functools

import jax
import jax.numpy as jnp
from jax import lax
from jax.experimental import pallas as pl
from jax.experimental.pallas import tpu as pltpu
from jax.experimental.pallas import tpu_sc as plsc

D_MODEL = 1024
HEAD_DIM = 64
N_HEADS_SWA = 8
N_KV_SWA = 2
WINDOW = 128
N_HEADS_MOBA = 8
MOBA_BLOCK = 256
MOBA_TOPK = 3
ROPE_THETA = 10000.0
N_EXPERTS = 32
TOP_K = 4
D_FF = 1024
SWIGLU_LIMIT = 7.0
SWIGLU_ALPHA = 1.702
LN_EPS = 1e-5
DEPTH = 1
DEEPNORM_ALPHA = (2 * DEPTH) ** 0.25

W_Q_SWA = N_HEADS_SWA * HEAD_DIM
W_KV_SWA = N_KV_SWA * HEAD_DIM
W_MOBA = N_HEADS_MOBA * HEAD_DIM
IN_WIDTH = W_Q_SWA + 2 * W_KV_SWA + 3 * W_MOBA
LANES = 128
COL_K_SWA = W_Q_SWA // LANES
COL_V_SWA = COL_K_SWA + 1
COL_Q_MOBA = COL_V_SWA + 1
COL_K_MOBA = COL_Q_MOBA + W_MOBA // LANES
COL_V_MOBA = COL_K_MOBA + W_MOBA // LANES

ROW_TILE = 512
EXPERT_ROWS = 512
PACKED = D_MODEL // 2
NEG_BIG = -1e30
LOG2E = 1.4426950408889634
MOE_GROUPS = 2
VMEM_LIMIT = 48 * 1024 * 1024

SC_CORES = 2
SC_SUBCORES = 16
SC_WORKERS = SC_CORES * SC_SUBCORES
SC_ROWS = 64

_PROJ_CHUNKS = (
    (0, 256, True, True), (256, 256, True, True),
    (512, 128, True, False), (640, 128, False, False),
    (768, 256, True, True), (1024, 256, True, True),
    (1280, 256, True, False), (1536, 256, True, False),
    (1792, 256, False, False), (2048, 256, False, False),
)


def _params(*sem):
    return pltpu.CompilerParams(dimension_semantics=sem, vmem_limit_bytes=VMEM_LIMIT)


def _pack_bf16_pairs(v):
    n = v.shape[1] // 2
    bits = lax.bitcast_convert_type(v.astype(jnp.bfloat16).astype(jnp.float32), jnp.uint32)
    word = (bits[:, :n] >> 16) | (bits[:, n:] & jnp.uint32(0xFFFF0000))
    return lax.bitcast_convert_type(word, jnp.int32)


def _unpack_bf16_pairs(word):
    bits = lax.bitcast_convert_type(word, jnp.uint32)
    lo = lax.bitcast_convert_type(bits << 16, jnp.float32)
    hi = lax.bitcast_convert_type(bits & jnp.uint32(0xFFFF0000), jnp.float32)
    return lo, hi


def _inproj_kernel(x_ref, w_ref, b_ref, cos_ref, sin_ref, o_ref):
    xb = x_ref[...].astype(jnp.bfloat16)
    for start, width, rope, scaled in _PROJ_CHUNKS:
        t = jnp.dot(xb, w_ref[:, start:start + width], preferred_element_type=jnp.float32)
        t = t + b_ref[:, start:start + width]
        if rope:
            lane = lax.broadcasted_iota(jnp.int32, t.shape, 1)
            first_half = (lane % HEAD_DIM) < (HEAD_DIM // 2)
            rot = jnp.where(first_half,
                            pltpu.roll(t, width - HEAD_DIM // 2, 1),
                            pltpu.roll(t, HEAD_DIM // 2, 1))
            t = t * cos_ref[:, :width] + rot * sin_ref[:, :width]
        if scaled:
            t = t * (HEAD_DIM ** -0.5 * LOG2E)
        o_ref[:, start:start + width] = t.astype(o_ref.dtype)


def _inproj(x2, w_in, b_in, cos_t, sin_t, seq):
    n_tok = x2.shape[0]
    per_seq = seq // ROW_TILE
    return pl.pallas_call(
        _inproj_kernel,
        grid=(n_tok // ROW_TILE,),
        in_specs=[
            pl.BlockSpec((ROW_TILE, D_MODEL), lambda i: (i, 0)),
            pl.BlockSpec((D_MODEL, IN_WIDTH), lambda i: (0, 0)),
            pl.BlockSpec((1, IN_WIDTH), lambda i: (0, 0)),
            pl.BlockSpec((ROW_TILE, 256), lambda i: (i % per_seq, 0)),
            pl.BlockSpec((ROW_TILE, 256), lambda i: (i % per_seq, 0)),
        ],
        out_specs=pl.BlockSpec((ROW_TILE, IN_WIDTH), lambda i: (i, 0)),
        out_shape=jax.ShapeDtypeStruct((n_tok, IN_WIDTH), jnp.bfloat16),
        compiler_params=_params("parallel"),
        name="inproj_rope",
    )(x2, w_in, b_in, cos_t, sin_t)


def _swa_kernel(sink_ref, q_ref, k_ref, v_ref, o_ref, kd_ref, vd_ref, *, seq):
    lane = lax.broadcasted_iota(jnp.int32, (seq, LANES), 1)
    low = lane < HEAD_DIM
    k = k_ref[...].astype(jnp.float32)
    kr = pltpu.roll(k, HEAD_DIM, 1)
    kd_ref[0, WINDOW:, :] = jnp.where(low, k, kr).astype(kd_ref.dtype)
    kd_ref[1, WINDOW:, :] = jnp.where(low, kr, k).astype(kd_ref.dtype)
    v = v_ref[...].astype(jnp.float32)
    vd_ref[0, WINDOW:, :] = jnp.where(low, v, 1.0).astype(vd_ref.dtype)
    vd_ref[1, WINDOW:, :] = jnp.where(low, pltpu.roll(v, HEAD_DIM, 1), 1.0).astype(vd_ref.dtype)
    kd_ref[:, :WINDOW, :] = jnp.zeros((N_KV_SWA, WINDOW, LANES), kd_ref.dtype)
    vd_ref[:, :WINDOW, :] = jnp.zeros((N_KV_SWA, WINDOW, LANES), vd_ref.dtype)

    group = N_HEADS_SWA // N_KV_SWA
    rows = group * WINDOW
    r_in = lax.broadcasted_iota(jnp.int32, (rows, 2 * WINDOW), 0) % WINDOW
    c_id = lax.broadcasted_iota(jnp.int32, (rows, 2 * WINDOW), 1)
    band = (c_id > r_in) & (c_id <= r_in + WINDOW)
    head_in_group = lax.broadcasted_iota(jnp.int32, (rows, 1), 0) // WINDOW
    qlane_low = lax.broadcasted_iota(jnp.int32, (WINDOW, LANES), 1) < HEAD_DIM
    sinks = []
    for g in range(N_KV_SWA):
        col = jnp.zeros((rows, 1), jnp.float32)
        for j in range(group):
            col = jnp.where(head_in_group == j, sink_ref[g * group + j] * LOG2E, col)
        sinks.append(col)

    def block(n, mask):
        r0 = pl.multiple_of(n * WINDOW, WINDOW)
        for g in range(N_KV_SWA):
            parts = []
            for c in (2 * g, 2 * g + 1):
                qc = q_ref[pl.ds(r0, WINDOW), c * LANES:(c + 1) * LANES]
                zero = jnp.zeros_like(qc)
                parts.append(jnp.where(qlane_low, qc, zero))
                parts.append(jnp.where(qlane_low, zero, qc))
            qcat = jnp.concatenate(parts, axis=0)
            kd = kd_ref[g, pl.ds(r0, 2 * WINDOW), :]
            vd = vd_ref[g, pl.ds(r0, 2 * WINDOW), :]
            s = lax.dot_general(qcat, kd, (((1,), (1,)), ((), ())),
                                preferred_element_type=jnp.float32)
            s = jnp.where(mask, s, -jnp.inf)
            m = jnp.maximum(jnp.max(s, axis=-1, keepdims=True), sinks[g])
            p = jnp.exp2(s - m)
            o = jnp.dot(p.astype(vd.dtype), vd, preferred_element_type=jnp.float32)
            sink_term = jnp.exp2(sinks[g] - m)
            for ci, c in enumerate((2 * g, 2 * g + 1)):
                o_lo = o[(2 * ci) * WINDOW:(2 * ci + 1) * WINDOW]
                o_hi = o[(2 * ci + 1) * WINDOW:(2 * ci + 2) * WINDOW]
                e_lo = sink_term[(2 * ci) * WINDOW:(2 * ci + 1) * WINDOW]
                e_hi = sink_term[(2 * ci + 1) * WINDOW:(2 * ci + 2) * WINDOW]
                num = jnp.where(qlane_low, o_lo, pltpu.roll(o_hi, HEAD_DIM, 1))
                den = jnp.where(qlane_low, pltpu.roll(o_lo, HEAD_DIM, 1) + e_lo, o_hi + e_hi)
                o_ref[pl.ds(r0, WINDOW), c * LANES:(c + 1) * LANES] = (num / den).astype(o_ref.dtype)

    block(0, band & (c_id >= WINDOW))

    def body(n, carry):
        block(n, band)
        return carry

    lax.fori_loop(1, seq // WINDOW, body, 0)


def _swa(proj, sinks, batch, seq):
    grid_spec = pltpu.PrefetchScalarGridSpec(
        num_scalar_prefetch=0,
        grid=(batch,),
        in_specs=[
            pl.BlockSpec(memory_space=pltpu.SMEM),
            pl.BlockSpec((seq, W_Q_SWA), lambda b: (b, 0)),
            pl.BlockSpec((seq, LANES), lambda b: (b, COL_K_SWA)),
            pl.BlockSpec((seq, LANES), lambda b: (b, COL_V_SWA)),
        ],
        out_specs=pl.BlockSpec((seq, W_Q_SWA), lambda b: (b, 0)),
        scratch_shapes=[pltpu.VMEM((N_KV_SWA, WINDOW + seq, LANES), jnp.bfloat16),
                        pltpu.VMEM((N_KV_SWA, WINDOW + seq, LANES), jnp.bfloat16)],
    )
    return pl.pallas_call(
        functools.partial(_swa_kernel, seq=seq),
        grid_spec=grid_spec,
        out_shape=jax.ShapeDtypeStruct((batch * seq, W_Q_SWA), jnp.bfloat16),
        compiler_params=_params("parallel"),
        name="swa_sink_attention",
    )(sinks, proj, proj, proj)


def _moba_kernel(q_ref, k_ref, v_ref, o_ref, qa_ref, ka_ref, va_ref, *, seq):
    nblk = seq // MOBA_BLOCK
    pad_rows = 16
    q_all, k_all, v_all = q_ref[...], k_ref[...], v_ref[...]
    kmean = jnp.sum(k_all.astype(jnp.float32).reshape(nblk, MOBA_BLOCK, LANES), axis=1) / MOBA_BLOCK
    kmean = jnp.concatenate([kmean, jnp.zeros((pad_rows - nblk, LANES), jnp.float32)], axis=0)
    klane_low = lax.broadcasted_iota(jnp.int32, (pad_rows, LANES), 1) < HEAD_DIM
    lane = lax.broadcasted_iota(jnp.int32, (seq, LANES), 1)
    low = lane < HEAD_DIM
    key_blk = lax.broadcasted_iota(jnp.int32, (seq, LANES), 0) // MOBA_BLOCK
    j_id = lax.broadcasted_iota(jnp.int32, (pad_rows, seq), 0)
    q_blk = lax.broadcasted_iota(jnp.int32, (pad_rows, seq), 1) // MOBA_BLOCK
    eligible = j_id < q_blk

    for half in range(2):
        own = low if half == 0 else ~low
        spare = HEAD_DIM if half == 0 else 0
        ka_ref[half] = jnp.where(own, k_all, (lane - spare == key_blk).astype(k_all.dtype))
        va_ref[half] = jnp.where(own, v_all, jnp.ones_like(v_all))
        km = jnp.where(klane_low if half == 0 else ~klane_low, kmean, 0.0).astype(jnp.bfloat16)
        gate = lax.dot_general(km, q_all, (((1,), (1,)), ((), ())),
                               preferred_element_type=jnp.float32)
        gate = jnp.where(eligible, gate, -jnp.inf)
        beaten = jnp.zeros((pad_rows, seq), jnp.int32)
        for jp in range(nblk):
            row = gate[jp:jp + 1, :]
            wins = (row > gate) | ((row == gate) & (jp < j_id))
            beaten = beaten + wins.astype(jnp.int32)
        dropped = eligible & (beaten >= MOBA_TOPK)
        bias = jnp.where(dropped, NEG_BIG, 0.0)
        pieces = [bias, jnp.zeros((LANES - spare - pad_rows, seq), jnp.float32)]
        if spare:
            pieces.insert(0, jnp.zeros((spare, seq), jnp.float32))
        bias_t = jnp.concatenate(pieces, axis=0).T
        qa_ref[half] = jnp.where(own, q_all, bias_t.astype(q_all.dtype))

    qlane_low = lax.broadcasted_iota(jnp.int32, (MOBA_BLOCK, LANES), 1) < HEAD_DIM
    rr = lax.broadcasted_iota(jnp.int32, (MOBA_BLOCK, MOBA_BLOCK), 0)
    cc = lax.broadcasted_iota(jnp.int32, (MOBA_BLOCK, MOBA_BLOCK), 1)
    causal = cc <= rr
    for i in range(nblk):
        r0 = i * MOBA_BLOCK
        n_keys = r0 + MOBA_BLOCK
        acc = []
        for half in range(2):
            s = lax.dot_general(qa_ref[half, r0:n_keys, :], ka_ref[half, 0:n_keys, :], (((1,), (1,)), ((), ())),
                                preferred_element_type=jnp.float32)
            own_blk = jnp.where(causal, s[:, r0:n_keys], NEG_BIG)
            s = jnp.concatenate([s[:, :r0], own_blk], axis=1) if i else own_blk
            m = jnp.max(s, axis=-1, keepdims=True)
            p = jnp.exp2(s - m).astype(jnp.bfloat16)
            acc.append(jnp.dot(p, va_ref[half, 0:n_keys, :], preferred_element_type=jnp.float32))
        num = jnp.where(qlane_low, acc[0], acc[1])
        den = pltpu.roll(jnp.where(qlane_low, acc[1], acc[0]), HEAD_DIM, 1)
        o_ref[r0:n_keys, :] = (num / den).astype(o_ref.dtype)


def _moba(proj, batch, seq):
    pairs = W_MOBA // LANES
    return pl.pallas_call(
        functools.partial(_moba_kernel, seq=seq),
        grid=(batch, pairs),
        in_specs=[
            pl.BlockSpec((seq, LANES), lambda b, p: (b, COL_Q_MOBA + p)),
            pl.BlockSpec((seq, LANES), lambda b, p: (b, COL_K_MOBA + p)),
            pl.BlockSpec((seq, LANES), lambda b, p: (b, COL_V_MOBA + p)),
        ],
        out_specs=pl.BlockSpec((seq, LANES), lambda b, p: (b, p)),
        out_shape=jax.ShapeDtypeStruct((batch * seq, W_MOBA), jnp.bfloat16),
        scratch_shapes=[pltpu.VMEM((2, seq, LANES), jnp.bfloat16)] * 3,
        compiler_params=_params("parallel", "parallel"),
        name="moba_attention",
    )(proj, proj, proj)


def _layer_norm(h, g, b):
    mu = jnp.mean(h, axis=-1, keepdims=True)
    d = h - mu
    var = jnp.mean(d * d, axis=-1, keepdims=True)
    return d * lax.rsqrt(var + LN_EPS) * g + b


def _outproj_kernel(oa_ref, ob_ref, wa_ref, wb_ref, bo_ref, x_ref, g_ref, b_ref, wr_ref, br_ref, tri_ref,
                    x1_ref, x1p_ref, idx_ref, gate_ref, rank_ref, count_ref, running_ref, *, tiles_per_group):
    @pl.when(pl.program_id(0) % tiles_per_group == 0)
    def _():
        running_ref[...] = jnp.zeros(running_ref.shape, running_ref.dtype)

    mix = jnp.dot(oa_ref[...], wa_ref[...], preferred_element_type=jnp.float32)
    mix = mix + jnp.dot(ob_ref[...], wb_ref[...], preferred_element_type=jnp.float32)
    mix = mix + bo_ref[...]
    x1 = _layer_norm(DEEPNORM_ALPHA * x_ref[...] + mix, g_ref[...], b_ref[...])
    x1_ref[...] = x1
    x1p_ref[...] = _pack_bf16_pairs(x1)
    logits = jnp.dot(x1.astype(jnp.bfloat16), wr_ref[...], preferred_element_type=jnp.float32) + br_ref[...]
    lane = lax.broadcasted_iota(jnp.int32, logits.shape, 1)
    idx_out = jnp.zeros(logits.shape, jnp.int32)
    val_out = jnp.zeros(logits.shape, jnp.float32)
    top = None
    total = None
    onehots = []
    for k in range(TOP_K):
        m = jnp.max(logits, axis=-1, keepdims=True)
        idx = jnp.min(jnp.where(logits == m, lane, LANES), axis=-1, keepdims=True)
        picked = lane == idx
        onehots.append(picked)
        logits = jnp.where(picked, -jnp.inf, logits)
        if k == 0:
            top = m
        e = jnp.exp(m - top)
        total = e if k == 0 else total + e
        idx_out = jnp.where(lane == k, idx, idx_out)
        val_out = jnp.where(lane == k, e, val_out)
    idx_ref[...] = idx_out[:, :TOP_K]
    gate_ref[...] = (val_out / total)[:, :TOP_K]

    picks = jnp.concatenate([p.astype(jnp.bfloat16) for p in onehots], axis=1)
    before = jnp.dot(tri_ref[...], picks, preferred_element_type=jnp.float32)
    base = running_ref[...]
    rank_out = jnp.zeros(logits.shape, jnp.float32)
    for k in range(TOP_K):
        pk = onehots[k].astype(jnp.float32)
        here = before[:, k * LANES:(k + 1) * LANES] + base
        rank_k = jnp.sum(pk * here, axis=-1, keepdims=True)
        rank_out = jnp.where(lane == k, rank_k, rank_out)
        base = base + jnp.sum(pk, axis=0, keepdims=True)
    running_ref[...] = base
    rank_ref[...] = rank_out[:, :TOP_K].astype(jnp.int32)
    count_ref[0] = base.astype(jnp.int32)


def _outproj(o_a, o_b, w_a, w_b, b_out, x2, g, b, w_r, b_r):
    n_tok = x2.shape[0]
    tiles_per_group = n_tok // ROW_TILE // MOE_GROUPS
    row = lambda i: (i, 0)
    fixed = lambda i: (0, 0)
    r = lax.broadcasted_iota(jnp.int32, (ROW_TILE, ROW_TILE), 0)
    c = lax.broadcasted_iota(jnp.int32, (ROW_TILE, ROW_TILE), 1)
    tri = (c < r).astype(jnp.bfloat16)
    return pl.pallas_call(
        functools.partial(_outproj_kernel, tiles_per_group=tiles_per_group),
        grid=(n_tok // ROW_TILE,),
        in_specs=[
            pl.BlockSpec((ROW_TILE, W_Q_SWA), row),
            pl.BlockSpec((ROW_TILE, W_MOBA), row),
            pl.BlockSpec((W_Q_SWA, D_MODEL), fixed),
            pl.BlockSpec((W_MOBA, D_MODEL), fixed),
            pl.BlockSpec((1, D_MODEL), fixed),
            pl.BlockSpec((ROW_TILE, D_MODEL), row),
            pl.BlockSpec((1, D_MODEL), fixed),
            pl.BlockSpec((1, D_MODEL), fixed),
            pl.BlockSpec((D_MODEL, LANES), fixed),
            pl.BlockSpec((1, LANES), fixed),
            pl.BlockSpec((ROW_TILE, ROW_TILE), fixed),
        ],
        out_specs=[
            pl.BlockSpec((ROW_TILE, D_MODEL), row),
            pl.BlockSpec((ROW_TILE, PACKED), row),
            pl.BlockSpec((ROW_TILE, TOP_K), row),
            pl.BlockSpec((ROW_TILE, TOP_K), row),
            pl.BlockSpec((ROW_TILE, TOP_K), row),
            pl.BlockSpec((1, 1, LANES), lambda i: (i // tiles_per_group, 0, 0)),
        ],
        out_shape=[
            jax.ShapeDtypeStruct((n_tok, D_MODEL), jnp.float32),
            jax.ShapeDtypeStruct((n_tok, PACKED), jnp.int32),
            jax.ShapeDtypeStruct((n_tok, TOP_K), jnp.int32),
            jax.ShapeDtypeStruct((n_tok, TOP_K), jnp.float32),
            jax.ShapeDtypeStruct((n_tok, TOP_K), jnp.int32),
            jax.ShapeDtypeStruct((MOE_GROUPS, 1, LANES), jnp.int32),
        ],
        scratch_shapes=[pltpu.VMEM((1, LANES), jnp.float32)],
        compiler_params=_params("arbitrary"),
        name="outproj_ln_router",
    )(o_a, o_b, w_a, w_b, b_out, x2, g, b, w_r, b_r, tri)


def _sc_worker_id():
    return lax.axis_index("s") * SC_CORES + lax.axis_index("c")


def _sc_scatter_rows(rows, pos3, n_out, row0):
    n_tok = pos3.shape[0] * SC_ROWS
    steps = n_tok // SC_ROWS // SC_WORKERS
    assert steps * SC_ROWS * SC_WORKERS == n_tok and steps % 2 == 0
    mesh = plsc.VectorSubcoreMesh(core_axis_name="c", subcore_axis_name="s")

    @functools.partial(
        pl.kernel, mesh=mesh,
        out_type=jax.ShapeDtypeStruct((n_out, PACKED), jnp.int32),
        scratch_types=[pltpu.VMEM((2, TOP_K, SC_ROWS), jnp.int32), pltpu.VMEM((2, SC_ROWS, PACKED), jnp.int32),
                       pltpu.SemaphoreType.DMA((2,)), pltpu.SemaphoreType.DMA((2,))],
        name="sc_dispatch_scatter")
    def scatter(x_hbm, pos_hbm, out_hbm, idx_v, rows_v, sem_ld, sem_st):
        base = _sc_worker_id() * steps

        def loads(s, b):
            return (pltpu.make_async_copy(pos_hbm.at[base + s], idx_v.at[b], sem_ld.at[b]),
                    pltpu.make_async_copy(x_hbm.at[pl.ds(row0 + (base + s) * SC_ROWS, SC_ROWS)], rows_v.at[b],
                                          sem_ld.at[b]))

        def stores(b):
            return [pltpu.make_async_copy(rows_v.at[b], out_hbm.at[idx_v.at[b, k]], sem_st.at[b])
                    for k in range(TOP_K)]

        for c in loads(0, 0):
            c.start()

        @pl.loop(0, steps, step=2)
        def _(s0):
            for b in range(2):
                s = s0 + b
                for c in loads(s, b):
                    c.wait()

                @pl.when(s >= 1)
                def _():
                    for c in stores(1 - b):
                        c.wait()

                @pl.when(s + 1 < steps)
                def _():
                    for c in loads(s + 1, 1 - b):
                        c.start()

                for c in stores(b):
                    c.start()

        for c in stores(1):
            c.wait()

    return scatter(rows, pos3)


def _sc_gather_rows(table, idx2):
    n_blk = idx2.shape[0]
    steps = n_blk // SC_WORKERS
    assert steps * SC_WORKERS == n_blk and steps % 2 == 0 and idx2.shape[1] == SC_ROWS
    mesh = plsc.VectorSubcoreMesh(core_axis_name="c", subcore_axis_name="s")

    @functools.partial(
        pl.kernel, mesh=mesh,
        out_type=jax.ShapeDtypeStruct((n_blk * SC_ROWS, PACKED), jnp.int32),
        scratch_types=[pltpu.VMEM((steps, SC_ROWS), jnp.int32), pltpu.VMEM((2, SC_ROWS, PACKED), jnp.int32),
                       pltpu.SemaphoreType.DMA((2,)), pltpu.SemaphoreType.DMA((2,))],
        name="sc_combine_gather")
    def gather(y_hbm, idx_hbm, out_hbm, idx_v, rows_v, sem_ld, sem_st):
        base = _sc_worker_id() * steps
        pltpu.sync_copy(idx_hbm.at[pl.ds(base, steps)], idx_v)

        def fetch(s, b):
            return pltpu.make_async_copy(y_hbm.at[idx_v.at[s]], rows_v.at[b], sem_ld.at[b])

        def store(s, b):
            return pltpu.make_async_copy(rows_v.at[b], out_hbm.at[pl.ds((base + s) * SC_ROWS, SC_ROWS)],
                                         sem_st.at[b])

        fetch(0, 0).start()

        @pl.loop(0, steps, step=2)
        def _(s0):
            for b in range(2):
                s = s0 + b
                fetch(s, b).wait()

                @pl.when(s >= 1)
                def _():
                    store(s - 1, 1 - b).wait()

                @pl.when(s + 1 < steps)
                def _():
                    fetch(s + 1, 1 - b).start()

                store(s, b).start()

        store(steps - 1, 1).wait()

    return gather(table, idx2)


def _deinterleave_kernel(w_ref, perm_ref, g_ref, l_ref):
    for c in range(w_ref.shape[2] // 256):
        t = w_ref[0, :, c * 256:(c + 1) * 256].astype(jnp.bfloat16)
        r = jnp.dot(t, perm_ref[...], preferred_element_type=jnp.float32)
        g_ref[0, :, c * LANES:(c + 1) * LANES] = r[:, :LANES].astype(g_ref.dtype)
        l_ref[0, :, c * LANES:(c + 1) * LANES] = r[:, LANES:].astype(l_ref.dtype)


def _deinterleave_w1(w1):
    n_exp, d, two_f = w1.shape
    rows = 512
    i = lax.broadcasted_iota(jnp.int32, (256, 256), 0)
    o = lax.broadcasted_iota(jnp.int32, (256, 256), 1)
    perm = (i == jnp.where(o < LANES, 2 * o, 2 * (o - LANES) + 1)).astype(jnp.bfloat16)
    return pl.pallas_call(
        _deinterleave_kernel,
        grid=(n_exp, d // rows),
        in_specs=[pl.BlockSpec((1, rows, two_f), lambda e, r: (e, r, 0)),
                  pl.BlockSpec((256, 256), lambda e, r: (0, 0))],
        out_specs=[pl.BlockSpec((1, rows, two_f // 2), lambda e, r: (e, r, 0)),
                   pl.BlockSpec((1, rows, two_f // 2), lambda e, r: (e, r, 0))],
        out_shape=[jax.ShapeDtypeStruct((n_exp, d, two_f // 2), jnp.bfloat16),
                   jax.ShapeDtypeStruct((n_exp, d, two_f // 2), jnp.bfloat16)],
        compiler_params=_params("parallel", "parallel"),
        name="w1_deinterleave",
    )(w1, perm)


def _expert_kernel(be_ref, br_ref, nv_ref, x_ref, wg_ref, wl_ref, bg_ref, bl_ref, w2_ref, b2_ref, y_ref):
    i = pl.program_id(0)

    @pl.when(i < nv_ref[0])
    def _():
        valid = lax.broadcasted_iota(jnp.int32, x_ref.shape, 0) < br_ref[i]
        lo, hi = _unpack_bf16_pairs(jnp.where(valid, x_ref[...], 0))
        xb = jnp.concatenate([lo.astype(jnp.bfloat16), hi.astype(jnp.bfloat16)], axis=1)
        hg = jnp.dot(xb, wg_ref[0], preferred_element_type=jnp.float32) + bg_ref[0]
        hl = jnp.dot(xb, wl_ref[0], preferred_element_type=jnp.float32) + bl_ref[0]
        glu = jnp.minimum(hg, SWIGLU_LIMIT)
        lin = jnp.clip(hl, -SWIGLU_LIMIT, SWIGLU_LIMIT)
        act = glu * jax.nn.sigmoid(SWIGLU_ALPHA * glu) * (lin + 1.0)
        y = jnp.dot(act.astype(jnp.bfloat16), w2_ref[0], preferred_element_type=jnp.float32) + b2_ref[0]
        y_ref[...] = _pack_bf16_pairs(y)

    @pl.when(i >= nv_ref[0])
    def _():
        y_ref[...] = jnp.zeros(y_ref.shape, y_ref.dtype)


def _experts(block_expert, block_rows, n_valid, xg, w1g, w1l, b1g, b1l, w2, b2):
    n_rows = xg.shape[0]
    n_blocks = n_rows // EXPERT_ROWS

    def row(i, be, br, nv):
        return (jnp.minimum(i, nv[0] - 1), 0)

    def per_expert(i, be, br, nv):
        return (be[i], 0, 0)

    grid_spec = pltpu.PrefetchScalarGridSpec(
        num_scalar_prefetch=3,
        grid=(n_blocks,),
        in_specs=[
            pl.BlockSpec((EXPERT_ROWS, PACKED), row),
            pl.BlockSpec((1, D_MODEL, D_FF), per_expert),
            pl.BlockSpec((1, D_MODEL, D_FF), per_expert),
            pl.BlockSpec((1, 1, D_FF), per_expert),
            pl.BlockSpec((1, 1, D_FF), per_expert),
            pl.BlockSpec((1, D_FF, D_MODEL), per_expert),
            pl.BlockSpec((1, 1, D_MODEL), per_expert),
        ],
        out_specs=pl.BlockSpec((EXPERT_ROWS, PACKED), lambda i, be, br, nv: (i, 0)),
    )
    return pl.pallas_call(
        _expert_kernel,
        grid_spec=grid_spec,
        out_shape=jax.ShapeDtypeStruct((n_rows, PACKED), jnp.int32),
        compiler_params=_params("arbitrary"),
        name="grouped_experts",
    )(block_expert, block_rows, n_valid, xg, w1g, w1l, b1g, b1l, w2, b2)


def _combine_kernel(y_ref, gate_ref, x1_ref, g_ref, b_ref, *rest):
    o_ref = rest[-1]
    gates = gate_ref[...]
    lo_sum = None
    hi_sum = None
    for k in range(TOP_K):
        lo, hi = _unpack_bf16_pairs(y_ref[k])
        gk = gates[:, k:k + 1]
        lo_sum = gk * lo if k == 0 else lo_sum + gk * lo
        hi_sum = gk * hi if k == 0 else hi_sum + gk * hi
    moe = jnp.concatenate([lo_sum, hi_sum], axis=1)
    o_ref[...] = _layer_norm(DEEPNORM_ALPHA * x1_ref[...] + moe, g_ref[...], b_ref[...])


def _combine(yg, gates, x1, g, b, tile0, prev):
    n_tok = x1.shape[0]
    row = lambda i: (i + tile0, 0)
    fixed = lambda i: (0, 0)
    in_specs = [
        pl.BlockSpec((TOP_K, ROW_TILE, PACKED), lambda i: (0, i, 0)),
        pl.BlockSpec((ROW_TILE, TOP_K), row),
        pl.BlockSpec((ROW_TILE, D_MODEL), row),
        pl.BlockSpec((1, D_MODEL), fixed),
        pl.BlockSpec((1, D_MODEL), fixed),
    ]
    args = [yg, gates, x1, g, b]
    aliases = {}
    if prev is not None:
        in_specs.append(pl.BlockSpec(memory_space=pl.ANY))
        args.append(prev)
        aliases = {len(args) - 1: 0}
    return pl.pallas_call(
        _combine_kernel,
        grid=(yg.shape[1] // ROW_TILE,),
        in_specs=in_specs,
        out_specs=pl.BlockSpec((ROW_TILE, D_MODEL), row),
        out_shape=jax.ShapeDtypeStruct((n_tok, D_MODEL), jnp.float32),
        input_output_aliases=aliases,
        compiler_params=_params("parallel"),
        name="combine_ln",
    )(*args)


def _route(top_idx, rank, counts, n_blocks):
    counts = counts[0, :N_EXPERTS]
    blocks_per = (counts + EXPERT_ROWS - 1) // EXPERT_ROWS
    blk_end = jnp.cumsum(blocks_per)
    blk_start = blk_end - blocks_per
    pos = (blk_start * EXPERT_ROWS)[top_idx] + rank
    blk = jnp.arange(n_blocks, dtype=jnp.int32)
    block_expert = jnp.minimum(jnp.sum((blk[:, None] >= blk_end[None, :]).astype(jnp.int32), axis=1),
                               N_EXPERTS - 1)
    inside = blk - blk_start[block_expert]
    block_rows = jnp.clip(counts[block_expert] - inside * EXPERT_ROWS, 0, EXPERT_ROWS).astype(jnp.int32)
    n_valid = blk_end[-1:].astype(jnp.int32)
    return pos, block_expert.astype(jnp.int32), block_rows, n_valid


def _rope_tables(seq):
    inv_freq = 1.0 / (ROPE_THETA ** (jnp.arange(0, HEAD_DIM, 2, dtype=jnp.float32) / HEAD_DIM))
    ang = jnp.arange(seq, dtype=jnp.float32)[:, None] * inv_freq[None, :]
    cos, sin = jnp.cos(ang), jnp.sin(ang)
    cos_t = jnp.tile(jnp.concatenate([cos, cos], axis=1), (1, 256 // HEAD_DIM))
    sin_t = jnp.tile(jnp.concatenate([-sin, sin], axis=1), (1, 256 // HEAD_DIM))
    return cos_t, sin_t


def kernel(x, w_in, b_in, sinks, w_out, b_out, ln1_g, ln1_b, w_router, b_router, w1, b1, w2, b2, ln2_g, ln2_b):
    batch, seq, d = x.shape
    assert d == D_MODEL and seq % ROW_TILE == 0 and seq % MOBA_BLOCK == 0 and w_in.shape[0] == DEPTH == 1
    n_tok = batch * seq
    bf16 = jnp.bfloat16
    x2 = x.reshape(n_tok, d)
    cos_t, sin_t = _rope_tables(seq)

    proj = _inproj(x2, w_in[0].astype(bf16), b_in[0].reshape(1, IN_WIDTH), cos_t, sin_t, seq)
    o_a = _swa(proj, sinks[0], batch, seq)
    o_b = _moba(proj, batch, seq)

    w_o = w_out[0].astype(bf16)
    w_r = jnp.pad(w_router[0], ((0, 0), (0, LANES - N_EXPERTS))).astype(bf16)
    b_r = jnp.pad(b_router[0], (0, LANES - N_EXPERTS), constant_values=NEG_BIG).reshape(1, LANES)
    x1, x1p, top_idx, gates, rank, counts = _outproj(
        o_a, o_b, w_o[:W_Q_SWA], w_o[W_Q_SWA:], b_out[0].reshape(1, d), x2,
        ln1_g[0].reshape(1, d), ln1_b[0].reshape(1, d), w_r, b_r)

    w1g, w1l = _deinterleave_w1(w1[0])
    b1r = b1[0].reshape(N_EXPERTS, 1, D_FF, 2)
    b1g, b1l = b1r[..., 0], b1r[..., 1]
    w2b = w2[0].astype(bf16)
    b2r = b2[0].reshape(N_EXPERTS, 1, d)
    g2, be2 = ln2_g[0].reshape(1, d), ln2_b[0].reshape(1, d)

    tok_g = n_tok // MOE_GROUPS
    n_blocks = tok_g * TOP_K // EXPERT_ROWS + N_EXPERTS
    out = None
    for grp in range(MOE_GROUPS):
        t0 = grp * tok_g
        pos, block_expert, block_rows, n_valid = _route(
            top_idx[t0:t0 + tok_g], rank[t0:t0 + tok_g], counts[grp], n_blocks)
        pos3 = pos.reshape(tok_g // SC_ROWS, SC_ROWS, TOP_K).transpose(0, 2, 1)
        xg = _sc_scatter_rows(x1p, pos3, n_blocks * EXPERT_ROWS, t0)
        y = _experts(block_expert, block_rows, n_valid, xg, w1g, w1l, b1g, b1l, w2b, b2r)
        yg = _sc_gather_rows(y, pos.T.reshape(tok_g * TOP_K // SC_ROWS, SC_ROWS))
        out = _combine(yg.reshape(TOP_K, tok_g, PACKED), gates, x1, g2, be2, t0 // ROW_TILE, out)
    return out.reshape(batch, seq, d)
```

```python
import functools

import jax
import jax.numpy as jnp
from jax import lax
from jax.experimental import pallas as pl
from jax.experimental.pallas import tpu as pltpu
from jax.experimental.pallas import tpu_sc as plsc

D_MODEL = 1024
HEAD_DIM = 64
N_HEADS_SWA = 8
N_KV_SWA = 2
WINDOW = 128
N_HEADS_MOBA = 8
MOBA_BLOCK = 256
MOBA_TOPK = 3
ROPE_THETA = 10000.0
N_EXPERTS = 32
TOP_K = 4
D_FF = 1024
SWIGLU_LIMIT = 7.0
SWIGLU_ALPHA = 1.702
LN_EPS = 1e-5
DEPTH = 1
DEEPNORM_ALPHA = (2 * DEPTH) ** 0.25

W_Q_SWA = N_HEADS_SWA * HEAD_DIM
W_KV_SWA = N_KV_SWA * HEAD_DIM
W_MOBA = N_HEADS_MOBA * HEAD_DIM
IN_WIDTH = W_Q_SWA + 2 * W_KV_SWA + 3 * W_MOBA
LANES = 128
COL_K_SWA = W_Q_SWA // LANES
COL_V_SWA = COL_K_SWA + 1
COL_Q_MOBA = COL_V_SWA + 1
COL_K_MOBA = COL_Q_MOBA + W_MOBA // LANES
COL_V_MOBA = COL_K_MOBA + W_MOBA // LANES

ROW_TILE = 512
EXPERT_ROWS = 512
PACKED = D_MODEL // 2
NEG_BIG = -1e30
LOG2E = 1.4426950408889634
MOBA_LOOKAHEAD = 1
SWA_UNROLL = 3
MOE_SPLIT = (3, 1)
VMEM_LIMIT = 48 * 1024 * 1024

SC_CORES = 2
SC_SUBCORES = 16
SC_WORKERS = SC_CORES * SC_SUBCORES
SC_ROWS = 64

_PROJ_CHUNKS = (
    (0, 256, 256, True), (256, 256, 256, True),
    (512, 256, 128, False),
    (768, 256, 256, True), (1024, 256, 256, True),
    (1280, 256, 256, False), (1536, 256, 256, False),
    (1792, 256, 0, False), (2048, 256, 0, False),
)


def _params(*sem):
    return pltpu.CompilerParams(dimension_semantics=sem, vmem_limit_bytes=VMEM_LIMIT)


def _pack_bf16_pairs(v):
    n = v.shape[1] // 2
    bits = lax.bitcast_convert_type(v.astype(jnp.bfloat16).astype(jnp.float32), jnp.uint32)
    word = (bits[:, :n] >> 16) | (bits[:, n:] & jnp.uint32(0xFFFF0000))
    return lax.bitcast_convert_type(word, jnp.int32)


def _unpack_bf16_pairs(word):
    bits = lax.bitcast_convert_type(word, jnp.uint32)
    lo = lax.bitcast_convert_type(bits << 16, jnp.float32)
    hi = lax.bitcast_convert_type(bits & jnp.uint32(0xFFFF0000), jnp.float32)
    return lo, hi


def _inproj_kernel(x_ref, w_ref, b_ref, cos_ref, sin_ref, o_ref):
    xb = x_ref[...].astype(jnp.bfloat16)
    for start, width, rope, scaled in _PROJ_CHUNKS:
        t = jnp.dot(xb, w_ref[:, start:start + width], preferred_element_type=jnp.float32)
        t = t + b_ref[:, start:start + width]
        if rope:
            lane = lax.broadcasted_iota(jnp.int32, t.shape, 1)
            first_half = (lane % HEAD_DIM) < (HEAD_DIM // 2)
            rot = jnp.where(first_half,
                            pltpu.roll(t, width - HEAD_DIM // 2, 1),
                            pltpu.roll(t, HEAD_DIM // 2, 1))
            roped = t * cos_ref[:, :width] + rot * sin_ref[:, :width]
            t = roped if rope == width else jnp.where(lane < rope, roped, t)
        if scaled:
            t = t * (HEAD_DIM ** -0.5 * LOG2E)
        o_ref[:, start:start + width] = t.astype(o_ref.dtype)


def _inproj(x2, w_in, b_in, cos_t, sin_t, seq):
    n_tok = x2.shape[0]
    per_seq = seq // ROW_TILE
    return pl.pallas_call(
        _inproj_kernel,
        grid=(n_tok // ROW_TILE,),
        in_specs=[
            pl.BlockSpec((ROW_TILE, D_MODEL), lambda i: (i, 0)),
            pl.BlockSpec((D_MODEL, IN_WIDTH), lambda i: (0, 0)),
            pl.BlockSpec((1, IN_WIDTH), lambda i: (0, 0)),
            pl.BlockSpec((ROW_TILE, 256), lambda i: (i % per_seq, 0)),
            pl.BlockSpec((ROW_TILE, 256), lambda i: (i % per_seq, 0)),
        ],
        out_specs=pl.BlockSpec((ROW_TILE, IN_WIDTH), lambda i: (i, 0)),
        out_shape=jax.ShapeDtypeStruct((n_tok, IN_WIDTH), jnp.bfloat16),
        compiler_params=_params("parallel"),
        name="inproj_rope",
    )(x2, w_in, b_in, cos_t, sin_t)


def _swa_kernel(sink_ref, q_ref, k_ref, v_ref, o_ref, kd_ref, vd_ref, *, seq):
    lane = lax.broadcasted_iota(jnp.int32, (seq, LANES), 1)
    low = lane < HEAD_DIM
    k = k_ref[...].astype(jnp.float32)
    kr = pltpu.roll(k, HEAD_DIM, 1)
    kd_ref[0, WINDOW:, :] = jnp.where(low, k, kr).astype(kd_ref.dtype)
    kd_ref[1, WINDOW:, :] = jnp.where(low, kr, k).astype(kd_ref.dtype)
    v = v_ref[...].astype(jnp.float32)
    vd_ref[0, WINDOW:, :] = jnp.where(low, v, 1.0).astype(vd_ref.dtype)
    vd_ref[1, WINDOW:, :] = jnp.where(low, pltpu.roll(v, HEAD_DIM, 1), 1.0).astype(vd_ref.dtype)
    kd_ref[:, :WINDOW, :] = jnp.zeros((N_KV_SWA, WINDOW, LANES), kd_ref.dtype)
    vd_ref[:, :WINDOW, :] = jnp.zeros((N_KV_SWA, WINDOW, LANES), vd_ref.dtype)

    group = N_HEADS_SWA // N_KV_SWA
    rows = group * WINDOW
    r_in = lax.broadcasted_iota(jnp.int32, (rows, 2 * WINDOW), 0) % WINDOW
    c_id = lax.broadcasted_iota(jnp.int32, (rows, 2 * WINDOW), 1)
    band = (c_id > r_in) & (c_id <= r_in + WINDOW)
    head_in_group = lax.broadcasted_iota(jnp.int32, (rows, 1), 0) // WINDOW
    qlane_low = lax.broadcasted_iota(jnp.int32, (WINDOW, LANES), 1) < HEAD_DIM
    sinks = []
    for g in range(N_KV_SWA):
        col = jnp.zeros((rows, 1), jnp.float32)
        for j in range(group):
            col = jnp.where(head_in_group == j, sink_ref[g * group + j] * LOG2E, col)
        sinks.append(col)

    def scores(n, mask):
        r0 = pl.multiple_of(n * WINDOW, WINDOW)
        out = []
        for g in range(N_KV_SWA):
            parts = []
            for c in (2 * g, 2 * g + 1):
                qc = q_ref[pl.ds(r0, WINDOW), c * LANES:(c + 1) * LANES]
                zero = jnp.zeros_like(qc)
                parts.append(jnp.where(qlane_low, qc, zero))
                parts.append(jnp.where(qlane_low, zero, qc))
            qcat = jnp.concatenate(parts, axis=0)
            kd = kd_ref[g, pl.ds(r0, 2 * WINDOW), :]
            s = lax.dot_general(qcat, kd, (((1,), (1,)), ((), ())),
                                preferred_element_type=jnp.float32)
            out.append(jnp.where(mask, s, -jnp.inf))
        return out

    def finish(n, scored):
        r0 = pl.multiple_of(n * WINDOW, WINDOW)
        for g, s in enumerate(scored):
            vd = vd_ref[g, pl.ds(r0, 2 * WINDOW), :]
            m = jnp.maximum(jnp.max(s, axis=-1, keepdims=True), sinks[g])
            p = jnp.exp2(s - m)
            o = jnp.dot(p.astype(vd.dtype), vd, preferred_element_type=jnp.float32)
            sink_term = jnp.exp2(sinks[g] - m)
            for ci, c in enumerate((2 * g, 2 * g + 1)):
                o_lo = o[(2 * ci) * WINDOW:(2 * ci + 1) * WINDOW]
                o_hi = o[(2 * ci + 1) * WINDOW:(2 * ci + 2) * WINDOW]
                e_lo = sink_term[(2 * ci) * WINDOW:(2 * ci + 1) * WINDOW]
                e_hi = sink_term[(2 * ci + 1) * WINDOW:(2 * ci + 2) * WINDOW]
                num = jnp.where(qlane_low, o_lo, pltpu.roll(o_hi, HEAD_DIM, 1))
                den = jnp.where(qlane_low, pltpu.roll(o_lo, HEAD_DIM, 1) + e_lo, o_hi + e_hi)
                o_ref[pl.ds(r0, WINDOW), c * LANES:(c + 1) * LANES] = (num / den).astype(o_ref.dtype)

    finish(0, scores(0, band & (c_id >= WINDOW)))

    def body(it, carry):
        n0 = 1 + it * SWA_UNROLL
        scored = [scores(n0 + u, band) for u in range(SWA_UNROLL)]
        for u in range(SWA_UNROLL):
            finish(n0 + u, scored[u])
        return carry

    lax.fori_loop(0, (seq // WINDOW - 1) // SWA_UNROLL, body, 0)


def _swa(proj, sinks, batch, seq):
    grid_spec = pltpu.PrefetchScalarGridSpec(
        num_scalar_prefetch=0,
        grid=(batch,),
        in_specs=[
            pl.BlockSpec(memory_space=pltpu.SMEM),
            pl.BlockSpec((seq, W_Q_SWA), lambda b: (b, 0)),
            pl.BlockSpec((seq, LANES), lambda b: (b, COL_K_SWA)),
            pl.BlockSpec((seq, LANES), lambda b: (b, COL_V_SWA)),
        ],
        out_specs=pl.BlockSpec((seq, W_Q_SWA), lambda b: (b, 0)),
        scratch_shapes=[pltpu.VMEM((N_KV_SWA, WINDOW + seq, LANES), jnp.bfloat16),
                        pltpu.VMEM((N_KV_SWA, WINDOW + seq, LANES), jnp.bfloat16)],
    )
    return pl.pallas_call(
        functools.partial(_swa_kernel, seq=seq),
        grid_spec=grid_spec,
        out_shape=jax.ShapeDtypeStruct((batch * seq, W_Q_SWA), jnp.bfloat16),
        compiler_params=_params("parallel"),
        name="swa_sink_attention",
    )(sinks, proj, proj, proj)


def _moba_kernel(q_ref, k_ref, v_ref, o_ref, qa_ref, ka_ref, va_ref, *, seq):
    nblk = seq // MOBA_BLOCK
    pad_rows = 16
    q_all, k_all, v_all = q_ref[...], k_ref[...], v_ref[...]
    kmean = jnp.sum(k_all.astype(jnp.float32).reshape(nblk, MOBA_BLOCK, LANES), axis=1) / MOBA_BLOCK
    kmean = jnp.concatenate([kmean, jnp.zeros((pad_rows - nblk, LANES), jnp.float32)], axis=0)
    klane_low = lax.broadcasted_iota(jnp.int32, (pad_rows, LANES), 1) < HEAD_DIM
    lane = lax.broadcasted_iota(jnp.int32, (seq, LANES), 1)
    low = lane < HEAD_DIM
    key_blk = lax.broadcasted_iota(jnp.int32, (seq, LANES), 0) // MOBA_BLOCK
    j_id = lax.broadcasted_iota(jnp.int32, (pad_rows, seq), 0)
    q_blk = lax.broadcasted_iota(jnp.int32, (pad_rows, seq), 1) // MOBA_BLOCK
    eligible = j_id < q_blk

    for half in range(2):
        own = low if half == 0 else ~low
        spare = HEAD_DIM if half == 0 else 0
        ka_ref[half] = jnp.where(own, k_all, (lane - spare == key_blk).astype(k_all.dtype))
        va_ref[half] = jnp.where(own, v_all, jnp.ones_like(v_all))
        km = jnp.where(klane_low if half == 0 else ~klane_low, kmean, 0.0).astype(jnp.bfloat16)
        gate = lax.dot_general(km, q_all, (((1,), (1,)), ((), ())),
                               preferred_element_type=jnp.float32)
        gate = jnp.where(eligible, gate, -jnp.inf)
        beaten = jnp.zeros((pad_rows, seq), jnp.int32)
        for jp in range(nblk):
            row = gate[jp:jp + 1, :]
            wins = (row > gate) | ((row == gate) & (jp < j_id))
            beaten = beaten + wins.astype(jnp.int32)
        dropped = eligible & (beaten >= MOBA_TOPK)
        bias = jnp.where(dropped, NEG_BIG, 0.0)
        pieces = [bias, jnp.zeros((LANES - spare - pad_rows, seq), jnp.float32)]
        if spare:
            pieces.insert(0, jnp.zeros((spare, seq), jnp.float32))
        bias_t = jnp.concatenate(pieces, axis=0).T
        qa_ref[half] = jnp.where(own, q_all, bias_t.astype(q_all.dtype))

    qlane_low = lax.broadcasted_iota(jnp.int32, (MOBA_BLOCK, LANES), 1) < HEAD_DIM
    rr = lax.broadcasted_iota(jnp.int32, (MOBA_BLOCK, MOBA_BLOCK), 0)
    cc = lax.broadcasted_iota(jnp.int32, (MOBA_BLOCK, MOBA_BLOCK), 1)
    causal = cc <= rr

    def scores(i, half):
        r0 = i * MOBA_BLOCK
        n_keys = r0 + MOBA_BLOCK
        s = lax.dot_general(qa_ref[half, r0:n_keys, :], ka_ref[half, 0:n_keys, :], (((1,), (1,)), ((), ())),
                            preferred_element_type=jnp.float32)
        own_blk = jnp.where(causal, s[:, r0:n_keys], NEG_BIG)
        return jnp.concatenate([s[:, :r0], own_blk], axis=1) if i else own_blk

    units = [(i, half) for i in range(nblk) for half in range(2)]
    pending = [scores(*u) for u in units[:MOBA_LOOKAHEAD]]
    acc = []
    for n, (i, half) in enumerate(units):
        s = pending.pop(0)
        if n + MOBA_LOOKAHEAD < len(units):
            pending.append(scores(*units[n + MOBA_LOOKAHEAD]))
        n_keys = (i + 1) * MOBA_BLOCK
        m = jnp.max(s, axis=-1, keepdims=True)
        p = jnp.exp2(s - m).astype(jnp.bfloat16)
        acc.append(jnp.dot(p, va_ref[half, 0:n_keys, :], preferred_element_type=jnp.float32))
        if half == 1:
            num = jnp.where(qlane_low, acc[0], acc[1])
            den = pltpu.roll(jnp.where(qlane_low, acc[1], acc[0]), HEAD_DIM, 1)
            o_ref[i * MOBA_BLOCK:n_keys, :] = (num / den).astype(o_ref.dtype)
            acc = []


def _moba(proj, batch, seq):
    pairs = W_MOBA // LANES
    return pl.pallas_call(
        functools.partial(_moba_kernel, seq=seq),
        grid=(batch, pairs),
        in_specs=[
            pl.BlockSpec((seq, LANES), lambda b, p: (b, COL_Q_MOBA + p)),
            pl.BlockSpec((seq, LANES), lambda b, p: (b, COL_K_MOBA + p)),
            pl.BlockSpec((seq, LANES), lambda b, p: (b, COL_V_MOBA + p)),
        ],
        out_specs=pl.BlockSpec((seq, LANES), lambda b, p: (b, p)),
        out_shape=jax.ShapeDtypeStruct((batch * seq, W_MOBA), jnp.bfloat16),
        scratch_shapes=[pltpu.VMEM((2, seq, LANES), jnp.bfloat16)] * 3,
        compiler_params=_params("parallel", "parallel"),
        name="moba_attention",
    )(proj, proj, proj)


def _layer_norm(h, g, b):
    mu = jnp.mean(h, axis=-1, keepdims=True)
    d = h - mu
    var = jnp.mean(d * d, axis=-1, keepdims=True)
    return d * lax.rsqrt(var + LN_EPS) * g + b


def _outproj_kernel(oa_ref, ob_ref, wa_ref, wb_ref, bo_ref, x_ref, g_ref, b_ref, wr_ref, br_ref, tri_ref,
                    x1_ref, x1p_ref, idx_ref, gate_ref, rank_ref, count_ref, running_ref, *, group_starts):
    first_of_group = pl.program_id(0) == group_starts[0]
    for t in group_starts[1:]:
        first_of_group = first_of_group | (pl.program_id(0) == t)

    @pl.when(first_of_group)
    def _():
        running_ref[...] = jnp.zeros(running_ref.shape, running_ref.dtype)

    half_rows = ROW_TILE // 2
    halves = [slice(h * half_rows, (h + 1) * half_rows) for h in range(2)]

    def project(rows):
        mix = jnp.dot(oa_ref[rows, :], wa_ref[...], preferred_element_type=jnp.float32)
        mix = mix + jnp.dot(ob_ref[rows, :], wb_ref[...], preferred_element_type=jnp.float32)
        return mix + bo_ref[...]

    def normalise(rows, mix):
        x1 = _layer_norm(DEEPNORM_ALPHA * x_ref[rows, :] + mix, g_ref[...], b_ref[...])
        x1_ref[rows, :] = x1
        x1p_ref[rows, :] = _pack_bf16_pairs(x1)
        return jnp.dot(x1.astype(jnp.bfloat16), wr_ref[...], preferred_element_type=jnp.float32) + br_ref[...]

    mixes = [project(rows) for rows in halves]
    logits = jnp.concatenate([normalise(rows, mix) for rows, mix in zip(halves, mixes)], axis=0)
    lane = lax.broadcasted_iota(jnp.int32, logits.shape, 1)
    idx_out = jnp.zeros(logits.shape, jnp.int32)
    val_out = jnp.zeros(logits.shape, jnp.float32)
    top = None
    total = None
    onehots = []
    for k in range(TOP_K):
        m = jnp.max(logits, axis=-1, keepdims=True)
        idx = jnp.min(jnp.where(logits == m, lane, LANES), axis=-1, keepdims=True)
        picked = lane == idx
        onehots.append(picked)
        logits = jnp.where(picked, -jnp.inf, logits)
        if k == 0:
            top = m
        e = jnp.exp(m - top)
        total = e if k == 0 else total + e
        idx_out = jnp.where(lane == k, idx, idx_out)
        val_out = jnp.where(lane == k, e, val_out)
    idx_ref[...] = idx_out[:, :TOP_K]
    gate_ref[...] = (val_out / total)[:, :TOP_K]

    picks = jnp.concatenate([p.astype(jnp.bfloat16) for p in onehots], axis=1)
    before = jnp.dot(tri_ref[...], picks, preferred_element_type=jnp.float32)
    base = running_ref[...]
    rank_out = jnp.zeros(logits.shape, jnp.float32)
    for k in range(TOP_K):
        pk = onehots[k].astype(jnp.float32)
        here = before[:, k * LANES:(k + 1) * LANES] + base
        rank_k = jnp.sum(pk * here, axis=-1, keepdims=True)
        rank_out = jnp.where(lane == k, rank_k, rank_out)
        base = base + jnp.sum(pk, axis=0, keepdims=True)
    running_ref[...] = base
    rank_ref[...] = rank_out[:, :TOP_K].astype(jnp.int32)
    count_ref[0] = base.astype(jnp.int32)


def _outproj(o_a, o_b, w_a, w_b, b_out, x2, g, b, w_r, b_r):
    n_tok = x2.shape[0]
    group_starts = tuple(t0 // ROW_TILE for t0, _ in _moe_groups(n_tok))
    row = lambda i: (i, 0)
    fixed = lambda i: (0, 0)
    r = lax.broadcasted_iota(jnp.int32, (ROW_TILE, ROW_TILE), 0)
    c = lax.broadcasted_iota(jnp.int32, (ROW_TILE, ROW_TILE), 1)
    tri = (c < r).astype(jnp.bfloat16)
    return pl.pallas_call(
        functools.partial(_outproj_kernel, group_starts=group_starts),
        grid=(n_tok // ROW_TILE,),
        in_specs=[
            pl.BlockSpec((ROW_TILE, W_Q_SWA), row),
            pl.BlockSpec((ROW_TILE, W_MOBA), row),
            pl.BlockSpec((W_Q_SWA, D_MODEL), fixed),
            pl.BlockSpec((W_MOBA, D_MODEL), fixed),
            pl.BlockSpec((1, D_MODEL), fixed),
            pl.BlockSpec((ROW_TILE, D_MODEL), row),
            pl.BlockSpec((1, D_MODEL), fixed),
            pl.BlockSpec((1, D_MODEL), fixed),
            pl.BlockSpec((D_MODEL, LANES), fixed),
            pl.BlockSpec((1, LANES), fixed),
            pl.BlockSpec((ROW_TILE, ROW_TILE), fixed),
        ],
        out_specs=[
            pl.BlockSpec((ROW_TILE, D_MODEL), row),
            pl.BlockSpec((ROW_TILE, PACKED), row),
            pl.BlockSpec((ROW_TILE, TOP_K), row),
            pl.BlockSpec((ROW_TILE, TOP_K), row),
            pl.BlockSpec((ROW_TILE, TOP_K), row),
            pl.BlockSpec((1, 1, LANES), lambda i: (sum((i >= t).astype(jnp.int32) for t in group_starts[1:]), 0, 0)),
        ],
        out_shape=[
            jax.ShapeDtypeStruct((n_tok, D_MODEL), jnp.float32),
            jax.ShapeDtypeStruct((n_tok, PACKED), jnp.int32),
            jax.ShapeDtypeStruct((n_tok, TOP_K), jnp.int32),
            jax.ShapeDtypeStruct((n_tok, TOP_K), jnp.float32),
            jax.ShapeDtypeStruct((n_tok, TOP_K), jnp.int32),
            jax.ShapeDtypeStruct((len(group_starts), 1, LANES), jnp.int32),
        ],
        scratch_shapes=[pltpu.VMEM((1, LANES), jnp.float32)],
        compiler_params=_params("arbitrary"),
        name="outproj_ln_router",
    )(o_a, o_b, w_a, w_b, b_out, x2, g, b, w_r, b_r, tri)


def _sc_worker_id():
    return lax.axis_index("s") * SC_CORES + lax.axis_index("c")


def _sc_scatter_rows(rows, pos3, n_out, row0):
    n_tok = pos3.shape[0] * SC_ROWS
    steps = n_tok // SC_ROWS // SC_WORKERS
    assert steps * SC_ROWS * SC_WORKERS == n_tok and steps % 2 == 0
    mesh = plsc.VectorSubcoreMesh(core_axis_name="c", subcore_axis_name="s")

    @functools.partial(
        pl.kernel, mesh=mesh,
        out_type=jax.ShapeDtypeStruct((n_out, PACKED), jnp.int32),
        scratch_types=[pltpu.VMEM((2, TOP_K, SC_ROWS), jnp.int32), pltpu.VMEM((2, SC_ROWS, PACKED), jnp.int32),
                       pltpu.SemaphoreType.DMA((2,)), pltpu.SemaphoreType.DMA((2,))],
        name="sc_dispatch_scatter")
    def scatter(x_hbm, pos_hbm, out_hbm, idx_v, rows_v, sem_ld, sem_st):
        base = _sc_worker_id() * steps

        def loads(s, b):
            return (pltpu.make_async_copy(pos_hbm.at[base + s], idx_v.at[b], sem_ld.at[b]),
                    pltpu.make_async_copy(x_hbm.at[pl.ds(row0 + (base + s) * SC_ROWS, SC_ROWS)], rows_v.at[b],
                                          sem_ld.at[b]))

        def stores(b):
            return [pltpu.make_async_copy(rows_v.at[b], out_hbm.at[idx_v.at[b, k]], sem_st.at[b])
                    for k in range(TOP_K)]

        for c in loads(0, 0):
            c.start()

        @pl.loop(0, steps, step=2)
        def _(s0):
            for b in range(2):
                s = s0 + b
                for c in loads(s, b):
                    c.wait()

                @pl.when(s >= 1)
                def _():
                    for c in stores(1 - b):
                        c.wait()

                @pl.when(s + 1 < steps)
                def _():
                    for c in loads(s + 1, 1 - b):
                        c.start()

                for c in stores(b):
                    c.start()

        for c in stores(1):
            c.wait()

    return scatter(rows, pos3)


def _sc_gather_rows(table, idx2):
    n_blk = idx2.shape[0]
    steps = n_blk // SC_WORKERS
    assert steps * SC_WORKERS == n_blk and steps % 2 == 0 and idx2.shape[1] == SC_ROWS
    mesh = plsc.VectorSubcoreMesh(core_axis_name="c", subcore_axis_name="s")

    @functools.partial(
        pl.kernel, mesh=mesh,
        out_type=jax.ShapeDtypeStruct((n_blk * SC_ROWS, PACKED), jnp.int32),
        scratch_types=[pltpu.VMEM((steps, SC_ROWS), jnp.int32), pltpu.VMEM((2, SC_ROWS, PACKED), jnp.int32),
                       pltpu.SemaphoreType.DMA((2,)), pltpu.SemaphoreType.DMA((2,))],
        name="sc_combine_gather")
    def gather(y_hbm, idx_hbm, out_hbm, idx_v, rows_v, sem_ld, sem_st):
        base = _sc_worker_id() * steps
        pltpu.sync_copy(idx_hbm.at[pl.ds(base, steps)], idx_v)

        def fetch(s, b):
            return pltpu.make_async_copy(y_hbm.at[idx_v.at[s]], rows_v.at[b], sem_ld.at[b])

        def store(s, b):
            return pltpu.make_async_copy(rows_v.at[b], out_hbm.at[pl.ds((base + s) * SC_ROWS, SC_ROWS)],
                                         sem_st.at[b])

        fetch(0, 0).start()

        @pl.loop(0, steps, step=2)
        def _(s0):
            for b in range(2):
                s = s0 + b
                fetch(s, b).wait()

                @pl.when(s >= 1)
                def _():
                    store(s - 1, 1 - b).wait()

                @pl.when(s + 1 < steps)
                def _():
                    fetch(s + 1, 1 - b).start()

                store(s, b).start()

        store(steps - 1, 1).wait()

    return gather(table, idx2)


def _deinterleave_kernel(w_ref, perm_ref, g_ref, l_ref):
    for c in range(w_ref.shape[2] // 256):
        t = w_ref[0, :, c * 256:(c + 1) * 256].astype(jnp.bfloat16)
        r = jnp.dot(t, perm_ref[...], preferred_element_type=jnp.float32)
        g_ref[0, :, c * LANES:(c + 1) * LANES] = r[:, :LANES].astype(g_ref.dtype)
        l_ref[0, :, c * LANES:(c + 1) * LANES] = r[:, LANES:].astype(l_ref.dtype)


def _deinterleave_w1(w1):
    n_exp, d, two_f = w1.shape
    rows = 512
    i = lax.broadcasted_iota(jnp.int32, (256, 256), 0)
    o = lax.broadcasted_iota(jnp.int32, (256, 256), 1)
    perm = (i == jnp.where(o < LANES, 2 * o, 2 * (o - LANES) + 1)).astype(jnp.bfloat16)
    return pl.pallas_call(
        _deinterleave_kernel,
        grid=(n_exp, d // rows),
        in_specs=[pl.BlockSpec((1, rows, two_f), lambda e, r: (e, r, 0)),
                  pl.BlockSpec((256, 256), lambda e, r: (0, 0))],
        out_specs=[pl.BlockSpec((1, rows, two_f // 2), lambda e, r: (e, r, 0)),
                   pl.BlockSpec((1, rows, two_f // 2), lambda e, r: (e, r, 0))],
        out_shape=[jax.ShapeDtypeStruct((n_exp, d, two_f // 2), jnp.bfloat16),
                   jax.ShapeDtypeStruct((n_exp, d, two_f // 2), jnp.bfloat16)],
        compiler_params=_params("parallel", "parallel"),
        name="w1_deinterleave",
    )(w1, perm)


def _expert_kernel(be_ref, br_ref, nv_ref, x_ref, wg_ref, wl_ref, bg_ref, bl_ref, w2_ref, b2_ref, y_ref):
    i = pl.program_id(0)

    @pl.when(i < nv_ref[0])
    def _():
        valid = lax.broadcasted_iota(jnp.int32, x_ref.shape, 0) < br_ref[i]
        lo, hi = _unpack_bf16_pairs(jnp.where(valid, x_ref[...], 0))
        xb = jnp.concatenate([lo.astype(jnp.bfloat16), hi.astype(jnp.bfloat16)], axis=1)
        hg = jnp.dot(xb, wg_ref[0], preferred_element_type=jnp.float32) + bg_ref[0]
        hl = jnp.dot(xb, wl_ref[0], preferred_element_type=jnp.float32) + bl_ref[0]
        glu = jnp.minimum(hg, SWIGLU_LIMIT)
        lin = jnp.clip(hl, -SWIGLU_LIMIT, SWIGLU_LIMIT)
        act = glu * jax.nn.sigmoid(SWIGLU_ALPHA * glu) * (lin + 1.0)
        y = jnp.dot(act.astype(jnp.bfloat16), w2_ref[0], preferred_element_type=jnp.float32) + b2_ref[0]
        y_ref[...] = _pack_bf16_pairs(y)

    @pl.when(i >= nv_ref[0])
    def _():
        y_ref[...] = jnp.zeros(y_ref.shape, y_ref.dtype)


def _experts(block_expert, block_rows, n_valid, xg, w1g, w1l, b1g, b1l, w2, b2):
    n_rows = xg.shape[0]
    n_blocks = n_rows // EXPERT_ROWS

    def row(i, be, br, nv):
        return (jnp.minimum(i, nv[0] - 1), 0)

    def per_expert(i, be, br, nv):
        return (be[i], 0, 0)

    grid_spec = pltpu.PrefetchScalarGridSpec(
        num_scalar_prefetch=3,
        grid=(n_blocks,),
        in_specs=[
            pl.BlockSpec((EXPERT_ROWS, PACKED), row),
            pl.BlockSpec((1, D_MODEL, D_FF), per_expert),
            pl.BlockSpec((1, D_MODEL, D_FF), per_expert),
            pl.BlockSpec((1, 1, D_FF), per_expert),
            pl.BlockSpec((1, 1, D_FF), per_expert),
            pl.BlockSpec((1, D_FF, D_MODEL), per_expert),
            pl.BlockSpec((1, 1, D_MODEL), per_expert),
        ],
        out_specs=pl.BlockSpec((EXPERT_ROWS, PACKED), lambda i, be, br, nv: (i, 0)),
    )
    return pl.pallas_call(
        _expert_kernel,
        grid_spec=grid_spec,
        out_shape=jax.ShapeDtypeStruct((n_rows, PACKED), jnp.int32),
        compiler_params=_params("arbitrary"),
        name="grouped_experts",
    )(block_expert, block_rows, n_valid, xg, w1g, w1l, b1g, b1l, w2, b2)


def _combine_kernel(y_ref, gate_ref, x1_ref, g_ref, b_ref, *rest):
    o_ref = rest[-1]
    gates = gate_ref[...]
    lo_sum = None
    hi_sum = None
    for k in range(TOP_K):
        lo, hi = _unpack_bf16_pairs(y_ref[k])
        gk = gates[:, k:k + 1]
        lo_sum = gk * lo if k == 0 else lo_sum + gk * lo
        hi_sum = gk * hi if k == 0 else hi_sum + gk * hi
    moe = jnp.concatenate([lo_sum, hi_sum], axis=1)
    o_ref[...] = _layer_norm(DEEPNORM_ALPHA * x1_ref[...] + moe, g_ref[...], b_ref[...])


def _combine(yg, gates, x1, g, b, tile0, prev):
    n_tok = x1.shape[0]
    row = lambda i: (i + tile0, 0)
    fixed = lambda i: (0, 0)
    in_specs = [
        pl.BlockSpec((TOP_K, ROW_TILE, PACKED), lambda i: (0, i, 0)),
        pl.BlockSpec((ROW_TILE, TOP_K), row),
        pl.BlockSpec((ROW_TILE, D_MODEL), row),
        pl.BlockSpec((1, D_MODEL), fixed),
        pl.BlockSpec((1, D_MODEL), fixed),
    ]
    args = [yg, gates, x1, g, b]
    aliases = {}
    if prev is not None:
        in_specs.append(pl.BlockSpec(memory_space=pl.ANY))
        args.append(prev)
        aliases = {len(args) - 1: 0}
    return pl.pallas_call(
        _combine_kernel,
        grid=(yg.shape[1] // ROW_TILE,),
        in_specs=in_specs,
        out_specs=pl.BlockSpec((ROW_TILE, D_MODEL), row),
        out_shape=jax.ShapeDtypeStruct((n_tok, D_MODEL), jnp.float32),
        input_output_aliases=aliases,
        compiler_params=_params("parallel"),
        name="combine_ln",
    )(*args)


def _route(top_idx, rank, counts, n_blocks):
    counts = counts[0, :N_EXPERTS]
    blocks_per = (counts + EXPERT_ROWS - 1) // EXPERT_ROWS
    blk_end = jnp.cumsum(blocks_per)
    blk_start = blk_end - blocks_per
    pos = (blk_start * EXPERT_ROWS)[top_idx] + rank
    blk = jnp.arange(n_blocks, dtype=jnp.int32)
    block_expert = jnp.minimum(jnp.sum((blk[:, None] >= blk_end[None, :]).astype(jnp.int32), axis=1),
                               N_EXPERTS - 1)
    inside = blk - blk_start[block_expert]
    block_rows = jnp.clip(counts[block_expert] - inside * EXPERT_ROWS, 0, EXPERT_ROWS).astype(jnp.int32)
    n_valid = blk_end[-1:].astype(jnp.int32)
    return pos, block_expert.astype(jnp.int32), block_rows, n_valid


def _moe_groups(n_tok):
    unit = 2 * SC_ROWS * SC_WORKERS
    parts = sum(MOE_SPLIT)
    if n_tok % (parts * unit):
        return ((0, n_tok),)
    groups, t0 = [], 0
    for share in MOE_SPLIT:
        groups.append((t0, n_tok * share // parts))
        t0 += groups[-1][1]
    return tuple(groups)


def _rope_tables(seq):
    inv_freq = 1.0 / (ROPE_THETA ** (jnp.arange(0, HEAD_DIM, 2, dtype=jnp.float32) / HEAD_DIM))
    ang = jnp.arange(seq, dtype=jnp.float32)[:, None] * inv_freq[None, :]
    cos, sin = jnp.cos(ang), jnp.sin(ang)
    cos_t = jnp.tile(jnp.concatenate([cos, cos], axis=1), (1, 256 // HEAD_DIM))
    sin_t = jnp.tile(jnp.concatenate([-sin, sin], axis=1), (1, 256 // HEAD_DIM))
    return cos_t, sin_t


def kernel(x, w_in, b_in, sinks, w_out, b_out, ln1_g, ln1_b, w_router, b_router, w1, b1, w2, b2, ln2_g, ln2_b):
    batch, seq, d = x.shape
    assert d == D_MODEL and seq % ROW_TILE == 0 and seq % MOBA_BLOCK == 0 and w_in.shape[0] == DEPTH == 1
    assert (seq // WINDOW - 1) % SWA_UNROLL == 0
    n_tok = batch * seq
    bf16 = jnp.bfloat16
    x2 = x.reshape(n_tok, d)
    cos_t, sin_t = _rope_tables(seq)

    proj = _inproj(x2, w_in[0].astype(bf16), b_in[0].reshape(1, IN_WIDTH), cos_t, sin_t, seq)
    o_a = _swa(proj, sinks[0], batch, seq)
    o_b = _moba(proj, batch, seq)

    w_o = w_out[0].astype(bf16)
    w_r = jnp.pad(w_router[0], ((0, 0), (0, LANES - N_EXPERTS))).astype(bf16)
    b_r = jnp.pad(b_router[0], (0, LANES - N_EXPERTS), constant_values=NEG_BIG).reshape(1, LANES)
    x1, x1p, top_idx, gates, rank, counts = _outproj(
        o_a, o_b, w_o[:W_Q_SWA], w_o[W_Q_SWA:], b_out[0].reshape(1, d), x2,
        ln1_g[0].reshape(1, d), ln1_b[0].reshape(1, d), w_r, b_r)

    w1g, w1l = _deinterleave_w1(w1[0])
    b1r = b1[0].reshape(N_EXPERTS, 1, D_FF, 2)
    b1g, b1l = b1r[..., 0], b1r[..., 1]
    w2b = w2[0].astype(bf16)
    b2r = b2[0].reshape(N_EXPERTS, 1, d)
    g2, be2 = ln2_g[0].reshape(1, d), ln2_b[0].reshape(1, d)

    out = None
    for grp, (t0, tok_g) in enumerate(_moe_groups(n_tok)):
        n_blocks = tok_g * TOP_K // EXPERT_ROWS + N_EXPERTS
        pos, block_expert, block_rows, n_valid = _route(
            top_idx[t0:t0 + tok_g], rank[t0:t0 + tok_g], counts[grp], n_blocks)
        pos3 = pos.reshape(tok_g // SC_ROWS, SC_ROWS, TOP_K).transpose(0, 2, 1)
        xg = _sc_scatter_rows(x1p, pos3, n_blocks * EXPERT_ROWS, t0)
        y = _experts(block_expert, block_rows, n_valid, xg, w1g, w1l, b1g, b1l, w2b, b2r)
        yg = _sc_gather_rows(y, pos.T.reshape(tok_g * TOP_K // SC_ROWS, SC_ROWS))
        out = _combine(yg.reshape(TOP_K, tok_g, PACKED), gates, x1, g2, be2, t0 // ROW_TILE, out)
    return out.reshape(batch, seq, d)
```

```python
import functools

import jax
import jax.numpy as jnp
from jax import lax
from jax.experimental import pallas as pl
from jax.experimental.pallas import tpu as pltpu
from jax.experimental.pallas import tpu_sc as plsc

D_MODEL = 1024
HEAD_DIM = 64
N_HEADS_SWA = 8
N_KV_SWA = 2
WINDOW = 128
N_HEADS_MOBA = 8
MOBA_BLOCK = 256
MOBA_TOPK = 3
ROPE_THETA = 10000.0
N_EXPERTS = 32
TOP_K = 4
D_FF = 1024
SWIGLU_LIMIT = 7.0
SWIGLU_ALPHA = 1.702
LN_EPS = 1e-5
DEPTH = 1
DEEPNORM_ALPHA = (2 * DEPTH) ** 0.25

W_Q_SWA = N_HEADS_SWA * HEAD_DIM
W_KV_SWA = N_KV_SWA * HEAD_DIM
W_MOBA = N_HEADS_MOBA * HEAD_DIM
IN_WIDTH = W_Q_SWA + 2 * W_KV_SWA + 3 * W_MOBA
LANES = 128
COL_K_SWA = W_Q_SWA // LANES
COL_V_SWA = COL_K_SWA + 1
COL_Q_MOBA = COL_V_SWA + 1
COL_K_MOBA = COL_Q_MOBA + W_MOBA // LANES
COL_V_MOBA = COL_K_MOBA + W_MOBA // LANES

ROW_TILE = 512
EXPERT_ROWS = 512
PACKED = D_MODEL // 2
NEG_BIG = -1e30
LOG2E = 1.4426950408889634
MOBA_LOOKAHEAD = 1
SWA_UNROLL = 3
MOE_SPLIT = (3, 1)
VMEM_LIMIT = 48 * 1024 * 1024
EXPERT_VMEM_LIMIT = 60 * 1024 * 1024

SC_CORES = 2
SC_SUBCORES = 16
SC_WORKERS = SC_CORES * SC_SUBCORES
SC_ROWS = 64

_PROJ_CHUNKS = (
    (0, 256, 256, True), (256, 256, 256, True),
    (512, 256, 128, False),
    (768, 256, 256, True), (1024, 256, 256, True),
    (1280, 256, 256, False), (1536, 256, 256, False),
    (1792, 256, 0, False), (2048, 256, 0, False),
)


def _params(*sem):
    return pltpu.CompilerParams(dimension_semantics=sem, vmem_limit_bytes=VMEM_LIMIT)


def _pack_bf16_pairs(v):
    n = v.shape[1] // 2
    bits = lax.bitcast_convert_type(v.astype(jnp.bfloat16).astype(jnp.float32), jnp.uint32)
    word = (bits[:, :n] >> 16) | (bits[:, n:] & jnp.uint32(0xFFFF0000))
    return lax.bitcast_convert_type(word, jnp.int32)


def _unpack_bf16_pairs(word):
    bits = lax.bitcast_convert_type(word, jnp.uint32)
    lo = lax.bitcast_convert_type(bits << 16, jnp.float32)
    hi = lax.bitcast_convert_type(bits & jnp.uint32(0xFFFF0000), jnp.float32)
    return lo, hi


def _inproj_kernel(x_ref, w_ref, b_ref, cos_ref, sin_ref, o_ref):
    xb = x_ref[...].astype(jnp.bfloat16)
    for start, width, rope, scaled in _PROJ_CHUNKS:
        t = jnp.dot(xb, w_ref[:, start:start + width], preferred_element_type=jnp.float32)
        t = t + b_ref[:, start:start + width]
        if rope:
            lane = lax.broadcasted_iota(jnp.int32, t.shape, 1)
            first_half = (lane % HEAD_DIM) < (HEAD_DIM // 2)
            rot = jnp.where(first_half,
                            pltpu.roll(t, width - HEAD_DIM // 2, 1),
                            pltpu.roll(t, HEAD_DIM // 2, 1))
            roped = t * cos_ref[:, :width] + rot * sin_ref[:, :width]
            t = roped if rope == width else jnp.where(lane < rope, roped, t)
        if scaled:
            t = t * (HEAD_DIM ** -0.5 * LOG2E)
        o_ref[:, start:start + width] = t.astype(o_ref.dtype)


def _inproj(x2, w_in, b_in, cos_t, sin_t, seq):
    n_tok = x2.shape[0]
    per_seq = seq // ROW_TILE
    return pl.pallas_call(
        _inproj_kernel,
        grid=(n_tok // ROW_TILE,),
        in_specs=[
            pl.BlockSpec((ROW_TILE, D_MODEL), lambda i: (i, 0)),
            pl.BlockSpec((D_MODEL, IN_WIDTH), lambda i: (0, 0)),
            pl.BlockSpec((1, IN_WIDTH), lambda i: (0, 0)),
            pl.BlockSpec((ROW_TILE, 256), lambda i: (i % per_seq, 0)),
            pl.BlockSpec((ROW_TILE, 256), lambda i: (i % per_seq, 0)),
        ],
        out_specs=pl.BlockSpec((ROW_TILE, IN_WIDTH), lambda i: (i, 0)),
        out_shape=jax.ShapeDtypeStruct((n_tok, IN_WIDTH), jnp.bfloat16),
        compiler_params=_params("parallel"),
        name="inproj_rope",
    )(x2, w_in, b_in, cos_t, sin_t)


def _swa_kernel(sink_ref, q_ref, k_ref, v_ref, o_ref, kd_ref, vd_ref, *, seq):
    lane = lax.broadcasted_iota(jnp.int32, (seq, LANES), 1)
    low = lane < HEAD_DIM
    k = k_ref[...].astype(jnp.float32)
    kr = pltpu.roll(k, HEAD_DIM, 1)
    kd_ref[0, WINDOW:, :] = jnp.where(low, k, kr).astype(kd_ref.dtype)
    kd_ref[1, WINDOW:, :] = jnp.where(low, kr, k).astype(kd_ref.dtype)
    v = v_ref[...].astype(jnp.float32)
    vd_ref[0, WINDOW:, :] = jnp.where(low, v, 1.0).astype(vd_ref.dtype)
    vd_ref[1, WINDOW:, :] = jnp.where(low, pltpu.roll(v, HEAD_DIM, 1), 1.0).astype(vd_ref.dtype)
    kd_ref[:, :WINDOW, :] = jnp.zeros((N_KV_SWA, WINDOW, LANES), kd_ref.dtype)
    vd_ref[:, :WINDOW, :] = jnp.zeros((N_KV_SWA, WINDOW, LANES), vd_ref.dtype)

    group = N_HEADS_SWA // N_KV_SWA
    rows = group * WINDOW
    r_in = lax.broadcasted_iota(jnp.int32, (rows, 2 * WINDOW), 0) % WINDOW
    c_id = lax.broadcasted_iota(jnp.int32, (rows, 2 * WINDOW), 1)
    band = (c_id > r_in) & (c_id <= r_in + WINDOW)
    head_in_group = lax.broadcasted_iota(jnp.int32, (rows, 1), 0) // WINDOW
    qlane_low = lax.broadcasted_iota(jnp.int32, (WINDOW, LANES), 1) < HEAD_DIM
    sinks = []
    for g in range(N_KV_SWA):
        col = jnp.zeros((rows, 1), jnp.float32)
        for j in range(group):
            col = jnp.where(head_in_group == j, sink_ref[g * group + j] * LOG2E, col)
        sinks.append(col)

    def scores(n, mask):
        r0 = pl.multiple_of(n * WINDOW, WINDOW)
        out = []
        for g in range(N_KV_SWA):
            parts = []
            for c in (2 * g, 2 * g + 1):
                qc = q_ref[pl.ds(r0, WINDOW), c * LANES:(c + 1) * LANES]
                zero = jnp.zeros_like(qc)
                parts.append(jnp.where(qlane_low, qc, zero))
                parts.append(jnp.where(qlane_low, zero, qc))
            qcat = jnp.concatenate(parts, axis=0)
            kd = kd_ref[g, pl.ds(r0, 2 * WINDOW), :]
            s = lax.dot_general(qcat, kd, (((1,), (1,)), ((), ())),
                                preferred_element_type=jnp.float32)
            out.append(jnp.where(mask, s, -jnp.inf))
        return out

    def finish(n, scored):
        r0 = pl.multiple_of(n * WINDOW, WINDOW)
        for g, s in enumerate(scored):
            vd = vd_ref[g, pl.ds(r0, 2 * WINDOW), :]
            m = jnp.maximum(jnp.max(s, axis=-1, keepdims=True), sinks[g])
            p = jnp.exp2(s - m)
            o = jnp.dot(p.astype(vd.dtype), vd, preferred_element_type=jnp.float32)
            sink_term = jnp.exp2(sinks[g] - m)
            for ci, c in enumerate((2 * g, 2 * g + 1)):
                o_lo = o[(2 * ci) * WINDOW:(2 * ci + 1) * WINDOW]
                o_hi = o[(2 * ci + 1) * WINDOW:(2 * ci + 2) * WINDOW]
                e_lo = sink_term[(2 * ci) * WINDOW:(2 * ci + 1) * WINDOW]
                e_hi = sink_term[(2 * ci + 1) * WINDOW:(2 * ci + 2) * WINDOW]
                num = jnp.where(qlane_low, o_lo, pltpu.roll(o_hi, HEAD_DIM, 1))
                den = jnp.where(qlane_low, pltpu.roll(o_lo, HEAD_DIM, 1) + e_lo, o_hi + e_hi)
                o_ref[pl.ds(r0, WINDOW), c * LANES:(c + 1) * LANES] = (num / den).astype(o_ref.dtype)

    finish(0, scores(0, band & (c_id >= WINDOW)))

    def body(it, carry):
        n0 = 1 + it * SWA_UNROLL
        scored = [scores(n0 + u, band) for u in range(SWA_UNROLL)]
        for u in range(SWA_UNROLL):
            finish(n0 + u, scored[u])
        return carry

    lax.fori_loop(0, (seq // WINDOW - 1) // SWA_UNROLL, body, 0)


def _swa(proj, sinks, batch, seq):
    grid_spec = pltpu.PrefetchScalarGridSpec(
        num_scalar_prefetch=0,
        grid=(batch,),
        in_specs=[
            pl.BlockSpec(memory_space=pltpu.SMEM),
            pl.BlockSpec((seq, W_Q_SWA), lambda b: (b, 0)),
            pl.BlockSpec((seq, LANES), lambda b: (b, COL_K_SWA)),
            pl.BlockSpec((seq, LANES), lambda b: (b, COL_V_SWA)),
        ],
        out_specs=pl.BlockSpec((seq, W_Q_SWA), lambda b: (b, 0)),
        scratch_shapes=[pltpu.VMEM((N_KV_SWA, WINDOW + seq, LANES), jnp.bfloat16),
                        pltpu.VMEM((N_KV_SWA, WINDOW + seq, LANES), jnp.bfloat16)],
    )
    return pl.pallas_call(
        functools.partial(_swa_kernel, seq=seq),
        grid_spec=grid_spec,
        out_shape=jax.ShapeDtypeStruct((batch * seq, W_Q_SWA), jnp.bfloat16),
        compiler_params=_params("parallel"),
        name="swa_sink_attention",
    )(sinks, proj, proj, proj)


def _moba_kernel(q_ref, k_ref, v_ref, o_ref, qa_ref, ka_ref, va_ref, *, seq):
    nblk = seq // MOBA_BLOCK
    pad_rows = 16
    q_all, k_all, v_all = q_ref[...], k_ref[...], v_ref[...]
    kmean = jnp.sum(k_all.astype(jnp.float32).reshape(nblk, MOBA_BLOCK, LANES), axis=1) / MOBA_BLOCK
    kmean = jnp.concatenate([kmean, jnp.zeros((pad_rows - nblk, LANES), jnp.float32)], axis=0)
    klane_low = lax.broadcasted_iota(jnp.int32, (pad_rows, LANES), 1) < HEAD_DIM
    lane = lax.broadcasted_iota(jnp.int32, (seq, LANES), 1)
    low = lane < HEAD_DIM
    key_blk = lax.broadcasted_iota(jnp.int32, (seq, LANES), 0) // MOBA_BLOCK
    j_id = lax.broadcasted_iota(jnp.int32, (pad_rows, seq), 0)
    q_blk = lax.broadcasted_iota(jnp.int32, (pad_rows, seq), 1) // MOBA_BLOCK
    eligible = j_id < q_blk

    for half in range(2):
        own = low if half == 0 else ~low
        spare = HEAD_DIM if half == 0 else 0
        ka_ref[half] = jnp.where(own, k_all, (lane - spare == key_blk).astype(k_all.dtype))
        va_ref[half] = jnp.where(own, v_all, jnp.ones_like(v_all))
        km = jnp.where(klane_low if half == 0 else ~klane_low, kmean, 0.0).astype(jnp.bfloat16)
        gate = lax.dot_general(km, q_all, (((1,), (1,)), ((), ())),
                               preferred_element_type=jnp.float32)
        gate = jnp.where(eligible, gate, -jnp.inf)
        beaten = jnp.zeros((pad_rows, seq), jnp.int32)
        for jp in range(nblk):
            row = gate[jp:jp + 1, :]
            wins = (row > gate) | ((row == gate) & (jp < j_id))
            beaten = beaten + wins.astype(jnp.int32)
        dropped = eligible & (beaten >= MOBA_TOPK)
        bias = jnp.where(dropped, NEG_BIG, 0.0)
        pieces = [bias, jnp.zeros((LANES - spare - pad_rows, seq), jnp.float32)]
        if spare:
            pieces.insert(0, jnp.zeros((spare, seq), jnp.float32))
        bias_t = jnp.concatenate(pieces, axis=0).T
        qa_ref[half] = jnp.where(own, q_all, bias_t.astype(q_all.dtype))

    qlane_low = lax.broadcasted_iota(jnp.int32, (MOBA_BLOCK, LANES), 1) < HEAD_DIM
    rr = lax.broadcasted_iota(jnp.int32, (MOBA_BLOCK, MOBA_BLOCK), 0)
    cc = lax.broadcasted_iota(jnp.int32, (MOBA_BLOCK, MOBA_BLOCK), 1)
    causal = cc <= rr

    def scores(i, half):
        r0 = i * MOBA_BLOCK
        n_keys = r0 + MOBA_BLOCK
        s = lax.dot_general(qa_ref[half, r0:n_keys, :], ka_ref[half, 0:n_keys, :], (((1,), (1,)), ((), ())),
                            preferred_element_type=jnp.float32)
        own_blk = jnp.where(causal, s[:, r0:n_keys], NEG_BIG)
        return jnp.concatenate([s[:, :r0], own_blk], axis=1) if i else own_blk

    units = [(i, half) for i in range(nblk) for half in range(2)]
    pending = [scores(*u) for u in units[:MOBA_LOOKAHEAD]]
    acc = []
    for n, (i, half) in enumerate(units):
        s = pending.pop(0)
        if n + MOBA_LOOKAHEAD < len(units):
            pending.append(scores(*units[n + MOBA_LOOKAHEAD]))
        n_keys = (i + 1) * MOBA_BLOCK
        m = jnp.max(s, axis=-1, keepdims=True)
        p = jnp.exp2(s - m).astype(jnp.bfloat16)
        acc.append(jnp.dot(p, va_ref[half, 0:n_keys, :], preferred_element_type=jnp.float32))
        if half == 1:
            num = jnp.where(qlane_low, acc[0], acc[1])
            den = pltpu.roll(jnp.where(qlane_low, acc[1], acc[0]), HEAD_DIM, 1)
            o_ref[i * MOBA_BLOCK:n_keys, :] = (num / den).astype(o_ref.dtype)
            acc = []


def _moba(proj, batch, seq):
    pairs = W_MOBA // LANES
    return pl.pallas_call(
        functools.partial(_moba_kernel, seq=seq),
        grid=(batch, pairs),
        in_specs=[
            pl.BlockSpec((seq, LANES), lambda b, p: (b, COL_Q_MOBA + p)),
            pl.BlockSpec((seq, LANES), lambda b, p: (b, COL_K_MOBA + p)),
            pl.BlockSpec((seq, LANES), lambda b, p: (b, COL_V_MOBA + p)),
        ],
        out_specs=pl.BlockSpec((seq, LANES), lambda b, p: (b, p)),
        out_shape=jax.ShapeDtypeStruct((batch * seq, W_MOBA), jnp.bfloat16),
        scratch_shapes=[pltpu.VMEM((2, seq, LANES), jnp.bfloat16)] * 3,
        compiler_params=_params("parallel", "parallel"),
        name="moba_attention",
    )(proj, proj, proj)


def _layer_norm(h, g, b):
    mu = jnp.mean(h, axis=-1, keepdims=True)
    d = h - mu
    var = jnp.mean(d * d, axis=-1, keepdims=True)
    return d * lax.rsqrt(var + LN_EPS) * g + b


def _outproj_kernel(oa_ref, ob_ref, wa_ref, wb_ref, bo_ref, x_ref, g_ref, b_ref, wr_ref, br_ref, tri_ref,
                    x1_ref, x1p_ref, idx_ref, gate_ref, rank_ref, count_ref, running_ref, *, group_starts):
    first_of_group = pl.program_id(0) == group_starts[0]
    for t in group_starts[1:]:
        first_of_group = first_of_group | (pl.program_id(0) == t)

    @pl.when(first_of_group)
    def _():
        running_ref[...] = jnp.zeros(running_ref.shape, running_ref.dtype)

    half_rows = ROW_TILE // 2
    halves = [slice(h * half_rows, (h + 1) * half_rows) for h in range(2)]

    def project(rows):
        mix = jnp.dot(oa_ref[rows, :], wa_ref[...], preferred_element_type=jnp.float32)
        mix = mix + jnp.dot(ob_ref[rows, :], wb_ref[...], preferred_element_type=jnp.float32)
        return mix + bo_ref[...]

    def normalise(rows, mix):
        x1 = _layer_norm(DEEPNORM_ALPHA * x_ref[rows, :] + mix, g_ref[...], b_ref[...])
        x1_ref[rows, :] = x1
        x1p_ref[rows, :] = _pack_bf16_pairs(x1)
        return jnp.dot(x1.astype(jnp.bfloat16), wr_ref[...], preferred_element_type=jnp.float32) + br_ref[...]

    mixes = [project(rows) for rows in halves]
    logits = jnp.concatenate([normalise(rows, mix) for rows, mix in zip(halves, mixes)], axis=0)
    lane = lax.broadcasted_iota(jnp.int32, logits.shape, 1)
    idx_out = jnp.zeros(logits.shape, jnp.int32)
    val_out = jnp.zeros(logits.shape, jnp.float32)
    top = None
    total = None
    onehots = []
    for k in range(TOP_K):
        m = jnp.max(logits, axis=-1, keepdims=True)
        idx = jnp.min(jnp.where(logits == m, lane, LANES), axis=-1, keepdims=True)
        picked = lane == idx
        onehots.append(picked)
        logits = jnp.where(picked, -jnp.inf, logits)
        if k == 0:
            top = m
        e = jnp.exp(m - top)
        total = e if k == 0 else total + e
        idx_out = jnp.where(lane == k, idx, idx_out)
        val_out = jnp.where(lane == k, e, val_out)
    idx_ref[...] = idx_out.T[:TOP_K, :]
    gate_ref[...] = (val_out / total)[:, :TOP_K]

    picks = jnp.concatenate([p.astype(jnp.bfloat16) for p in onehots], axis=1)
    before = jnp.dot(tri_ref[...], picks, preferred_element_type=jnp.float32)
    base = running_ref[...]
    rank_out = jnp.zeros(logits.shape, jnp.float32)
    for k in range(TOP_K):
        pk = onehots[k].astype(jnp.float32)
        here = before[:, k * LANES:(k + 1) * LANES] + base
        rank_k = jnp.sum(pk * here, axis=-1, keepdims=True)
        rank_out = jnp.where(lane == k, rank_k, rank_out)
        base = base + jnp.sum(pk, axis=0, keepdims=True)
    running_ref[...] = base
    rank_ref[...] = rank_out.astype(jnp.int32).T[:TOP_K, :]
    count_ref[0] = base.astype(jnp.int32)


def _outproj(o_a, o_b, w_a, w_b, b_out, x2, g, b, w_r, b_r):
    n_tok = x2.shape[0]
    group_starts = tuple(t0 // ROW_TILE for t0, _ in _moe_groups(n_tok))
    row = lambda i: (i, 0)
    fixed = lambda i: (0, 0)
    r = lax.broadcasted_iota(jnp.int32, (ROW_TILE, ROW_TILE), 0)
    c = lax.broadcasted_iota(jnp.int32, (ROW_TILE, ROW_TILE), 1)
    tri = (c < r).astype(jnp.bfloat16)
    return pl.pallas_call(
        functools.partial(_outproj_kernel, group_starts=group_starts),
        grid=(n_tok // ROW_TILE,),
        in_specs=[
            pl.BlockSpec((ROW_TILE, W_Q_SWA), row),
            pl.BlockSpec((ROW_TILE, W_MOBA), row),
            pl.BlockSpec((W_Q_SWA, D_MODEL), fixed),
            pl.BlockSpec((W_MOBA, D_MODEL), fixed),
            pl.BlockSpec((1, D_MODEL), fixed),
            pl.BlockSpec((ROW_TILE, D_MODEL), row),
            pl.BlockSpec((1, D_MODEL), fixed),
            pl.BlockSpec((1, D_MODEL), fixed),
            pl.BlockSpec((D_MODEL, LANES), fixed),
            pl.BlockSpec((1, LANES), fixed),
            pl.BlockSpec((ROW_TILE, ROW_TILE), fixed),
        ],
        out_specs=[
            pl.BlockSpec((ROW_TILE, D_MODEL), row),
            pl.BlockSpec((ROW_TILE, PACKED), row),
            pl.BlockSpec((TOP_K, ROW_TILE), lambda i: (0, i)),
            pl.BlockSpec((ROW_TILE, TOP_K), row),
            pl.BlockSpec((TOP_K, ROW_TILE), lambda i: (0, i)),
            pl.BlockSpec((1, 1, LANES), lambda i: (sum((i >= t).astype(jnp.int32) for t in group_starts[1:]), 0, 0)),
        ],
        out_shape=[
            jax.ShapeDtypeStruct((n_tok, D_MODEL), jnp.float32),
            jax.ShapeDtypeStruct((n_tok, PACKED), jnp.int32),
            jax.ShapeDtypeStruct((TOP_K, n_tok), jnp.int32),
            jax.ShapeDtypeStruct((n_tok, TOP_K), jnp.float32),
            jax.ShapeDtypeStruct((TOP_K, n_tok), jnp.int32),
            jax.ShapeDtypeStruct((len(group_starts), 1, LANES), jnp.int32),
        ],
        scratch_shapes=[pltpu.VMEM((1, LANES), jnp.float32)],
        compiler_params=_params("arbitrary"),
        name="outproj_ln_router",
    )(o_a, o_b, w_a, w_b, b_out, x2, g, b, w_r, b_r, tri)


def _sc_worker_id():
    return lax.axis_index("s") * SC_CORES + lax.axis_index("c")


def _sc_scatter_rows(rows, pos3, n_out, row0):
    n_tok = pos3.shape[0] * SC_ROWS
    steps = n_tok // SC_ROWS // SC_WORKERS
    assert steps * SC_ROWS * SC_WORKERS == n_tok and steps % 2 == 0
    mesh = plsc.VectorSubcoreMesh(core_axis_name="c", subcore_axis_name="s")

    @functools.partial(
        pl.kernel, mesh=mesh,
        out_type=jax.ShapeDtypeStruct((n_out, PACKED), jnp.int32),
        scratch_types=[pltpu.VMEM((2, TOP_K, SC_ROWS), jnp.int32), pltpu.VMEM((2, SC_ROWS, PACKED), jnp.int32),
                       pltpu.SemaphoreType.DMA((2,)), pltpu.SemaphoreType.DMA((2,))],
        name="sc_dispatch_scatter")
    def scatter(x_hbm, pos_hbm, out_hbm, idx_v, rows_v, sem_ld, sem_st):
        base = _sc_worker_id() * steps

        def loads(s, b):
            return (pltpu.make_async_copy(pos_hbm.at[base + s], idx_v.at[b], sem_ld.at[b]),
                    pltpu.make_async_copy(x_hbm.at[pl.ds(row0 + (base + s) * SC_ROWS, SC_ROWS)], rows_v.at[b],
                                          sem_ld.at[b]))

        def stores(b):
            return [pltpu.make_async_copy(rows_v.at[b], out_hbm.at[idx_v.at[b, k]], sem_st.at[b])
                    for k in range(TOP_K)]

        for c in loads(0, 0):
            c.start()

        @pl.loop(0, steps, step=2)
        def _(s0):
            for b in range(2):
                s = s0 + b
                for c in loads(s, b):
                    c.wait()

                @pl.when(s >= 1)
                def _():
                    for c in stores(1 - b):
                        c.wait()

                @pl.when(s + 1 < steps)
                def _():
                    for c in loads(s + 1, 1 - b):
                        c.start()

                for c in stores(b):
                    c.start()

        for c in stores(1):
            c.wait()

    return scatter(rows, pos3)


def _sc_gather_rows(table, idx2):
    n_blk = idx2.shape[0]
    steps = n_blk // SC_WORKERS
    assert steps * SC_WORKERS == n_blk and steps % 2 == 0 and idx2.shape[1] == SC_ROWS
    mesh = plsc.VectorSubcoreMesh(core_axis_name="c", subcore_axis_name="s")

    @functools.partial(
        pl.kernel, mesh=mesh,
        out_type=jax.ShapeDtypeStruct((n_blk * SC_ROWS, PACKED), jnp.int32),
        scratch_types=[pltpu.VMEM((steps, SC_ROWS), jnp.int32), pltpu.VMEM((2, SC_ROWS, PACKED), jnp.int32),
                       pltpu.SemaphoreType.DMA((2,)), pltpu.SemaphoreType.DMA((2,))],
        name="sc_combine_gather")
    def gather(y_hbm, idx_hbm, out_hbm, idx_v, rows_v, sem_ld, sem_st):
        base = _sc_worker_id() * steps
        pltpu.sync_copy(idx_hbm.at[pl.ds(base, steps)], idx_v)

        def fetch(s, b):
            return pltpu.make_async_copy(y_hbm.at[idx_v.at[s]], rows_v.at[b], sem_ld.at[b])

        def store(s, b):
            return pltpu.make_async_copy(rows_v.at[b], out_hbm.at[pl.ds((base + s) * SC_ROWS, SC_ROWS)],
                                         sem_st.at[b])

        fetch(0, 0).start()

        @pl.loop(0, steps, step=2)
        def _(s0):
            for b in range(2):
                s = s0 + b
                fetch(s, b).wait()

                @pl.when(s >= 1)
                def _():
                    store(s - 1, 1 - b).wait()

                @pl.when(s + 1 < steps)
                def _():
                    fetch(s + 1, 1 - b).start()

                store(s, b).start()

        store(steps - 1, 1).wait()

    return gather(table, idx2)


def _expert_kernel(be_ref, br_ref, nv_ref, x_ref, w1_ref, perm_ref, bg_ref, bl_ref, w2_ref, b2_ref, y_ref,
                   wg_ref, wl_ref, w2b_ref):
    i = pl.program_id(0)
    live = i < nv_ref[0]
    new_expert = (i == 0) | (be_ref[i] != be_ref[jnp.maximum(i - 1, 0)])

    @pl.when(live & new_expert)
    def _():
        for c in range(2 * D_FF // 256):
            t = w1_ref[0, :, c * 256:(c + 1) * 256].astype(jnp.bfloat16)
            r = jnp.dot(t, perm_ref[...], preferred_element_type=jnp.float32)
            wg_ref[:, c * LANES:(c + 1) * LANES] = r[:, :LANES].astype(wg_ref.dtype)
            wl_ref[:, c * LANES:(c + 1) * LANES] = r[:, LANES:].astype(wl_ref.dtype)
        w2b_ref[...] = w2_ref[0].astype(w2b_ref.dtype)

    @pl.when(live)
    def _():
        valid = lax.broadcasted_iota(jnp.int32, x_ref.shape, 0) < br_ref[i]
        lo, hi = _unpack_bf16_pairs(jnp.where(valid, x_ref[...], 0))
        xb = jnp.concatenate([lo.astype(jnp.bfloat16), hi.astype(jnp.bfloat16)], axis=1)
        hg = jnp.dot(xb, wg_ref[...], preferred_element_type=jnp.float32) + bg_ref[0]
        hl = jnp.dot(xb, wl_ref[...], preferred_element_type=jnp.float32) + bl_ref[0]
        glu = jnp.minimum(hg, SWIGLU_LIMIT)
        lin = jnp.clip(hl, -SWIGLU_LIMIT, SWIGLU_LIMIT)
        act = glu * jax.nn.sigmoid(SWIGLU_ALPHA * glu) * (lin + 1.0)
        y = jnp.dot(act.astype(jnp.bfloat16), w2b_ref[...], preferred_element_type=jnp.float32) + b2_ref[0]
        y_ref[...] = _pack_bf16_pairs(y)

    @pl.when(jnp.logical_not(live))
    def _():
        y_ref[...] = jnp.zeros(y_ref.shape, y_ref.dtype)


def _experts(block_expert, block_rows, n_valid, xg, w1, perm, b1g, b1l, w2, b2):
    n_rows = xg.shape[0]
    n_blocks = n_rows // EXPERT_ROWS

    def row(i, be, br, nv):
        return (jnp.minimum(i, nv[0] - 1), 0)

    def per_expert(i, be, br, nv):
        return (be[i], 0, 0)

    grid_spec = pltpu.PrefetchScalarGridSpec(
        num_scalar_prefetch=3,
        grid=(n_blocks,),
        in_specs=[
            pl.BlockSpec((EXPERT_ROWS, PACKED), row),
            pl.BlockSpec((1, D_MODEL, 2 * D_FF), per_expert),
            pl.BlockSpec((256, 256), lambda i, be, br, nv: (0, 0)),
            pl.BlockSpec((1, 1, D_FF), per_expert),
            pl.BlockSpec((1, 1, D_FF), per_expert),
            pl.BlockSpec((1, D_FF, D_MODEL), per_expert),
            pl.BlockSpec((1, 1, D_MODEL), per_expert),
        ],
        out_specs=pl.BlockSpec((EXPERT_ROWS, PACKED), lambda i, be, br, nv: (i, 0)),
        scratch_shapes=[pltpu.VMEM((D_MODEL, D_FF), jnp.bfloat16), pltpu.VMEM((D_MODEL, D_FF), jnp.bfloat16),
                        pltpu.VMEM((D_FF, D_MODEL), jnp.bfloat16)],
    )
    return pl.pallas_call(
        _expert_kernel,
        grid_spec=grid_spec,
        out_shape=jax.ShapeDtypeStruct((n_rows, PACKED), jnp.int32),
        compiler_params=pltpu.CompilerParams(dimension_semantics=("arbitrary",),
                                             vmem_limit_bytes=EXPERT_VMEM_LIMIT),
        name="grouped_experts",
    )(block_expert, block_rows, n_valid, xg, w1, perm, b1g, b1l, w2, b2)


def _split_columns_perm():
    i = lax.broadcasted_iota(jnp.int32, (256, 256), 0)
    o = lax.broadcasted_iota(jnp.int32, (256, 256), 1)
    return (i == jnp.where(o < LANES, 2 * o, 2 * (o - LANES) + 1)).astype(jnp.bfloat16)


def _combine_kernel(y_ref, gate_ref, x1_ref, g_ref, b_ref, *rest):
    o_ref = rest[-1]
    gates = gate_ref[...]
    lo_sum = None
    hi_sum = None
    for k in range(TOP_K):
        lo, hi = _unpack_bf16_pairs(y_ref[k])
        gk = gates[:, k:k + 1]
        lo_sum = gk * lo if k == 0 else lo_sum + gk * lo
        hi_sum = gk * hi if k == 0 else hi_sum + gk * hi
    moe = jnp.concatenate([lo_sum, hi_sum], axis=1)
    o_ref[...] = _layer_norm(DEEPNORM_ALPHA * x1_ref[...] + moe, g_ref[...], b_ref[...])


def _combine(yg, gates, x1, g, b, tile0, prev):
    n_tok = x1.shape[0]
    row = lambda i: (i + tile0, 0)
    fixed = lambda i: (0, 0)
    in_specs = [
        pl.BlockSpec((TOP_K, ROW_TILE, PACKED), lambda i: (0, i, 0)),
        pl.BlockSpec((ROW_TILE, TOP_K), row),
        pl.BlockSpec((ROW_TILE, D_MODEL), row),
        pl.BlockSpec((1, D_MODEL), fixed),
        pl.BlockSpec((1, D_MODEL), fixed),
    ]
    args = [yg, gates, x1, g, b]
    aliases = {}
    if prev is not None:
        in_specs.append(pl.BlockSpec(memory_space=pl.ANY))
        args.append(prev)
        aliases = {len(args) - 1: 0}
    return pl.pallas_call(
        _combine_kernel,
        grid=(yg.shape[1] // ROW_TILE,),
        in_specs=in_specs,
        out_specs=pl.BlockSpec((ROW_TILE, D_MODEL), row),
        out_shape=jax.ShapeDtypeStruct((n_tok, D_MODEL), jnp.float32),
        input_output_aliases=aliases,
        compiler_params=_params("parallel"),
        name="combine_ln",
    )(*args)


def _route(top_idx, rank, counts, n_blocks):
    counts = counts[0, :N_EXPERTS]
    blocks_per = (counts + EXPERT_ROWS - 1) // EXPERT_ROWS
    blk_end = jnp.cumsum(blocks_per)
    blk_start = blk_end - blocks_per
    pos = (blk_start * EXPERT_ROWS)[top_idx] + rank
    blk = jnp.arange(n_blocks, dtype=jnp.int32)
    block_expert = jnp.minimum(jnp.sum((blk[:, None] >= blk_end[None, :]).astype(jnp.int32), axis=1),
                               N_EXPERTS - 1)
    inside = blk - blk_start[block_expert]
    block_rows = jnp.clip(counts[block_expert] - inside * EXPERT_ROWS, 0, EXPERT_ROWS).astype(jnp.int32)
    n_valid = blk_end[-1:].astype(jnp.int32)
    return pos, block_expert.astype(jnp.int32), block_rows, n_valid


def _moe_groups(n_tok):
    unit = 2 * SC_ROWS * SC_WORKERS
    parts = sum(MOE_SPLIT)
    if n_tok % (parts * unit):
        return ((0, n_tok),)
    groups, t0 = [], 0
    for share in MOE_SPLIT:
        groups.append((t0, n_tok * share // parts))
        t0 += groups[-1][1]
    return tuple(groups)


def _rope_tables(seq):
    inv_freq = 1.0 / (ROPE_THETA ** (jnp.arange(0, HEAD_DIM, 2, dtype=jnp.float32) / HEAD_DIM))
    ang = jnp.arange(seq, dtype=jnp.float32)[:, None] * inv_freq[None, :]
    cos, sin = jnp.cos(ang), jnp.sin(ang)
    cos_t = jnp.tile(jnp.concatenate([cos, cos], axis=1), (1, 256 // HEAD_DIM))
    sin_t = jnp.tile(jnp.concatenate([-sin, sin], axis=1), (1, 256 // HEAD_DIM))
    return cos_t, sin_t


def kernel(x, w_in, b_in, sinks, w_out, b_out, ln1_g, ln1_b, w_router, b_router, w1, b1, w2, b2, ln2_g, ln2_b):
    batch, seq, d = x.shape
    assert d == D_MODEL and seq % ROW_TILE == 0 and seq % MOBA_BLOCK == 0 and w_in.shape[0] == DEPTH == 1
    assert (seq // WINDOW - 1) % SWA_UNROLL == 0
    n_tok = batch * seq
    bf16 = jnp.bfloat16
    x2 = x.reshape(n_tok, d)
    cos_t, sin_t = _rope_tables(seq)

    proj = _inproj(x2, w_in[0].astype(bf16), b_in[0].reshape(1, IN_WIDTH), cos_t, sin_t, seq)
    o_a = _swa(proj, sinks[0], batch, seq)
    o_b = _moba(proj, batch, seq)

    w_o = w_out[0].astype(bf16)
    w_r = jnp.pad(w_router[0], ((0, 0), (0, LANES - N_EXPERTS))).astype(bf16)
    b_r = jnp.pad(b_router[0], (0, LANES - N_EXPERTS), constant_values=NEG_BIG).reshape(1, LANES)
    x1, x1p, top_idx, gates, rank, counts = _outproj(
        o_a, o_b, w_o[:W_Q_SWA], w_o[W_Q_SWA:], b_out[0].reshape(1, d), x2,
        ln1_g[0].reshape(1, d), ln1_b[0].reshape(1, d), w_r, b_r)

    perm = _split_columns_perm()
    b1r = b1[0].reshape(N_EXPERTS, 1, D_FF, 2)
    b1g, b1l = b1r[..., 0], b1r[..., 1]
    b2r = b2[0].reshape(N_EXPERTS, 1, d)
    g2, be2 = ln2_g[0].reshape(1, d), ln2_b[0].reshape(1, d)

    out = None
    for grp, (t0, tok_g) in enumerate(_moe_groups(n_tok)):
        n_blocks = tok_g * TOP_K // EXPERT_ROWS + N_EXPERTS
        pos, block_expert, block_rows, n_valid = _route(
            top_idx[:, t0:t0 + tok_g], rank[:, t0:t0 + tok_g], counts[grp], n_blocks)
        pos3 = pos.reshape(TOP_K, tok_g // SC_ROWS, SC_ROWS).transpose(1, 0, 2)
        xg = _sc_scatter_rows(x1p, pos3, n_blocks * EXPERT_ROWS, t0)
        y = _experts(block_expert, block_rows, n_valid, xg, w1[0], perm, b1g, b1l, w2[0], b2r)
        yg = _sc_gather_rows(y, pos.reshape(tok_g * TOP_K // SC_ROWS, SC_ROWS))
        out = _combine(yg.reshape(TOP_K, tok_g, PACKED), gates, x1, g2, be2, t0 // ROW_TILE, out)
    return out.reshape(batch, seq, d)
```

```python
import functools

import jax
import jax.numpy as jnp
from jax import lax
from jax.experimental import pallas as pl
from jax.experimental.pallas import tpu as pltpu
from jax.experimental.pallas import tpu_sc as plsc

D_MODEL = 1024
HEAD_DIM = 64
N_HEADS_SWA = 8
N_KV_SWA = 2
WINDOW = 128
N_HEADS_MOBA = 8
MOBA_BLOCK = 256
MOBA_TOPK = 3
ROPE_THETA = 10000.0
N_EXPERTS = 32
TOP_K = 4
D_FF = 1024
SWIGLU_LIMIT = 7.0
SWIGLU_ALPHA = 1.702
LN_EPS = 1e-5
DEPTH = 1
DEEPNORM_ALPHA = (2 * DEPTH) ** 0.25

W_Q_SWA = N_HEADS_SWA * HEAD_DIM
W_KV_SWA = N_KV_SWA * HEAD_DIM
W_MOBA = N_HEADS_MOBA * HEAD_DIM
IN_WIDTH = W_Q_SWA + 2 * W_KV_SWA + 3 * W_MOBA
LANES = 128
COL_K_SWA = W_Q_SWA // LANES
COL_V_SWA = COL_K_SWA + 1
COL_Q_MOBA = COL_V_SWA + 1
COL_K_MOBA = COL_Q_MOBA + W_MOBA // LANES
COL_V_MOBA = COL_K_MOBA + W_MOBA // LANES

ROW_TILE = 512
EXPERT_ROWS = 512
PACKED = D_MODEL // 2
NEG_BIG = -1e30
LOG2E = 1.4426950408889634
MOBA_LOOKAHEAD = 1
SWA_UNROLL = 3
MOE_SPLIT = (3, 1)
VMEM_LIMIT = 48 * 1024 * 1024
EXPERT_VMEM_LIMIT = 60 * 1024 * 1024

SC_CORES = 2
SC_SUBCORES = 16
SC_WORKERS = SC_CORES * SC_SUBCORES
SC_ROWS = 64

_PROJ_CHUNKS = (
    (0, 256, 256, True), (256, 256, 256, True),
    (512, 256, 128, False),
    (768, 256, 256, True), (1024, 256, 256, True),
    (1280, 256, 256, False), (1536, 256, 256, False),
    (1792, 256, 0, False), (2048, 256, 0, False),
)


def _params(*sem):
    return pltpu.CompilerParams(dimension_semantics=sem, vmem_limit_bytes=VMEM_LIMIT)


def _pack_bf16_pairs(v):
    n = v.shape[1] // 2
    bits = lax.bitcast_convert_type(v.astype(jnp.bfloat16).astype(jnp.float32), jnp.uint32)
    word = (bits[:, :n] >> 16) | (bits[:, n:] & jnp.uint32(0xFFFF0000))
    return lax.bitcast_convert_type(word, jnp.int32)


def _unpack_bf16_pairs(word):
    bits = lax.bitcast_convert_type(word, jnp.uint32)
    lo = lax.bitcast_convert_type(bits << 16, jnp.float32)
    hi = lax.bitcast_convert_type(bits & jnp.uint32(0xFFFF0000), jnp.float32)
    return lo, hi


def _inproj_kernel(x_ref, w_ref, b_ref, cos_ref, sin_ref, o_ref):
    xb = x_ref[...].astype(jnp.bfloat16)
    for start, width, rope, scaled in _PROJ_CHUNKS:
        t = jnp.dot(xb, w_ref[:, start:start + width], preferred_element_type=jnp.float32)
        t = t + b_ref[:, start:start + width]
        if rope:
            lane = lax.broadcasted_iota(jnp.int32, t.shape, 1)
            first_half = (lane % HEAD_DIM) < (HEAD_DIM // 2)
            rot = jnp.where(first_half,
                            pltpu.roll(t, width - HEAD_DIM // 2, 1),
                            pltpu.roll(t, HEAD_DIM // 2, 1))
            roped = t * cos_ref[:, :width] + rot * sin_ref[:, :width]
            t = roped if rope == width else jnp.where(lane < rope, roped, t)
        if scaled:
            t = t * (HEAD_DIM ** -0.5 * LOG2E)
        o_ref[:, start:start + width] = t.astype(o_ref.dtype)


def _inproj(x2, w_in, b_in, cos_t, sin_t, seq):
    n_tok = x2.shape[0]
    per_seq = seq // ROW_TILE
    return pl.pallas_call(
        _inproj_kernel,
        grid=(n_tok // ROW_TILE,),
        in_specs=[
            pl.BlockSpec((ROW_TILE, D_MODEL), lambda i: (i, 0)),
            pl.BlockSpec((D_MODEL, IN_WIDTH), lambda i: (0, 0)),
            pl.BlockSpec((1, IN_WIDTH), lambda i: (0, 0)),
            pl.BlockSpec((ROW_TILE, 256), lambda i: (i % per_seq, 0)),
            pl.BlockSpec((ROW_TILE, 256), lambda i: (i % per_seq, 0)),
        ],
        out_specs=pl.BlockSpec((ROW_TILE, IN_WIDTH), lambda i: (i, 0)),
        out_shape=jax.ShapeDtypeStruct((n_tok, IN_WIDTH), jnp.bfloat16),
        compiler_params=_params("parallel"),
        name="inproj_rope",
    )(x2, w_in, b_in, cos_t, sin_t)


def _swa_kernel(sink_ref, q_ref, k_ref, v_ref, o_ref, kd_ref, vd_ref, *, seq):
    lane = lax.broadcasted_iota(jnp.int32, (seq, LANES), 1)
    low = lane < HEAD_DIM
    k = k_ref[...].astype(jnp.float32)
    kr = pltpu.roll(k, HEAD_DIM, 1)
    kd_ref[0, WINDOW:, :] = jnp.where(low, k, kr).astype(kd_ref.dtype)
    kd_ref[1, WINDOW:, :] = jnp.where(low, kr, k).astype(kd_ref.dtype)
    v = v_ref[...].astype(jnp.float32)
    vd_ref[0, WINDOW:, :] = jnp.where(low, v, 1.0).astype(vd_ref.dtype)
    vd_ref[1, WINDOW:, :] = jnp.where(low, pltpu.roll(v, HEAD_DIM, 1), 1.0).astype(vd_ref.dtype)
    kd_ref[:, :WINDOW, :] = jnp.zeros((N_KV_SWA, WINDOW, LANES), kd_ref.dtype)
    vd_ref[:, :WINDOW, :] = jnp.zeros((N_KV_SWA, WINDOW, LANES), vd_ref.dtype)

    group = N_HEADS_SWA // N_KV_SWA
    rows = group * WINDOW
    r_in = lax.broadcasted_iota(jnp.int32, (rows, 2 * WINDOW), 0) % WINDOW
    c_id = lax.broadcasted_iota(jnp.int32, (rows, 2 * WINDOW), 1)
    band = (c_id > r_in) & (c_id <= r_in + WINDOW)
    head_in_group = lax.broadcasted_iota(jnp.int32, (rows, 1), 0) // WINDOW
    qlane_low = lax.broadcasted_iota(jnp.int32, (WINDOW, LANES), 1) < HEAD_DIM
    sinks = []
    for g in range(N_KV_SWA):
        col = jnp.zeros((rows, 1), jnp.float32)
        for j in range(group):
            col = jnp.where(head_in_group == j, sink_ref[g * group + j] * LOG2E, col)
        sinks.append(col)

    def scores(n, mask):
        r0 = pl.multiple_of(n * WINDOW, WINDOW)
        out = []
        for g in range(N_KV_SWA):
            parts = []
            for c in (2 * g, 2 * g + 1):
                qc = q_ref[pl.ds(r0, WINDOW), c * LANES:(c + 1) * LANES]
                zero = jnp.zeros_like(qc)
                parts.append(jnp.where(qlane_low, qc, zero))
                parts.append(jnp.where(qlane_low, zero, qc))
            qcat = jnp.concatenate(parts, axis=0)
            kd = kd_ref[g, pl.ds(r0, 2 * WINDOW), :]
            s = lax.dot_general(qcat, kd, (((1,), (1,)), ((), ())),
                                preferred_element_type=jnp.float32)
            out.append(jnp.where(mask, s, -jnp.inf))
        return out

    def finish(n, scored):
        r0 = pl.multiple_of(n * WINDOW, WINDOW)
        for g, s in enumerate(scored):
            vd = vd_ref[g, pl.ds(r0, 2 * WINDOW), :]
            m = jnp.maximum(jnp.max(s, axis=-1, keepdims=True), sinks[g])
            p = jnp.exp2(s - m)
            o = jnp.dot(p.astype(vd.dtype), vd, preferred_element_type=jnp.float32)
            sink_term = jnp.exp2(sinks[g] - m)
            for ci, c in enumerate((2 * g, 2 * g + 1)):
                o_lo = o[(2 * ci) * WINDOW:(2 * ci + 1) * WINDOW]
                o_hi = o[(2 * ci + 1) * WINDOW:(2 * ci + 2) * WINDOW]
                e_lo = sink_term[(2 * ci) * WINDOW:(2 * ci + 1) * WINDOW]
                e_hi = sink_term[(2 * ci + 1) * WINDOW:(2 * ci + 2) * WINDOW]
                num = jnp.where(qlane_low, o_lo, pltpu.roll(o_hi, HEAD_DIM, 1))
                den = jnp.where(qlane_low, pltpu.roll(o_lo, HEAD_DIM, 1) + e_lo, o_hi + e_hi)
                o_ref[pl.ds(r0, WINDOW), c * LANES:(c + 1) * LANES] = (num / den).astype(o_ref.dtype)

    finish(0, scores(0, band & (c_id >= WINDOW)))

    def body(it, carry):
        n0 = 1 + it * SWA_UNROLL
        scored = [scores(n0 + u, band) for u in range(SWA_UNROLL)]
        for u in range(SWA_UNROLL):
            finish(n0 + u, scored[u])
        return carry

    lax.fori_loop(0, (seq // WINDOW - 1) // SWA_UNROLL, body, 0)


def _swa(proj, sinks, batch, seq):
    grid_spec = pltpu.PrefetchScalarGridSpec(
        num_scalar_prefetch=0,
        grid=(batch,),
        in_specs=[
            pl.BlockSpec(memory_space=pltpu.SMEM),
            pl.BlockSpec((seq, W_Q_SWA), lambda b: (b, 0)),
            pl.BlockSpec((seq, LANES), lambda b: (b, COL_K_SWA)),
            pl.BlockSpec((seq, LANES), lambda b: (b, COL_V_SWA)),
        ],
        out_specs=pl.BlockSpec((seq, W_Q_SWA), lambda b: (b, 0)),
        scratch_shapes=[pltpu.VMEM((N_KV_SWA, WINDOW + seq, LANES), jnp.bfloat16),
                        pltpu.VMEM((N_KV_SWA, WINDOW + seq, LANES), jnp.bfloat16)],
    )
    return pl.pallas_call(
        functools.partial(_swa_kernel, seq=seq),
        grid_spec=grid_spec,
        out_shape=jax.ShapeDtypeStruct((batch * seq, W_Q_SWA), jnp.bfloat16),
        compiler_params=_params("parallel"),
        name="swa_sink_attention",
    )(sinks, proj, proj, proj)


def _moba_kernel(q_ref, k_ref, v_ref, o_ref, qa_ref, ka_ref, va_ref, *, seq):
    nblk = seq // MOBA_BLOCK
    pad_rows = 16
    q_all, k_all, v_all = q_ref[...], k_ref[...], v_ref[...]
    kmean = jnp.sum(k_all.astype(jnp.float32).reshape(nblk, MOBA_BLOCK, LANES), axis=1) / MOBA_BLOCK
    kmean = jnp.concatenate([kmean, jnp.zeros((pad_rows - nblk, LANES), jnp.float32)], axis=0)
    klane_low = lax.broadcasted_iota(jnp.int32, (pad_rows, LANES), 1) < HEAD_DIM
    lane = lax.broadcasted_iota(jnp.int32, (seq, LANES), 1)
    low = lane < HEAD_DIM
    key_blk = lax.broadcasted_iota(jnp.int32, (seq, LANES), 0) // MOBA_BLOCK
    j_id = lax.broadcasted_iota(jnp.int32, (pad_rows, seq), 0)
    q_blk = lax.broadcasted_iota(jnp.int32, (pad_rows, seq), 1) // MOBA_BLOCK
    eligible = j_id < q_blk

    for half in range(2):
        own = low if half == 0 else ~low
        spare = HEAD_DIM if half == 0 else 0
        ka_ref[half] = jnp.where(own, k_all, (lane - spare == key_blk).astype(k_all.dtype))
        va_ref[half] = jnp.where(own, v_all, jnp.ones_like(v_all))
        km = jnp.where(klane_low if half == 0 else ~klane_low, kmean, 0.0).astype(jnp.bfloat16)
        gate = lax.dot_general(km, q_all, (((1,), (1,)), ((), ())),
                               preferred_element_type=jnp.float32)
        gate = jnp.where(eligible, gate, -jnp.inf)
        beaten = jnp.zeros((pad_rows, seq), jnp.int32)
        for jp in range(nblk):
            row = gate[jp:jp + 1, :]
            wins = (row > gate) | ((row == gate) & (jp < j_id))
            beaten = beaten + wins.astype(jnp.int32)
        dropped = eligible & (beaten >= MOBA_TOPK)
        bias = jnp.where(dropped, NEG_BIG, 0.0)
        pieces = [bias, jnp.zeros((LANES - spare - pad_rows, seq), jnp.float32)]
        if spare:
            pieces.insert(0, jnp.zeros((spare, seq), jnp.float32))
        bias_t = jnp.concatenate(pieces, axis=0).T
        qa_ref[half] = jnp.where(own, q_all, bias_t.astype(q_all.dtype))

    qlane_low = lax.broadcasted_iota(jnp.int32, (MOBA_BLOCK, LANES), 1) < HEAD_DIM
    rr = lax.broadcasted_iota(jnp.int32, (MOBA_BLOCK, MOBA_BLOCK), 0)
    cc = lax.broadcasted_iota(jnp.int32, (MOBA_BLOCK, MOBA_BLOCK), 1)
    causal = cc <= rr

    def scores(i, half):
        r0 = i * MOBA_BLOCK
        n_keys = r0 + MOBA_BLOCK
        s = lax.dot_general(qa_ref[half, r0:n_keys, :], ka_ref[half, 0:n_keys, :], (((1,), (1,)), ((), ())),
                            preferred_element_type=jnp.float32)
        own_blk = jnp.where(causal, s[:, r0:n_keys], NEG_BIG)
        return jnp.concatenate([s[:, :r0], own_blk], axis=1) if i else own_blk

    units = [(i, half) for i in range(nblk) for half in range(2)]
    pending = [scores(*u) for u in units[:MOBA_LOOKAHEAD]]
    acc = []
    for n, (i, half) in enumerate(units):
        s = pending.pop(0)
        if n + MOBA_LOOKAHEAD < len(units):
            pending.append(scores(*units[n + MOBA_LOOKAHEAD]))
        n_keys = (i + 1) * MOBA_BLOCK
        m = jnp.max(s, axis=-1, keepdims=True)
        p = jnp.exp2(s - m).astype(jnp.bfloat16)
        acc.append(jnp.dot(p, va_ref[half, 0:n_keys, :], preferred_element_type=jnp.float32))
        if half == 1:
            num = jnp.where(qlane_low, acc[0], acc[1])
            den = pltpu.roll(jnp.where(qlane_low, acc[1], acc[0]), HEAD_DIM, 1)
            o_ref[i * MOBA_BLOCK:n_keys, :] = (num / den).astype(o_ref.dtype)
            acc = []


def _moba(proj, batch, seq):
    pairs = W_MOBA // LANES
    return pl.pallas_call(
        functools.partial(_moba_kernel, seq=seq),
        grid=(batch, pairs),
        in_specs=[
            pl.BlockSpec((seq, LANES), lambda b, p: (b, COL_Q_MOBA + p)),
            pl.BlockSpec((seq, LANES), lambda b, p: (b, COL_K_MOBA + p)),
            pl.BlockSpec((seq, LANES), lambda b, p: (b, COL_V_MOBA + p)),
        ],
        out_specs=pl.BlockSpec((seq, LANES), lambda b, p: (b, p)),
        out_shape=jax.ShapeDtypeStruct((batch * seq, W_MOBA), jnp.bfloat16),
        scratch_shapes=[pltpu.VMEM((2, seq, LANES), jnp.bfloat16)] * 3,
        compiler_params=_params("parallel", "parallel"),
        name="moba_attention",
    )(proj, proj, proj)


def _layer_norm(h, g, b):
    mu = jnp.mean(h, axis=-1, keepdims=True)
    d = h - mu
    var = jnp.mean(d * d, axis=-1, keepdims=True)
    return d * lax.rsqrt(var + LN_EPS) * g + b


def _outproj_kernel(oa_ref, ob_ref, wa_ref, wb_ref, bo_ref, x_ref, g_ref, b_ref, wr_ref, br_ref, tri_ref,
                    x1_ref, x1p_ref, idx_ref, gate_ref, rank_ref, count_ref, running_ref, *, group_starts):
    first_of_group = pl.program_id(0) == group_starts[0]
    for t in group_starts[1:]:
        first_of_group = first_of_group | (pl.program_id(0) == t)

    @pl.when(first_of_group)
    def _():
        running_ref[...] = jnp.zeros(running_ref.shape, running_ref.dtype)

    half_rows = ROW_TILE // 2
    halves = [slice(h * half_rows, (h + 1) * half_rows) for h in range(2)]

    def project(rows):
        mix = jnp.dot(oa_ref[rows, :], wa_ref[...], preferred_element_type=jnp.float32)
        mix = mix + jnp.dot(ob_ref[rows, :], wb_ref[...], preferred_element_type=jnp.float32)
        return mix + bo_ref[...]

    def normalise(rows, mix):
        x1 = _layer_norm(DEEPNORM_ALPHA * x_ref[rows, :] + mix, g_ref[...], b_ref[...])
        x1_ref[rows, :] = x1
        x1p_ref[rows, :] = _pack_bf16_pairs(x1)
        return jnp.dot(x1.astype(jnp.bfloat16), wr_ref[...], preferred_element_type=jnp.float32) + br_ref[...]

    mixes = [project(rows) for rows in halves]
    logits = jnp.concatenate([normalise(rows, mix) for rows, mix in zip(halves, mixes)], axis=0)
    lane = lax.broadcasted_iota(jnp.int32, logits.shape, 1)
    idx_out = jnp.zeros(logits.shape, jnp.int32)
    val_out = jnp.zeros(logits.shape, jnp.float32)
    top = None
    total = None
    onehots = []
    for k in range(TOP_K):
        m = jnp.max(logits, axis=-1, keepdims=True)
        idx = jnp.min(jnp.where(logits == m, lane, LANES), axis=-1, keepdims=True)
        picked = lane == idx
        onehots.append(picked)
        logits = jnp.where(picked, -jnp.inf, logits)
        if k == 0:
            top = m
        e = jnp.exp(m - top)
        total = e if k == 0 else total + e
        idx_out = jnp.where(lane == k, idx, idx_out)
        val_out = jnp.where(lane == k, e, val_out)
    idx_ref[...] = idx_out.T[:TOP_K, :]
    gate_ref[...] = (val_out / total)[:, :TOP_K]

    picks = jnp.concatenate([p.astype(jnp.bfloat16) for p in onehots], axis=1)
    before = jnp.dot(tri_ref[...], picks, preferred_element_type=jnp.float32)
    base = running_ref[...]
    rank_out = jnp.zeros(logits.shape, jnp.float32)
    for k in range(TOP_K):
        pk = onehots[k].astype(jnp.float32)
        here = before[:, k * LANES:(k + 1) * LANES] + base
        rank_k = jnp.sum(pk * here, axis=-1, keepdims=True)
        rank_out = jnp.where(lane == k, rank_k, rank_out)
        base = base + jnp.sum(pk, axis=0, keepdims=True)
    running_ref[...] = base
    rank_ref[...] = rank_out.astype(jnp.int32).T[:TOP_K, :]
    count_ref[0] = base.astype(jnp.int32)


def _outproj(o_a, o_b, w_a, w_b, b_out, x2, g, b, w_r, b_r):
    n_tok = x2.shape[0]
    group_starts = tuple(t0 // ROW_TILE for t0, _ in _moe_groups(n_tok))
    row = lambda i: (i, 0)
    fixed = lambda i: (0, 0)
    r = lax.broadcasted_iota(jnp.int32, (ROW_TILE, ROW_TILE), 0)
    c = lax.broadcasted_iota(jnp.int32, (ROW_TILE, ROW_TILE), 1)
    tri = (c < r).astype(jnp.bfloat16)
    return pl.pallas_call(
        functools.partial(_outproj_kernel, group_starts=group_starts),
        grid=(n_tok // ROW_TILE,),
        in_specs=[
            pl.BlockSpec((ROW_TILE, W_Q_SWA), row),
            pl.BlockSpec((ROW_TILE, W_MOBA), row),
            pl.BlockSpec((W_Q_SWA, D_MODEL), fixed),
            pl.BlockSpec((W_MOBA, D_MODEL), fixed),
            pl.BlockSpec((1, D_MODEL), fixed),
            pl.BlockSpec((ROW_TILE, D_MODEL), row),
            pl.BlockSpec((1, D_MODEL), fixed),
            pl.BlockSpec((1, D_MODEL), fixed),
            pl.BlockSpec((D_MODEL, LANES), fixed),
            pl.BlockSpec((1, LANES), fixed),
            pl.BlockSpec((ROW_TILE, ROW_TILE), fixed),
        ],
        out_specs=[
            pl.BlockSpec((ROW_TILE, D_MODEL), row),
            pl.BlockSpec((ROW_TILE, PACKED), row),
            pl.BlockSpec((TOP_K, ROW_TILE), lambda i: (0, i)),
            pl.BlockSpec((ROW_TILE, TOP_K), row),
            pl.BlockSpec((TOP_K, ROW_TILE), lambda i: (0, i)),
            pl.BlockSpec((1, 1, LANES), lambda i: (sum((i >= t).astype(jnp.int32) for t in group_starts[1:]), 0, 0)),
        ],
        out_shape=[
            jax.ShapeDtypeStruct((n_tok, D_MODEL), jnp.float32),
            jax.ShapeDtypeStruct((n_tok, PACKED), jnp.int32),
            jax.ShapeDtypeStruct((TOP_K, n_tok), jnp.int32),
            jax.ShapeDtypeStruct((n_tok, TOP_K), jnp.float32),
            jax.ShapeDtypeStruct((TOP_K, n_tok), jnp.int32),
            jax.ShapeDtypeStruct((len(group_starts), 1, LANES), jnp.int32),
        ],
        scratch_shapes=[pltpu.VMEM((1, LANES), jnp.float32)],
        compiler_params=_params("arbitrary"),
        name="outproj_ln_router",
    )(o_a, o_b, w_a, w_b, b_out, x2, g, b, w_r, b_r, tri)


def _sc_worker_id():
    return lax.axis_index("s") * SC_CORES + lax.axis_index("c")


def _sc_scatter_rows(rows, pos3, n_out, row0):
    n_tok = pos3.shape[0] * SC_ROWS
    steps = n_tok // SC_ROWS // SC_WORKERS
    assert steps * SC_ROWS * SC_WORKERS == n_tok and steps % 2 == 0
    mesh = plsc.VectorSubcoreMesh(core_axis_name="c", subcore_axis_name="s")

    @functools.partial(
        pl.kernel, mesh=mesh,
        out_type=jax.ShapeDtypeStruct((n_out, PACKED), jnp.int32),
        scratch_types=[pltpu.VMEM((2, TOP_K, SC_ROWS), jnp.int32), pltpu.VMEM((2, SC_ROWS, PACKED), jnp.int32),
                       pltpu.SemaphoreType.DMA((2,)), pltpu.SemaphoreType.DMA((2,))],
        name="sc_dispatch_scatter")
    def scatter(x_hbm, pos_hbm, out_hbm, idx_v, rows_v, sem_ld, sem_st):
        base = _sc_worker_id() * steps

        def loads(s, b):
            return (pltpu.make_async_copy(pos_hbm.at[base + s], idx_v.at[b], sem_ld.at[b]),
                    pltpu.make_async_copy(x_hbm.at[pl.ds(row0 + (base + s) * SC_ROWS, SC_ROWS)], rows_v.at[b],
                                          sem_ld.at[b]))

        def stores(b):
            return [pltpu.make_async_copy(rows_v.at[b], out_hbm.at[idx_v.at[b, k]], sem_st.at[b])
                    for k in range(TOP_K)]

        for c in loads(0, 0):
            c.start()

        @pl.loop(0, steps, step=2)
        def _(s0):
            for b in range(2):
                s = s0 + b
                for c in loads(s, b):
                    c.wait()

                @pl.when(s >= 1)
                def _():
                    for c in stores(1 - b):
                        c.wait()

                @pl.when(s + 1 < steps)
                def _():
                    for c in loads(s + 1, 1 - b):
                        c.start()

                for c in stores(b):
                    c.start()

        for c in stores(1):
            c.wait()

    return scatter(rows, pos3)


def _sc_gather_rows(table, idx2):
    n_blk = idx2.shape[0]
    steps = n_blk // SC_WORKERS
    assert steps * SC_WORKERS == n_blk and steps % 2 == 0 and idx2.shape[1] == SC_ROWS
    mesh = plsc.VectorSubcoreMesh(core_axis_name="c", subcore_axis_name="s")

    @functools.partial(
        pl.kernel, mesh=mesh,
        out_type=jax.ShapeDtypeStruct((n_blk * SC_ROWS, PACKED), jnp.int32),
        scratch_types=[pltpu.VMEM((steps, SC_ROWS), jnp.int32), pltpu.VMEM((2, SC_ROWS, PACKED), jnp.int32),
                       pltpu.SemaphoreType.DMA((2,)), pltpu.SemaphoreType.DMA((2,))],
        name="sc_combine_gather")
    def gather(y_hbm, idx_hbm, out_hbm, idx_v, rows_v, sem_ld, sem_st):
        base = _sc_worker_id() * steps
        pltpu.sync_copy(idx_hbm.at[pl.ds(base, steps)], idx_v)

        def fetch(s, b):
            return pltpu.make_async_copy(y_hbm.at[idx_v.at[s]], rows_v.at[b], sem_ld.at[b])

        def store(s, b):
            return pltpu.make_async_copy(rows_v.at[b], out_hbm.at[pl.ds((base + s) * SC_ROWS, SC_ROWS)],
                                         sem_st.at[b])

        fetch(0, 0).start()

        @pl.loop(0, steps, step=2)
        def _(s0):
            for b in range(2):
                s = s0 + b
                fetch(s, b).wait()

                @pl.when(s >= 1)
                def _():
                    store(s - 1, 1 - b).wait()

                @pl.when(s + 1 < steps)
                def _():
                    fetch(s + 1, 1 - b).start()

                store(s, b).start()

        store(steps - 1, 1).wait()

    return gather(table, idx2)


def _expert_kernel(be_ref, br_ref, nv_ref, x_ref, w1_ref, perm_ref, bg_ref, bl_ref, w2_ref, b2_ref, y_ref,
                   wg_ref, wl_ref, w2b_ref):
    i = pl.program_id(0)
    live = i < nv_ref[0]
    new_expert = (i == 0) | (be_ref[i] != be_ref[jnp.maximum(i - 1, 0)])

    @pl.when(live & new_expert)
    def _():
        for c in range(2 * D_FF // 256):
            t = w1_ref[0, :, c * 256:(c + 1) * 256].astype(jnp.bfloat16)
            r = jnp.dot(t, perm_ref[...], preferred_element_type=jnp.float32)
            wg_ref[:, c * LANES:(c + 1) * LANES] = r[:, :LANES].astype(wg_ref.dtype)
            wl_ref[:, c * LANES:(c + 1) * LANES] = r[:, LANES:].astype(wl_ref.dtype)
        w2b_ref[...] = w2_ref[0].astype(w2b_ref.dtype)

    @pl.when(live)
    def _():
        valid = lax.broadcasted_iota(jnp.int32, x_ref.shape, 0) < br_ref[i]
        lo, hi = _unpack_bf16_pairs(jnp.where(valid, x_ref[...], 0))
        xb = jnp.concatenate([lo.astype(jnp.bfloat16), hi.astype(jnp.bfloat16)], axis=1)
        hg = jnp.dot(xb, wg_ref[...], preferred_element_type=jnp.float32) + bg_ref[0]
        hl = jnp.dot(xb, wl_ref[...], preferred_element_type=jnp.float32) + bl_ref[0]
        glu = jnp.minimum(hg, SWIGLU_LIMIT)
        lin = jnp.clip(hl, -SWIGLU_LIMIT, SWIGLU_LIMIT)
        act = glu * jax.nn.sigmoid(SWIGLU_ALPHA * glu) * (lin + 1.0)
        y = jnp.dot(act.astype(jnp.bfloat16), w2b_ref[...], preferred_element_type=jnp.float32) + b2_ref[0]
        y_ref[...] = _pack_bf16_pairs(y)

    @pl.when(jnp.logical_not(live))
    def _():
        y_ref[...] = jnp.zeros(y_ref.shape, y_ref.dtype)


def _experts(block_expert, block_rows, n_valid, xg, w1, perm, b1g, b1l, w2, b2):
    n_rows = xg.shape[0]
    n_blocks = n_rows // EXPERT_ROWS

    def row(i, be, br, nv):
        return (jnp.minimum(i, nv[0] - 1), 0)

    def per_expert(i, be, br, nv):
        return (be[i], 0, 0)

    grid_spec = pltpu.PrefetchScalarGridSpec(
        num_scalar_prefetch=3,
        grid=(n_blocks,),
        in_specs=[
            pl.BlockSpec((EXPERT_ROWS, PACKED), row),
            pl.BlockSpec((1, D_MODEL, 2 * D_FF), per_expert),
            pl.BlockSpec((256, 256), lambda i, be, br, nv: (0, 0)),
            pl.BlockSpec((1, 1, D_FF), per_expert),
            pl.BlockSpec((1, 1, D_FF), per_expert),
            pl.BlockSpec((1, D_FF, D_MODEL), per_expert),
            pl.BlockSpec((1, 1, D_MODEL), per_expert),
        ],
        out_specs=pl.BlockSpec((EXPERT_ROWS, PACKED), lambda i, be, br, nv: (i, 0)),
        scratch_shapes=[pltpu.VMEM((D_MODEL, D_FF), jnp.bfloat16), pltpu.VMEM((D_MODEL, D_FF), jnp.bfloat16),
                        pltpu.VMEM((D_FF, D_MODEL), jnp.bfloat16)],
    )
    return pl.pallas_call(
        _expert_kernel,
        grid_spec=grid_spec,
        out_shape=jax.ShapeDtypeStruct((n_rows, PACKED), jnp.int32),
        compiler_params=pltpu.CompilerParams(dimension_semantics=("arbitrary",),
                                             vmem_limit_bytes=EXPERT_VMEM_LIMIT),
        name="grouped_experts",
    )(block_expert, block_rows, n_valid, xg, w1, perm, b1g, b1l, w2, b2)


def _split_columns_perm():
    i = lax.broadcasted_iota(jnp.int32, (256, 256), 0)
    o = lax.broadcasted_iota(jnp.int32, (256, 256), 1)
    return (i == jnp.where(o < LANES, 2 * o, 2 * (o - LANES) + 1)).astype(jnp.bfloat16)


def _combine_kernel(y_ref, gate_ref, x1_ref, g_ref, b_ref, *rest):
    o_ref = rest[-1]
    gates = gate_ref[...]
    lo_sum = None
    hi_sum = None
    for k in range(TOP_K):
        lo, hi = _unpack_bf16_pairs(y_ref[k])
        gk = gates[:, k:k + 1]
        lo_sum = gk * lo if k == 0 else lo_sum + gk * lo
        hi_sum = gk * hi if k == 0 else hi_sum + gk * hi
    moe = jnp.concatenate([lo_sum, hi_sum], axis=1)
    o_ref[...] = _layer_norm(DEEPNORM_ALPHA * x1_ref[...] + moe, g_ref[...], b_ref[...])


def _combine(yg, gates, x1, g, b, tile0, prev):
    n_tok = x1.shape[0]
    row = lambda i: (i + tile0, 0)
    fixed = lambda i: (0, 0)
    in_specs = [
        pl.BlockSpec((TOP_K, ROW_TILE, PACKED), lambda i: (0, i, 0)),
        pl.BlockSpec((ROW_TILE, TOP_K), row),
        pl.BlockSpec((ROW_TILE, D_MODEL), row),
        pl.BlockSpec((1, D_MODEL), fixed),
        pl.BlockSpec((1, D_MODEL), fixed),
    ]
    args = [yg, gates, x1, g, b]
    aliases = {}
    if prev is not None:
        in_specs.append(pl.BlockSpec(memory_space=pl.ANY))
        args.append(prev)
        aliases = {len(args) - 1: 0}
    return pl.pallas_call(
        _combine_kernel,
        grid=(yg.shape[1] // ROW_TILE,),
        in_specs=in_specs,
        out_specs=pl.BlockSpec((ROW_TILE, D_MODEL), row),
        out_shape=jax.ShapeDtypeStruct((n_tok, D_MODEL), jnp.float32),
        input_output_aliases=aliases,
        compiler_params=_params("parallel"),
        name="combine_ln",
    )(*args)


def _route(top_idx, rank, counts, n_blocks):
    counts = counts[0, :N_EXPERTS]
    blocks_per = (counts + EXPERT_ROWS - 1) // EXPERT_ROWS
    blk_end = jnp.cumsum(blocks_per)
    blk_start = blk_end - blocks_per
    pos = rank
    for e in range(N_EXPERTS):
        pos = pos + jnp.where(top_idx == e, blk_start[e] * EXPERT_ROWS, 0)
    blk = jnp.arange(n_blocks, dtype=jnp.int32)
    block_expert = jnp.minimum(jnp.sum((blk[:, None] >= blk_end[None, :]).astype(jnp.int32), axis=1),
                               N_EXPERTS - 1)
    inside = blk - blk_start[block_expert]
    block_rows = jnp.clip(counts[block_expert] - inside * EXPERT_ROWS, 0, EXPERT_ROWS).astype(jnp.int32)
    n_valid = blk_end[-1:].astype(jnp.int32)
    return pos, block_expert.astype(jnp.int32), block_rows, n_valid


def _moe_groups(n_tok):
    unit = 2 * SC_ROWS * SC_WORKERS
    parts = sum(MOE_SPLIT)
    if n_tok % (parts * unit):
        return ((0, n_tok),)
    groups, t0 = [], 0
    for share in MOE_SPLIT:
        groups.append((t0, n_tok * share // parts))
        t0 += groups[-1][1]
    return tuple(groups)


def _rope_tables(seq):
    inv_freq = 1.0 / (ROPE_THETA ** (jnp.arange(0, HEAD_DIM, 2, dtype=jnp.float32) / HEAD_DIM))
    ang = jnp.arange(seq, dtype=jnp.float32)[:, None] * inv_freq[None, :]
    cos, sin = jnp.cos(ang), jnp.sin(ang)
    cos_t = jnp.tile(jnp.concatenate([cos, cos], axis=1), (1, 256 // HEAD_DIM))
    sin_t = jnp.tile(jnp.concatenate([-sin, sin], axis=1), (1, 256 // HEAD_DIM))
    return cos_t, sin_t


def kernel(x, w_in, b_in, sinks, w_out, b_out, ln1_g, ln1_b, w_router, b_router, w1, b1, w2, b2, ln2_g, ln2_b):
    batch, seq, d = x.shape
    assert d == D_MODEL and seq % ROW_TILE == 0 and seq % MOBA_BLOCK == 0 and w_in.shape[0] == DEPTH == 1
    assert (seq // WINDOW - 1) % SWA_UNROLL == 0
    n_tok = batch * seq
    bf16 = jnp.bfloat16
    x2 = x.reshape(n_tok, d)
    cos_t, sin_t = _rope_tables(seq)

    proj = _inproj(x2, w_in[0].astype(bf16), b_in[0].reshape(1, IN_WIDTH), cos_t, sin_t, seq)
    o_a = _swa(proj, sinks[0], batch, seq)
    o_b = _moba(proj, batch, seq)

    w_o = w_out[0].astype(bf16)
    w_r = jnp.pad(w_router[0], ((0, 0), (0, LANES - N_EXPERTS))).astype(bf16)
    b_r = jnp.pad(b_router[0], (0, LANES - N_EXPERTS), constant_values=NEG_BIG).reshape(1, LANES)
    x1, x1p, top_idx, gates, rank, counts = _outproj(
        o_a, o_b, w_o[:W_Q_SWA], w_o[W_Q_SWA:], b_out[0].reshape(1, d), x2,
        ln1_g[0].reshape(1, d), ln1_b[0].reshape(1, d), w_r, b_r)

    perm = _split_columns_perm()
    b1r = b1[0].reshape(N_EXPERTS, 1, D_FF, 2)
    b1g, b1l = b1r[..., 0], b1r[..., 1]
    b2r = b2[0].reshape(N_EXPERTS, 1, d)
    g2, be2 = ln2_g[0].reshape(1, d), ln2_b[0].reshape(1, d)

    out = None
    for grp, (t0, tok_g) in enumerate(_moe_groups(n_tok)):
        n_blocks = tok_g * TOP_K // EXPERT_ROWS + N_EXPERTS
        pos, block_expert, block_rows, n_valid = _route(
            top_idx[:, t0:t0 + tok_g], rank[:, t0:t0 + tok_g], counts[grp], n_blocks)
        pos3 = pos.reshape(TOP_K, tok_g // SC_ROWS, SC_ROWS).transpose(1, 0, 2)
        xg = _sc_scatter_rows(x1p, pos3, n_blocks * EXPERT_ROWS, t0)
        y = _experts(block_expert, block_rows, n_valid, xg, w1[0], perm, b1g, b1l, w2[0], b2r)
        yg = _sc_gather_rows(y, pos.reshape(tok_g * TOP_K // SC_ROWS, SC_ROWS))
        out = _combine(yg.reshape(TOP_K, tok_g, PACKED), gates, x1, g2, be2, t0 // ROW_TILE, out)
    return out.reshape(batch, seq, d)
```

```python
import functools

import jax
import jax.numpy as jnp
from jax import lax
from jax.experimental import pallas as pl
from jax.experimental.pallas import tpu as pltpu
from jax.experimental.pallas import tpu_sc as plsc

D_MODEL = 1024
HEAD_DIM = 64
N_HEADS_SWA = 8
N_KV_SWA = 2
WINDOW = 128
N_HEADS_MOBA = 8
MOBA_BLOCK = 256
MOBA_TOPK = 3
ROPE_THETA = 10000.0
N_EXPERTS = 32
TOP_K = 4
D_FF = 1024
SWIGLU_LIMIT = 7.0
SWIGLU_ALPHA = 1.702
LN_EPS = 1e-5
DEPTH = 1
DEEPNORM_ALPHA = (2 * DEPTH) ** 0.25

W_Q_SWA = N_HEADS_SWA * HEAD_DIM
W_KV_SWA = N_KV_SWA * HEAD_DIM
W_MOBA = N_HEADS_MOBA * HEAD_DIM
IN_WIDTH = W_Q_SWA + 2 * W_KV_SWA + 3 * W_MOBA
LANES = 128
COL_K_SWA = W_Q_SWA // LANES
COL_V_SWA = COL_K_SWA + 1
COL_Q_MOBA = COL_V_SWA + 1
COL_K_MOBA = COL_Q_MOBA + W_MOBA // LANES
COL_V_MOBA = COL_K_MOBA + W_MOBA // LANES

ROW_TILE = 512
EXPERT_ROWS = 512
PACKED = D_MODEL // 2
NEG_BIG = -1e30
LOG2E = 1.4426950408889634
MOBA_LOOKAHEAD = 1
SWA_UNROLL = 3
MOE_SPLIT = (3, 1)
VMEM_LIMIT = 48 * 1024 * 1024
EXPERT_VMEM_LIMIT = 60 * 1024 * 1024

SC_CORES = 2
SC_SUBCORES = 16
SC_WORKERS = SC_CORES * SC_SUBCORES
SC_ROWS = 64

_PROJ_CHUNKS = (
    (0, 256, 256, True), (256, 256, 256, True),
    (512, 256, 128, False),
    (768, 256, 256, True), (1024, 256, 256, True),
    (1280, 256, 256, False), (1536, 256, 256, False),
    (1792, 256, 0, False), (2048, 256, 0, False),
)


def _params(*sem):
    return pltpu.CompilerParams(dimension_semantics=sem, vmem_limit_bytes=VMEM_LIMIT)


def _pack_bf16_pairs(v):
    n = v.shape[1] // 2
    bits = lax.bitcast_convert_type(v.astype(jnp.bfloat16).astype(jnp.float32), jnp.uint32)
    word = (bits[:, :n] >> 16) | (bits[:, n:] & jnp.uint32(0xFFFF0000))
    return lax.bitcast_convert_type(word, jnp.int32)


def _unpack_bf16_pairs(word):
    bits = lax.bitcast_convert_type(word, jnp.uint32)
    lo = lax.bitcast_convert_type(bits << 16, jnp.float32)
    hi = lax.bitcast_convert_type(bits & jnp.uint32(0xFFFF0000), jnp.float32)
    return lo, hi


def _inproj_kernel(x_ref, w_ref, b_ref, cos_ref, sin_ref, o_ref):
    xb = x_ref[...].astype(jnp.bfloat16)
    for start, width, rope, scaled in _PROJ_CHUNKS:
        t = jnp.dot(xb, w_ref[:, start:start + width], preferred_element_type=jnp.float32)
        t = t + b_ref[:, start:start + width]
        if rope:
            lane = lax.broadcasted_iota(jnp.int32, t.shape, 1)
            first_half = (lane % HEAD_DIM) < (HEAD_DIM // 2)
            rot = jnp.where(first_half,
                            pltpu.roll(t, width - HEAD_DIM // 2, 1),
                            pltpu.roll(t, HEAD_DIM // 2, 1))
            roped = t * cos_ref[:, :width] + rot * sin_ref[:, :width]
            t = roped if rope == width else jnp.where(lane < rope, roped, t)
        if scaled:
            t = t * (HEAD_DIM ** -0.5 * LOG2E)
        o_ref[:, start:start + width] = t.astype(o_ref.dtype)


def _inproj(x2, w_in, b_in, cos_t, sin_t, seq):
    n_tok = x2.shape[0]
    per_seq = seq // ROW_TILE
    return pl.pallas_call(
        _inproj_kernel,
        grid=(n_tok // ROW_TILE,),
        in_specs=[
            pl.BlockSpec((ROW_TILE, D_MODEL), lambda i: (i, 0)),
            pl.BlockSpec((D_MODEL, IN_WIDTH), lambda i: (0, 0)),
            pl.BlockSpec((1, IN_WIDTH), lambda i: (0, 0)),
            pl.BlockSpec((ROW_TILE, 256), lambda i: (i % per_seq, 0)),
            pl.BlockSpec((ROW_TILE, 256), lambda i: (i % per_seq, 0)),
        ],
        out_specs=pl.BlockSpec((ROW_TILE, IN_WIDTH), lambda i: (i, 0)),
        out_shape=jax.ShapeDtypeStruct((n_tok, IN_WIDTH), jnp.bfloat16),
        compiler_params=_params("parallel"),
        name="inproj_rope",
    )(x2, w_in, b_in, cos_t, sin_t)


def _swa_kernel(sink_ref, q_ref, k_ref, v_ref, o_ref, kd_ref, vd_ref, *, seq):
    lane = lax.broadcasted_iota(jnp.int32, (seq, LANES), 1)
    low = lane < HEAD_DIM
    k = k_ref[...].astype(jnp.float32)
    kr = pltpu.roll(k, HEAD_DIM, 1)
    kd_ref[0, WINDOW:, :] = jnp.where(low, k, kr).astype(kd_ref.dtype)
    kd_ref[1, WINDOW:, :] = jnp.where(low, kr, k).astype(kd_ref.dtype)
    v = v_ref[...].astype(jnp.float32)
    vd_ref[0, WINDOW:, :] = jnp.where(low, v, 1.0).astype(vd_ref.dtype)
    vd_ref[1, WINDOW:, :] = jnp.where(low, pltpu.roll(v, HEAD_DIM, 1), 1.0).astype(vd_ref.dtype)
    kd_ref[:, :WINDOW, :] = jnp.zeros((N_KV_SWA, WINDOW, LANES), kd_ref.dtype)
    vd_ref[:, :WINDOW, :] = jnp.zeros((N_KV_SWA, WINDOW, LANES), vd_ref.dtype)

    group = N_HEADS_SWA // N_KV_SWA
    rows = group * WINDOW
    r_in = lax.broadcasted_iota(jnp.int32, (rows, 2 * WINDOW), 0) % WINDOW
    c_id = lax.broadcasted_iota(jnp.int32, (rows, 2 * WINDOW), 1)
    band = (c_id > r_in) & (c_id <= r_in + WINDOW)
    head_in_group = lax.broadcasted_iota(jnp.int32, (rows, 1), 0) // WINDOW
    qlane_low = lax.broadcasted_iota(jnp.int32, (WINDOW, LANES), 1) < HEAD_DIM
    sinks = []
    for g in range(N_KV_SWA):
        col = jnp.zeros((rows, 1), jnp.float32)
        for j in range(group):
            col = jnp.where(head_in_group == j, sink_ref[g * group + j] * LOG2E, col)
        sinks.append(col)

    def scores(n, mask):
        r0 = pl.multiple_of(n * WINDOW, WINDOW)
        out = []
        for g in range(N_KV_SWA):
            parts = []
            for c in (2 * g, 2 * g + 1):
                qc = q_ref[pl.ds(r0, WINDOW), c * LANES:(c + 1) * LANES]
                zero = jnp.zeros_like(qc)
                parts.append(jnp.where(qlane_low, qc, zero))
                parts.append(jnp.where(qlane_low, zero, qc))
            qcat = jnp.concatenate(parts, axis=0)
            kd = kd_ref[g, pl.ds(r0, 2 * WINDOW), :]
            s = lax.dot_general(qcat, kd, (((1,), (1,)), ((), ())),
                                preferred_element_type=jnp.float32)
            out.append(jnp.where(mask, s, -jnp.inf))
        return out

    def finish(n, scored):
        r0 = pl.multiple_of(n * WINDOW, WINDOW)
        for g, s in enumerate(scored):
            vd = vd_ref[g, pl.ds(r0, 2 * WINDOW), :]
            m = jnp.maximum(jnp.max(s, axis=-1, keepdims=True), sinks[g])
            p = jnp.exp2(s - m)
            o = jnp.dot(p.astype(vd.dtype), vd, preferred_element_type=jnp.float32)
            sink_term = jnp.exp2(sinks[g] - m)
            for ci, c in enumerate((2 * g, 2 * g + 1)):
                o_lo = o[(2 * ci) * WINDOW:(2 * ci + 1) * WINDOW]
                o_hi = o[(2 * ci + 1) * WINDOW:(2 * ci + 2) * WINDOW]
                e_lo = sink_term[(2 * ci) * WINDOW:(2 * ci + 1) * WINDOW]
                e_hi = sink_term[(2 * ci + 1) * WINDOW:(2 * ci + 2) * WINDOW]
                num = jnp.where(qlane_low, o_lo, pltpu.roll(o_hi, HEAD_DIM, 1))
                den = jnp.where(qlane_low, pltpu.roll(o_lo, HEAD_DIM, 1) + e_lo, o_hi + e_hi)
                o_ref[pl.ds(r0, WINDOW), c * LANES:(c + 1) * LANES] = (num / den).astype(o_ref.dtype)

    finish(0, scores(0, band & (c_id >= WINDOW)))

    def body(it, carry):
        n0 = 1 + it * SWA_UNROLL
        scored = [scores(n0 + u, band) for u in range(SWA_UNROLL)]
        for u in range(SWA_UNROLL):
            finish(n0 + u, scored[u])
        return carry

    lax.fori_loop(0, (seq // WINDOW - 1) // SWA_UNROLL, body, 0)


def _swa(proj, sinks, batch, seq):
    grid_spec = pltpu.PrefetchScalarGridSpec(
        num_scalar_prefetch=0,
        grid=(batch,),
        in_specs=[
            pl.BlockSpec(memory_space=pltpu.SMEM),
            pl.BlockSpec((seq, W_Q_SWA), lambda b: (b, 0)),
            pl.BlockSpec((seq, LANES), lambda b: (b, COL_K_SWA)),
            pl.BlockSpec((seq, LANES), lambda b: (b, COL_V_SWA)),
        ],
        out_specs=pl.BlockSpec((seq, W_Q_SWA), lambda b: (b, 0)),
        scratch_shapes=[pltpu.VMEM((N_KV_SWA, WINDOW + seq, LANES), jnp.bfloat16),
                        pltpu.VMEM((N_KV_SWA, WINDOW + seq, LANES), jnp.bfloat16)],
    )
    return pl.pallas_call(
        functools.partial(_swa_kernel, seq=seq),
        grid_spec=grid_spec,
        out_shape=jax.ShapeDtypeStruct((batch * seq, W_Q_SWA), jnp.bfloat16),
        compiler_params=_params("parallel"),
        name="swa_sink_attention",
    )(sinks, proj, proj, proj)


def _moba_kernel(q_ref, k_ref, v_ref, o_ref, qa_ref, ka_ref, va_ref, *, seq):
    nblk = seq // MOBA_BLOCK
    pad_rows = 16
    q_all, k_all, v_all = q_ref[...], k_ref[...], v_ref[...]
    kmean = jnp.sum(k_all.astype(jnp.float32).reshape(nblk, MOBA_BLOCK, LANES), axis=1) / MOBA_BLOCK
    kmean = jnp.concatenate([kmean, jnp.zeros((pad_rows - nblk, LANES), jnp.float32)], axis=0)
    klane_low = lax.broadcasted_iota(jnp.int32, (pad_rows, LANES), 1) < HEAD_DIM
    lane = lax.broadcasted_iota(jnp.int32, (seq, LANES), 1)
    low = lane < HEAD_DIM
    key_blk = lax.broadcasted_iota(jnp.int32, (seq, LANES), 0) // MOBA_BLOCK
    j_id = lax.broadcasted_iota(jnp.int32, (pad_rows, seq), 0)
    q_blk = lax.broadcasted_iota(jnp.int32, (pad_rows, seq), 1) // MOBA_BLOCK
    eligible = j_id < q_blk

    for half in range(2):
        own = low if half == 0 else ~low
        spare = HEAD_DIM if half == 0 else 0
        ka_ref[half] = jnp.where(own, k_all, (lane - spare == key_blk).astype(k_all.dtype))
        va_ref[half] = jnp.where(own, v_all, jnp.ones_like(v_all))
        km = jnp.where(klane_low if half == 0 else ~klane_low, kmean, 0.0).astype(jnp.bfloat16)
        gate = lax.dot_general(km, q_all, (((1,), (1,)), ((), ())),
                               preferred_element_type=jnp.float32)
        gate = jnp.where(eligible, gate, -jnp.inf)
        beaten = jnp.zeros((pad_rows, seq), jnp.int32)
        for jp in range(nblk):
            row = gate[jp:jp + 1, :]
            wins = (row > gate) | ((row == gate) & (jp < j_id))
            beaten = beaten + wins.astype(jnp.int32)
        dropped = eligible & (beaten >= MOBA_TOPK)
        bias = jnp.where(dropped, NEG_BIG, 0.0)
        pieces = [bias, jnp.zeros((LANES - spare - pad_rows, seq), jnp.float32)]
        if spare:
            pieces.insert(0, jnp.zeros((spare, seq), jnp.float32))
        bias_t = jnp.concatenate(pieces, axis=0).T
        qa_ref[half] = jnp.where(own, q_all, bias_t.astype(q_all.dtype))

    qlane_low = lax.broadcasted_iota(jnp.int32, (MOBA_BLOCK, LANES), 1) < HEAD_DIM
    rr = lax.broadcasted_iota(jnp.int32, (MOBA_BLOCK, MOBA_BLOCK), 0)
    cc = lax.broadcasted_iota(jnp.int32, (MOBA_BLOCK, MOBA_BLOCK), 1)
    causal = cc <= rr

    def scores(i, half):
        r0 = i * MOBA_BLOCK
        n_keys = r0 + MOBA_BLOCK
        s = lax.dot_general(qa_ref[half, r0:n_keys, :], ka_ref[half, 0:n_keys, :], (((1,), (1,)), ((), ())),
                            preferred_element_type=jnp.float32)
        own_blk = jnp.where(causal, s[:, r0:n_keys], NEG_BIG)
        return jnp.concatenate([s[:, :r0], own_blk], axis=1) if i else own_blk

    units = [(i, half) for i in range(nblk) for half in range(2)]
    pending = [scores(*u) for u in units[:MOBA_LOOKAHEAD]]
    acc = []
    for n, (i, half) in enumerate(units):
        s = pending.pop(0)
        if n + MOBA_LOOKAHEAD < len(units):
            pending.append(scores(*units[n + MOBA_LOOKAHEAD]))
        n_keys = (i + 1) * MOBA_BLOCK
        m = jnp.max(s, axis=-1, keepdims=True)
        p = jnp.exp2(s - m).astype(jnp.bfloat16)
        acc.append(jnp.dot(p, va_ref[half, 0:n_keys, :], preferred_element_type=jnp.float32))
        if half == 1:
            num = jnp.where(qlane_low, acc[0], acc[1])
            den = pltpu.roll(jnp.where(qlane_low, acc[1], acc[0]), HEAD_DIM, 1)
            o_ref[i * MOBA_BLOCK:n_keys, :] = (num / den).astype(o_ref.dtype)
            acc = []


def _moba(proj, batch, seq):
    pairs = W_MOBA // LANES
    return pl.pallas_call(
        functools.partial(_moba_kernel, seq=seq),
        grid=(batch, pairs),
        in_specs=[
            pl.BlockSpec((seq, LANES), lambda b, p: (b, COL_Q_MOBA + p)),
            pl.BlockSpec((seq, LANES), lambda b, p: (b, COL_K_MOBA + p)),
            pl.BlockSpec((seq, LANES), lambda b, p: (b, COL_V_MOBA + p)),
        ],
        out_specs=pl.BlockSpec((seq, LANES), lambda b, p: (b, p)),
        out_shape=jax.ShapeDtypeStruct((batch * seq, W_MOBA), jnp.bfloat16),
        scratch_shapes=[pltpu.VMEM((2, seq, LANES), jnp.bfloat16)] * 3,
        compiler_params=_params("parallel", "parallel"),
        name="moba_attention",
    )(proj, proj, proj)


def _layer_norm(h, g, b):
    mu = jnp.mean(h, axis=-1, keepdims=True)
    d = h - mu
    var = jnp.mean(d * d, axis=-1, keepdims=True)
    return d * lax.rsqrt(var + LN_EPS) * g + b


def _outproj_kernel(oa_ref, ob_ref, wa_ref, wb_ref, bo_ref, x_ref, g_ref, b_ref, wr_ref, br_ref, tri_ref,
                    x1_ref, x1p_ref, idx_ref, gate_ref, rank_ref, count_ref, running_ref):
    @pl.when(pl.program_id(0) == 0)
    def _():
        running_ref[...] = jnp.zeros(running_ref.shape, running_ref.dtype)

    half_rows = ROW_TILE // 2
    halves = [slice(h * half_rows, (h + 1) * half_rows) for h in range(2)]

    def project(rows):
        mix = jnp.dot(oa_ref[rows, :], wa_ref[...], preferred_element_type=jnp.float32)
        mix = mix + jnp.dot(ob_ref[rows, :], wb_ref[...], preferred_element_type=jnp.float32)
        return mix + bo_ref[...]

    def normalise(rows, mix):
        x1 = _layer_norm(DEEPNORM_ALPHA * x_ref[rows, :] + mix, g_ref[...], b_ref[...])
        x1_ref[rows, :] = x1
        x1p_ref[rows, :] = _pack_bf16_pairs(x1)
        return jnp.dot(x1.astype(jnp.bfloat16), wr_ref[...], preferred_element_type=jnp.float32) + br_ref[...]

    mixes = [project(rows) for rows in halves]
    logits = jnp.concatenate([normalise(rows, mix) for rows, mix in zip(halves, mixes)], axis=0)
    lane = lax.broadcasted_iota(jnp.int32, logits.shape, 1)
    idx_out = jnp.zeros(logits.shape, jnp.int32)
    val_out = jnp.zeros(logits.shape, jnp.float32)
    top = None
    total = None
    onehots = []
    for k in range(TOP_K):
        m = jnp.max(logits, axis=-1, keepdims=True)
        idx = jnp.min(jnp.where(logits == m, lane, LANES), axis=-1, keepdims=True)
        picked = lane == idx
        onehots.append(picked)
        logits = jnp.where(picked, -jnp.inf, logits)
        if k == 0:
            top = m
        e = jnp.exp(m - top)
        total = e if k == 0 else total + e
        idx_out = jnp.where(lane == k, idx, idx_out)
        val_out = jnp.where(lane == k, e, val_out)
    idx_ref[...] = idx_out.T[:TOP_K, :]
    gate_ref[...] = (val_out / total)[:, :TOP_K]

    picks = jnp.concatenate([p.astype(jnp.bfloat16) for p in onehots], axis=1)
    before = jnp.dot(tri_ref[...], picks, preferred_element_type=jnp.float32)
    base = running_ref[...]
    rank_out = jnp.zeros(logits.shape, jnp.float32)
    for k in range(TOP_K):
        pk = onehots[k].astype(jnp.float32)
        here = before[:, k * LANES:(k + 1) * LANES] + base
        rank_k = jnp.sum(pk * here, axis=-1, keepdims=True)
        rank_out = jnp.where(lane == k, rank_k, rank_out)
        base = base + jnp.sum(pk, axis=0, keepdims=True)
    running_ref[...] = base
    rank_ref[...] = rank_out.astype(jnp.int32).T[:TOP_K, :]
    count_ref[...] = base.astype(jnp.int32)


def _outproj(o_a, o_b, w_a, w_b, b_out, x2, g, b, w_r, b_r, t0, n_tok):
    tile0 = t0 // ROW_TILE
    src = lambda i: (i + tile0, 0)
    row = lambda i: (i, 0)
    fixed = lambda i: (0, 0)
    r = lax.broadcasted_iota(jnp.int32, (ROW_TILE, ROW_TILE), 0)
    c = lax.broadcasted_iota(jnp.int32, (ROW_TILE, ROW_TILE), 1)
    tri = (c < r).astype(jnp.bfloat16)
    return pl.pallas_call(
        _outproj_kernel,
        grid=(n_tok // ROW_TILE,),
        in_specs=[
            pl.BlockSpec((ROW_TILE, W_Q_SWA), src),
            pl.BlockSpec((ROW_TILE, W_MOBA), src),
            pl.BlockSpec((W_Q_SWA, D_MODEL), fixed),
            pl.BlockSpec((W_MOBA, D_MODEL), fixed),
            pl.BlockSpec((1, D_MODEL), fixed),
            pl.BlockSpec((ROW_TILE, D_MODEL), src),
            pl.BlockSpec((1, D_MODEL), fixed),
            pl.BlockSpec((1, D_MODEL), fixed),
            pl.BlockSpec((D_MODEL, LANES), fixed),
            pl.BlockSpec((1, LANES), fixed),
            pl.BlockSpec((ROW_TILE, ROW_TILE), fixed),
        ],
        out_specs=[
            pl.BlockSpec((ROW_TILE, D_MODEL), row),
            pl.BlockSpec((ROW_TILE, PACKED), row),
            pl.BlockSpec((TOP_K, ROW_TILE), lambda i: (0, i)),
            pl.BlockSpec((ROW_TILE, TOP_K), row),
            pl.BlockSpec((TOP_K, ROW_TILE), lambda i: (0, i)),
            pl.BlockSpec((1, LANES), fixed),
        ],
        out_shape=[
            jax.ShapeDtypeStruct((n_tok, D_MODEL), jnp.float32),
            jax.ShapeDtypeStruct((n_tok, PACKED), jnp.int32),
            jax.ShapeDtypeStruct((TOP_K, n_tok), jnp.int32),
            jax.ShapeDtypeStruct((n_tok, TOP_K), jnp.float32),
            jax.ShapeDtypeStruct((TOP_K, n_tok), jnp.int32),
            jax.ShapeDtypeStruct((1, LANES), jnp.int32),
        ],
        scratch_shapes=[pltpu.VMEM((1, LANES), jnp.float32)],
        compiler_params=_params("arbitrary"),
        name="outproj_ln_router",
    )(o_a, o_b, w_a, w_b, b_out, x2, g, b, w_r, b_r, tri)


def _sc_worker_id():
    return lax.axis_index("s") * SC_CORES + lax.axis_index("c")


def _sc_scatter_rows(rows, pos3, n_out):
    n_tok = pos3.shape[0] * SC_ROWS
    steps = n_tok // SC_ROWS // SC_WORKERS
    assert steps * SC_ROWS * SC_WORKERS == n_tok and steps % 2 == 0
    mesh = plsc.VectorSubcoreMesh(core_axis_name="c", subcore_axis_name="s")

    @functools.partial(
        pl.kernel, mesh=mesh,
        out_type=jax.ShapeDtypeStruct((n_out, PACKED), jnp.int32),
        scratch_types=[pltpu.VMEM((2, TOP_K, SC_ROWS), jnp.int32), pltpu.VMEM((2, SC_ROWS, PACKED), jnp.int32),
                       pltpu.SemaphoreType.DMA((2,)), pltpu.SemaphoreType.DMA((2,))],
        name="sc_dispatch_scatter")
    def scatter(x_hbm, pos_hbm, out_hbm, idx_v, rows_v, sem_ld, sem_st):
        base = _sc_worker_id() * steps

        def loads(s, b):
            return (pltpu.make_async_copy(pos_hbm.at[base + s], idx_v.at[b], sem_ld.at[b]),
                    pltpu.make_async_copy(x_hbm.at[pl.ds((base + s) * SC_ROWS, SC_ROWS)], rows_v.at[b],
                                          sem_ld.at[b]))

        def stores(b):
            return [pltpu.make_async_copy(rows_v.at[b], out_hbm.at[idx_v.at[b, k]], sem_st.at[b])
                    for k in range(TOP_K)]

        for c in loads(0, 0):
            c.start()

        @pl.loop(0, steps, step=2)
        def _(s0):
            for b in range(2):
                s = s0 + b
                for c in loads(s, b):
                    c.wait()

                @pl.when(s >= 1)
                def _():
                    for c in stores(1 - b):
                        c.wait()

                @pl.when(s + 1 < steps)
                def _():
                    for c in loads(s + 1, 1 - b):
                        c.start()

                for c in stores(b):
                    c.start()

        for c in stores(1):
            c.wait()

    return scatter(rows, pos3)


def _sc_gather_rows(table, idx2):
    n_blk = idx2.shape[0]
    steps = n_blk // SC_WORKERS
    assert steps * SC_WORKERS == n_blk and steps % 2 == 0 and idx2.shape[1] == SC_ROWS
    mesh = plsc.VectorSubcoreMesh(core_axis_name="c", subcore_axis_name="s")

    @functools.partial(
        pl.kernel, mesh=mesh,
        out_type=jax.ShapeDtypeStruct((n_blk * SC_ROWS, PACKED), jnp.int32),
        scratch_types=[pltpu.VMEM((steps, SC_ROWS), jnp.int32), pltpu.VMEM((2, SC_ROWS, PACKED), jnp.int32),
                       pltpu.SemaphoreType.DMA((2,)), pltpu.SemaphoreType.DMA((2,))],
        name="sc_combine_gather")
    def gather(y_hbm, idx_hbm, out_hbm, idx_v, rows_v, sem_ld, sem_st):
        base = _sc_worker_id() * steps
        pltpu.sync_copy(idx_hbm.at[pl.ds(base, steps)], idx_v)

        def fetch(s, b):
            return pltpu.make_async_copy(y_hbm.at[idx_v.at[s]], rows_v.at[b], sem_ld.at[b])

        def store(s, b):
            return pltpu.make_async_copy(rows_v.at[b], out_hbm.at[pl.ds((base + s) * SC_ROWS, SC_ROWS)],
                                         sem_st.at[b])

        fetch(0, 0).start()

        @pl.loop(0, steps, step=2)
        def _(s0):
            for b in range(2):
                s = s0 + b
                fetch(s, b).wait()

                @pl.when(s >= 1)
                def _():
                    store(s - 1, 1 - b).wait()

                @pl.when(s + 1 < steps)
                def _():
                    fetch(s + 1, 1 - b).start()

                store(s, b).start()

        store(steps - 1, 1).wait()

    return gather(table, idx2)


def _expert_kernel(be_ref, br_ref, slot_ref, next_ref, nv_ref, x_ref, w1_hbm, perm_ref, bg_ref, bl_ref, w2_hbm,
                   b2_ref, y_ref, w1f_ref, w2f_ref, wg_ref, wl_ref, w2b_ref, sem):
    i = pl.program_id(0)
    live = i < nv_ref[0]
    new_expert = (i == 0) | (be_ref[i] != be_ref[jnp.maximum(i - 1, 0)])

    def weight_copies(expert, slot):
        return (pltpu.make_async_copy(w1_hbm.at[expert], w1f_ref.at[slot], sem.at[0, slot]),
                pltpu.make_async_copy(w2_hbm.at[expert], w2f_ref.at[slot], sem.at[1, slot]))

    @pl.when(live & new_expert)
    def _():
        slot = slot_ref[i]

        @pl.when(i == 0)
        def _():
            for copy in weight_copies(be_ref[i], slot):
                copy.start()

        for copy in weight_copies(be_ref[i], slot):
            copy.wait()

        @pl.when(next_ref[i] >= 0)
        def _():
            for copy in weight_copies(next_ref[i], 1 - slot):
                copy.start()

        for c in range(2 * D_FF // 256):
            t = w1f_ref[slot, :, c * 256:(c + 1) * 256].astype(jnp.bfloat16)
            r = jnp.dot(t, perm_ref[...], preferred_element_type=jnp.float32)
            wg_ref[:, c * LANES:(c + 1) * LANES] = r[:, :LANES].astype(wg_ref.dtype)
            wl_ref[:, c * LANES:(c + 1) * LANES] = r[:, LANES:].astype(wl_ref.dtype)
        w2b_ref[...] = w2f_ref[slot].astype(w2b_ref.dtype)

    @pl.when(live)
    def _():
        valid = lax.broadcasted_iota(jnp.int32, x_ref.shape, 0) < br_ref[i]
        lo, hi = _unpack_bf16_pairs(jnp.where(valid, x_ref[...], 0))
        xb = jnp.concatenate([lo.astype(jnp.bfloat16), hi.astype(jnp.bfloat16)], axis=1)
        hg = jnp.dot(xb, wg_ref[...], preferred_element_type=jnp.float32) + bg_ref[0]
        hl = jnp.dot(xb, wl_ref[...], preferred_element_type=jnp.float32) + bl_ref[0]
        glu = jnp.minimum(hg, SWIGLU_LIMIT)
        lin = jnp.clip(hl, -SWIGLU_LIMIT, SWIGLU_LIMIT)
        act = glu * jax.nn.sigmoid(SWIGLU_ALPHA * glu) * (lin + 1.0)
        y = jnp.dot(act.astype(jnp.bfloat16), w2b_ref[...], preferred_element_type=jnp.float32) + b2_ref[0]
        y_ref[...] = _pack_bf16_pairs(y)

    @pl.when(jnp.logical_not(live))
    def _():
        y_ref[...] = jnp.zeros(y_ref.shape, y_ref.dtype)


def _experts(tables, xg, w1, perm, b1g, b1l, w2, b2):
    n_rows = xg.shape[0]
    n_blocks = n_rows // EXPERT_ROWS

    def row(i, be, br, slot, nxt, nv):
        return (jnp.minimum(i, nv[0] - 1), 0)

    def per_expert(i, be, br, slot, nxt, nv):
        return (be[i], 0, 0)

    grid_spec = pltpu.PrefetchScalarGridSpec(
        num_scalar_prefetch=5,
        grid=(n_blocks,),
        in_specs=[
            pl.BlockSpec((EXPERT_ROWS, PACKED), row),
            pl.BlockSpec(memory_space=pl.ANY),
            pl.BlockSpec((256, 256), lambda i, *_: (0, 0)),
            pl.BlockSpec((1, 1, D_FF), per_expert),
            pl.BlockSpec((1, 1, D_FF), per_expert),
            pl.BlockSpec(memory_space=pl.ANY),
            pl.BlockSpec((1, 1, D_MODEL), per_expert),
        ],
        out_specs=pl.BlockSpec((EXPERT_ROWS, PACKED), lambda i, *_: (i, 0)),
        scratch_shapes=[pltpu.VMEM((2, D_MODEL, 2 * D_FF), jnp.float32), pltpu.VMEM((2, D_FF, D_MODEL), jnp.float32),
                        pltpu.VMEM((D_MODEL, D_FF), jnp.bfloat16), pltpu.VMEM((D_MODEL, D_FF), jnp.bfloat16),
                        pltpu.VMEM((D_FF, D_MODEL), jnp.bfloat16), pltpu.SemaphoreType.DMA((2, 2))],
    )
    return pl.pallas_call(
        _expert_kernel,
        grid_spec=grid_spec,
        out_shape=jax.ShapeDtypeStruct((n_rows, PACKED), jnp.int32),
        compiler_params=pltpu.CompilerParams(dimension_semantics=("arbitrary",),
                                             vmem_limit_bytes=EXPERT_VMEM_LIMIT),
        name="grouped_experts",
    )(*tables, xg, w1, perm, b1g, b1l, w2, b2)


def _split_columns_perm():
    i = lax.broadcasted_iota(jnp.int32, (256, 256), 0)
    o = lax.broadcasted_iota(jnp.int32, (256, 256), 1)
    return (i == jnp.where(o < LANES, 2 * o, 2 * (o - LANES) + 1)).astype(jnp.bfloat16)


def _combine_kernel(y_ref, gate_ref, x1_ref, g_ref, b_ref, *rest):
    o_ref = rest[-1]
    gates = gate_ref[...]
    lo_sum = None
    hi_sum = None
    for k in range(TOP_K):
        lo, hi = _unpack_bf16_pairs(y_ref[k])
        gk = gates[:, k:k + 1]
        lo_sum = gk * lo if k == 0 else lo_sum + gk * lo
        hi_sum = gk * hi if k == 0 else hi_sum + gk * hi
    moe = jnp.concatenate([lo_sum, hi_sum], axis=1)
    o_ref[...] = _layer_norm(DEEPNORM_ALPHA * x1_ref[...] + moe, g_ref[...], b_ref[...])


def _combine(yg, gates, x1, g, b, tile0, n_tok, prev):
    row = lambda i: (i, 0)
    fixed = lambda i: (0, 0)
    in_specs = [
        pl.BlockSpec((TOP_K, ROW_TILE, PACKED), lambda i: (0, i, 0)),
        pl.BlockSpec((ROW_TILE, TOP_K), row),
        pl.BlockSpec((ROW_TILE, D_MODEL), row),
        pl.BlockSpec((1, D_MODEL), fixed),
        pl.BlockSpec((1, D_MODEL), fixed),
    ]
    args = [yg, gates, x1, g, b]
    aliases = {}
    if prev is not None:
        in_specs.append(pl.BlockSpec(memory_space=pl.ANY))
        args.append(prev)
        aliases = {len(args) - 1: 0}
    return pl.pallas_call(
        _combine_kernel,
        grid=(yg.shape[1] // ROW_TILE,),
        in_specs=in_specs,
        out_specs=pl.BlockSpec((ROW_TILE, D_MODEL), lambda i: (i + tile0, 0)),
        out_shape=jax.ShapeDtypeStruct((n_tok, D_MODEL), jnp.float32),
        input_output_aliases=aliases,
        compiler_params=_params("parallel"),
        name="combine_ln",
    )(*args)


def _route(top_idx, rank, counts, n_blocks):
    counts = counts[0, :N_EXPERTS]
    blocks_per = (counts + EXPERT_ROWS - 1) // EXPERT_ROWS
    blk_end = jnp.cumsum(blocks_per)
    blk_start = blk_end - blocks_per
    pos = rank
    for e in range(N_EXPERTS):
        pos = pos + jnp.where(top_idx == e, blk_start[e] * EXPERT_ROWS, 0)
    blk = jnp.arange(n_blocks, dtype=jnp.int32)
    block_expert = jnp.minimum(jnp.sum((blk[:, None] >= blk_end[None, :]).astype(jnp.int32), axis=1),
                               N_EXPERTS - 1)
    inside = blk - blk_start[block_expert]
    block_rows = jnp.clip(counts[block_expert] - inside * EXPERT_ROWS, 0, EXPERT_ROWS).astype(jnp.int32)
    n_valid = blk_end[-1:].astype(jnp.int32)
    experts = jnp.arange(N_EXPERTS, dtype=jnp.int32)
    has_rows = blocks_per > 0
    run_of = jnp.cumsum(has_rows.astype(jnp.int32)) - 1
    later = (experts[None, :] > experts[:, None]) & has_rows[None, :]
    next_of = jnp.min(jnp.where(later, experts[None, :], N_EXPERTS), axis=1)
    next_of = jnp.where(next_of == N_EXPERTS, -1, next_of)
    block_slot = (run_of[block_expert] % 2).astype(jnp.int32)
    block_next = next_of[block_expert].astype(jnp.int32)
    return pos, (block_expert.astype(jnp.int32), block_rows, block_slot, block_next, n_valid)


def _moe_groups(n_tok):
    unit = 2 * SC_ROWS * SC_WORKERS
    parts = sum(MOE_SPLIT)
    if n_tok % (parts * unit):
        return ((0, n_tok),)
    groups, t0 = [], 0
    for share in MOE_SPLIT:
        groups.append((t0, n_tok * share // parts))
        t0 += groups[-1][1]
    return tuple(groups)


def _rope_tables(seq):
    inv_freq = 1.0 / (ROPE_THETA ** (jnp.arange(0, HEAD_DIM, 2, dtype=jnp.float32) / HEAD_DIM))
    ang = jnp.arange(seq, dtype=jnp.float32)[:, None] * inv_freq[None, :]
    cos, sin = jnp.cos(ang), jnp.sin(ang)
    cos_t = jnp.tile(jnp.concatenate([cos, cos], axis=1), (1, 256 // HEAD_DIM))
    sin_t = jnp.tile(jnp.concatenate([-sin, sin], axis=1), (1, 256 // HEAD_DIM))
    return cos_t, sin_t


def kernel(x, w_in, b_in, sinks, w_out, b_out, ln1_g, ln1_b, w_router, b_router, w1, b1, w2, b2, ln2_g, ln2_b):
    batch, seq, d = x.shape
    assert d == D_MODEL and seq % ROW_TILE == 0 and seq % MOBA_BLOCK == 0 and w_in.shape[0] == DEPTH == 1
    assert (seq // WINDOW - 1) % SWA_UNROLL == 0
    n_tok = batch * seq
    bf16 = jnp.bfloat16
    x2 = x.reshape(n_tok, d)
    cos_t, sin_t = _rope_tables(seq)

    proj = _inproj(x2, w_in[0].astype(bf16), b_in[0].reshape(1, IN_WIDTH), cos_t, sin_t, seq)
    o_a = _swa(proj, sinks[0], batch, seq)
    o_b = _moba(proj, batch, seq)

    w_o = w_out[0].astype(bf16)
    w_oa, w_ob = w_o[:W_Q_SWA], w_o[W_Q_SWA:]
    w_r = jnp.pad(w_router[0], ((0, 0), (0, LANES - N_EXPERTS))).astype(bf16)
    b_r = jnp.pad(b_router[0], (0, LANES - N_EXPERTS), constant_values=NEG_BIG).reshape(1, LANES)
    b_o, g1, be1 = b_out[0].reshape(1, d), ln1_g[0].reshape(1, d), ln1_b[0].reshape(1, d)
    perm = _split_columns_perm()
    b1r = b1[0].reshape(N_EXPERTS, 1, D_FF, 2)
    b1g, b1l = b1r[..., 0], b1r[..., 1]
    b2r = b2[0].reshape(N_EXPERTS, 1, d)
    g2, be2 = ln2_g[0].reshape(1, d), ln2_b[0].reshape(1, d)

    out = None
    for t0, tok_g in _moe_groups(n_tok):
        x1, x1p, top_idx, gates, rank, counts = _outproj(o_a, o_b, w_oa, w_ob, b_o, x2, g1, be1, w_r, b_r, t0, tok_g)
        n_blocks = tok_g * TOP_K // EXPERT_ROWS + N_EXPERTS
        pos, tables = _route(top_idx, rank, counts, n_blocks)
        pos3 = pos.reshape(TOP_K, tok_g // SC_ROWS, SC_ROWS).transpose(1, 0, 2)
        xg = _sc_scatter_rows(x1p, pos3, n_blocks * EXPERT_ROWS)
        y = _experts(tables, xg, w1[0], perm, b1g, b1l, w2[0], b2r)
        yg = _sc_gather_rows(y, pos.reshape(tok_g * TOP_K // SC_ROWS, SC_ROWS))
        out = _combine(yg.reshape(TOP_K, tok_g, PACKED), gates, x1, g2, be2, t0 // ROW_TILE, n_tok, out)
    return out.reshape(batch, seq, d)
```

```python
import functools
import math

import jax
import jax.numpy as jnp
from jax import lax
from jax.experimental import pallas as pl
from jax.experimental.pallas import tpu as pltpu
from jax.experimental.pallas import tpu_sc as plsc

D_MODEL = 1024
HEAD_DIM = 64
N_HEADS_SWA = 8
N_KV_SWA = 2
WINDOW = 128
N_HEADS_MOBA = 8
MOBA_BLOCK = 256
MOBA_TOPK = 3
ROPE_THETA = 10000.0
N_EXPERTS = 32
TOP_K = 4
D_FF = 1024
SWIGLU_LIMIT = 7.0
SWIGLU_ALPHA = 1.702
LN_EPS = 1e-5
DEPTH = 1
DEEPNORM_ALPHA = (2 * DEPTH) ** 0.25

W_Q_SWA = N_HEADS_SWA * HEAD_DIM
W_KV_SWA = N_KV_SWA * HEAD_DIM
W_MOBA = N_HEADS_MOBA * HEAD_DIM
IN_WIDTH = W_Q_SWA + 2 * W_KV_SWA + 3 * W_MOBA
LANES = 128
COL_K_SWA = W_Q_SWA // LANES
COL_V_SWA = COL_K_SWA + 1
COL_Q_MOBA = COL_V_SWA + 1
COL_K_MOBA = COL_Q_MOBA + W_MOBA // LANES
COL_V_MOBA = COL_K_MOBA + W_MOBA // LANES

ROW_TILE = 512
EXPERT_ROWS = 1024
EXPERT_HALF = 512
PACKED = D_MODEL // 2
NEG_BIG = -1e30
LOG2E = 1.4426950408889634
MOBA_LOOKAHEAD = 1
SWA_UNROLL = 3
MOE_SPLIT = (3, 1)
VMEM_LIMIT = 48 * 1024 * 1024
EXPERT_VMEM_LIMIT = 60 * 1024 * 1024

SC_CORES = 2
SC_SUBCORES = 16
SC_WORKERS = SC_CORES * SC_SUBCORES
SC_ROWS = 64

_PROJ_CHUNKS = (
    (0, 256, 256, True), (256, 256, 256, True),
    (512, 256, 128, False),
    (768, 256, 256, True), (1024, 256, 256, True),
    (1280, 256, 256, False), (1536, 256, 256, False),
    (1792, 256, 0, False), (2048, 256, 0, False),
)


def _params(*sem):
    return pltpu.CompilerParams(dimension_semantics=sem, vmem_limit_bytes=VMEM_LIMIT)


def _pack_bf16_pairs(v):
    n = v.shape[1] // 2
    bits = lax.bitcast_convert_type(v.astype(jnp.bfloat16).astype(jnp.float32), jnp.uint32)
    word = (bits[:, :n] >> 16) | (bits[:, n:] & jnp.uint32(0xFFFF0000))
    return lax.bitcast_convert_type(word, jnp.int32)


def _unpack_bf16_pairs(word):
    bits = lax.bitcast_convert_type(word, jnp.uint32)
    lo = lax.bitcast_convert_type(bits << 16, jnp.float32)
    hi = lax.bitcast_convert_type(bits & jnp.uint32(0xFFFF0000), jnp.float32)
    return lo, hi


def _inproj_kernel(x_ref, w_ref, b_ref, cos_ref, sin_ref, o_ref):
    xb = x_ref[...].astype(jnp.bfloat16)
    for start, width, rope, scaled in _PROJ_CHUNKS:
        t = jnp.dot(xb, w_ref[:, start:start + width], preferred_element_type=jnp.float32)
        t = t + b_ref[:, start:start + width]
        if rope:
            lane = lax.broadcasted_iota(jnp.int32, t.shape, 1)
            first_half = (lane % HEAD_DIM) < (HEAD_DIM // 2)
            rot = jnp.where(first_half,
                            pltpu.roll(t, width - HEAD_DIM // 2, 1),
                            pltpu.roll(t, HEAD_DIM // 2, 1))
            roped = t * cos_ref[:, :width] + rot * sin_ref[:, :width]
            t = roped if rope == width else jnp.where(lane < rope, roped, t)
        if scaled:
            t = t * (HEAD_DIM ** -0.5 * LOG2E)
        o_ref[:, start:start + width] = t.astype(o_ref.dtype)


def _inproj(x2, w_in, b_in, cos_t, sin_t, seq):
    n_tok = x2.shape[0]
    per_seq = seq // ROW_TILE
    return pl.pallas_call(
        _inproj_kernel,
        grid=(n_tok // ROW_TILE,),
        in_specs=[
            pl.BlockSpec((ROW_TILE, D_MODEL), lambda i: (i, 0)),
            pl.BlockSpec((D_MODEL, IN_WIDTH), lambda i: (0, 0)),
            pl.BlockSpec((1, IN_WIDTH), lambda i: (0, 0)),
            pl.BlockSpec((ROW_TILE, 256), lambda i: (i % per_seq, 0)),
            pl.BlockSpec((ROW_TILE, 256), lambda i: (i % per_seq, 0)),
        ],
        out_specs=pl.BlockSpec((ROW_TILE, IN_WIDTH), lambda i: (i, 0)),
        out_shape=jax.ShapeDtypeStruct((n_tok, IN_WIDTH), jnp.bfloat16),
        compiler_params=_params("parallel"),
        name="inproj_rope",
    )(x2, w_in, b_in, cos_t, sin_t)


def _swa_kernel(sink_ref, q_ref, k_ref, v_ref, o_ref, kd_ref, vd_ref, *, seq):
    lane = lax.broadcasted_iota(jnp.int32, (seq, LANES), 1)
    low = lane < HEAD_DIM
    k = k_ref[...].astype(jnp.float32)
    kr = pltpu.roll(k, HEAD_DIM, 1)
    kd_ref[0, WINDOW:, :] = jnp.where(low, k, kr).astype(kd_ref.dtype)
    kd_ref[1, WINDOW:, :] = jnp.where(low, kr, k).astype(kd_ref.dtype)
    v = v_ref[...].astype(jnp.float32)
    vd_ref[0, WINDOW:, :] = jnp.where(low, v, 1.0).astype(vd_ref.dtype)
    vd_ref[1, WINDOW:, :] = jnp.where(low, pltpu.roll(v, HEAD_DIM, 1), 1.0).astype(vd_ref.dtype)
    kd_ref[:, :WINDOW, :] = jnp.zeros((N_KV_SWA, WINDOW, LANES), kd_ref.dtype)
    vd_ref[:, :WINDOW, :] = jnp.zeros((N_KV_SWA, WINDOW, LANES), vd_ref.dtype)

    group = N_HEADS_SWA // N_KV_SWA
    rows = group * WINDOW
    r_in = lax.broadcasted_iota(jnp.int32, (rows, 2 * WINDOW), 0) % WINDOW
    c_id = lax.broadcasted_iota(jnp.int32, (rows, 2 * WINDOW), 1)
    band = (c_id > r_in) & (c_id <= r_in + WINDOW)
    head_in_group = lax.broadcasted_iota(jnp.int32, (rows, 1), 0) // WINDOW
    qlane_low = lax.broadcasted_iota(jnp.int32, (WINDOW, LANES), 1) < HEAD_DIM
    sinks = []
    for g in range(N_KV_SWA):
        col = jnp.zeros((rows, 1), jnp.float32)
        for j in range(group):
            col = jnp.where(head_in_group == j, sink_ref[g * group + j] * LOG2E, col)
        sinks.append(col)

    def scores(n, mask):
        r0 = pl.multiple_of(n * WINDOW, WINDOW)
        out = []
        for g in range(N_KV_SWA):
            parts = []
            for c in (2 * g, 2 * g + 1):
                qc = q_ref[pl.ds(r0, WINDOW), c * LANES:(c + 1) * LANES]
                zero = jnp.zeros_like(qc)
                parts.append(jnp.where(qlane_low, qc, zero))
                parts.append(jnp.where(qlane_low, zero, qc))
            qcat = jnp.concatenate(parts, axis=0)
            kd = kd_ref[g, pl.ds(r0, 2 * WINDOW), :]
            s = lax.dot_general(qcat, kd, (((1,), (1,)), ((), ())),
                                preferred_element_type=jnp.float32)
            out.append(jnp.where(mask, s, -jnp.inf))
        return out

    def finish(n, scored):
        r0 = pl.multiple_of(n * WINDOW, WINDOW)
        for g, s in enumerate(scored):
            vd = vd_ref[g, pl.ds(r0, 2 * WINDOW), :]
            m = jnp.maximum(jnp.max(s, axis=-1, keepdims=True), sinks[g])
            p = jnp.exp2(s - m)
            o = jnp.dot(p.astype(vd.dtype), vd, preferred_element_type=jnp.float32)
            sink_term = jnp.exp2(sinks[g] - m)
            for ci, c in enumerate((2 * g, 2 * g + 1)):
                o_lo = o[(2 * ci) * WINDOW:(2 * ci + 1) * WINDOW]
                o_hi = o[(2 * ci + 1) * WINDOW:(2 * ci + 2) * WINDOW]
                e_lo = sink_term[(2 * ci) * WINDOW:(2 * ci + 1) * WINDOW]
                e_hi = sink_term[(2 * ci + 1) * WINDOW:(2 * ci + 2) * WINDOW]
                num = jnp.where(qlane_low, o_lo, pltpu.roll(o_hi, HEAD_DIM, 1))
                den = jnp.where(qlane_low, pltpu.roll(o_lo, HEAD_DIM, 1) + e_lo, o_hi + e_hi)
                o_ref[pl.ds(r0, WINDOW), c * LANES:(c + 1) * LANES] = (num / den).astype(o_ref.dtype)

    finish(0, scores(0, band & (c_id >= WINDOW)))

    def body(it, carry):
        n0 = 1 + it * SWA_UNROLL
        scored = [scores(n0 + u, band) for u in range(SWA_UNROLL)]
        for u in range(SWA_UNROLL):
            finish(n0 + u, scored[u])
        return carry

    lax.fori_loop(0, (seq // WINDOW - 1) // SWA_UNROLL, body, 0)


def _swa(proj, sinks, b0, batch, seq):
    grid_spec = pltpu.PrefetchScalarGridSpec(
        num_scalar_prefetch=0,
        grid=(batch,),
        in_specs=[
            pl.BlockSpec(memory_space=pltpu.SMEM),
            pl.BlockSpec((seq, W_Q_SWA), lambda b: (b + b0, 0)),
            pl.BlockSpec((seq, LANES), lambda b: (b + b0, COL_K_SWA)),
            pl.BlockSpec((seq, LANES), lambda b: (b + b0, COL_V_SWA)),
        ],
        out_specs=pl.BlockSpec((seq, W_Q_SWA), lambda b: (b, 0)),
        scratch_shapes=[pltpu.VMEM((N_KV_SWA, WINDOW + seq, LANES), jnp.bfloat16),
                        pltpu.VMEM((N_KV_SWA, WINDOW + seq, LANES), jnp.bfloat16)],
    )
    return pl.pallas_call(
        functools.partial(_swa_kernel, seq=seq),
        grid_spec=grid_spec,
        out_shape=jax.ShapeDtypeStruct((batch * seq, W_Q_SWA), jnp.bfloat16),
        compiler_params=_params("parallel"),
        name="swa_sink_attention",
    )(sinks, proj, proj, proj)


def _moba_kernel(q_ref, k_ref, v_ref, o_ref, qa_ref, ka_ref, va_ref, *, seq):
    nblk = seq // MOBA_BLOCK
    pad_rows = 16
    q_all, k_all, v_all = q_ref[...], k_ref[...], v_ref[...]
    kmean = jnp.sum(k_all.astype(jnp.float32).reshape(nblk, MOBA_BLOCK, LANES), axis=1) / MOBA_BLOCK
    kmean = jnp.concatenate([kmean, jnp.zeros((pad_rows - nblk, LANES), jnp.float32)], axis=0)
    klane_low = lax.broadcasted_iota(jnp.int32, (pad_rows, LANES), 1) < HEAD_DIM
    lane = lax.broadcasted_iota(jnp.int32, (seq, LANES), 1)
    low = lane < HEAD_DIM
    key_blk = lax.broadcasted_iota(jnp.int32, (seq, LANES), 0) // MOBA_BLOCK
    j_id = lax.broadcasted_iota(jnp.int32, (pad_rows, seq), 0)
    q_blk = lax.broadcasted_iota(jnp.int32, (pad_rows, seq), 1) // MOBA_BLOCK
    eligible = j_id < q_blk

    for half in range(2):
        own = low if half == 0 else ~low
        spare = HEAD_DIM if half == 0 else 0
        ka_ref[half] = jnp.where(own, k_all, (lane - spare == key_blk).astype(k_all.dtype))
        va_ref[half] = jnp.where(own, v_all, jnp.ones_like(v_all))
        km = jnp.where(klane_low if half == 0 else ~klane_low, kmean, 0.0).astype(jnp.bfloat16)
        gate = lax.dot_general(km, q_all, (((1,), (1,)), ((), ())),
                               preferred_element_type=jnp.float32)
        gate = jnp.where(eligible, gate, -jnp.inf)
        beaten = jnp.zeros((pad_rows, seq), jnp.int32)
        for jp in range(nblk):
            row = gate[jp:jp + 1, :]
            wins = (row > gate) | ((row == gate) & (jp < j_id))
            beaten = beaten + wins.astype(jnp.int32)
        dropped = eligible & (beaten >= MOBA_TOPK)
        bias = jnp.where(dropped, NEG_BIG, 0.0)
        pieces = [bias, jnp.zeros((LANES - spare - pad_rows, seq), jnp.float32)]
        if spare:
            pieces.insert(0, jnp.zeros((spare, seq), jnp.float32))
        bias_t = jnp.concatenate(pieces, axis=0).T
        qa_ref[half] = jnp.where(own, q_all, bias_t.astype(q_all.dtype))

    qlane_low = lax.broadcasted_iota(jnp.int32, (MOBA_BLOCK, LANES), 1) < HEAD_DIM
    rr = lax.broadcasted_iota(jnp.int32, (MOBA_BLOCK, MOBA_BLOCK), 0)
    cc = lax.broadcasted_iota(jnp.int32, (MOBA_BLOCK, MOBA_BLOCK), 1)
    causal = cc <= rr

    def scores(i, half):
        r0 = i * MOBA_BLOCK
        n_keys = r0 + MOBA_BLOCK
        s = lax.dot_general(qa_ref[half, r0:n_keys, :], ka_ref[half, 0:n_keys, :], (((1,), (1,)), ((), ())),
                            preferred_element_type=jnp.float32)
        own_blk = jnp.where(causal, s[:, r0:n_keys], NEG_BIG)
        return jnp.concatenate([s[:, :r0], own_blk], axis=1) if i else own_blk

    units = [(i, half) for i in range(nblk) for half in range(2)]
    pending = [scores(*u) for u in units[:MOBA_LOOKAHEAD]]
    acc = []
    for n, (i, half) in enumerate(units):
        s = pending.pop(0)
        if n + MOBA_LOOKAHEAD < len(units):
            pending.append(scores(*units[n + MOBA_LOOKAHEAD]))
        n_keys = (i + 1) * MOBA_BLOCK
        m = jnp.max(s, axis=-1, keepdims=True)
        p = jnp.exp2(s - m).astype(jnp.bfloat16)
        acc.append(jnp.dot(p, va_ref[half, 0:n_keys, :], preferred_element_type=jnp.float32))
        if half == 1:
            num = jnp.where(qlane_low, acc[0], acc[1])
            den = pltpu.roll(jnp.where(qlane_low, acc[1], acc[0]), HEAD_DIM, 1)
            o_ref[i * MOBA_BLOCK:n_keys, :] = (num / den).astype(o_ref.dtype)
            acc = []


def _moba(proj, b0, batch, seq):
    pairs = W_MOBA // LANES
    return pl.pallas_call(
        functools.partial(_moba_kernel, seq=seq),
        grid=(batch, pairs),
        in_specs=[
            pl.BlockSpec((seq, LANES), lambda b, p: (b + b0, COL_Q_MOBA + p)),
            pl.BlockSpec((seq, LANES), lambda b, p: (b + b0, COL_K_MOBA + p)),
            pl.BlockSpec((seq, LANES), lambda b, p: (b + b0, COL_V_MOBA + p)),
        ],
        out_specs=pl.BlockSpec((seq, LANES), lambda b, p: (b, p)),
        out_shape=jax.ShapeDtypeStruct((batch * seq, W_MOBA), jnp.bfloat16),
        scratch_shapes=[pltpu.VMEM((2, seq, LANES), jnp.bfloat16)] * 3,
        compiler_params=_params("parallel", "parallel"),
        name="moba_attention",
    )(proj, proj, proj)


def _layer_norm(h, g, b):
    mu = jnp.mean(h, axis=-1, keepdims=True)
    d = h - mu
    var = jnp.mean(d * d, axis=-1, keepdims=True)
    return d * lax.rsqrt(var + LN_EPS) * g + b


def _outproj_kernel(oa_ref, ob_ref, wa_ref, wb_ref, bo_ref, x_ref, g_ref, b_ref, wr_ref, br_ref, tri_ref,
                    x1_ref, x1p_ref, idx_ref, gate_ref, rank_ref, count_ref, running_ref):
    @pl.when(pl.program_id(0) == 0)
    def _():
        running_ref[...] = jnp.zeros(running_ref.shape, running_ref.dtype)

    half_rows = ROW_TILE // 2
    halves = [slice(h * half_rows, (h + 1) * half_rows) for h in range(2)]

    def project(rows):
        mix = jnp.dot(oa_ref[rows, :], wa_ref[...], preferred_element_type=jnp.float32)
        mix = mix + jnp.dot(ob_ref[rows, :], wb_ref[...], preferred_element_type=jnp.float32)
        return mix + bo_ref[...]

    def normalise(rows, mix):
        x1 = _layer_norm(DEEPNORM_ALPHA * x_ref[rows, :] + mix, g_ref[...], b_ref[...])
        x1_ref[rows, :] = x1
        x1p_ref[rows, :] = _pack_bf16_pairs(x1)
        return jnp.dot(x1.astype(jnp.bfloat16), wr_ref[...], preferred_element_type=jnp.float32) + br_ref[...]

    mixes = [project(rows) for rows in halves]
    logits = jnp.concatenate([normalise(rows, mix) for rows, mix in zip(halves, mixes)], axis=0)
    lane = lax.broadcasted_iota(jnp.int32, logits.shape, 1)
    idx_out = jnp.zeros(logits.shape, jnp.int32)
    val_out = jnp.zeros(logits.shape, jnp.float32)
    top = None
    total = None
    onehots = []
    for k in range(TOP_K):
        m = jnp.max(logits, axis=-1, keepdims=True)
        idx = jnp.min(jnp.where(logits == m, lane, LANES), axis=-1, keepdims=True)
        picked = lane == idx
        onehots.append(picked)
        logits = jnp.where(picked, -jnp.inf, logits)
        if k == 0:
            top = m
        e = jnp.exp(m - top)
        total = e if k == 0 else total + e
        idx_out = jnp.where(lane == k, idx, idx_out)
        val_out = jnp.where(lane == k, e, val_out)
    idx_ref[...] = idx_out.T[:TOP_K, :]
    gate_ref[...] = (val_out / total)[:, :TOP_K]

    picks = jnp.concatenate([p.astype(jnp.bfloat16) for p in onehots], axis=1)
    before = jnp.dot(tri_ref[...], picks, preferred_element_type=jnp.float32)
    base = running_ref[...]
    rank_out = jnp.zeros(logits.shape, jnp.float32)
    for k in range(TOP_K):
        pk = onehots[k].astype(jnp.float32)
        here = before[:, k * LANES:(k + 1) * LANES] + base
        rank_k = jnp.sum(pk * here, axis=-1, keepdims=True)
        rank_out = jnp.where(lane == k, rank_k, rank_out)
        base = base + jnp.sum(pk, axis=0, keepdims=True)
    running_ref[...] = base
    rank_ref[...] = rank_out.astype(jnp.int32).T[:TOP_K, :]
    count_ref[...] = base.astype(jnp.int32)


def _outproj(o_a, o_b, w_a, w_b, b_out, x2, g, b, w_r, b_r, t0, n_tok):
    tile0 = t0 // ROW_TILE
    src = lambda i: (i + tile0, 0)
    row = lambda i: (i, 0)
    fixed = lambda i: (0, 0)
    r = lax.broadcasted_iota(jnp.int32, (ROW_TILE, ROW_TILE), 0)
    c = lax.broadcasted_iota(jnp.int32, (ROW_TILE, ROW_TILE), 1)
    tri = (c < r).astype(jnp.bfloat16)
    return pl.pallas_call(
        _outproj_kernel,
        grid=(n_tok // ROW_TILE,),
        in_specs=[
            pl.BlockSpec((ROW_TILE, W_Q_SWA), row),
            pl.BlockSpec((ROW_TILE, W_MOBA), row),
            pl.BlockSpec((W_Q_SWA, D_MODEL), fixed),
            pl.BlockSpec((W_MOBA, D_MODEL), fixed),
            pl.BlockSpec((1, D_MODEL), fixed),
            pl.BlockSpec((ROW_TILE, D_MODEL), src),
            pl.BlockSpec((1, D_MODEL), fixed),
            pl.BlockSpec((1, D_MODEL), fixed),
            pl.BlockSpec((D_MODEL, LANES), fixed),
            pl.BlockSpec((1, LANES), fixed),
            pl.BlockSpec((ROW_TILE, ROW_TILE), fixed),
        ],
        out_specs=[
            pl.BlockSpec((ROW_TILE, D_MODEL), row),
            pl.BlockSpec((ROW_TILE, PACKED), row),
            pl.BlockSpec((TOP_K, ROW_TILE), lambda i: (0, i)),
            pl.BlockSpec((ROW_TILE, TOP_K), row),
            pl.BlockSpec((TOP_K, ROW_TILE), lambda i: (0, i)),
            pl.BlockSpec((1, LANES), fixed),
        ],
        out_shape=[
            jax.ShapeDtypeStruct((n_tok, D_MODEL), jnp.float32),
            jax.ShapeDtypeStruct((n_tok, PACKED), jnp.int32),
            jax.ShapeDtypeStruct((TOP_K, n_tok), jnp.int32),
            jax.ShapeDtypeStruct((n_tok, TOP_K), jnp.float32),
            jax.ShapeDtypeStruct((TOP_K, n_tok), jnp.int32),
            jax.ShapeDtypeStruct((1, LANES), jnp.int32),
        ],
        scratch_shapes=[pltpu.VMEM((1, LANES), jnp.float32)],
        compiler_params=_params("arbitrary"),
        name="outproj_ln_router",
    )(o_a, o_b, w_a, w_b, b_out, x2, g, b, w_r, b_r, tri)


def _sc_worker_id():
    return lax.axis_index("s") * SC_CORES + lax.axis_index("c")


def _sc_scatter_rows(rows, pos3, n_out):
    n_tok = pos3.shape[0] * SC_ROWS
    steps = n_tok // SC_ROWS // SC_WORKERS
    assert steps * SC_ROWS * SC_WORKERS == n_tok and steps % 2 == 0
    mesh = plsc.VectorSubcoreMesh(core_axis_name="c", subcore_axis_name="s")

    @functools.partial(
        pl.kernel, mesh=mesh,
        out_type=jax.ShapeDtypeStruct((n_out, PACKED), jnp.int32),
        scratch_types=[pltpu.VMEM((2, TOP_K, SC_ROWS), jnp.int32), pltpu.VMEM((2, SC_ROWS, PACKED), jnp.int32),
                       pltpu.SemaphoreType.DMA((2,)), pltpu.SemaphoreType.DMA((2,))],
        name="sc_dispatch_scatter")
    def scatter(x_hbm, pos_hbm, out_hbm, idx_v, rows_v, sem_ld, sem_st):
        base = _sc_worker_id() * steps

        def loads(s, b):
            return (pltpu.make_async_copy(pos_hbm.at[base + s], idx_v.at[b], sem_ld.at[b]),
                    pltpu.make_async_copy(x_hbm.at[pl.ds((base + s) * SC_ROWS, SC_ROWS)], rows_v.at[b],
                                          sem_ld.at[b]))

        def stores(b):
            return [pltpu.make_async_copy(rows_v.at[b], out_hbm.at[idx_v.at[b, k]], sem_st.at[b])
                    for k in range(TOP_K)]

        for c in loads(0, 0):
            c.start()

        @pl.loop(0, steps, step=2)
        def _(s0):
            for b in range(2):
                s = s0 + b
                for c in loads(s, b):
                    c.wait()

                @pl.when(s >= 1)
                def _():
                    for c in stores(1 - b):
                        c.wait()

                @pl.when(s + 1 < steps)
                def _():
                    for c in loads(s + 1, 1 - b):
                        c.start()

                for c in stores(b):
                    c.start()

        for c in stores(1):
            c.wait()

    return scatter(rows, pos3)


def _sc_gather_rows(table, idx2):
    n_blk = idx2.shape[0]
    steps = n_blk // SC_WORKERS
    assert steps * SC_WORKERS == n_blk and steps % 2 == 0 and idx2.shape[1] == SC_ROWS
    mesh = plsc.VectorSubcoreMesh(core_axis_name="c", subcore_axis_name="s")

    @functools.partial(
        pl.kernel, mesh=mesh,
        out_type=jax.ShapeDtypeStruct((n_blk * SC_ROWS, PACKED), jnp.int32),
        scratch_types=[pltpu.VMEM((steps, SC_ROWS), jnp.int32), pltpu.VMEM((2, SC_ROWS, PACKED), jnp.int32),
                       pltpu.SemaphoreType.DMA((2,)), pltpu.SemaphoreType.DMA((2,))],
        name="sc_combine_gather")
    def gather(y_hbm, idx_hbm, out_hbm, idx_v, rows_v, sem_ld, sem_st):
        base = _sc_worker_id() * steps
        pltpu.sync_copy(idx_hbm.at[pl.ds(base, steps)], idx_v)

        def fetch(s, b):
            return pltpu.make_async_copy(y_hbm.at[idx_v.at[s]], rows_v.at[b], sem_ld.at[b])

        def store(s, b):
            return pltpu.make_async_copy(rows_v.at[b], out_hbm.at[pl.ds((base + s) * SC_ROWS, SC_ROWS)],
                                         sem_st.at[b])

        fetch(0, 0).start()

        @pl.loop(0, steps, step=2)
        def _(s0):
            for b in range(2):
                s = s0 + b
                fetch(s, b).wait()

                @pl.when(s >= 1)
                def _():
                    store(s - 1, 1 - b).wait()

                @pl.when(s + 1 < steps)
                def _():
                    fetch(s + 1, 1 - b).start()

                store(s, b).start()

        store(steps - 1, 1).wait()

    return gather(table, idx2)


def _expert_kernel(be_ref, br_ref, slot_ref, next_ref, nv_ref, x_ref, w1_hbm, perm_ref, bg_ref, bl_ref, w2_hbm,
                   b2_ref, y_ref, w1f_ref, w2f_ref, wg_ref, wl_ref, w2b_ref, sem):
    i = pl.program_id(0)
    live = i < nv_ref[0]
    new_expert = (i == 0) | (be_ref[i] != be_ref[jnp.maximum(i - 1, 0)])

    def weight_copies(expert, slot):
        return (pltpu.make_async_copy(w1_hbm.at[expert], w1f_ref.at[slot], sem.at[0, slot]),
                pltpu.make_async_copy(w2_hbm.at[expert], w2f_ref.at[slot], sem.at[1, slot]))

    @pl.when(live & new_expert)
    def _():
        slot = slot_ref[i]

        @pl.when(i == 0)
        def _():
            for copy in weight_copies(be_ref[i], slot):
                copy.start()

        for copy in weight_copies(be_ref[i], slot):
            copy.wait()

        @pl.when(next_ref[i] >= 0)
        def _():
            for copy in weight_copies(next_ref[i], 1 - slot):
                copy.start()

        for c in range(2 * D_FF // 256):
            t = w1f_ref[slot, :, c * 256:(c + 1) * 256].astype(jnp.bfloat16)
            r = jnp.dot(t, perm_ref[...], preferred_element_type=jnp.float32)
            wg_ref[:, c * LANES:(c + 1) * LANES] = r[:, :LANES].astype(wg_ref.dtype)
            wl_ref[:, c * LANES:(c + 1) * LANES] = r[:, LANES:].astype(wl_ref.dtype)
        w2b_ref[...] = w2f_ref[slot].astype(w2b_ref.dtype)

    def mlp(rows):
        valid = lax.broadcasted_iota(jnp.int32, (rows, PACKED), 0) < br_ref[i]
        lo, hi = _unpack_bf16_pairs(jnp.where(valid, x_ref[0:rows, :], 0))
        xb = jnp.concatenate([lo.astype(jnp.bfloat16), hi.astype(jnp.bfloat16)], axis=1)
        hg = jnp.dot(xb, wg_ref[...], preferred_element_type=jnp.float32) + bg_ref[0]
        hl = jnp.dot(xb, wl_ref[...], preferred_element_type=jnp.float32) + bl_ref[0]
        glu = jnp.minimum(hg, SWIGLU_LIMIT)
        lin = jnp.clip(hl, -SWIGLU_LIMIT, SWIGLU_LIMIT)
        act = glu * jax.nn.sigmoid(SWIGLU_ALPHA * glu) * (lin + 1.0)
        y = jnp.dot(act.astype(jnp.bfloat16), w2b_ref[...], preferred_element_type=jnp.float32) + b2_ref[0]
        y_ref[0:rows, :] = _pack_bf16_pairs(y)

    @pl.when(live & (br_ref[i] > EXPERT_HALF))
    def _():
        mlp(EXPERT_ROWS)

    @pl.when(live & (br_ref[i] <= EXPERT_HALF))
    def _():
        mlp(EXPERT_HALF)
        y_ref[EXPERT_HALF:, :] = jnp.zeros((EXPERT_ROWS - EXPERT_HALF, PACKED), y_ref.dtype)

    @pl.when(jnp.logical_not(live))
    def _():
        y_ref[...] = jnp.zeros(y_ref.shape, y_ref.dtype)


def _experts(tables, xg, w1, perm, b1g, b1l, w2, b2):
    n_rows = xg.shape[0]
    n_blocks = n_rows // EXPERT_ROWS

    def row(i, be, br, slot, nxt, nv):
        return (jnp.minimum(i, nv[0] - 1), 0)

    def per_expert(i, be, br, slot, nxt, nv):
        return (be[i], 0, 0)

    grid_spec = pltpu.PrefetchScalarGridSpec(
        num_scalar_prefetch=5,
        grid=(n_blocks,),
        in_specs=[
            pl.BlockSpec((EXPERT_ROWS, PACKED), row),
            pl.BlockSpec(memory_space=pl.ANY),
            pl.BlockSpec((256, 256), lambda i, *_: (0, 0)),
            pl.BlockSpec((1, 1, D_FF), per_expert),
            pl.BlockSpec((1, 1, D_FF), per_expert),
            pl.BlockSpec(memory_space=pl.ANY),
            pl.BlockSpec((1, 1, D_MODEL), per_expert),
        ],
        out_specs=pl.BlockSpec((EXPERT_ROWS, PACKED), lambda i, *_: (i, 0)),
        scratch_shapes=[pltpu.VMEM((2, D_MODEL, 2 * D_FF), jnp.float32), pltpu.VMEM((2, D_FF, D_MODEL), jnp.float32),
                        pltpu.VMEM((D_MODEL, D_FF), jnp.bfloat16), pltpu.VMEM((D_MODEL, D_FF), jnp.bfloat16),
                        pltpu.VMEM((D_FF, D_MODEL), jnp.bfloat16), pltpu.SemaphoreType.DMA((2, 2))],
    )
    return pl.pallas_call(
        _expert_kernel,
        grid_spec=grid_spec,
        out_shape=jax.ShapeDtypeStruct((n_rows, PACKED), jnp.int32),
        compiler_params=pltpu.CompilerParams(dimension_semantics=("arbitrary",),
                                             vmem_limit_bytes=EXPERT_VMEM_LIMIT),
        name="grouped_experts",
    )(*tables, xg, w1, perm, b1g, b1l, w2, b2)


def _split_columns_perm():
    i = lax.broadcasted_iota(jnp.int32, (256, 256), 0)
    o = lax.broadcasted_iota(jnp.int32, (256, 256), 1)
    return (i == jnp.where(o < LANES, 2 * o, 2 * (o - LANES) + 1)).astype(jnp.bfloat16)


def _combine_kernel(y_ref, gate_ref, x1_ref, g_ref, b_ref, *rest):
    o_ref = rest[-1]
    gates = gate_ref[...]
    lo_sum = None
    hi_sum = None
    for k in range(TOP_K):
        lo, hi = _unpack_bf16_pairs(y_ref[k])
        gk = gates[:, k:k + 1]
        lo_sum = gk * lo if k == 0 else lo_sum + gk * lo
        hi_sum = gk * hi if k == 0 else hi_sum + gk * hi
    moe = jnp.concatenate([lo_sum, hi_sum], axis=1)
    o_ref[...] = _layer_norm(DEEPNORM_ALPHA * x1_ref[...] + moe, g_ref[...], b_ref[...])


def _combine(yg, gates, x1, g, b, tile0, n_tok, prev):
    row = lambda i: (i, 0)
    fixed = lambda i: (0, 0)
    in_specs = [
        pl.BlockSpec((TOP_K, ROW_TILE, PACKED), lambda i: (0, i, 0)),
        pl.BlockSpec((ROW_TILE, TOP_K), row),
        pl.BlockSpec((ROW_TILE, D_MODEL), row),
        pl.BlockSpec((1, D_MODEL), fixed),
        pl.BlockSpec((1, D_MODEL), fixed),
    ]
    args = [yg, gates, x1, g, b]
    aliases = {}
    if prev is not None:
        in_specs.append(pl.BlockSpec(memory_space=pl.ANY))
        args.append(prev)
        aliases = {len(args) - 1: 0}
    return pl.pallas_call(
        _combine_kernel,
        grid=(yg.shape[1] // ROW_TILE,),
        in_specs=in_specs,
        out_specs=pl.BlockSpec((ROW_TILE, D_MODEL), lambda i: (i + tile0, 0)),
        out_shape=jax.ShapeDtypeStruct((n_tok, D_MODEL), jnp.float32),
        input_output_aliases=aliases,
        compiler_params=_params("parallel"),
        name="combine_ln",
    )(*args)


def _route(top_idx, rank, counts, n_blocks):
    counts = counts[0, :N_EXPERTS]
    blocks_per = (counts + EXPERT_ROWS - 1) // EXPERT_ROWS
    blk_end = jnp.cumsum(blocks_per)
    blk_start = blk_end - blocks_per
    pos = rank
    for e in range(N_EXPERTS):
        pos = pos + jnp.where(top_idx == e, blk_start[e] * EXPERT_ROWS, 0)
    blk = jnp.arange(n_blocks, dtype=jnp.int32)
    block_expert = jnp.minimum(jnp.sum((blk[:, None] >= blk_end[None, :]).astype(jnp.int32), axis=1),
                               N_EXPERTS - 1)
    inside = blk - blk_start[block_expert]
    block_rows = jnp.clip(counts[block_expert] - inside * EXPERT_ROWS, 0, EXPERT_ROWS).astype(jnp.int32)
    n_valid = blk_end[-1:].astype(jnp.int32)
    experts = jnp.arange(N_EXPERTS, dtype=jnp.int32)
    has_rows = blocks_per > 0
    run_of = jnp.cumsum(has_rows.astype(jnp.int32)) - 1
    later = (experts[None, :] > experts[:, None]) & has_rows[None, :]
    next_of = jnp.min(jnp.where(later, experts[None, :], N_EXPERTS), axis=1)
    next_of = jnp.where(next_of == N_EXPERTS, -1, next_of)
    block_slot = (run_of[block_expert] % 2).astype(jnp.int32)
    block_next = next_of[block_expert].astype(jnp.int32)
    return pos, (block_expert.astype(jnp.int32), block_rows, block_slot, block_next, n_valid)


def _moe_groups(n_tok, seq):
    unit = math.lcm(2 * SC_ROWS * SC_WORKERS, seq)
    parts = sum(MOE_SPLIT)
    if n_tok % (parts * unit):
        return ((0, n_tok),)
    groups, t0 = [], 0
    for share in MOE_SPLIT:
        groups.append((t0, n_tok * share // parts))
        t0 += groups[-1][1]
    return tuple(groups)


def _rope_tables(seq):
    inv_freq = 1.0 / (ROPE_THETA ** (jnp.arange(0, HEAD_DIM, 2, dtype=jnp.float32) / HEAD_DIM))
    ang = jnp.arange(seq, dtype=jnp.float32)[:, None] * inv_freq[None, :]
    cos, sin = jnp.cos(ang), jnp.sin(ang)
    cos_t = jnp.tile(jnp.concatenate([cos, cos], axis=1), (1, 256 // HEAD_DIM))
    sin_t = jnp.tile(jnp.concatenate([-sin, sin], axis=1), (1, 256 // HEAD_DIM))
    return cos_t, sin_t


def kernel(x, w_in, b_in, sinks, w_out, b_out, ln1_g, ln1_b, w_router, b_router, w1, b1, w2, b2, ln2_g, ln2_b):
    batch, seq, d = x.shape
    assert d == D_MODEL and seq % ROW_TILE == 0 and seq % MOBA_BLOCK == 0 and w_in.shape[0] == DEPTH == 1
    assert (seq // WINDOW - 1) % SWA_UNROLL == 0
    n_tok = batch * seq
    bf16 = jnp.bfloat16
    x2 = x.reshape(n_tok, d)
    cos_t, sin_t = _rope_tables(seq)

    proj = _inproj(x2, w_in[0].astype(bf16), b_in[0].reshape(1, IN_WIDTH), cos_t, sin_t, seq)

    w_o = w_out[0].astype(bf16)
    w_oa, w_ob = w_o[:W_Q_SWA], w_o[W_Q_SWA:]
    w_r = jnp.pad(w_router[0], ((0, 0), (0, LANES - N_EXPERTS))).astype(bf16)
    b_r = jnp.pad(b_router[0], (0, LANES - N_EXPERTS), constant_values=NEG_BIG).reshape(1, LANES)
    b_o, g1, be1 = b_out[0].reshape(1, d), ln1_g[0].reshape(1, d), ln1_b[0].reshape(1, d)
    perm = _split_columns_perm()
    b1r = b1[0].reshape(N_EXPERTS, 1, D_FF, 2)
    b1g, b1l = b1r[..., 0], b1r[..., 1]
    b2r = b2[0].reshape(N_EXPERTS, 1, d)
    g2, be2 = ln2_g[0].reshape(1, d), ln2_b[0].reshape(1, d)

    staged = []
    for t0, tok_g in _moe_groups(n_tok, seq):
        o_a = _swa(proj, sinks[0], t0 // seq, tok_g // seq, seq)
        o_b = _moba(proj, t0 // seq, tok_g // seq, seq)
        x1, x1p, top_idx, gates, rank, counts = _outproj(o_a, o_b, w_oa, w_ob, b_o, x2, g1, be1, w_r, b_r, t0, tok_g)
        n_blocks = tok_g * TOP_K // EXPERT_ROWS + N_EXPERTS
        pos, tables = _route(top_idx, rank, counts, n_blocks)
        pos3 = pos.reshape(TOP_K, tok_g // SC_ROWS, SC_ROWS).transpose(1, 0, 2)
        xg = _sc_scatter_rows(x1p, pos3, n_blocks * EXPERT_ROWS)
        staged.append((t0, tok_g, x1, gates, pos, tables, xg))
    gathered = []
    for t0, tok_g, x1, gates, pos, tables, xg in staged:
        y = _experts(tables, xg, w1[0], perm, b1g, b1l, w2[0], b2r)
        gathered.append(_sc_gather_rows(y, pos.reshape(tok_g * TOP_K // SC_ROWS, SC_ROWS)))
    out = None
    for (t0, tok_g, x1, gates, _, _, _), yg in zip(staged, gathered):
        out = _combine(yg.reshape(TOP_K, tok_g, PACKED), gates, x1, g2, be2, t0 // ROW_TILE, n_tok, out)
    return out.reshape(batch, seq, d)
```

```python
import functools
import math

import jax
import jax.numpy as jnp
from jax import lax
from jax.experimental import pallas as pl
from jax.experimental.pallas import tpu as pltpu
from jax.experimental.pallas import tpu_sc as plsc

D_MODEL = 1024
HEAD_DIM = 64
N_HEADS_SWA = 8
N_KV_SWA = 2
WINDOW = 128
N_HEADS_MOBA = 8
MOBA_BLOCK = 256
MOBA_TOPK = 3
ROPE_THETA = 10000.0
N_EXPERTS = 32
TOP_K = 4
D_FF = 1024
SWIGLU_LIMIT = 7.0
SWIGLU_ALPHA = 1.702
LN_EPS = 1e-5
DEPTH = 1
DEEPNORM_ALPHA = (2 * DEPTH) ** 0.25

W_Q_SWA = N_HEADS_SWA * HEAD_DIM
W_KV_SWA = N_KV_SWA * HEAD_DIM
W_MOBA = N_HEADS_MOBA * HEAD_DIM
IN_WIDTH = W_Q_SWA + 2 * W_KV_SWA + 3 * W_MOBA
LANES = 128
COL_K_SWA = W_Q_SWA // LANES
COL_V_SWA = COL_K_SWA + 1
COL_Q_MOBA = COL_V_SWA + 1
COL_K_MOBA = COL_Q_MOBA + W_MOBA // LANES
COL_V_MOBA = COL_K_MOBA + W_MOBA // LANES

ROW_TILE = 512
EXPERT_ROWS = 1024
EXPERT_HALF = 512
PACKED = D_MODEL // 2
NEG_BIG = -1e30
LOG2E = 1.4426950408889634
MOBA_LOOKAHEAD = 1
SWA_UNROLL = 3
MOE_SPLIT = (3, 1)
VMEM_LIMIT = 48 * 1024 * 1024
EXPERT_VMEM_LIMIT = 60 * 1024 * 1024

SC_CORES = 2
SC_SUBCORES = 16
SC_WORKERS = SC_CORES * SC_SUBCORES
SC_ROWS = 64

_PROJ_CHUNKS = (
    (0, 256, 256, True), (256, 256, 256, True),
    (512, 256, 128, False),
    (768, 256, 256, True), (1024, 256, 256, True),
    (1280, 256, 256, False), (1536, 256, 256, False),
    (1792, 256, 0, False), (2048, 256, 0, False),
)


def _params(*sem):
    return pltpu.CompilerParams(dimension_semantics=sem, vmem_limit_bytes=VMEM_LIMIT)


def _pack_bf16_pairs(v):
    n = v.shape[1] // 2
    bits = lax.bitcast_convert_type(v.astype(jnp.bfloat16).astype(jnp.float32), jnp.uint32)
    word = (bits[:, :n] >> 16) | (bits[:, n:] & jnp.uint32(0xFFFF0000))
    return lax.bitcast_convert_type(word, jnp.int32)


def _unpack_bf16_pairs(word):
    bits = lax.bitcast_convert_type(word, jnp.uint32)
    lo = lax.bitcast_convert_type(bits << 16, jnp.float32)
    hi = lax.bitcast_convert_type(bits & jnp.uint32(0xFFFF0000), jnp.float32)
    return lo, hi


def _inproj_kernel(x_ref, w_ref, b_ref, cos_ref, sin_ref, o_ref):
    xb = x_ref[...].astype(jnp.bfloat16)
    for start, width, rope, scaled in _PROJ_CHUNKS:
        t = jnp.dot(xb, w_ref[:, start:start + width], preferred_element_type=jnp.float32)
        t = t + b_ref[:, start:start + width]
        if rope:
            lane = lax.broadcasted_iota(jnp.int32, t.shape, 1)
            first_half = (lane % HEAD_DIM) < (HEAD_DIM // 2)
            rot = jnp.where(first_half,
                            pltpu.roll(t, width - HEAD_DIM // 2, 1),
                            pltpu.roll(t, HEAD_DIM // 2, 1))
            roped = t * cos_ref[:, :width] + rot * sin_ref[:, :width]
            t = roped if rope == width else jnp.where(lane < rope, roped, t)
        if scaled:
            t = t * (HEAD_DIM ** -0.5 * LOG2E)
        o_ref[:, start:start + width] = t.astype(o_ref.dtype)


def _inproj(x2, w_in, b_in, cos_t, sin_t, seq):
    n_tok = x2.shape[0]
    per_seq = seq // ROW_TILE
    return pl.pallas_call(
        _inproj_kernel,
        grid=(n_tok // ROW_TILE,),
        in_specs=[
            pl.BlockSpec((ROW_TILE, D_MODEL), lambda i: (i, 0)),
            pl.BlockSpec((D_MODEL, IN_WIDTH), lambda i: (0, 0)),
            pl.BlockSpec((1, IN_WIDTH), lambda i: (0, 0)),
            pl.BlockSpec((ROW_TILE, 256), lambda i: (i % per_seq, 0)),
            pl.BlockSpec((ROW_TILE, 256), lambda i: (i % per_seq, 0)),
        ],
        out_specs=pl.BlockSpec((ROW_TILE, IN_WIDTH), lambda i: (i, 0)),
        out_shape=jax.ShapeDtypeStruct((n_tok, IN_WIDTH), jnp.bfloat16),
        compiler_params=_params("parallel"),
        name="inproj_rope",
    )(x2, w_in, b_in, cos_t, sin_t)


def _swa_kernel(sink_ref, q_ref, k_ref, v_ref, o_ref, kd_ref, vd_ref, *, seq):
    lane = lax.broadcasted_iota(jnp.int32, (seq, LANES), 1)
    low = lane < HEAD_DIM
    k = k_ref[...].astype(jnp.float32)
    kr = pltpu.roll(k, HEAD_DIM, 1)
    kd_ref[0, WINDOW:, :] = jnp.where(low, k, kr).astype(kd_ref.dtype)
    kd_ref[1, WINDOW:, :] = jnp.where(low, kr, k).astype(kd_ref.dtype)
    v = v_ref[...].astype(jnp.float32)
    vd_ref[0, WINDOW:, :] = jnp.where(low, v, 1.0).astype(vd_ref.dtype)
    vd_ref[1, WINDOW:, :] = jnp.where(low, pltpu.roll(v, HEAD_DIM, 1), 1.0).astype(vd_ref.dtype)
    kd_ref[:, :WINDOW, :] = jnp.zeros((N_KV_SWA, WINDOW, LANES), kd_ref.dtype)
    vd_ref[:, :WINDOW, :] = jnp.zeros((N_KV_SWA, WINDOW, LANES), vd_ref.dtype)

    group = N_HEADS_SWA // N_KV_SWA
    rows = group * WINDOW
    r_in = lax.broadcasted_iota(jnp.int32, (rows, 2 * WINDOW), 0) % WINDOW
    c_id = lax.broadcasted_iota(jnp.int32, (rows, 2 * WINDOW), 1)
    band = (c_id > r_in) & (c_id <= r_in + WINDOW)
    head_in_group = lax.broadcasted_iota(jnp.int32, (rows, 1), 0) // WINDOW
    qlane_low = lax.broadcasted_iota(jnp.int32, (WINDOW, LANES), 1) < HEAD_DIM
    sinks = []
    for g in range(N_KV_SWA):
        col = jnp.zeros((rows, 1), jnp.float32)
        for j in range(group):
            col = jnp.where(head_in_group == j, sink_ref[g * group + j] * LOG2E, col)
        sinks.append(col)

    def scores(n, mask):
        r0 = pl.multiple_of(n * WINDOW, WINDOW)
        out = []
        for g in range(N_KV_SWA):
            parts = []
            for c in (2 * g, 2 * g + 1):
                qc = q_ref[pl.ds(r0, WINDOW), c * LANES:(c + 1) * LANES]
                zero = jnp.zeros_like(qc)
                parts.append(jnp.where(qlane_low, qc, zero))
                parts.append(jnp.where(qlane_low, zero, qc))
            qcat = jnp.concatenate(parts, axis=0)
            kd = kd_ref[g, pl.ds(r0, 2 * WINDOW), :]
            s = lax.dot_general(qcat, kd, (((1,), (1,)), ((), ())),
                                preferred_element_type=jnp.float32)
            out.append(jnp.where(mask, s, -jnp.inf))
        return out

    def finish(n, scored):
        r0 = pl.multiple_of(n * WINDOW, WINDOW)
        for g, s in enumerate(scored):
            vd = vd_ref[g, pl.ds(r0, 2 * WINDOW), :]
            m = jnp.maximum(jnp.max(s, axis=-1, keepdims=True), sinks[g])
            p = jnp.exp2(s - m)
            o = jnp.dot(p.astype(vd.dtype), vd, preferred_element_type=jnp.float32)
            sink_term = jnp.exp2(sinks[g] - m)
            for ci, c in enumerate((2 * g, 2 * g + 1)):
                o_lo = o[(2 * ci) * WINDOW:(2 * ci + 1) * WINDOW]
                o_hi = o[(2 * ci + 1) * WINDOW:(2 * ci + 2) * WINDOW]
                e_lo = sink_term[(2 * ci) * WINDOW:(2 * ci + 1) * WINDOW]
                e_hi = sink_term[(2 * ci + 1) * WINDOW:(2 * ci + 2) * WINDOW]
                num = jnp.where(qlane_low, o_lo, pltpu.roll(o_hi, HEAD_DIM, 1))
                den = jnp.where(qlane_low, pltpu.roll(o_lo, HEAD_DIM, 1) + e_lo, o_hi + e_hi)
                o_ref[pl.ds(r0, WINDOW), c * LANES:(c + 1) * LANES] = (num / den).astype(o_ref.dtype)

    finish(0, scores(0, band & (c_id >= WINDOW)))

    def body(it, carry):
        n0 = 1 + it * SWA_UNROLL
        scored = [scores(n0 + u, band) for u in range(SWA_UNROLL)]
        for u in range(SWA_UNROLL):
            finish(n0 + u, scored[u])
        return carry

    lax.fori_loop(0, (seq // WINDOW - 1) // SWA_UNROLL, body, 0)


def _swa(proj, sinks, b0, batch, seq):
    grid_spec = pltpu.PrefetchScalarGridSpec(
        num_scalar_prefetch=0,
        grid=(batch,),
        in_specs=[
            pl.BlockSpec(memory_space=pltpu.SMEM),
            pl.BlockSpec((seq, W_Q_SWA), lambda b: (b + b0, 0)),
            pl.BlockSpec((seq, LANES), lambda b: (b + b0, COL_K_SWA)),
            pl.BlockSpec((seq, LANES), lambda b: (b + b0, COL_V_SWA)),
        ],
        out_specs=pl.BlockSpec((seq, W_Q_SWA), lambda b: (b, 0)),
        scratch_shapes=[pltpu.VMEM((N_KV_SWA, WINDOW + seq, LANES), jnp.bfloat16),
                        pltpu.VMEM((N_KV_SWA, WINDOW + seq, LANES), jnp.bfloat16)],
    )
    return pl.pallas_call(
        functools.partial(_swa_kernel, seq=seq),
        grid_spec=grid_spec,
        out_shape=jax.ShapeDtypeStruct((batch * seq, W_Q_SWA), jnp.bfloat16),
        compiler_params=_params("parallel"),
        name="swa_sink_attention",
    )(sinks, proj, proj, proj)


def _moba_kernel(q_ref, k_ref, v_ref, o_ref, qa_ref, ka_ref, va_ref, *, seq):
    nblk = seq // MOBA_BLOCK
    pad_rows = 16
    q_all, k_all, v_all = q_ref[...], k_ref[...], v_ref[...]
    kmean = jnp.sum(k_all.astype(jnp.float32).reshape(nblk, MOBA_BLOCK, LANES), axis=1) / MOBA_BLOCK
    kmean = jnp.concatenate([kmean, jnp.zeros((pad_rows - nblk, LANES), jnp.float32)], axis=0)
    klane_low = lax.broadcasted_iota(jnp.int32, (pad_rows, LANES), 1) < HEAD_DIM
    lane = lax.broadcasted_iota(jnp.int32, (seq, LANES), 1)
    low = lane < HEAD_DIM
    key_blk = lax.broadcasted_iota(jnp.int32, (seq, LANES), 0) // MOBA_BLOCK
    j_id = lax.broadcasted_iota(jnp.int32, (pad_rows, seq), 0)
    q_blk = lax.broadcasted_iota(jnp.int32, (pad_rows, seq), 1) // MOBA_BLOCK
    eligible = j_id < q_blk

    for half in range(2):
        own = low if half == 0 else ~low
        spare = HEAD_DIM if half == 0 else 0
        ka_ref[half] = jnp.where(own, k_all, (lane - spare == key_blk).astype(k_all.dtype))
        va_ref[half] = jnp.where(own, v_all, jnp.ones_like(v_all))
        km = jnp.where(klane_low if half == 0 else ~klane_low, kmean, 0.0).astype(jnp.bfloat16)
        gate = lax.dot_general(km, q_all, (((1,), (1,)), ((), ())),
                               preferred_element_type=jnp.float32)
        gate = jnp.where(eligible, gate, -jnp.inf)
        beaten = jnp.zeros((pad_rows, seq), jnp.int32)
        for jp in range(nblk):
            row = gate[jp:jp + 1, :]
            wins = (row > gate) | ((row == gate) & (jp < j_id))
            beaten = beaten + wins.astype(jnp.int32)
        dropped = eligible & (beaten >= MOBA_TOPK)
        bias = jnp.where(dropped, NEG_BIG, 0.0)
        pieces = [bias, jnp.zeros((LANES - spare - pad_rows, seq), jnp.float32)]
        if spare:
            pieces.insert(0, jnp.zeros((spare, seq), jnp.float32))
        bias_t = jnp.concatenate(pieces, axis=0).T
        qa_ref[half] = jnp.where(own, q_all, bias_t.astype(q_all.dtype))

    qlane_low = lax.broadcasted_iota(jnp.int32, (MOBA_BLOCK, LANES), 1) < HEAD_DIM
    rr = lax.broadcasted_iota(jnp.int32, (MOBA_BLOCK, MOBA_BLOCK), 0)
    cc = lax.broadcasted_iota(jnp.int32, (MOBA_BLOCK, MOBA_BLOCK), 1)
    causal = cc <= rr

    def scores(i, half):
        r0 = i * MOBA_BLOCK
        n_keys = r0 + MOBA_BLOCK
        s = lax.dot_general(qa_ref[half, r0:n_keys, :], ka_ref[half, 0:n_keys, :], (((1,), (1,)), ((), ())),
                            preferred_element_type=jnp.float32)
        own_blk = jnp.where(causal, s[:, r0:n_keys], NEG_BIG)
        return jnp.concatenate([s[:, :r0], own_blk], axis=1) if i else own_blk

    units = [(i, half) for i in range(nblk) for half in range(2)]
    pending = [scores(*u) for u in units[:MOBA_LOOKAHEAD]]
    acc = []
    for n, (i, half) in enumerate(units):
        s = pending.pop(0)
        if n + MOBA_LOOKAHEAD < len(units):
            pending.append(scores(*units[n + MOBA_LOOKAHEAD]))
        n_keys = (i + 1) * MOBA_BLOCK
        m = jnp.max(s, axis=-1, keepdims=True)
        p = jnp.exp2(s - m).astype(jnp.bfloat16)
        acc.append(jnp.dot(p, va_ref[half, 0:n_keys, :], preferred_element_type=jnp.float32))
        if half == 1:
            num = jnp.where(qlane_low, acc[0], acc[1])
            den = pltpu.roll(jnp.where(qlane_low, acc[1], acc[0]), HEAD_DIM, 1)
            o_ref[i * MOBA_BLOCK:n_keys, :] = (num / den).astype(o_ref.dtype)
            acc = []


def _moba(proj, b0, batch, seq):
    pairs = W_MOBA // LANES
    return pl.pallas_call(
        functools.partial(_moba_kernel, seq=seq),
        grid=(batch, pairs),
        in_specs=[
            pl.BlockSpec((seq, LANES), lambda b, p: (b + b0, COL_Q_MOBA + p)),
            pl.BlockSpec((seq, LANES), lambda b, p: (b + b0, COL_K_MOBA + p)),
            pl.BlockSpec((seq, LANES), lambda b, p: (b + b0, COL_V_MOBA + p)),
        ],
        out_specs=pl.BlockSpec((seq, LANES), lambda b, p: (b, p)),
        out_shape=jax.ShapeDtypeStruct((batch * seq, W_MOBA), jnp.bfloat16),
        scratch_shapes=[pltpu.VMEM((2, seq, LANES), jnp.bfloat16)] * 3,
        compiler_params=_params("parallel", "parallel"),
        name="moba_attention",
    )(proj, proj, proj)


def _layer_norm(h, g, b):
    mu = jnp.mean(h, axis=-1, keepdims=True)
    d = h - mu
    var = jnp.mean(d * d, axis=-1, keepdims=True)
    return d * lax.rsqrt(var + LN_EPS) * g + b


def _outproj_kernel(oa_ref, ob_ref, wo_ref, bo_ref, x_ref, g_ref, b_ref, wr_ref, br_ref, tri_ref,
                    x1_ref, x1p_ref, idx_ref, gate_ref, rank_ref, count_ref, running_ref):
    @pl.when(pl.program_id(0) == 0)
    def _():
        running_ref[...] = jnp.zeros(running_ref.shape, running_ref.dtype)

    half_rows = ROW_TILE // 2
    halves = [slice(h * half_rows, (h + 1) * half_rows) for h in range(2)]

    def project(rows):
        heads = jnp.concatenate([oa_ref[rows, :], ob_ref[rows, :]], axis=1)
        return jnp.dot(heads, wo_ref[...], preferred_element_type=jnp.float32) + bo_ref[...]

    def normalise(rows, mix):
        x1 = _layer_norm(DEEPNORM_ALPHA * x_ref[rows, :] + mix, g_ref[...], b_ref[...])
        x1_ref[rows, :] = x1
        x1p_ref[rows, :] = _pack_bf16_pairs(x1)
        return jnp.dot(x1.astype(jnp.bfloat16), wr_ref[...], preferred_element_type=jnp.float32) + br_ref[...]

    mixes = [project(rows) for rows in halves]
    logits = jnp.concatenate([normalise(rows, mix) for rows, mix in zip(halves, mixes)], axis=0)
    lane = lax.broadcasted_iota(jnp.int32, logits.shape, 1)
    idx_out = jnp.zeros(logits.shape, jnp.int32)
    val_out = jnp.zeros(logits.shape, jnp.float32)
    top = None
    total = None
    onehots = []
    for k in range(TOP_K):
        m = jnp.max(logits, axis=-1, keepdims=True)
        idx = jnp.min(jnp.where(logits == m, lane, LANES), axis=-1, keepdims=True)
        picked = lane == idx
        onehots.append(picked)
        logits = jnp.where(picked, -jnp.inf, logits)
        if k == 0:
            top = m
        e = jnp.exp(m - top)
        total = e if k == 0 else total + e
        idx_out = jnp.where(lane == k, idx, idx_out)
        val_out = jnp.where(lane == k, e, val_out)
    idx_ref[...] = idx_out.T[:TOP_K, :]
    gate_ref[...] = (val_out / total)[:, :TOP_K]

    picks = jnp.concatenate([p.astype(jnp.bfloat16) for p in onehots], axis=1)
    before = jnp.dot(tri_ref[...], picks, preferred_element_type=jnp.float32)
    base = running_ref[...]
    rank_out = jnp.zeros(logits.shape, jnp.float32)
    for k in range(TOP_K):
        pk = onehots[k].astype(jnp.float32)
        here = before[:, k * LANES:(k + 1) * LANES] + base
        rank_k = jnp.sum(pk * here, axis=-1, keepdims=True)
        rank_out = jnp.where(lane == k, rank_k, rank_out)
        base = base + jnp.sum(pk, axis=0, keepdims=True)
    running_ref[...] = base
    rank_ref[...] = rank_out.astype(jnp.int32).T[:TOP_K, :]
    count_ref[...] = base.astype(jnp.int32)


def _outproj(o_a, o_b, w_o, b_out, x2, g, b, w_r, b_r, t0, n_tok):
    tile0 = t0 // ROW_TILE
    src = lambda i: (i + tile0, 0)
    row = lambda i: (i, 0)
    fixed = lambda i: (0, 0)
    r = lax.broadcasted_iota(jnp.int32, (ROW_TILE, ROW_TILE), 0)
    c = lax.broadcasted_iota(jnp.int32, (ROW_TILE, ROW_TILE), 1)
    tri = (c < r).astype(jnp.bfloat16)
    return pl.pallas_call(
        _outproj_kernel,
        grid=(n_tok // ROW_TILE,),
        in_specs=[
            pl.BlockSpec((ROW_TILE, W_Q_SWA), row),
            pl.BlockSpec((ROW_TILE, W_MOBA), row),
            pl.BlockSpec((W_Q_SWA + W_MOBA, D_MODEL), fixed),
            pl.BlockSpec((1, D_MODEL), fixed),
            pl.BlockSpec((ROW_TILE, D_MODEL), src),
            pl.BlockSpec((1, D_MODEL), fixed),
            pl.BlockSpec((1, D_MODEL), fixed),
            pl.BlockSpec((D_MODEL, LANES), fixed),
            pl.BlockSpec((1, LANES), fixed),
            pl.BlockSpec((ROW_TILE, ROW_TILE), fixed),
        ],
        out_specs=[
            pl.BlockSpec((ROW_TILE, D_MODEL), row),
            pl.BlockSpec((ROW_TILE, PACKED), row),
            pl.BlockSpec((TOP_K, ROW_TILE), lambda i: (0, i)),
            pl.BlockSpec((ROW_TILE, TOP_K), row),
            pl.BlockSpec((TOP_K, ROW_TILE), lambda i: (0, i)),
            pl.BlockSpec((1, LANES), fixed),
        ],
        out_shape=[
            jax.ShapeDtypeStruct((n_tok, D_MODEL), jnp.float32),
            jax.ShapeDtypeStruct((n_tok, PACKED), jnp.int32),
            jax.ShapeDtypeStruct((TOP_K, n_tok), jnp.int32),
            jax.ShapeDtypeStruct((n_tok, TOP_K), jnp.float32),
            jax.ShapeDtypeStruct((TOP_K, n_tok), jnp.int32),
            jax.ShapeDtypeStruct((1, LANES), jnp.int32),
        ],
        scratch_shapes=[pltpu.VMEM((1, LANES), jnp.float32)],
        compiler_params=_params("arbitrary"),
        name="outproj_ln_router",
    )(o_a, o_b, w_o, b_out, x2, g, b, w_r, b_r, tri)


def _sc_worker_id():
    return lax.axis_index("s") * SC_CORES + lax.axis_index("c")


def _sc_scatter_rows(rows, pos3, n_out):
    n_tok = pos3.shape[0] * SC_ROWS
    steps = n_tok // SC_ROWS // SC_WORKERS
    assert steps * SC_ROWS * SC_WORKERS == n_tok and steps % 2 == 0
    mesh = plsc.VectorSubcoreMesh(core_axis_name="c", subcore_axis_name="s")

    @functools.partial(
        pl.kernel, mesh=mesh,
        out_type=jax.ShapeDtypeStruct((n_out, PACKED), jnp.int32),
        scratch_types=[pltpu.VMEM((2, TOP_K, SC_ROWS), jnp.int32), pltpu.VMEM((2, SC_ROWS, PACKED), jnp.int32),
                       pltpu.SemaphoreType.DMA((2,)), pltpu.SemaphoreType.DMA((2,))],
        name="sc_dispatch_scatter")
    def scatter(x_hbm, pos_hbm, out_hbm, idx_v, rows_v, sem_ld, sem_st):
        base = _sc_worker_id() * steps

        def loads(s, b):
            return (pltpu.make_async_copy(pos_hbm.at[base + s], idx_v.at[b], sem_ld.at[b]),
                    pltpu.make_async_copy(x_hbm.at[pl.ds((base + s) * SC_ROWS, SC_ROWS)], rows_v.at[b],
                                          sem_ld.at[b]))

        def stores(b):
            return [pltpu.make_async_copy(rows_v.at[b], out_hbm.at[idx_v.at[b, k]], sem_st.at[b])
                    for k in range(TOP_K)]

        for c in loads(0, 0):
            c.start()

        @pl.loop(0, steps, step=2)
        def _(s0):
            for b in range(2):
                s = s0 + b
                for c in loads(s, b):
                    c.wait()

                @pl.when(s >= 1)
                def _():
                    for c in stores(1 - b):
                        c.wait()

                @pl.when(s + 1 < steps)
                def _():
                    for c in loads(s + 1, 1 - b):
                        c.start()

                for c in stores(b):
                    c.start()

        for c in stores(1):
            c.wait()

    return scatter(rows, pos3)


def _sc_gather_rows(table, idx2):
    n_blk = idx2.shape[0]
    steps = n_blk // SC_WORKERS
    assert steps * SC_WORKERS == n_blk and steps % 2 == 0 and idx2.shape[1] == SC_ROWS
    mesh = plsc.VectorSubcoreMesh(core_axis_name="c", subcore_axis_name="s")

    @functools.partial(
        pl.kernel, mesh=mesh,
        out_type=jax.ShapeDtypeStruct((n_blk * SC_ROWS, PACKED), jnp.int32),
        scratch_types=[pltpu.VMEM((steps, SC_ROWS), jnp.int32), pltpu.VMEM((2, SC_ROWS, PACKED), jnp.int32),
                       pltpu.SemaphoreType.DMA((2,)), pltpu.SemaphoreType.DMA((2,))],
        name="sc_combine_gather")
    def gather(y_hbm, idx_hbm, out_hbm, idx_v, rows_v, sem_ld, sem_st):
        base = _sc_worker_id() * steps
        pltpu.sync_copy(idx_hbm.at[pl.ds(base, steps)], idx_v)

        def fetch(s, b):
            return pltpu.make_async_copy(y_hbm.at[idx_v.at[s]], rows_v.at[b], sem_ld.at[b])

        def store(s, b):
            return pltpu.make_async_copy(rows_v.at[b], out_hbm.at[pl.ds((base + s) * SC_ROWS, SC_ROWS)],
                                         sem_st.at[b])

        fetch(0, 0).start()

        @pl.loop(0, steps, step=2)
        def _(s0):
            for b in range(2):
                s = s0 + b
                fetch(s, b).wait()

                @pl.when(s >= 1)
                def _():
                    store(s - 1, 1 - b).wait()

                @pl.when(s + 1 < steps)
                def _():
                    fetch(s + 1, 1 - b).start()

                store(s, b).start()

        store(steps - 1, 1).wait()

    return gather(table, idx2)


def _expert_kernel(be_ref, br_ref, slot_ref, next_ref, nv_ref, x_ref, w1_hbm, perm_ref, bg_ref, bl_ref, w2_hbm,
                   b2_ref, y_ref, w1f_ref, w2f_ref, wg_ref, wl_ref, w2b_ref, sem):
    i = pl.program_id(0)
    live = i < nv_ref[0]
    new_expert = (i == 0) | (be_ref[i] != be_ref[jnp.maximum(i - 1, 0)])

    def weight_copies(expert, slot):
        return (pltpu.make_async_copy(w1_hbm.at[expert], w1f_ref.at[slot], sem.at[0, slot]),
                pltpu.make_async_copy(w2_hbm.at[expert], w2f_ref.at[slot], sem.at[1, slot]))

    @pl.when(live & new_expert)
    def _():
        slot = slot_ref[i]

        @pl.when(i == 0)
        def _():
            for copy in weight_copies(be_ref[i], slot):
                copy.start()

        for copy in weight_copies(be_ref[i], slot):
            copy.wait()

        @pl.when(next_ref[i] >= 0)
        def _():
            for copy in weight_copies(next_ref[i], 1 - slot):
                copy.start()

        for c in range(2 * D_FF // 256):
            t = w1f_ref[slot, :, c * 256:(c + 1) * 256].astype(jnp.bfloat16)
            r = jnp.dot(t, perm_ref[...], preferred_element_type=jnp.float32)
            wg_ref[:, c * LANES:(c + 1) * LANES] = r[:, :LANES].astype(wg_ref.dtype)
            wl_ref[:, c * LANES:(c + 1) * LANES] = r[:, LANES:].astype(wl_ref.dtype)
        w2b_ref[...] = w2f_ref[slot].astype(w2b_ref.dtype)

    def mlp(rows):
        valid = lax.broadcasted_iota(jnp.int32, (rows, PACKED), 0) < br_ref[i]
        lo, hi = _unpack_bf16_pairs(jnp.where(valid, x_ref[0:rows, :], 0))
        xb = jnp.concatenate([lo.astype(jnp.bfloat16), hi.astype(jnp.bfloat16)], axis=1)
        hg = jnp.dot(xb, wg_ref[...], preferred_element_type=jnp.float32) + bg_ref[0]
        hl = jnp.dot(xb, wl_ref[...], preferred_element_type=jnp.float32) + bl_ref[0]
        glu = jnp.minimum(hg, SWIGLU_LIMIT)
        lin = jnp.clip(hl, -SWIGLU_LIMIT, SWIGLU_LIMIT)
        act = glu * jax.nn.sigmoid(SWIGLU_ALPHA * glu) * (lin + 1.0)
        y = jnp.dot(act.astype(jnp.bfloat16), w2b_ref[...], preferred_element_type=jnp.float32) + b2_ref[0]
        y_ref[0:rows, :] = _pack_bf16_pairs(y)

    @pl.when(live & (br_ref[i] > EXPERT_HALF))
    def _():
        mlp(EXPERT_ROWS)

    @pl.when(live & (br_ref[i] <= EXPERT_HALF))
    def _():
        mlp(EXPERT_HALF)
        y_ref[EXPERT_HALF:, :] = jnp.zeros((EXPERT_ROWS - EXPERT_HALF, PACKED), y_ref.dtype)

    @pl.when(jnp.logical_not(live))
    def _():
        y_ref[...] = jnp.zeros(y_ref.shape, y_ref.dtype)


def _experts(tables, xg, w1, perm, b1g, b1l, w2, b2):
    n_rows = xg.shape[0]
    n_blocks = n_rows // EXPERT_ROWS

    def row(i, be, br, slot, nxt, nv):
        return (jnp.minimum(i, nv[0] - 1), 0)

    def per_expert(i, be, br, slot, nxt, nv):
        return (be[i], 0, 0)

    grid_spec = pltpu.PrefetchScalarGridSpec(
        num_scalar_prefetch=5,
        grid=(n_blocks,),
        in_specs=[
            pl.BlockSpec((EXPERT_ROWS, PACKED), row),
            pl.BlockSpec(memory_space=pl.ANY),
            pl.BlockSpec((256, 256), lambda i, *_: (0, 0)),
            pl.BlockSpec((1, 1, D_FF), per_expert),
            pl.BlockSpec((1, 1, D_FF), per_expert),
            pl.BlockSpec(memory_space=pl.ANY),
            pl.BlockSpec((1, 1, D_MODEL), per_expert),
        ],
        out_specs=pl.BlockSpec((EXPERT_ROWS, PACKED), lambda i, *_: (i, 0)),
        scratch_shapes=[pltpu.VMEM((2, D_MODEL, 2 * D_FF), jnp.float32), pltpu.VMEM((2, D_FF, D_MODEL), jnp.float32),
                        pltpu.VMEM((D_MODEL, D_FF), jnp.bfloat16), pltpu.VMEM((D_MODEL, D_FF), jnp.bfloat16),
                        pltpu.VMEM((D_FF, D_MODEL), jnp.bfloat16), pltpu.SemaphoreType.DMA((2, 2))],
    )
    return pl.pallas_call(
        _expert_kernel,
        grid_spec=grid_spec,
        out_shape=jax.ShapeDtypeStruct((n_rows, PACKED), jnp.int32),
        compiler_params=pltpu.CompilerParams(dimension_semantics=("arbitrary",),
                                             vmem_limit_bytes=EXPERT_VMEM_LIMIT),
        name="grouped_experts",
    )(*tables, xg, w1, perm, b1g, b1l, w2, b2)


def _split_columns_perm():
    i = lax.broadcasted_iota(jnp.int32, (256, 256), 0)
    o = lax.broadcasted_iota(jnp.int32, (256, 256), 1)
    return (i == jnp.where(o < LANES, 2 * o, 2 * (o - LANES) + 1)).astype(jnp.bfloat16)


def _combine_kernel(y_ref, gate_ref, x1_ref, g_ref, b_ref, *rest):
    o_ref = rest[-1]
    gates = gate_ref[...]
    lo_sum = None
    hi_sum = None
    for k in range(TOP_K):
        lo, hi = _unpack_bf16_pairs(y_ref[k])
        gk = gates[:, k:k + 1]
        lo_sum = gk * lo if k == 0 else lo_sum + gk * lo
        hi_sum = gk * hi if k == 0 else hi_sum + gk * hi
    moe = jnp.concatenate([lo_sum, hi_sum], axis=1)
    o_ref[...] = _layer_norm(DEEPNORM_ALPHA * x1_ref[...] + moe, g_ref[...], b_ref[...])


def _combine(yg, gates, x1, g, b, tile0, n_tok, prev):
    row = lambda i: (i, 0)
    fixed = lambda i: (0, 0)
    in_specs = [
        pl.BlockSpec((TOP_K, ROW_TILE, PACKED), lambda i: (0, i, 0)),
        pl.BlockSpec((ROW_TILE, TOP_K), row),
        pl.BlockSpec((ROW_TILE, D_MODEL), row),
        pl.BlockSpec((1, D_MODEL), fixed),
        pl.BlockSpec((1, D_MODEL), fixed),
    ]
    args = [yg, gates, x1, g, b]
    aliases = {}
    if prev is not None:
        in_specs.append(pl.BlockSpec(memory_space=pl.ANY))
        args.append(prev)
        aliases = {len(args) - 1: 0}
    return pl.pallas_call(
        _combine_kernel,
        grid=(yg.shape[1] // ROW_TILE,),
        in_specs=in_specs,
        out_specs=pl.BlockSpec((ROW_TILE, D_MODEL), lambda i: (i + tile0, 0)),
        out_shape=jax.ShapeDtypeStruct((n_tok, D_MODEL), jnp.float32),
        input_output_aliases=aliases,
        compiler_params=_params("parallel"),
        name="combine_ln",
    )(*args)


def _route(top_idx, rank, counts, n_blocks):
    counts = counts[0, :N_EXPERTS]
    blocks_per = (counts + EXPERT_ROWS - 1) // EXPERT_ROWS
    blk_end = jnp.cumsum(blocks_per)
    blk_start = blk_end - blocks_per
    pos = rank
    for e in range(N_EXPERTS):
        pos = pos + jnp.where(top_idx == e, blk_start[e] * EXPERT_ROWS, 0)
    n_valid = blk_end[-1:].astype(jnp.int32)
    experts = jnp.arange(N_EXPERTS, dtype=jnp.int32)
    has_rows = blocks_per > 0
    run_of = jnp.cumsum(has_rows.astype(jnp.int32)) - 1
    later = (experts[None, :] > experts[:, None]) & has_rows[None, :]
    next_of = jnp.min(jnp.where(later, experts[None, :], N_EXPERTS), axis=1)
    next_of = jnp.where(next_of == N_EXPERTS, -1, next_of)
    blk = jnp.arange(n_blocks, dtype=jnp.int32)
    owner = ((blk[:, None] >= blk_start[None, :]) & (blk[:, None] < blk_end[None, :])).astype(jnp.int32)
    pick = lambda per_expert: jnp.sum(owner * per_expert[None, :], axis=1).astype(jnp.int32)
    block_expert = pick(experts)
    block_rows = jnp.clip(pick(counts) - (blk - pick(blk_start)) * EXPERT_ROWS, 0, EXPERT_ROWS).astype(jnp.int32)
    return pos, (block_expert, block_rows, pick(run_of) % 2, pick(next_of), n_valid)


def _moe_groups(n_tok, seq):
    unit = math.lcm(2 * SC_ROWS * SC_WORKERS, seq)
    parts = sum(MOE_SPLIT)
    if n_tok % (parts * unit):
        return ((0, n_tok),)
    groups, t0 = [], 0
    for share in MOE_SPLIT:
        groups.append((t0, n_tok * share // parts))
        t0 += groups[-1][1]
    return tuple(groups)


def _rope_tables(seq):
    inv_freq = 1.0 / (ROPE_THETA ** (jnp.arange(0, HEAD_DIM, 2, dtype=jnp.float32) / HEAD_DIM))
    ang = jnp.arange(seq, dtype=jnp.float32)[:, None] * inv_freq[None, :]
    cos, sin = jnp.cos(ang), jnp.sin(ang)
    cos_t = jnp.tile(jnp.concatenate([cos, cos], axis=1), (1, 256 // HEAD_DIM))
    sin_t = jnp.tile(jnp.concatenate([-sin, sin], axis=1), (1, 256 // HEAD_DIM))
    return cos_t, sin_t


def kernel(x, w_in, b_in, sinks, w_out, b_out, ln1_g, ln1_b, w_router, b_router, w1, b1, w2, b2, ln2_g, ln2_b):
    batch, seq, d = x.shape
    assert d == D_MODEL and seq % ROW_TILE == 0 and seq % MOBA_BLOCK == 0 and w_in.shape[0] == DEPTH == 1
    assert (seq // WINDOW - 1) % SWA_UNROLL == 0
    n_tok = batch * seq
    bf16 = jnp.bfloat16
    x2 = x.reshape(n_tok, d)
    cos_t, sin_t = _rope_tables(seq)

    proj = _inproj(x2, w_in[0].astype(bf16), b_in[0].reshape(1, IN_WIDTH), cos_t, sin_t, seq)

    w_o = w_out[0].astype(bf16)
    w_r = jnp.pad(w_router[0], ((0, 0), (0, LANES - N_EXPERTS))).astype(bf16)
    b_r = jnp.pad(b_router[0], (0, LANES - N_EXPERTS), constant_values=NEG_BIG).reshape(1, LANES)
    b_o, g1, be1 = b_out[0].reshape(1, d), ln1_g[0].reshape(1, d), ln1_b[0].reshape(1, d)
    perm = _split_columns_perm()
    b1r = b1[0].reshape(N_EXPERTS, 1, D_FF, 2)
    b1g, b1l = b1r[..., 0], b1r[..., 1]
    b2r = b2[0].reshape(N_EXPERTS, 1, d)
    g2, be2 = ln2_g[0].reshape(1, d), ln2_b[0].reshape(1, d)

    def dispatch(stage):
        t0, tok_g, x1, x1p, gates, pos, tables, n_blocks = stage
        pos3 = pos.reshape(TOP_K, tok_g // SC_ROWS, SC_ROWS).transpose(1, 0, 2)
        return _sc_scatter_rows(x1p, pos3, n_blocks * EXPERT_ROWS)

    def experts_and_gather(stage, xg):
        t0, tok_g, x1, x1p, gates, pos, tables, n_blocks = stage
        y = _experts(tables, xg, w1[0], perm, b1g, b1l, w2[0], b2r)
        return _sc_gather_rows(y, pos.reshape(tok_g * TOP_K // SC_ROWS, SC_ROWS))

    stages, gathered, waiting = [], [], None
    for t0, tok_g in _moe_groups(n_tok, seq):
        o_a = _swa(proj, sinks[0], t0 // seq, tok_g // seq, seq)
        o_b = _moba(proj, t0 // seq, tok_g // seq, seq)
        x1, x1p, top_idx, gates, rank, counts = _outproj(o_a, o_b, w_o, b_o, x2, g1, be1, w_r, b_r, t0, tok_g)
        n_blocks = tok_g * TOP_K // EXPERT_ROWS + N_EXPERTS
        pos, tables = _route(top_idx, rank, counts, n_blocks)
        stages.append((t0, tok_g, x1, x1p, gates, pos, tables, n_blocks))
        if waiting is not None:
            gathered.append(experts_and_gather(*waiting))
        waiting = (stages[-1], dispatch(stages[-1]))
    gathered.append(experts_and_gather(*waiting))
    out = None
    for (t0, tok_g, x1, _, gates, _, _, _), yg in zip(stages, gathered):
        out = _combine(yg.reshape(TOP_K, tok_g, PACKED), gates, x1, g2, be2, t0 // ROW_TILE, n_tok, out)
    return out.reshape(batch, seq, d)
```

```python
import functools
import math

import jax
import jax.numpy as jnp
from jax import lax
from jax.experimental import pallas as pl
from jax.experimental.pallas import tpu as pltpu
from jax.experimental.pallas import tpu_sc as plsc

D_MODEL = 1024
HEAD_DIM = 64
N_HEADS_SWA = 8
N_KV_SWA = 2
WINDOW = 128
N_HEADS_MOBA = 8
MOBA_BLOCK = 256
MOBA_TOPK = 3
ROPE_THETA = 10000.0
N_EXPERTS = 32
TOP_K = 4
D_FF = 1024
SWIGLU_LIMIT = 7.0
SWIGLU_ALPHA = 1.702
LN_EPS = 1e-5
DEPTH = 1
DEEPNORM_ALPHA = (2 * DEPTH) ** 0.25

W_Q_SWA = N_HEADS_SWA * HEAD_DIM
W_KV_SWA = N_KV_SWA * HEAD_DIM
W_MOBA = N_HEADS_MOBA * HEAD_DIM
IN_WIDTH = W_Q_SWA + 2 * W_KV_SWA + 3 * W_MOBA
LANES = 128
COL_K_SWA = W_Q_SWA // LANES
COL_V_SWA = COL_K_SWA + 1
COL_Q_MOBA = COL_V_SWA + 1
COL_K_MOBA = COL_Q_MOBA + W_MOBA // LANES
COL_V_MOBA = COL_K_MOBA + W_MOBA // LANES

ROW_TILE = 512
EXPERT_ROWS = 1024
EXPERT_HALF = 512
PACKED = D_MODEL // 2
NEG_BIG = -1e30
LOG2E = 1.4426950408889634
MOBA_LOOKAHEAD = 1
SWA_UNROLL = 3
MOE_SPLIT = (3, 1)
VMEM_LIMIT = 48 * 1024 * 1024
EXPERT_VMEM_LIMIT = 60 * 1024 * 1024

SC_CORES = 2
SC_SUBCORES = 16
SC_WORKERS = SC_CORES * SC_SUBCORES
SC_ROWS = 64

_PROJ_CHUNKS = (
    (0, 256, 256, True), (256, 256, 256, True),
    (512, 256, 128, False),
    (768, 256, 256, True), (1024, 256, 256, True),
    (1280, 256, 256, False), (1536, 256, 256, False),
    (1792, 256, 0, False), (2048, 256, 0, False),
)


def _params(*sem):
    return pltpu.CompilerParams(dimension_semantics=sem, vmem_limit_bytes=VMEM_LIMIT)


def _pack_bf16_pairs(v):
    n = v.shape[1] // 2
    bits = lax.bitcast_convert_type(v.astype(jnp.bfloat16).astype(jnp.float32), jnp.uint32)
    word = (bits[:, :n] >> 16) | (bits[:, n:] & jnp.uint32(0xFFFF0000))
    return lax.bitcast_convert_type(word, jnp.int32)


def _unpack_bf16_pairs(word):
    bits = lax.bitcast_convert_type(word, jnp.uint32)
    lo = lax.bitcast_convert_type(bits << 16, jnp.float32)
    hi = lax.bitcast_convert_type(bits & jnp.uint32(0xFFFF0000), jnp.float32)
    return lo, hi


def _inproj_kernel(x_ref, w_ref, b_ref, cos_ref, sin_ref, o_ref):
    xb = x_ref[...].astype(jnp.bfloat16)
    for start, width, rope, scaled in _PROJ_CHUNKS:
        t = jnp.dot(xb, w_ref[:, start:start + width], preferred_element_type=jnp.float32)
        t = t + b_ref[:, start:start + width]
        if rope:
            lane = lax.broadcasted_iota(jnp.int32, t.shape, 1)
            first_half = (lane % HEAD_DIM) < (HEAD_DIM // 2)
            rot = jnp.where(first_half,
                            pltpu.roll(t, width - HEAD_DIM // 2, 1),
                            pltpu.roll(t, HEAD_DIM // 2, 1))
            roped = t * cos_ref[:, :width] + rot * sin_ref[:, :width]
            t = roped if rope == width else jnp.where(lane < rope, roped, t)
        if scaled:
            t = t * (HEAD_DIM ** -0.5 * LOG2E)
        o_ref[:, start:start + width] = t.astype(o_ref.dtype)


def _inproj(x2, w_in, b_in, cos_t, sin_t, seq):
    n_tok = x2.shape[0]
    per_seq = seq // ROW_TILE
    return pl.pallas_call(
        _inproj_kernel,
        grid=(n_tok // ROW_TILE,),
        in_specs=[
            pl.BlockSpec((ROW_TILE, D_MODEL), lambda i: (i, 0)),
            pl.BlockSpec((D_MODEL, IN_WIDTH), lambda i: (0, 0)),
            pl.BlockSpec((1, IN_WIDTH), lambda i: (0, 0)),
            pl.BlockSpec((ROW_TILE, 256), lambda i: (i % per_seq, 0)),
            pl.BlockSpec((ROW_TILE, 256), lambda i: (i % per_seq, 0)),
        ],
        out_specs=pl.BlockSpec((ROW_TILE, IN_WIDTH), lambda i: (i, 0)),
        out_shape=jax.ShapeDtypeStruct((n_tok, IN_WIDTH), jnp.bfloat16),
        compiler_params=_params("parallel"),
        name="inproj_rope",
    )(x2, w_in, b_in, cos_t, sin_t)


def _swa_kernel(sink_ref, q_ref, k_ref, v_ref, o_ref, kd_ref, vd_ref, *, seq):
    lane = lax.broadcasted_iota(jnp.int32, (seq, LANES), 1)
    low = lane < HEAD_DIM
    k = k_ref[...].astype(jnp.float32)
    kr = pltpu.roll(k, HEAD_DIM, 1)
    kd_ref[0, WINDOW:, :] = jnp.where(low, k, kr).astype(kd_ref.dtype)
    kd_ref[1, WINDOW:, :] = jnp.where(low, kr, k).astype(kd_ref.dtype)
    v = v_ref[...].astype(jnp.float32)
    vd_ref[0, WINDOW:, :] = jnp.where(low, v, 1.0).astype(vd_ref.dtype)
    vd_ref[1, WINDOW:, :] = jnp.where(low, pltpu.roll(v, HEAD_DIM, 1), 1.0).astype(vd_ref.dtype)
    kd_ref[:, :WINDOW, :] = jnp.zeros((N_KV_SWA, WINDOW, LANES), kd_ref.dtype)
    vd_ref[:, :WINDOW, :] = jnp.zeros((N_KV_SWA, WINDOW, LANES), vd_ref.dtype)

    group = N_HEADS_SWA // N_KV_SWA
    rows = group * WINDOW
    r_in = lax.broadcasted_iota(jnp.int32, (rows, 2 * WINDOW), 0) % WINDOW
    c_id = lax.broadcasted_iota(jnp.int32, (rows, 2 * WINDOW), 1)
    band = (c_id > r_in) & (c_id <= r_in + WINDOW)
    head_in_group = lax.broadcasted_iota(jnp.int32, (rows, 1), 0) // WINDOW
    qlane_low = lax.broadcasted_iota(jnp.int32, (WINDOW, LANES), 1) < HEAD_DIM
    sinks = []
    for g in range(N_KV_SWA):
        col = jnp.zeros((rows, 1), jnp.float32)
        for j in range(group):
            col = jnp.where(head_in_group == j, sink_ref[g * group + j] * LOG2E, col)
        sinks.append(col)

    def scores(n, mask):
        r0 = pl.multiple_of(n * WINDOW, WINDOW)
        out = []
        for g in range(N_KV_SWA):
            parts = []
            for c in (2 * g, 2 * g + 1):
                qc = q_ref[pl.ds(r0, WINDOW), c * LANES:(c + 1) * LANES]
                zero = jnp.zeros_like(qc)
                parts.append(jnp.where(qlane_low, qc, zero))
                parts.append(jnp.where(qlane_low, zero, qc))
            qcat = jnp.concatenate(parts, axis=0)
            kd = kd_ref[g, pl.ds(r0, 2 * WINDOW), :]
            s = lax.dot_general(qcat, kd, (((1,), (1,)), ((), ())),
                                preferred_element_type=jnp.float32)
            out.append(jnp.where(mask, s, -jnp.inf))
        return out

    def finish(n, scored):
        r0 = pl.multiple_of(n * WINDOW, WINDOW)
        for g, s in enumerate(scored):
            vd = vd_ref[g, pl.ds(r0, 2 * WINDOW), :]
            m = jnp.maximum(jnp.max(s, axis=-1, keepdims=True), sinks[g])
            p = jnp.exp2(s - m)
            o = jnp.dot(p.astype(vd.dtype), vd, preferred_element_type=jnp.float32)
            sink_term = jnp.exp2(sinks[g] - m)
            for ci, c in enumerate((2 * g, 2 * g + 1)):
                o_lo = o[(2 * ci) * WINDOW:(2 * ci + 1) * WINDOW]
                o_hi = o[(2 * ci + 1) * WINDOW:(2 * ci + 2) * WINDOW]
                e_lo = sink_term[(2 * ci) * WINDOW:(2 * ci + 1) * WINDOW]
                e_hi = sink_term[(2 * ci + 1) * WINDOW:(2 * ci + 2) * WINDOW]
                num = jnp.where(qlane_low, o_lo, pltpu.roll(o_hi, HEAD_DIM, 1))
                den = jnp.where(qlane_low, pltpu.roll(o_lo, HEAD_DIM, 1) + e_lo, o_hi + e_hi)
                o_ref[pl.ds(r0, WINDOW), c * LANES:(c + 1) * LANES] = (num / den).astype(o_ref.dtype)

    finish(0, scores(0, band & (c_id >= WINDOW)))

    def body(it, carry):
        n0 = 1 + it * SWA_UNROLL
        scored = [scores(n0 + u, band) for u in range(SWA_UNROLL)]
        for u in range(SWA_UNROLL):
            finish(n0 + u, scored[u])
        return carry

    lax.fori_loop(0, (seq // WINDOW - 1) // SWA_UNROLL, body, 0)


def _swa(proj, sinks, b0, batch, seq):
    grid_spec = pltpu.PrefetchScalarGridSpec(
        num_scalar_prefetch=0,
        grid=(batch,),
        in_specs=[
            pl.BlockSpec(memory_space=pltpu.SMEM),
            pl.BlockSpec((seq, W_Q_SWA), lambda b: (b + b0, 0)),
            pl.BlockSpec((seq, LANES), lambda b: (b + b0, COL_K_SWA)),
            pl.BlockSpec((seq, LANES), lambda b: (b + b0, COL_V_SWA)),
        ],
        out_specs=pl.BlockSpec((seq, W_Q_SWA), lambda b: (b, 0)),
        scratch_shapes=[pltpu.VMEM((N_KV_SWA, WINDOW + seq, LANES), jnp.bfloat16),
                        pltpu.VMEM((N_KV_SWA, WINDOW + seq, LANES), jnp.bfloat16)],
    )
    return pl.pallas_call(
        functools.partial(_swa_kernel, seq=seq),
        grid_spec=grid_spec,
        out_shape=jax.ShapeDtypeStruct((batch * seq, W_Q_SWA), jnp.bfloat16),
        compiler_params=_params("parallel"),
        name="swa_sink_attention",
    )(sinks, proj, proj, proj)


def _moba_kernel(q_ref, k_ref, v_ref, o_ref, qa_ref, ka_ref, va_ref, *, seq):
    nblk = seq // MOBA_BLOCK
    pad_rows = 16
    q_all, k_all, v_all = q_ref[...], k_ref[...], v_ref[...]
    kmean = jnp.sum(k_all.astype(jnp.float32).reshape(nblk, MOBA_BLOCK, LANES), axis=1) / MOBA_BLOCK
    kmean = jnp.concatenate([kmean, jnp.zeros((pad_rows - nblk, LANES), jnp.float32)], axis=0)
    klane_low = lax.broadcasted_iota(jnp.int32, (pad_rows, LANES), 1) < HEAD_DIM
    lane = lax.broadcasted_iota(jnp.int32, (seq, LANES), 1)
    low = lane < HEAD_DIM
    key_blk = lax.broadcasted_iota(jnp.int32, (seq, LANES), 0) // MOBA_BLOCK
    j_id = lax.broadcasted_iota(jnp.int32, (pad_rows, seq), 0)
    q_blk = lax.broadcasted_iota(jnp.int32, (pad_rows, seq), 1) // MOBA_BLOCK
    eligible = j_id < q_blk

    for half in range(2):
        own = low if half == 0 else ~low
        spare = HEAD_DIM if half == 0 else 0
        ka_ref[half] = jnp.where(own, k_all, (lane - spare == key_blk).astype(k_all.dtype))
        va_ref[half] = jnp.where(own, v_all, jnp.ones_like(v_all))
        km = jnp.where(klane_low if half == 0 else ~klane_low, kmean, 0.0).astype(jnp.bfloat16)
        gate = lax.dot_general(km, q_all, (((1,), (1,)), ((), ())),
                               preferred_element_type=jnp.float32)
        gate = jnp.where(eligible, gate, -jnp.inf)
        beaten = jnp.zeros((pad_rows, seq), jnp.int32)
        for jp in range(nblk):
            row = gate[jp:jp + 1, :]
            wins = (row > gate) | ((row == gate) & (jp < j_id))
            beaten = beaten + wins.astype(jnp.int32)
        dropped = eligible & (beaten >= MOBA_TOPK)
        bias = jnp.where(dropped, NEG_BIG, 0.0)
        pieces = [bias, jnp.zeros((LANES - spare - pad_rows, seq), jnp.float32)]
        if spare:
            pieces.insert(0, jnp.zeros((spare, seq), jnp.float32))
        bias_t = jnp.concatenate(pieces, axis=0).T
        qa_ref[half] = jnp.where(own, q_all, bias_t.astype(q_all.dtype))

    qlane_low = lax.broadcasted_iota(jnp.int32, (MOBA_BLOCK, LANES), 1) < HEAD_DIM
    rr = lax.broadcasted_iota(jnp.int32, (MOBA_BLOCK, MOBA_BLOCK), 0)
    cc = lax.broadcasted_iota(jnp.int32, (MOBA_BLOCK, MOBA_BLOCK), 1)
    causal = cc <= rr

    def scores(i, half):
        r0 = i * MOBA_BLOCK
        n_keys = r0 + MOBA_BLOCK
        s = lax.dot_general(qa_ref[half, r0:n_keys, :], ka_ref[half, 0:n_keys, :], (((1,), (1,)), ((), ())),
                            preferred_element_type=jnp.float32)
        own_blk = jnp.where(causal, s[:, r0:n_keys], NEG_BIG)
        return jnp.concatenate([s[:, :r0], own_blk], axis=1) if i else own_blk

    units = [(i, half) for i in range(nblk) for half in range(2)]
    pending = [scores(*u) for u in units[:MOBA_LOOKAHEAD]]
    acc = []
    for n, (i, half) in enumerate(units):
        s = pending.pop(0)
        if n + MOBA_LOOKAHEAD < len(units):
            pending.append(scores(*units[n + MOBA_LOOKAHEAD]))
        n_keys = (i + 1) * MOBA_BLOCK
        m = jnp.max(s, axis=-1, keepdims=True)
        p = jnp.exp2(s - m).astype(jnp.bfloat16)
        acc.append(jnp.dot(p, va_ref[half, 0:n_keys, :], preferred_element_type=jnp.float32))
        if half == 1:
            num = jnp.where(qlane_low, acc[0], acc[1])
            den = pltpu.roll(jnp.where(qlane_low, acc[1], acc[0]), HEAD_DIM, 1)
            o_ref[i * MOBA_BLOCK:n_keys, :] = (num / den).astype(o_ref.dtype)
            acc = []


def _moba(proj, b0, batch, seq):
    pairs = W_MOBA // LANES
    return pl.pallas_call(
        functools.partial(_moba_kernel, seq=seq),
        grid=(batch, pairs),
        in_specs=[
            pl.BlockSpec((seq, LANES), lambda b, p: (b + b0, COL_Q_MOBA + p)),
            pl.BlockSpec((seq, LANES), lambda b, p: (b + b0, COL_K_MOBA + p)),
            pl.BlockSpec((seq, LANES), lambda b, p: (b + b0, COL_V_MOBA + p)),
        ],
        out_specs=pl.BlockSpec((seq, LANES), lambda b, p: (b, p)),
        out_shape=jax.ShapeDtypeStruct((batch * seq, W_MOBA), jnp.bfloat16),
        scratch_shapes=[pltpu.VMEM((2, seq, LANES), jnp.bfloat16)] * 3,
        compiler_params=_params("parallel", "parallel"),
        name="moba_attention",
    )(proj, proj, proj)


def _layer_norm(h, g, b):
    mu = jnp.mean(h, axis=-1, keepdims=True)
    d = h - mu
    var = jnp.mean(d * d, axis=-1, keepdims=True)
    return d * lax.rsqrt(var + LN_EPS) * g + b


def _outproj_kernel(oa_ref, ob_ref, wo_ref, bo_ref, x_ref, g_ref, b_ref, wr_ref, br_ref, tri_ref,
                    x1_ref, x1p_ref, idx_ref, gate_ref, rank_ref, count_ref, running_ref):
    @pl.when(pl.program_id(0) == 0)
    def _():
        running_ref[...] = jnp.zeros(running_ref.shape, running_ref.dtype)

    half_rows = ROW_TILE // 2
    halves = [slice(h * half_rows, (h + 1) * half_rows) for h in range(2)]

    def project(rows):
        heads = jnp.concatenate([oa_ref[rows, :], ob_ref[rows, :]], axis=1)
        return jnp.dot(heads, wo_ref[...], preferred_element_type=jnp.float32) + bo_ref[...]

    def normalise(rows, mix):
        x1 = _layer_norm(DEEPNORM_ALPHA * x_ref[rows, :] + mix, g_ref[...], b_ref[...])
        x1_ref[rows, :] = x1
        x1p_ref[rows, :] = _pack_bf16_pairs(x1)
        return x1.astype(jnp.bfloat16)

    mixes = [project(rows) for rows in halves]
    x1b = jnp.concatenate([normalise(rows, mix) for rows, mix in zip(halves, mixes)], axis=0)

    logits = lax.dot_general(wr_ref[...], x1b, (((1,), (1,)), ((), ())),
                             preferred_element_type=jnp.float32) + br_ref[...]
    expert = lax.broadcasted_iota(jnp.int32, logits.shape, 0)
    idx_rows, val_rows, onehots = [], [], []
    top = None
    total = None
    for k in range(TOP_K):
        m = jnp.max(logits, axis=0, keepdims=True)
        idx = jnp.min(jnp.where(logits == m, expert, N_EXPERTS), axis=0, keepdims=True)
        picked = expert == idx
        onehots.append(picked)
        logits = jnp.where(picked, -jnp.inf, logits)
        if k == 0:
            top = m
        e = jnp.exp(m - top)
        total = e if k == 0 else total + e
        idx_rows.append(idx)
        val_rows.append(e)
    idx_ref[...] = jnp.concatenate(idx_rows, axis=0)
    gates_t = jnp.concatenate(val_rows, axis=0) / total
    gates_t = jnp.concatenate([gates_t, jnp.zeros((LANES - TOP_K, gates_t.shape[1]), jnp.float32)], axis=0)
    gate_ref[...] = gates_t.T[:, :TOP_K]

    picks = jnp.concatenate([p.astype(jnp.bfloat16) for p in onehots], axis=0)
    before = jnp.dot(picks, tri_ref[...], preferred_element_type=jnp.float32)
    base = running_ref[...]
    rank_rows = []
    for k in range(TOP_K):
        pk = onehots[k].astype(jnp.float32)
        here = before[k * N_EXPERTS:(k + 1) * N_EXPERTS, :] + base
        rank_rows.append(jnp.sum(pk * here, axis=0, keepdims=True))
        base = base + jnp.sum(pk, axis=1, keepdims=True)
    running_ref[...] = base
    rank_ref[...] = jnp.concatenate(rank_rows, axis=0).astype(jnp.int32)
    count_ref[...] = base.astype(jnp.int32)


def _outproj(o_a, o_b, w_o, b_out, x2, g, b, w_r, b_r, t0, n_tok):
    tile0 = t0 // ROW_TILE
    src = lambda i: (i + tile0, 0)
    row = lambda i: (i, 0)
    fixed = lambda i: (0, 0)
    r = lax.broadcasted_iota(jnp.int32, (ROW_TILE, ROW_TILE), 0)
    c = lax.broadcasted_iota(jnp.int32, (ROW_TILE, ROW_TILE), 1)
    tri = (r < c).astype(jnp.bfloat16)
    return pl.pallas_call(
        _outproj_kernel,
        grid=(n_tok // ROW_TILE,),
        in_specs=[
            pl.BlockSpec((ROW_TILE, W_Q_SWA), row),
            pl.BlockSpec((ROW_TILE, W_MOBA), row),
            pl.BlockSpec((W_Q_SWA + W_MOBA, D_MODEL), fixed),
            pl.BlockSpec((1, D_MODEL), fixed),
            pl.BlockSpec((ROW_TILE, D_MODEL), src),
            pl.BlockSpec((1, D_MODEL), fixed),
            pl.BlockSpec((1, D_MODEL), fixed),
            pl.BlockSpec((N_EXPERTS, D_MODEL), fixed),
            pl.BlockSpec((N_EXPERTS, 1), fixed),
            pl.BlockSpec((ROW_TILE, ROW_TILE), fixed),
        ],
        out_specs=[
            pl.BlockSpec((ROW_TILE, D_MODEL), row),
            pl.BlockSpec((ROW_TILE, PACKED), row),
            pl.BlockSpec((TOP_K, ROW_TILE), lambda i: (0, i)),
            pl.BlockSpec((ROW_TILE, TOP_K), row),
            pl.BlockSpec((TOP_K, ROW_TILE), lambda i: (0, i)),
            pl.BlockSpec((N_EXPERTS, 1), fixed),
        ],
        out_shape=[
            jax.ShapeDtypeStruct((n_tok, D_MODEL), jnp.float32),
            jax.ShapeDtypeStruct((n_tok, PACKED), jnp.int32),
            jax.ShapeDtypeStruct((TOP_K, n_tok), jnp.int32),
            jax.ShapeDtypeStruct((n_tok, TOP_K), jnp.float32),
            jax.ShapeDtypeStruct((TOP_K, n_tok), jnp.int32),
            jax.ShapeDtypeStruct((N_EXPERTS, 1), jnp.int32),
        ],
        scratch_shapes=[pltpu.VMEM((N_EXPERTS, 1), jnp.float32)],
        compiler_params=_params("arbitrary"),
        name="outproj_ln_router",
    )(o_a, o_b, w_o, b_out, x2, g, b, w_r, b_r, tri)


def _sc_worker_id():
    return lax.axis_index("s") * SC_CORES + lax.axis_index("c")


def _sc_scatter_rows(rows, pos3, n_out):
    n_tok = pos3.shape[0] * SC_ROWS
    steps = n_tok // SC_ROWS // SC_WORKERS
    assert steps * SC_ROWS * SC_WORKERS == n_tok and steps % 2 == 0
    mesh = plsc.VectorSubcoreMesh(core_axis_name="c", subcore_axis_name="s")

    @functools.partial(
        pl.kernel, mesh=mesh,
        out_type=jax.ShapeDtypeStruct((n_out, PACKED), jnp.int32),
        scratch_types=[pltpu.VMEM((2, TOP_K, SC_ROWS), jnp.int32), pltpu.VMEM((2, SC_ROWS, PACKED), jnp.int32),
                       pltpu.SemaphoreType.DMA((2,)), pltpu.SemaphoreType.DMA((2,))],
        name="sc_dispatch_scatter")
    def scatter(x_hbm, pos_hbm, out_hbm, idx_v, rows_v, sem_ld, sem_st):
        base = _sc_worker_id() * steps

        def loads(s, b):
            return (pltpu.make_async_copy(pos_hbm.at[base + s], idx_v.at[b], sem_ld.at[b]),
                    pltpu.make_async_copy(x_hbm.at[pl.ds((base + s) * SC_ROWS, SC_ROWS)], rows_v.at[b],
                                          sem_ld.at[b]))

        def stores(b):
            return [pltpu.make_async_copy(rows_v.at[b], out_hbm.at[idx_v.at[b, k]], sem_st.at[b])
                    for k in range(TOP_K)]

        for c in loads(0, 0):
            c.start()

        @pl.loop(0, steps, step=2)
        def _(s0):
            for b in range(2):
                s = s0 + b
                for c in loads(s, b):
                    c.wait()

                @pl.when(s >= 1)
                def _():
                    for c in stores(1 - b):
                        c.wait()

                @pl.when(s + 1 < steps)
                def _():
                    for c in loads(s + 1, 1 - b):
                        c.start()

                for c in stores(b):
                    c.start()

        for c in stores(1):
            c.wait()

    return scatter(rows, pos3)


def _sc_gather_rows(table, idx2):
    n_blk = idx2.shape[0]
    steps = n_blk // SC_WORKERS
    assert steps * SC_WORKERS == n_blk and steps % 2 == 0 and idx2.shape[1] == SC_ROWS
    mesh = plsc.VectorSubcoreMesh(core_axis_name="c", subcore_axis_name="s")

    @functools.partial(
        pl.kernel, mesh=mesh,
        out_type=jax.ShapeDtypeStruct((n_blk * SC_ROWS, PACKED), jnp.int32),
        scratch_types=[pltpu.VMEM((steps, SC_ROWS), jnp.int32), pltpu.VMEM((2, SC_ROWS, PACKED), jnp.int32),
                       pltpu.SemaphoreType.DMA((2,)), pltpu.SemaphoreType.DMA((2,))],
        name="sc_combine_gather")
    def gather(y_hbm, idx_hbm, out_hbm, idx_v, rows_v, sem_ld, sem_st):
        base = _sc_worker_id() * steps
        pltpu.sync_copy(idx_hbm.at[pl.ds(base, steps)], idx_v)

        def fetch(s, b):
            return pltpu.make_async_copy(y_hbm.at[idx_v.at[s]], rows_v.at[b], sem_ld.at[b])

        def store(s, b):
            return pltpu.make_async_copy(rows_v.at[b], out_hbm.at[pl.ds((base + s) * SC_ROWS, SC_ROWS)],
                                         sem_st.at[b])

        fetch(0, 0).start()

        @pl.loop(0, steps, step=2)
        def _(s0):
            for b in range(2):
                s = s0 + b
                fetch(s, b).wait()

                @pl.when(s >= 1)
                def _():
                    store(s - 1, 1 - b).wait()

                @pl.when(s + 1 < steps)
                def _():
                    fetch(s + 1, 1 - b).start()

                store(s, b).start()

        store(steps - 1, 1).wait()

    return gather(table, idx2)


def _expert_kernel(be_ref, br_ref, slot_ref, next_ref, nv_ref, x_ref, w1_hbm, perm_ref, bg_ref, bl_ref, w2_hbm,
                   b2_ref, y_ref, w1f_ref, w2f_ref, wg_ref, wl_ref, w2b_ref, sem):
    i = pl.program_id(0)
    live = i < nv_ref[0]
    new_expert = (i == 0) | (be_ref[i] != be_ref[jnp.maximum(i - 1, 0)])

    def weight_copies(expert, slot):
        return (pltpu.make_async_copy(w1_hbm.at[expert], w1f_ref.at[slot], sem.at[0, slot]),
                pltpu.make_async_copy(w2_hbm.at[expert], w2f_ref.at[slot], sem.at[1, slot]))

    @pl.when(live & new_expert)
    def _():
        slot = slot_ref[i]

        @pl.when(i == 0)
        def _():
            for copy in weight_copies(be_ref[i], slot):
                copy.start()

        for copy in weight_copies(be_ref[i], slot):
            copy.wait()

        @pl.when(next_ref[i] >= 0)
        def _():
            for copy in weight_copies(next_ref[i], 1 - slot):
                copy.start()

        for c in range(2 * D_FF // 256):
            t = w1f_ref[slot, :, c * 256:(c + 1) * 256].astype(jnp.bfloat16)
            r = jnp.dot(t, perm_ref[...], preferred_element_type=jnp.float32)
            wg_ref[:, c * LANES:(c + 1) * LANES] = r[:, :LANES].astype(wg_ref.dtype)
            wl_ref[:, c * LANES:(c + 1) * LANES] = r[:, LANES:].astype(wl_ref.dtype)
        w2b_ref[...] = w2f_ref[slot].astype(w2b_ref.dtype)

    def mlp(rows):
        valid = lax.broadcasted_iota(jnp.int32, (rows, PACKED), 0) < br_ref[i]
        lo, hi = _unpack_bf16_pairs(jnp.where(valid, x_ref[0:rows, :], 0))
        xb = jnp.concatenate([lo.astype(jnp.bfloat16), hi.astype(jnp.bfloat16)], axis=1)
        hg = jnp.dot(xb, wg_ref[...], preferred_element_type=jnp.float32) + bg_ref[0]
        hl = jnp.dot(xb, wl_ref[...], preferred_element_type=jnp.float32) + bl_ref[0]
        glu = jnp.minimum(hg, SWIGLU_LIMIT)
        lin = jnp.clip(hl, -SWIGLU_LIMIT, SWIGLU_LIMIT)
        act = glu * jax.nn.sigmoid(SWIGLU_ALPHA * glu) * (lin + 1.0)
        y = jnp.dot(act.astype(jnp.bfloat16), w2b_ref[...], preferred_element_type=jnp.float32) + b2_ref[0]
        y_ref[0:rows, :] = _pack_bf16_pairs(y)

    @pl.when(live & (br_ref[i] > EXPERT_HALF))
    def _():
        mlp(EXPERT_ROWS)

    @pl.when(live & (br_ref[i] <= EXPERT_HALF))
    def _():
        mlp(EXPERT_HALF)
        y_ref[EXPERT_HALF:, :] = jnp.zeros((EXPERT_ROWS - EXPERT_HALF, PACKED), y_ref.dtype)

    @pl.when(jnp.logical_not(live))
    def _():
        y_ref[...] = jnp.zeros(y_ref.shape, y_ref.dtype)


def _experts(tables, xg, w1, perm, b1g, b1l, w2, b2):
    n_rows = xg.shape[0]
    n_blocks = n_rows // EXPERT_ROWS

    def row(i, be, br, slot, nxt, nv):
        return (jnp.minimum(i, nv[0] - 1), 0)

    def per_expert(i, be, br, slot, nxt, nv):
        return (be[i], 0, 0)

    grid_spec = pltpu.PrefetchScalarGridSpec(
        num_scalar_prefetch=5,
        grid=(n_blocks,),
        in_specs=[
            pl.BlockSpec((EXPERT_ROWS, PACKED), row),
            pl.BlockSpec(memory_space=pl.ANY),
            pl.BlockSpec((256, 256), lambda i, *_: (0, 0)),
            pl.BlockSpec((1, 1, D_FF), per_expert),
            pl.BlockSpec((1, 1, D_FF), per_expert),
            pl.BlockSpec(memory_space=pl.ANY),
            pl.BlockSpec((1, 1, D_MODEL), per_expert),
        ],
        out_specs=pl.BlockSpec((EXPERT_ROWS, PACKED), lambda i, *_: (i, 0)),
        scratch_shapes=[pltpu.VMEM((2, D_MODEL, 2 * D_FF), jnp.float32), pltpu.VMEM((2, D_FF, D_MODEL), jnp.float32),
                        pltpu.VMEM((D_MODEL, D_FF), jnp.bfloat16), pltpu.VMEM((D_MODEL, D_FF), jnp.bfloat16),
                        pltpu.VMEM((D_FF, D_MODEL), jnp.bfloat16), pltpu.SemaphoreType.DMA((2, 2))],
    )
    return pl.pallas_call(
        _expert_kernel,
        grid_spec=grid_spec,
        out_shape=jax.ShapeDtypeStruct((n_rows, PACKED), jnp.int32),
        compiler_params=pltpu.CompilerParams(dimension_semantics=("arbitrary",),
                                             vmem_limit_bytes=EXPERT_VMEM_LIMIT),
        name="grouped_experts",
    )(*tables, xg, w1, perm, b1g, b1l, w2, b2)


def _split_columns_perm():
    i = lax.broadcasted_iota(jnp.int32, (256, 256), 0)
    o = lax.broadcasted_iota(jnp.int32, (256, 256), 1)
    return (i == jnp.where(o < LANES, 2 * o, 2 * (o - LANES) + 1)).astype(jnp.bfloat16)


def _combine_kernel(y_ref, gate_ref, x1_ref, g_ref, b_ref, *rest):
    o_ref = rest[-1]
    gates = gate_ref[...]
    lo_sum = None
    hi_sum = None
    for k in range(TOP_K):
        lo, hi = _unpack_bf16_pairs(y_ref[k])
        gk = gates[:, k:k + 1]
        lo_sum = gk * lo if k == 0 else lo_sum + gk * lo
        hi_sum = gk * hi if k == 0 else hi_sum + gk * hi
    moe = jnp.concatenate([lo_sum, hi_sum], axis=1)
    o_ref[...] = _layer_norm(DEEPNORM_ALPHA * x1_ref[...] + moe, g_ref[...], b_ref[...])


def _combine(yg, gates, x1, g, b, tile0, n_tok, prev):
    row = lambda i: (i, 0)
    fixed = lambda i: (0, 0)
    in_specs = [
        pl.BlockSpec((TOP_K, ROW_TILE, PACKED), lambda i: (0, i, 0)),
        pl.BlockSpec((ROW_TILE, TOP_K), row),
        pl.BlockSpec((ROW_TILE, D_MODEL), row),
        pl.BlockSpec((1, D_MODEL), fixed),
        pl.BlockSpec((1, D_MODEL), fixed),
    ]
    args = [yg, gates, x1, g, b]
    aliases = {}
    if prev is not None:
        in_specs.append(pl.BlockSpec(memory_space=pl.ANY))
        args.append(prev)
        aliases = {len(args) - 1: 0}
    return pl.pallas_call(
        _combine_kernel,
        grid=(yg.shape[1] // ROW_TILE,),
        in_specs=in_specs,
        out_specs=pl.BlockSpec((ROW_TILE, D_MODEL), lambda i: (i + tile0, 0)),
        out_shape=jax.ShapeDtypeStruct((n_tok, D_MODEL), jnp.float32),
        input_output_aliases=aliases,
        compiler_params=_params("parallel"),
        name="combine_ln",
    )(*args)


def _route(top_idx, rank, counts, n_blocks):
    counts = counts[:, 0]
    blocks_per = (counts + EXPERT_ROWS - 1) // EXPERT_ROWS
    blk_end = jnp.cumsum(blocks_per)
    blk_start = blk_end - blocks_per
    pos = rank
    for e in range(N_EXPERTS):
        pos = pos + jnp.where(top_idx == e, blk_start[e] * EXPERT_ROWS, 0)
    n_valid = blk_end[-1:].astype(jnp.int32)
    experts = jnp.arange(N_EXPERTS, dtype=jnp.int32)
    has_rows = blocks_per > 0
    run_of = jnp.cumsum(has_rows.astype(jnp.int32)) - 1
    later = (experts[None, :] > experts[:, None]) & has_rows[None, :]
    next_of = jnp.min(jnp.where(later, experts[None, :], N_EXPERTS), axis=1)
    next_of = jnp.where(next_of == N_EXPERTS, -1, next_of)
    blk = jnp.arange(n_blocks, dtype=jnp.int32)
    owner = ((blk[:, None] >= blk_start[None, :]) & (blk[:, None] < blk_end[None, :])).astype(jnp.int32)
    pick = lambda per_expert: jnp.sum(owner * per_expert[None, :], axis=1).astype(jnp.int32)
    block_expert = pick(experts)
    block_rows = jnp.clip(pick(counts) - (blk - pick(blk_start)) * EXPERT_ROWS, 0, EXPERT_ROWS).astype(jnp.int32)
    return pos, (block_expert, block_rows, pick(run_of) % 2, pick(next_of), n_valid)


def _moe_groups(n_tok, seq):
    unit = math.lcm(2 * SC_ROWS * SC_WORKERS, seq)
    parts = sum(MOE_SPLIT)
    if n_tok % (parts * unit):
        return ((0, n_tok),)
    groups, t0 = [], 0
    for share in MOE_SPLIT:
        groups.append((t0, n_tok * share // parts))
        t0 += groups[-1][1]
    return tuple(groups)


def _rope_tables(seq):
    inv_freq = 1.0 / (ROPE_THETA ** (jnp.arange(0, HEAD_DIM, 2, dtype=jnp.float32) / HEAD_DIM))
    ang = jnp.arange(seq, dtype=jnp.float32)[:, None] * inv_freq[None, :]
    cos, sin = jnp.cos(ang), jnp.sin(ang)
    cos_t = jnp.tile(jnp.concatenate([cos, cos], axis=1), (1, 256 // HEAD_DIM))
    sin_t = jnp.tile(jnp.concatenate([-sin, sin], axis=1), (1, 256 // HEAD_DIM))
    return cos_t, sin_t


def kernel(x, w_in, b_in, sinks, w_out, b_out, ln1_g, ln1_b, w_router, b_router, w1, b1, w2, b2, ln2_g, ln2_b):
    batch, seq, d = x.shape
    assert d == D_MODEL and seq % ROW_TILE == 0 and seq % MOBA_BLOCK == 0 and w_in.shape[0] == DEPTH == 1
    assert (seq // WINDOW - 1) % SWA_UNROLL == 0
    n_tok = batch * seq
    bf16 = jnp.bfloat16
    x2 = x.reshape(n_tok, d)
    cos_t, sin_t = _rope_tables(seq)

    proj = _inproj(x2, w_in[0].astype(bf16), b_in[0].reshape(1, IN_WIDTH), cos_t, sin_t, seq)

    w_o = w_out[0].astype(bf16)
    w_r = w_router[0].T.astype(bf16)
    b_r = b_router[0].reshape(N_EXPERTS, 1)
    b_o, g1, be1 = b_out[0].reshape(1, d), ln1_g[0].reshape(1, d), ln1_b[0].reshape(1, d)
    perm = _split_columns_perm()
    b1r = b1[0].reshape(N_EXPERTS, 1, D_FF, 2)
    b1g, b1l = b1r[..., 0], b1r[..., 1]
    b2r = b2[0].reshape(N_EXPERTS, 1, d)
    g2, be2 = ln2_g[0].reshape(1, d), ln2_b[0].reshape(1, d)

    def dispatch(stage):
        t0, tok_g, x1, x1p, gates, pos, tables, n_blocks = stage
        pos3 = pos.reshape(TOP_K, tok_g // SC_ROWS, SC_ROWS).transpose(1, 0, 2)
        return _sc_scatter_rows(x1p, pos3, n_blocks * EXPERT_ROWS)

    def experts_and_gather(stage, xg):
        t0, tok_g, x1, x1p, gates, pos, tables, n_blocks = stage
        y = _experts(tables, xg, w1[0], perm, b1g, b1l, w2[0], b2r)
        return _sc_gather_rows(y, pos.reshape(tok_g * TOP_K // SC_ROWS, SC_ROWS))

    stages, gathered, waiting = [], [], None
    for t0, tok_g in _moe_groups(n_tok, seq):
        o_a = _swa(proj, sinks[0], t0 // seq, tok_g // seq, seq)
        o_b = _moba(proj, t0 // seq, tok_g // seq, seq)
        x1, x1p, top_idx, gates, rank, counts = _outproj(o_a, o_b, w_o, b_o, x2, g1, be1, w_r, b_r, t0, tok_g)
        n_blocks = tok_g * TOP_K // EXPERT_ROWS + N_EXPERTS
        pos, tables = _route(top_idx, rank, counts, n_blocks)
        stages.append((t0, tok_g, x1, x1p, gates, pos, tables, n_blocks))
        if waiting is not None:
            gathered.append(experts_and_gather(*waiting))
        waiting = (stages[-1], dispatch(stages[-1]))
    gathered.append(experts_and_gather(*waiting))
    out = None
    for (t0, tok_g, x1, _, gates, _, _, _), yg in zip(stages, gathered):
        out = _combine(yg.reshape(TOP_K, tok_g, PACKED), gates, x1, g2, be2, t0 // ROW_TILE, n_tok, out)
    return out.reshape(batch, seq, d)
```

```python
import functools
import math

import jax
import jax.numpy as jnp
from jax import lax
from jax.experimental import pallas as pl
from jax.experimental.pallas import tpu as pltpu
from jax.experimental.pallas import tpu_sc as plsc

D_MODEL = 1024
HEAD_DIM = 64
N_HEADS_SWA = 8
N_KV_SWA = 2
WINDOW = 128
N_HEADS_MOBA = 8
MOBA_BLOCK = 256
MOBA_TOPK = 3
ROPE_THETA = 10000.0
N_EXPERTS = 32
TOP_K = 4
D_FF = 1024
SWIGLU_LIMIT = 7.0
SWIGLU_ALPHA = 1.702
LN_EPS = 1e-5
DEPTH = 1
DEEPNORM_ALPHA = (2 * DEPTH) ** 0.25

W_Q_SWA = N_HEADS_SWA * HEAD_DIM
W_KV_SWA = N_KV_SWA * HEAD_DIM
W_MOBA = N_HEADS_MOBA * HEAD_DIM
IN_WIDTH = W_Q_SWA + 2 * W_KV_SWA + 3 * W_MOBA
LANES = 128
COL_K_SWA = W_Q_SWA // LANES
COL_V_SWA = COL_K_SWA + 1
COL_Q_MOBA = COL_V_SWA + 1
COL_K_MOBA = COL_Q_MOBA + W_MOBA // LANES
COL_V_MOBA = COL_K_MOBA + W_MOBA // LANES

ROW_TILE = 512
EXPERT_ROWS = 1024
EXPERT_PART = 256
PACKED = D_MODEL // 2
NEG_BIG = -1e30
LOG2E = 1.4426950408889634
MOBA_LOOKAHEAD = 1
SWA_UNROLL = 3
MOE_SPLIT = (3, 1)
VMEM_LIMIT = 48 * 1024 * 1024
EXPERT_VMEM_LIMIT = 60 * 1024 * 1024

SC_CORES = 2
SC_SUBCORES = 16
SC_WORKERS = SC_CORES * SC_SUBCORES
SC_ROWS = 64

_PROJ_CHUNKS = (
    (0, 256, 256, True), (256, 256, 256, True),
    (512, 256, 128, False),
    (768, 256, 256, True), (1024, 256, 256, True),
    (1280, 256, 256, False), (1536, 256, 256, False),
    (1792, 256, 0, False), (2048, 256, 0, False),
)


def _params(*sem):
    return pltpu.CompilerParams(dimension_semantics=sem, vmem_limit_bytes=VMEM_LIMIT)


def _pack_bf16_pairs(v):
    n = v.shape[1] // 2
    bits = lax.bitcast_convert_type(v.astype(jnp.bfloat16).astype(jnp.float32), jnp.uint32)
    word = (bits[:, :n] >> 16) | (bits[:, n:] & jnp.uint32(0xFFFF0000))
    return lax.bitcast_convert_type(word, jnp.int32)


def _unpack_bf16_pairs(word):
    bits = lax.bitcast_convert_type(word, jnp.uint32)
    lo = lax.bitcast_convert_type(bits << 16, jnp.float32)
    hi = lax.bitcast_convert_type(bits & jnp.uint32(0xFFFF0000), jnp.float32)
    return lo, hi


def _inproj_kernel(x_ref, w_ref, b_ref, cos_ref, sin_ref, o_ref):
    xb = x_ref[...].astype(jnp.bfloat16)
    for start, width, rope, scaled in _PROJ_CHUNKS:
        t = jnp.dot(xb, w_ref[:, start:start + width], preferred_element_type=jnp.float32)
        t = t + b_ref[:, start:start + width]
        if rope:
            lane = lax.broadcasted_iota(jnp.int32, t.shape, 1)
            first_half = (lane % HEAD_DIM) < (HEAD_DIM // 2)
            rot = jnp.where(first_half,
                            pltpu.roll(t, width - HEAD_DIM // 2, 1),
                            pltpu.roll(t, HEAD_DIM // 2, 1))
            roped = t * cos_ref[:, :width] + rot * sin_ref[:, :width]
            t = roped if rope == width else jnp.where(lane < rope, roped, t)
        if scaled:
            t = t * (HEAD_DIM ** -0.5 * LOG2E)
        o_ref[:, start:start + width] = t.astype(o_ref.dtype)


def _inproj(x2, w_in, b_in, cos_t, sin_t, seq):
    n_tok = x2.shape[0]
    per_seq = seq // ROW_TILE
    return pl.pallas_call(
        _inproj_kernel,
        grid=(n_tok // ROW_TILE,),
        in_specs=[
            pl.BlockSpec((ROW_TILE, D_MODEL), lambda i: (i, 0)),
            pl.BlockSpec((D_MODEL, IN_WIDTH), lambda i: (0, 0)),
            pl.BlockSpec((1, IN_WIDTH), lambda i: (0, 0)),
            pl.BlockSpec((ROW_TILE, 256), lambda i: (i % per_seq, 0)),
            pl.BlockSpec((ROW_TILE, 256), lambda i: (i % per_seq, 0)),
        ],
        out_specs=pl.BlockSpec((ROW_TILE, IN_WIDTH), lambda i: (i, 0)),
        out_shape=jax.ShapeDtypeStruct((n_tok, IN_WIDTH), jnp.bfloat16),
        compiler_params=_params("parallel"),
        name="inproj_rope",
    )(x2, w_in, b_in, cos_t, sin_t)


def _swa_kernel(sink_ref, q_ref, k_ref, v_ref, o_ref, kd_ref, vd_ref, *, seq):
    lane = lax.broadcasted_iota(jnp.int32, (seq, LANES), 1)
    low = lane < HEAD_DIM
    k = k_ref[...].astype(jnp.float32)
    kr = pltpu.roll(k, HEAD_DIM, 1)
    kd_ref[0, WINDOW:, :] = jnp.where(low, k, kr).astype(kd_ref.dtype)
    kd_ref[1, WINDOW:, :] = jnp.where(low, kr, k).astype(kd_ref.dtype)
    v = v_ref[...].astype(jnp.float32)
    vd_ref[0, WINDOW:, :] = jnp.where(low, v, 1.0).astype(vd_ref.dtype)
    vd_ref[1, WINDOW:, :] = jnp.where(low, pltpu.roll(v, HEAD_DIM, 1), 1.0).astype(vd_ref.dtype)
    kd_ref[:, :WINDOW, :] = jnp.zeros((N_KV_SWA, WINDOW, LANES), kd_ref.dtype)
    vd_ref[:, :WINDOW, :] = jnp.zeros((N_KV_SWA, WINDOW, LANES), vd_ref.dtype)

    group = N_HEADS_SWA // N_KV_SWA
    rows = group * WINDOW
    r_in = lax.broadcasted_iota(jnp.int32, (rows, 2 * WINDOW), 0) % WINDOW
    c_id = lax.broadcasted_iota(jnp.int32, (rows, 2 * WINDOW), 1)
    band = (c_id > r_in) & (c_id <= r_in + WINDOW)
    head_in_group = lax.broadcasted_iota(jnp.int32, (rows, 1), 0) // WINDOW
    qlane_low = lax.broadcasted_iota(jnp.int32, (WINDOW, LANES), 1) < HEAD_DIM
    sinks = []
    for g in range(N_KV_SWA):
        col = jnp.zeros((rows, 1), jnp.float32)
        for j in range(group):
            col = jnp.where(head_in_group == j, sink_ref[g * group + j] * LOG2E, col)
        sinks.append(col)

    def scores(n, mask):
        r0 = pl.multiple_of(n * WINDOW, WINDOW)
        out = []
        for g in range(N_KV_SWA):
            parts = []
            for c in (2 * g, 2 * g + 1):
                qc = q_ref[pl.ds(r0, WINDOW), c * LANES:(c + 1) * LANES]
                zero = jnp.zeros_like(qc)
                parts.append(jnp.where(qlane_low, qc, zero))
                parts.append(jnp.where(qlane_low, zero, qc))
            qcat = jnp.concatenate(parts, axis=0)
            kd = kd_ref[g, pl.ds(r0, 2 * WINDOW), :]
            s = lax.dot_general(qcat, kd, (((1,), (1,)), ((), ())),
                                preferred_element_type=jnp.float32)
            out.append(jnp.where(mask, s, -jnp.inf))
        return out

    def finish(n, scored):
        r0 = pl.multiple_of(n * WINDOW, WINDOW)
        for g, s in enumerate(scored):
            vd = vd_ref[g, pl.ds(r0, 2 * WINDOW), :]
            m = jnp.maximum(jnp.max(s, axis=-1, keepdims=True), sinks[g])
            p = jnp.exp2(s - m)
            o = jnp.dot(p.astype(vd.dtype), vd, preferred_element_type=jnp.float32)
            sink_term = jnp.exp2(sinks[g] - m)
            for ci, c in enumerate((2 * g, 2 * g + 1)):
                o_lo = o[(2 * ci) * WINDOW:(2 * ci + 1) * WINDOW]
                o_hi = o[(2 * ci + 1) * WINDOW:(2 * ci + 2) * WINDOW]
                e_lo = sink_term[(2 * ci) * WINDOW:(2 * ci + 1) * WINDOW]
                e_hi = sink_term[(2 * ci + 1) * WINDOW:(2 * ci + 2) * WINDOW]
                num = jnp.where(qlane_low, o_lo, pltpu.roll(o_hi, HEAD_DIM, 1))
                den = jnp.where(qlane_low, pltpu.roll(o_lo, HEAD_DIM, 1) + e_lo, o_hi + e_hi)
                o_ref[pl.ds(r0, WINDOW), c * LANES:(c + 1) * LANES] = (num / den).astype(o_ref.dtype)

    finish(0, scores(0, band & (c_id >= WINDOW)))

    def body(it, carry):
        n0 = 1 + it * SWA_UNROLL
        nxt = scores(n0, band)
        for u in range(SWA_UNROLL):
            cur = nxt
            if u + 1 < SWA_UNROLL:
                nxt = scores(n0 + u + 1, band)
            finish(n0 + u, cur)
        return carry

    lax.fori_loop(0, (seq // WINDOW - 1) // SWA_UNROLL, body, 0)


def _swa(proj, sinks, b0, batch, seq):
    grid_spec = pltpu.PrefetchScalarGridSpec(
        num_scalar_prefetch=0,
        grid=(batch,),
        in_specs=[
            pl.BlockSpec(memory_space=pltpu.SMEM),
            pl.BlockSpec((seq, W_Q_SWA), lambda b: (b + b0, 0)),
            pl.BlockSpec((seq, LANES), lambda b: (b + b0, COL_K_SWA)),
            pl.BlockSpec((seq, LANES), lambda b: (b + b0, COL_V_SWA)),
        ],
        out_specs=pl.BlockSpec((seq, W_Q_SWA), lambda b: (b, 0)),
        scratch_shapes=[pltpu.VMEM((N_KV_SWA, WINDOW + seq, LANES), jnp.bfloat16),
                        pltpu.VMEM((N_KV_SWA, WINDOW + seq, LANES), jnp.bfloat16)],
    )
    return pl.pallas_call(
        functools.partial(_swa_kernel, seq=seq),
        grid_spec=grid_spec,
        out_shape=jax.ShapeDtypeStruct((batch * seq, W_Q_SWA), jnp.bfloat16),
        compiler_params=_params("parallel"),
        name="swa_sink_attention",
    )(sinks, proj, proj, proj)


def _moba_kernel(q_ref, k_ref, v_ref, o_ref, qa_ref, ka_ref, va_ref, *, seq):
    nblk = seq // MOBA_BLOCK
    pad_rows = 16
    q_all, k_all, v_all = q_ref[...], k_ref[...], v_ref[...]
    kmean = jnp.sum(k_all.astype(jnp.float32).reshape(nblk, MOBA_BLOCK, LANES), axis=1) / MOBA_BLOCK
    kmean = jnp.concatenate([kmean, jnp.zeros((pad_rows - nblk, LANES), jnp.float32)], axis=0)
    klane_low = lax.broadcasted_iota(jnp.int32, (pad_rows, LANES), 1) < HEAD_DIM
    lane = lax.broadcasted_iota(jnp.int32, (seq, LANES), 1)
    low = lane < HEAD_DIM
    key_blk = lax.broadcasted_iota(jnp.int32, (seq, LANES), 0) // MOBA_BLOCK
    j_id = lax.broadcasted_iota(jnp.int32, (pad_rows, seq), 0)
    q_blk = lax.broadcasted_iota(jnp.int32, (pad_rows, seq), 1) // MOBA_BLOCK
    eligible = j_id < q_blk

    for half in range(2):
        own = low if half == 0 else ~low
        spare = HEAD_DIM if half == 0 else 0
        ka_ref[half] = jnp.where(own, k_all, (lane - spare == key_blk).astype(k_all.dtype))
        va_ref[half] = jnp.where(own, v_all, jnp.ones_like(v_all))
        km = jnp.where(klane_low if half == 0 else ~klane_low, kmean, 0.0).astype(jnp.bfloat16)
        gate = lax.dot_general(km, q_all, (((1,), (1,)), ((), ())),
                               preferred_element_type=jnp.float32)
        gate = jnp.where(eligible, gate, -jnp.inf)
        beaten = jnp.zeros((pad_rows, seq), jnp.int32)
        for jp in range(nblk):
            row = gate[jp:jp + 1, :]
            wins = (row > gate) | ((row == gate) & (jp < j_id))
            beaten = beaten + wins.astype(jnp.int32)
        dropped = eligible & (beaten >= MOBA_TOPK)
        bias = jnp.where(dropped, NEG_BIG, 0.0)
        pieces = [bias, jnp.zeros((LANES - spare - pad_rows, seq), jnp.float32)]
        if spare:
            pieces.insert(0, jnp.zeros((spare, seq), jnp.float32))
        bias_t = jnp.concatenate(pieces, axis=0).T
        qa_ref[half] = jnp.where(own, q_all, bias_t.astype(q_all.dtype))

    qlane_low = lax.broadcasted_iota(jnp.int32, (MOBA_BLOCK, LANES), 1) < HEAD_DIM
    rr = lax.broadcasted_iota(jnp.int32, (MOBA_BLOCK, MOBA_BLOCK), 0)
    cc = lax.broadcasted_iota(jnp.int32, (MOBA_BLOCK, MOBA_BLOCK), 1)
    causal = cc <= rr

    def scores(i, half):
        r0 = i * MOBA_BLOCK
        n_keys = r0 + MOBA_BLOCK
        s = lax.dot_general(qa_ref[half, r0:n_keys, :], ka_ref[half, 0:n_keys, :], (((1,), (1,)), ((), ())),
                            preferred_element_type=jnp.float32)
        own_blk = jnp.where(causal, s[:, r0:n_keys], NEG_BIG)
        return jnp.concatenate([s[:, :r0], own_blk], axis=1) if i else own_blk

    units = [(i, half) for i in range(nblk) for half in range(2)]
    pending = [scores(*u) for u in units[:MOBA_LOOKAHEAD]]
    acc = []
    for n, (i, half) in enumerate(units):
        s = pending.pop(0)
        if n + MOBA_LOOKAHEAD < len(units):
            pending.append(scores(*units[n + MOBA_LOOKAHEAD]))
        n_keys = (i + 1) * MOBA_BLOCK
        m = jnp.max(s, axis=-1, keepdims=True)
        p = jnp.exp2(s - m).astype(jnp.bfloat16)
        acc.append(jnp.dot(p, va_ref[half, 0:n_keys, :], preferred_element_type=jnp.float32))
        if half == 1:
            num = jnp.where(qlane_low, acc[0], acc[1])
            den = pltpu.roll(jnp.where(qlane_low, acc[1], acc[0]), HEAD_DIM, 1)
            o_ref[i * MOBA_BLOCK:n_keys, :] = (num / den).astype(o_ref.dtype)
            acc = []


def _moba(proj, b0, batch, seq):
    pairs = W_MOBA // LANES
    return pl.pallas_call(
        functools.partial(_moba_kernel, seq=seq),
        grid=(batch, pairs),
        in_specs=[
            pl.BlockSpec((seq, LANES), lambda b, p: (b + b0, COL_Q_MOBA + p)),
            pl.BlockSpec((seq, LANES), lambda b, p: (b + b0, COL_K_MOBA + p)),
            pl.BlockSpec((seq, LANES), lambda b, p: (b + b0, COL_V_MOBA + p)),
        ],
        out_specs=pl.BlockSpec((seq, LANES), lambda b, p: (b, p)),
        out_shape=jax.ShapeDtypeStruct((batch * seq, W_MOBA), jnp.bfloat16),
        scratch_shapes=[pltpu.VMEM((2, seq, LANES), jnp.bfloat16)] * 3,
        compiler_params=_params("parallel", "parallel"),
        name="moba_attention",
    )(proj, proj, proj)


def _layer_norm(h, g, b):
    mu = jnp.mean(h, axis=-1, keepdims=True)
    d = h - mu
    var = jnp.mean(d * d, axis=-1, keepdims=True)
    return d * lax.rsqrt(var + LN_EPS) * g + b


def _outproj_kernel(oa_ref, ob_ref, wo_ref, bo_ref, x_ref, g_ref, b_ref, wr_ref, br_ref, tri_ref,
                    x1_ref, x1p_ref, idx_ref, gate_ref, rank_ref, count_ref, running_ref):
    @pl.when(pl.program_id(0) == 0)
    def _():
        running_ref[...] = jnp.zeros(running_ref.shape, running_ref.dtype)

    half_rows = ROW_TILE // 2
    halves = [slice(h * half_rows, (h + 1) * half_rows) for h in range(2)]

    def project(rows):
        heads = jnp.concatenate([oa_ref[rows, :], ob_ref[rows, :]], axis=1)
        return jnp.dot(heads, wo_ref[...], preferred_element_type=jnp.float32) + bo_ref[...]

    def normalise(rows, mix):
        x1 = _layer_norm(DEEPNORM_ALPHA * x_ref[rows, :] + mix, g_ref[...], b_ref[...])
        x1_ref[rows, :] = x1
        x1p_ref[rows, :] = _pack_bf16_pairs(x1)
        return x1.astype(jnp.bfloat16)

    mixes = [project(rows) for rows in halves]
    x1b = jnp.concatenate([normalise(rows, mix) for rows, mix in zip(halves, mixes)], axis=0)

    logits = lax.dot_general(wr_ref[...], x1b, (((1,), (1,)), ((), ())),
                             preferred_element_type=jnp.float32) + br_ref[...]
    expert = lax.broadcasted_iota(jnp.int32, logits.shape, 0)
    idx_rows, val_rows, onehots = [], [], []
    top = None
    total = None
    for k in range(TOP_K):
        m = jnp.max(logits, axis=0, keepdims=True)
        idx = jnp.min(jnp.where(logits == m, expert, N_EXPERTS), axis=0, keepdims=True)
        picked = expert == idx
        onehots.append(picked)
        logits = jnp.where(picked, -jnp.inf, logits)
        if k == 0:
            top = m
        e = jnp.exp(m - top)
        total = e if k == 0 else total + e
        idx_rows.append(idx)
        val_rows.append(e)
    idx_ref[...] = jnp.concatenate(idx_rows, axis=0)
    gates_t = jnp.concatenate(val_rows, axis=0) / total
    gates_t = jnp.concatenate([gates_t, jnp.zeros((LANES - TOP_K, gates_t.shape[1]), jnp.float32)], axis=0)
    gate_ref[...] = gates_t.T[:, :TOP_K]

    picks = jnp.concatenate([p.astype(jnp.bfloat16) for p in onehots], axis=0)
    before = jnp.dot(picks, tri_ref[...], preferred_element_type=jnp.float32)
    base = running_ref[...]
    rank_rows = []
    for k in range(TOP_K):
        pk = onehots[k].astype(jnp.float32)
        here = before[k * N_EXPERTS:(k + 1) * N_EXPERTS, :] + base
        rank_rows.append(jnp.sum(pk * here, axis=0, keepdims=True))
        base = base + jnp.sum(pk, axis=1, keepdims=True)
    running_ref[...] = base
    rank_ref[...] = jnp.concatenate(rank_rows, axis=0).astype(jnp.int32)
    count_ref[...] = base.astype(jnp.int32)


def _outproj(o_a, o_b, w_o, b_out, x2, g, b, w_r, b_r, t0, n_tok):
    tile0 = t0 // ROW_TILE
    src = lambda i: (i + tile0, 0)
    row = lambda i: (i, 0)
    fixed = lambda i: (0, 0)
    r = lax.broadcasted_iota(jnp.int32, (ROW_TILE, ROW_TILE), 0)
    c = lax.broadcasted_iota(jnp.int32, (ROW_TILE, ROW_TILE), 1)
    tri = (r < c).astype(jnp.bfloat16)
    return pl.pallas_call(
        _outproj_kernel,
        grid=(n_tok // ROW_TILE,),
        in_specs=[
            pl.BlockSpec((ROW_TILE, W_Q_SWA), row),
            pl.BlockSpec((ROW_TILE, W_MOBA), row),
            pl.BlockSpec((W_Q_SWA + W_MOBA, D_MODEL), fixed),
            pl.BlockSpec((1, D_MODEL), fixed),
            pl.BlockSpec((ROW_TILE, D_MODEL), src),
            pl.BlockSpec((1, D_MODEL), fixed),
            pl.BlockSpec((1, D_MODEL), fixed),
            pl.BlockSpec((N_EXPERTS, D_MODEL), fixed),
            pl.BlockSpec((N_EXPERTS, 1), fixed),
            pl.BlockSpec((ROW_TILE, ROW_TILE), fixed),
        ],
        out_specs=[
            pl.BlockSpec((ROW_TILE, D_MODEL), row),
            pl.BlockSpec((ROW_TILE, PACKED), row),
            pl.BlockSpec((TOP_K, ROW_TILE), lambda i: (0, i)),
            pl.BlockSpec((ROW_TILE, TOP_K), row),
            pl.BlockSpec((TOP_K, ROW_TILE), lambda i: (0, i)),
            pl.BlockSpec((N_EXPERTS, 1), fixed),
        ],
        out_shape=[
            jax.ShapeDtypeStruct((n_tok, D_MODEL), jnp.float32),
            jax.ShapeDtypeStruct((n_tok, PACKED), jnp.int32),
            jax.ShapeDtypeStruct((TOP_K, n_tok), jnp.int32),
            jax.ShapeDtypeStruct((n_tok, TOP_K), jnp.float32),
            jax.ShapeDtypeStruct((TOP_K, n_tok), jnp.int32),
            jax.ShapeDtypeStruct((N_EXPERTS, 1), jnp.int32),
        ],
        scratch_shapes=[pltpu.VMEM((N_EXPERTS, 1), jnp.float32)],
        compiler_params=_params("arbitrary"),
        name="outproj_ln_router",
    )(o_a, o_b, w_o, b_out, x2, g, b, w_r, b_r, tri)


def _sc_worker_id():
    return lax.axis_index("s") * SC_CORES + lax.axis_index("c")


def _sc_scatter_rows(rows, pos3, n_out):
    n_tok = pos3.shape[0] * SC_ROWS
    steps = n_tok // SC_ROWS // SC_WORKERS
    assert steps * SC_ROWS * SC_WORKERS == n_tok and steps % 2 == 0
    mesh = plsc.VectorSubcoreMesh(core_axis_name="c", subcore_axis_name="s")

    @functools.partial(
        pl.kernel, mesh=mesh,
        out_type=jax.ShapeDtypeStruct((n_out, PACKED), jnp.int32),
        scratch_types=[pltpu.VMEM((2, TOP_K, SC_ROWS), jnp.int32), pltpu.VMEM((2, SC_ROWS, PACKED), jnp.int32),
                       pltpu.SemaphoreType.DMA((2,)), pltpu.SemaphoreType.DMA((2,))],
        name="sc_dispatch_scatter")
    def scatter(x_hbm, pos_hbm, out_hbm, idx_v, rows_v, sem_ld, sem_st):
        base = _sc_worker_id() * steps

        def loads(s, b):
            return (pltpu.make_async_copy(pos_hbm.at[base + s], idx_v.at[b], sem_ld.at[b]),
                    pltpu.make_async_copy(x_hbm.at[pl.ds((base + s) * SC_ROWS, SC_ROWS)], rows_v.at[b],
                                          sem_ld.at[b]))

        def stores(b):
            return [pltpu.make_async_copy(rows_v.at[b], out_hbm.at[idx_v.at[b, k]], sem_st.at[b])
                    for k in range(TOP_K)]

        for c in loads(0, 0):
            c.start()

        @pl.loop(0, steps, step=2)
        def _(s0):
            for b in range(2):
                s = s0 + b
                for c in loads(s, b):
                    c.wait()

                @pl.when(s >= 1)
                def _():
                    for c in stores(1 - b):
                        c.wait()

                @pl.when(s + 1 < steps)
                def _():
                    for c in loads(s + 1, 1 - b):
                        c.start()

                for c in stores(b):
                    c.start()

        for c in stores(1):
            c.wait()

    return scatter(rows, pos3)


def _sc_gather_rows(table, idx2):
    n_blk = idx2.shape[0]
    steps = n_blk // SC_WORKERS
    assert steps * SC_WORKERS == n_blk and steps % 2 == 0 and idx2.shape[1] == SC_ROWS
    mesh = plsc.VectorSubcoreMesh(core_axis_name="c", subcore_axis_name="s")

    @functools.partial(
        pl.kernel, mesh=mesh,
        out_type=jax.ShapeDtypeStruct((n_blk * SC_ROWS, PACKED), jnp.int32),
        scratch_types=[pltpu.VMEM((steps, SC_ROWS), jnp.int32), pltpu.VMEM((2, SC_ROWS, PACKED), jnp.int32),
                       pltpu.SemaphoreType.DMA((2,)), pltpu.SemaphoreType.DMA((2,))],
        name="sc_combine_gather")
    def gather(y_hbm, idx_hbm, out_hbm, idx_v, rows_v, sem_ld, sem_st):
        base = _sc_worker_id() * steps
        pltpu.sync_copy(idx_hbm.at[pl.ds(base, steps)], idx_v)

        def fetch(s, b):
            return pltpu.make_async_copy(y_hbm.at[idx_v.at[s]], rows_v.at[b], sem_ld.at[b])

        def store(s, b):
            return pltpu.make_async_copy(rows_v.at[b], out_hbm.at[pl.ds((base + s) * SC_ROWS, SC_ROWS)],
                                         sem_st.at[b])

        fetch(0, 0).start()

        @pl.loop(0, steps, step=2)
        def _(s0):
            for b in range(2):
                s = s0 + b
                fetch(s, b).wait()

                @pl.when(s >= 1)
                def _():
                    store(s - 1, 1 - b).wait()

                @pl.when(s + 1 < steps)
                def _():
                    fetch(s + 1, 1 - b).start()

                store(s, b).start()

        store(steps - 1, 1).wait()

    return gather(table, idx2)


def _expert_kernel(be_ref, br_ref, slot_ref, next_ref, nv_ref, x_ref, w1_hbm, perm_ref, bg_ref, bl_ref, w2_hbm,
                   b2_ref, y_ref, w1f_ref, w2f_ref, wg_ref, wl_ref, w2b_ref, sem):
    i = pl.program_id(0)
    live = i < nv_ref[0]
    new_expert = (i == 0) | (be_ref[i] != be_ref[jnp.maximum(i - 1, 0)])

    def weight_copies(expert, slot):
        return (pltpu.make_async_copy(w1_hbm.at[expert], w1f_ref.at[slot], sem.at[0, slot]),
                pltpu.make_async_copy(w2_hbm.at[expert], w2f_ref.at[slot], sem.at[1, slot]))

    @pl.when(live & new_expert)
    def _():
        slot = slot_ref[i]

        @pl.when(i == 0)
        def _():
            for copy in weight_copies(be_ref[i], slot):
                copy.start()

        for copy in weight_copies(be_ref[i], slot):
            copy.wait()

        @pl.when(next_ref[i] >= 0)
        def _():
            for copy in weight_copies(next_ref[i], 1 - slot):
                copy.start()

        for c in range(2 * D_FF // 256):
            t = w1f_ref[slot, :, c * 256:(c + 1) * 256].astype(jnp.bfloat16)
            r = jnp.dot(t, perm_ref[...], preferred_element_type=jnp.float32)
            wg_ref[:, c * LANES:(c + 1) * LANES] = r[:, :LANES].astype(wg_ref.dtype)
            wl_ref[:, c * LANES:(c + 1) * LANES] = r[:, LANES:].astype(wl_ref.dtype)
        w2b_ref[...] = w2f_ref[slot].astype(w2b_ref.dtype)

    def mlp(rows):
        valid = lax.broadcasted_iota(jnp.int32, (rows, PACKED), 0) < br_ref[i]
        lo, hi = _unpack_bf16_pairs(jnp.where(valid, x_ref[0:rows, :], 0))
        xb = jnp.concatenate([lo.astype(jnp.bfloat16), hi.astype(jnp.bfloat16)], axis=1)
        hg = jnp.dot(xb, wg_ref[...], preferred_element_type=jnp.float32) + bg_ref[0]
        hl = jnp.dot(xb, wl_ref[...], preferred_element_type=jnp.float32) + bl_ref[0]
        glu = jnp.minimum(hg, SWIGLU_LIMIT)
        lin = jnp.clip(hl, -SWIGLU_LIMIT, SWIGLU_LIMIT)
        act = glu * jax.nn.sigmoid(SWIGLU_ALPHA * glu) * (lin + 1.0)
        y = jnp.dot(act.astype(jnp.bfloat16), w2b_ref[...], preferred_element_type=jnp.float32) + b2_ref[0]
        y_ref[0:rows, :] = _pack_bf16_pairs(y)

    parts = (br_ref[i] + EXPERT_PART - 1) // EXPERT_PART
    for n_parts in range(1, EXPERT_ROWS // EXPERT_PART + 1):
        @pl.when(live & (parts == n_parts))
        def _(rows=n_parts * EXPERT_PART):
            mlp(rows)
            if rows < EXPERT_ROWS:
                y_ref[rows:, :] = jnp.zeros((EXPERT_ROWS - rows, PACKED), y_ref.dtype)

    @pl.when(jnp.logical_not(live))
    def _():
        y_ref[...] = jnp.zeros(y_ref.shape, y_ref.dtype)


def _experts(tables, xg, w1, perm, b1g, b1l, w2, b2):
    n_rows = xg.shape[0]
    n_blocks = n_rows // EXPERT_ROWS

    def row(i, be, br, slot, nxt, nv):
        return (jnp.minimum(i, nv[0] - 1), 0)

    def per_expert(i, be, br, slot, nxt, nv):
        return (be[i], 0, 0)

    grid_spec = pltpu.PrefetchScalarGridSpec(
        num_scalar_prefetch=5,
        grid=(n_blocks,),
        in_specs=[
            pl.BlockSpec((EXPERT_ROWS, PACKED), row),
            pl.BlockSpec(memory_space=pl.ANY),
            pl.BlockSpec((256, 256), lambda i, *_: (0, 0)),
            pl.BlockSpec((1, 1, D_FF), per_expert),
            pl.BlockSpec((1, 1, D_FF), per_expert),
            pl.BlockSpec(memory_space=pl.ANY),
            pl.BlockSpec((1, 1, D_MODEL), per_expert),
        ],
        out_specs=pl.BlockSpec((EXPERT_ROWS, PACKED), lambda i, *_: (i, 0)),
        scratch_shapes=[pltpu.VMEM((2, D_MODEL, 2 * D_FF), jnp.float32), pltpu.VMEM((2, D_FF, D_MODEL), jnp.float32),
                        pltpu.VMEM((D_MODEL, D_FF), jnp.bfloat16), pltpu.VMEM((D_MODEL, D_FF), jnp.bfloat16),
                        pltpu.VMEM((D_FF, D_MODEL), jnp.bfloat16), pltpu.SemaphoreType.DMA((2, 2))],
    )
    return pl.pallas_call(
        _expert_kernel,
        grid_spec=grid_spec,
        out_shape=jax.ShapeDtypeStruct((n_rows, PACKED), jnp.int32),
        compiler_params=pltpu.CompilerParams(dimension_semantics=("arbitrary",),
                                             vmem_limit_bytes=EXPERT_VMEM_LIMIT),
        name="grouped_experts",
    )(*tables, xg, w1, perm, b1g, b1l, w2, b2)


def _split_columns_perm():
    i = lax.broadcasted_iota(jnp.int32, (256, 256), 0)
    o = lax.broadcasted_iota(jnp.int32, (256, 256), 1)
    return (i == jnp.where(o < LANES, 2 * o, 2 * (o - LANES) + 1)).astype(jnp.bfloat16)


def _combine_kernel(y_ref, gate_ref, x1_ref, g_ref, b_ref, *rest):
    o_ref = rest[-1]
    gates = gate_ref[...]
    lo_sum = None
    hi_sum = None
    for k in range(TOP_K):
        lo, hi = _unpack_bf16_pairs(y_ref[k])
        gk = gates[:, k:k + 1]
        lo_sum = gk * lo if k == 0 else lo_sum + gk * lo
        hi_sum = gk * hi if k == 0 else hi_sum + gk * hi
    moe = jnp.concatenate([lo_sum, hi_sum], axis=1)
    o_ref[...] = _layer_norm(DEEPNORM_ALPHA * x1_ref[...] + moe, g_ref[...], b_ref[...])


def _combine(yg, gates, x1, g, b, tile0, n_tok, prev):
    row = lambda i: (i, 0)
    fixed = lambda i: (0, 0)
    in_specs = [
        pl.BlockSpec((TOP_K, ROW_TILE, PACKED), lambda i: (0, i, 0)),
        pl.BlockSpec((ROW_TILE, TOP_K), row),
        pl.BlockSpec((ROW_TILE, D_MODEL), row),
        pl.BlockSpec((1, D_MODEL), fixed),
        pl.BlockSpec((1, D_MODEL), fixed),
    ]
    args = [yg, gates, x1, g, b]
    aliases = {}
    if prev is not None:
        in_specs.append(pl.BlockSpec(memory_space=pl.ANY))
        args.append(prev)
        aliases = {len(args) - 1: 0}
    return pl.pallas_call(
        _combine_kernel,
        grid=(yg.shape[1] // ROW_TILE,),
        in_specs=in_specs,
        out_specs=pl.BlockSpec((ROW_TILE, D_MODEL), lambda i: (i + tile0, 0)),
        out_shape=jax.ShapeDtypeStruct((n_tok, D_MODEL), jnp.float32),
        input_output_aliases=aliases,
        compiler_params=_params("parallel"),
        name="combine_ln",
    )(*args)


def _route(top_idx, rank, counts, n_blocks):
    counts = counts[:, 0]
    blocks_per = (counts + EXPERT_ROWS - 1) // EXPERT_ROWS
    blk_end = jnp.cumsum(blocks_per)
    blk_start = blk_end - blocks_per
    pos = rank
    for e in range(N_EXPERTS):
        pos = pos + jnp.where(top_idx == e, blk_start[e] * EXPERT_ROWS, 0)
    n_valid = blk_end[-1:].astype(jnp.int32)
    experts = jnp.arange(N_EXPERTS, dtype=jnp.int32)
    has_rows = blocks_per > 0
    run_of = jnp.cumsum(has_rows.astype(jnp.int32)) - 1
    later = (experts[None, :] > experts[:, None]) & has_rows[None, :]
    next_of = jnp.min(jnp.where(later, experts[None, :], N_EXPERTS), axis=1)
    next_of = jnp.where(next_of == N_EXPERTS, -1, next_of)
    blk = jnp.arange(n_blocks, dtype=jnp.int32)
    owner = ((blk[:, None] >= blk_start[None, :]) & (blk[:, None] < blk_end[None, :])).astype(jnp.int32)
    pick = lambda per_expert: jnp.sum(owner * per_expert[None, :], axis=1).astype(jnp.int32)
    block_expert = pick(experts)
    block_rows = jnp.clip(pick(counts) - (blk - pick(blk_start)) * EXPERT_ROWS, 0, EXPERT_ROWS).astype(jnp.int32)
    return pos, (block_expert, block_rows, pick(run_of) % 2, pick(next_of), n_valid)


def _moe_groups(n_tok, seq):
    unit = math.lcm(2 * SC_ROWS * SC_WORKERS, seq)
    parts = sum(MOE_SPLIT)
    if n_tok % (parts * unit):
        return ((0, n_tok),)
    groups, t0 = [], 0
    for share in MOE_SPLIT:
        groups.append((t0, n_tok * share // parts))
        t0 += groups[-1][1]
    return tuple(groups)


def _rope_tables(seq):
    inv_freq = 1.0 / (ROPE_THETA ** (jnp.arange(0, HEAD_DIM, 2, dtype=jnp.float32) / HEAD_DIM))
    ang = jnp.arange(seq, dtype=jnp.float32)[:, None] * inv_freq[None, :]
    cos, sin = jnp.cos(ang), jnp.sin(ang)
    cos_t = jnp.tile(jnp.concatenate([cos, cos], axis=1), (1, 256 // HEAD_DIM))
    sin_t = jnp.tile(jnp.concatenate([-sin, sin], axis=1), (1, 256 // HEAD_DIM))
    return cos_t, sin_t


def kernel(x, w_in, b_in, sinks, w_out, b_out, ln1_g, ln1_b, w_router, b_router, w1, b1, w2, b2, ln2_g, ln2_b):
    batch, seq, d = x.shape
    assert d == D_MODEL and seq % ROW_TILE == 0 and seq % MOBA_BLOCK == 0 and w_in.shape[0] == DEPTH == 1
    assert (seq // WINDOW - 1) % SWA_UNROLL == 0
    n_tok = batch * seq
    bf16 = jnp.bfloat16
    x2 = x.reshape(n_tok, d)
    cos_t, sin_t = _rope_tables(seq)

    proj = _inproj(x2, w_in[0].astype(bf16), b_in[0].reshape(1, IN_WIDTH), cos_t, sin_t, seq)

    w_o = w_out[0].astype(bf16)
    w_r = w_router[0].T.astype(bf16)
    b_r = b_router[0].reshape(N_EXPERTS, 1)
    b_o, g1, be1 = b_out[0].reshape(1, d), ln1_g[0].reshape(1, d), ln1_b[0].reshape(1, d)
    perm = _split_columns_perm()
    b1r = b1[0].reshape(N_EXPERTS, 1, D_FF, 2)
    b1g, b1l = b1r[..., 0], b1r[..., 1]
    b2r = b2[0].reshape(N_EXPERTS, 1, d)
    g2, be2 = ln2_g[0].reshape(1, d), ln2_b[0].reshape(1, d)

    def dispatch(stage):
        t0, tok_g, x1, x1p, gates, pos, tables, n_blocks = stage
        pos3 = pos.reshape(TOP_K, tok_g // SC_ROWS, SC_ROWS).transpose(1, 0, 2)
        return _sc_scatter_rows(x1p, pos3, n_blocks * EXPERT_ROWS)

    def experts_and_gather(stage, xg):
        t0, tok_g, x1, x1p, gates, pos, tables, n_blocks = stage
        y = _experts(tables, xg, w1[0], perm, b1g, b1l, w2[0], b2r)
        return _sc_gather_rows(y, pos.reshape(tok_g * TOP_K // SC_ROWS, SC_ROWS))

    stages, gathered, waiting = [], [], None
    for t0, tok_g in _moe_groups(n_tok, seq):
        o_a = _swa(proj, sinks[0], t0 // seq, tok_g // seq, seq)
        o_b = _moba(proj, t0 // seq, tok_g // seq, seq)
        x1, x1p, top_idx, gates, rank, counts = _outproj(o_a, o_b, w_o, b_o, x2, g1, be1, w_r, b_r, t0, tok_g)
        n_blocks = tok_g * TOP_K // EXPERT_ROWS + N_EXPERTS
        pos, tables = _route(top_idx, rank, counts, n_blocks)
        stages.append((t0, tok_g, x1, x1p, gates, pos, tables, n_blocks))
        if waiting is not None:
            gathered.append(experts_and_gather(*waiting))
        waiting = (stages[-1], dispatch(stages[-1]))
    gathered.append(experts_and_gather(*waiting))
    out = None
    for (t0, tok_g, x1, _, gates, _, _, _), yg in zip(stages, gathered):
        out = _combine(yg.reshape(TOP_K, tok_g, PACKED), gates, x1, g2, be2, t0 // ROW_TILE, n_tok, out)
    return out.reshape(batch, seq, d)
```

```python
import functools
import math

import jax
import jax.numpy as jnp
from jax import lax
from jax.experimental import pallas as pl
from jax.experimental.pallas import tpu as pltpu
from jax.experimental.pallas import tpu_sc as plsc

D_MODEL = 1024
HEAD_DIM = 64
N_HEADS_SWA = 8
N_KV_SWA = 2
WINDOW = 128
N_HEADS_MOBA = 8
MOBA_BLOCK = 256
MOBA_TOPK = 3
ROPE_THETA = 10000.0
N_EXPERTS = 32
TOP_K = 4
D_FF = 1024
SWIGLU_LIMIT = 7.0
SWIGLU_ALPHA = 1.702
LN_EPS = 1e-5
DEPTH = 1
DEEPNORM_ALPHA = (2 * DEPTH) ** 0.25

W_Q_SWA = N_HEADS_SWA * HEAD_DIM
W_KV_SWA = N_KV_SWA * HEAD_DIM
W_MOBA = N_HEADS_MOBA * HEAD_DIM
IN_WIDTH = W_Q_SWA + 2 * W_KV_SWA + 3 * W_MOBA
LANES = 128
COL_K_SWA = W_Q_SWA // LANES
COL_V_SWA = COL_K_SWA + 1
COL_Q_MOBA = COL_V_SWA + 1
COL_K_MOBA = COL_Q_MOBA + W_MOBA // LANES
COL_V_MOBA = COL_K_MOBA + W_MOBA // LANES

ROW_TILE = 1024
EXPERT_ROWS = 1024
EXPERT_PART = 256
PACKED = D_MODEL // 2
NEG_BIG = -1e30
LOG2E = 1.4426950408889634
MOBA_LOOKAHEAD = 1
SWA_UNROLL = 3
MOE_SPLIT = (3, 1)
VMEM_LIMIT = 48 * 1024 * 1024
EXPERT_VMEM_LIMIT = 60 * 1024 * 1024

SC_CORES = 2
SC_SUBCORES = 16
SC_WORKERS = SC_CORES * SC_SUBCORES
SC_ROWS = 64

_PROJ_CHUNKS = (
    (0, 256, 256, True), (256, 256, 256, True),
    (512, 256, 128, False),
    (768, 256, 256, True), (1024, 256, 256, True),
    (1280, 256, 256, False), (1536, 256, 256, False),
    (1792, 256, 0, False), (2048, 256, 0, False),
)


def _params(*sem):
    return pltpu.CompilerParams(dimension_semantics=sem, vmem_limit_bytes=VMEM_LIMIT)


def _pack_bf16_pairs(v):
    n = v.shape[1] // 2
    bits = lax.bitcast_convert_type(v.astype(jnp.bfloat16).astype(jnp.float32), jnp.uint32)
    word = (bits[:, :n] >> 16) | (bits[:, n:] & jnp.uint32(0xFFFF0000))
    return lax.bitcast_convert_type(word, jnp.int32)


def _unpack_bf16_pairs(word):
    bits = lax.bitcast_convert_type(word, jnp.uint32)
    lo = lax.bitcast_convert_type(bits << 16, jnp.float32)
    hi = lax.bitcast_convert_type(bits & jnp.uint32(0xFFFF0000), jnp.float32)
    return lo, hi


def _inproj_kernel(x_ref, w_ref, b_ref, cos_ref, sin_ref, o_ref):
    xb = x_ref[...].astype(jnp.bfloat16)
    for start, width, rope, scaled in _PROJ_CHUNKS:
        t = jnp.dot(xb, w_ref[:, start:start + width], preferred_element_type=jnp.float32)
        t = t + b_ref[:, start:start + width]
        if rope:
            lane = lax.broadcasted_iota(jnp.int32, t.shape, 1)
            first_half = (lane % HEAD_DIM) < (HEAD_DIM // 2)
            rot = jnp.where(first_half,
                            pltpu.roll(t, width - HEAD_DIM // 2, 1),
                            pltpu.roll(t, HEAD_DIM // 2, 1))
            roped = t * cos_ref[:, :width] + rot * sin_ref[:, :width]
            t = roped if rope == width else jnp.where(lane < rope, roped, t)
        if scaled:
            t = t * (HEAD_DIM ** -0.5 * LOG2E)
        o_ref[:, start:start + width] = t.astype(o_ref.dtype)


def _inproj(x2, w_in, b_in, cos_t, sin_t, seq):
    n_tok = x2.shape[0]
    per_seq = seq // ROW_TILE
    return pl.pallas_call(
        _inproj_kernel,
        grid=(n_tok // ROW_TILE,),
        in_specs=[
            pl.BlockSpec((ROW_TILE, D_MODEL), lambda i: (i, 0)),
            pl.BlockSpec((D_MODEL, IN_WIDTH), lambda i: (0, 0)),
            pl.BlockSpec((1, IN_WIDTH), lambda i: (0, 0)),
            pl.BlockSpec((ROW_TILE, 256), lambda i: (i % per_seq, 0)),
            pl.BlockSpec((ROW_TILE, 256), lambda i: (i % per_seq, 0)),
        ],
        out_specs=pl.BlockSpec((ROW_TILE, IN_WIDTH), lambda i: (i, 0)),
        out_shape=jax.ShapeDtypeStruct((n_tok, IN_WIDTH), jnp.bfloat16),
        compiler_params=_params("parallel"),
        name="inproj_rope",
    )(x2, w_in, b_in, cos_t, sin_t)


def _swa_kernel(sink_ref, q_ref, k_ref, v_ref, o_ref, kd_ref, vd_ref, *, seq):
    lane = lax.broadcasted_iota(jnp.int32, (seq, LANES), 1)
    low = lane < HEAD_DIM
    k = k_ref[...].astype(jnp.float32)
    kr = pltpu.roll(k, HEAD_DIM, 1)
    kd_ref[0, WINDOW:, :] = jnp.where(low, k, kr).astype(kd_ref.dtype)
    kd_ref[1, WINDOW:, :] = jnp.where(low, kr, k).astype(kd_ref.dtype)
    v = v_ref[...].astype(jnp.float32)
    vd_ref[0, WINDOW:, :] = jnp.where(low, v, 1.0).astype(vd_ref.dtype)
    vd_ref[1, WINDOW:, :] = jnp.where(low, pltpu.roll(v, HEAD_DIM, 1), 1.0).astype(vd_ref.dtype)
    kd_ref[:, :WINDOW, :] = jnp.zeros((N_KV_SWA, WINDOW, LANES), kd_ref.dtype)
    vd_ref[:, :WINDOW, :] = jnp.zeros((N_KV_SWA, WINDOW, LANES), vd_ref.dtype)

    group = N_HEADS_SWA // N_KV_SWA
    rows = group * WINDOW
    r_in = lax.broadcasted_iota(jnp.int32, (rows, 2 * WINDOW), 0) % WINDOW
    c_id = lax.broadcasted_iota(jnp.int32, (rows, 2 * WINDOW), 1)
    band = (c_id > r_in) & (c_id <= r_in + WINDOW)
    head_in_group = lax.broadcasted_iota(jnp.int32, (rows, 1), 0) // WINDOW
    qlane_low = lax.broadcasted_iota(jnp.int32, (WINDOW, LANES), 1) < HEAD_DIM
    sinks = []
    for g in range(N_KV_SWA):
        col = jnp.zeros((rows, 1), jnp.float32)
        for j in range(group):
            col = jnp.where(head_in_group == j, sink_ref[g * group + j] * LOG2E, col)
        sinks.append(col)

    def scores(n, mask):
        r0 = pl.multiple_of(n * WINDOW, WINDOW)
        out = []
        for g in range(N_KV_SWA):
            parts = []
            for c in (2 * g, 2 * g + 1):
                qc = q_ref[pl.ds(r0, WINDOW), c * LANES:(c + 1) * LANES]
                zero = jnp.zeros_like(qc)
                parts.append(jnp.where(qlane_low, qc, zero))
                parts.append(jnp.where(qlane_low, zero, qc))
            qcat = jnp.concatenate(parts, axis=0)
            kd = kd_ref[g, pl.ds(r0, 2 * WINDOW), :]
            s = lax.dot_general(qcat, kd, (((1,), (1,)), ((), ())),
                                preferred_element_type=jnp.float32)
            out.append(jnp.where(mask, s, -jnp.inf))
        return out

    def finish(n, scored):
        r0 = pl.multiple_of(n * WINDOW, WINDOW)
        for g, s in enumerate(scored):
            vd = vd_ref[g, pl.ds(r0, 2 * WINDOW), :]
            m = jnp.maximum(jnp.max(s, axis=-1, keepdims=True), sinks[g])
            p = jnp.exp2(s - m)
            o = jnp.dot(p.astype(vd.dtype), vd, preferred_element_type=jnp.float32)
            sink_term = jnp.exp2(sinks[g] - m)
            for ci, c in enumerate((2 * g, 2 * g + 1)):
                o_lo = o[(2 * ci) * WINDOW:(2 * ci + 1) * WINDOW]
                o_hi = o[(2 * ci + 1) * WINDOW:(2 * ci + 2) * WINDOW]
                e_lo = sink_term[(2 * ci) * WINDOW:(2 * ci + 1) * WINDOW]
                e_hi = sink_term[(2 * ci + 1) * WINDOW:(2 * ci + 2) * WINDOW]
                num = jnp.where(qlane_low, o_lo, pltpu.roll(o_hi, HEAD_DIM, 1))
                den = jnp.where(qlane_low, pltpu.roll(o_lo, HEAD_DIM, 1) + e_lo, o_hi + e_hi)
                o_ref[pl.ds(r0, WINDOW), c * LANES:(c + 1) * LANES] = (num / den).astype(o_ref.dtype)

    finish(0, scores(0, band & (c_id >= WINDOW)))

    def body(it, carry):
        n0 = 1 + it * SWA_UNROLL
        nxt = scores(n0, band)
        for u in range(SWA_UNROLL):
            cur = nxt
            if u + 1 < SWA_UNROLL:
                nxt = scores(n0 + u + 1, band)
            finish(n0 + u, cur)
        return carry

    lax.fori_loop(0, (seq // WINDOW - 1) // SWA_UNROLL, body, 0)


def _swa(proj, sinks, b0, batch, seq):
    grid_spec = pltpu.PrefetchScalarGridSpec(
        num_scalar_prefetch=0,
        grid=(batch,),
        in_specs=[
            pl.BlockSpec(memory_space=pltpu.SMEM),
            pl.BlockSpec((seq, W_Q_SWA), lambda b: (b + b0, 0)),
            pl.BlockSpec((seq, LANES), lambda b: (b + b0, COL_K_SWA)),
            pl.BlockSpec((seq, LANES), lambda b: (b + b0, COL_V_SWA)),
        ],
        out_specs=pl.BlockSpec((seq, W_Q_SWA), lambda b: (b, 0)),
        scratch_shapes=[pltpu.VMEM((N_KV_SWA, WINDOW + seq, LANES), jnp.bfloat16),
                        pltpu.VMEM((N_KV_SWA, WINDOW + seq, LANES), jnp.bfloat16)],
    )
    return pl.pallas_call(
        functools.partial(_swa_kernel, seq=seq),
        grid_spec=grid_spec,
        out_shape=jax.ShapeDtypeStruct((batch * seq, W_Q_SWA), jnp.bfloat16),
        compiler_params=_params("parallel"),
        name="swa_sink_attention",
    )(sinks, proj, proj, proj)


def _moba_kernel(q_ref, k_ref, v_ref, o_ref, qa_ref, ka_ref, va_ref, *, seq):
    nblk = seq // MOBA_BLOCK
    pad_rows = 16
    q_all, k_all, v_all = q_ref[...], k_ref[...], v_ref[...]
    kmean = jnp.sum(k_all.astype(jnp.float32).reshape(nblk, MOBA_BLOCK, LANES), axis=1) / MOBA_BLOCK
    kmean = jnp.concatenate([kmean, jnp.zeros((pad_rows - nblk, LANES), jnp.float32)], axis=0)
    klane_low = lax.broadcasted_iota(jnp.int32, (pad_rows, LANES), 1) < HEAD_DIM
    lane = lax.broadcasted_iota(jnp.int32, (seq, LANES), 1)
    low = lane < HEAD_DIM
    key_blk = lax.broadcasted_iota(jnp.int32, (seq, LANES), 0) // MOBA_BLOCK
    j_id = lax.broadcasted_iota(jnp.int32, (pad_rows, seq), 0)
    q_blk = lax.broadcasted_iota(jnp.int32, (pad_rows, seq), 1) // MOBA_BLOCK
    eligible = j_id < q_blk

    for half in range(2):
        own = low if half == 0 else ~low
        spare = HEAD_DIM if half == 0 else 0
        ka_ref[half] = jnp.where(own, k_all, (lane - spare == key_blk).astype(k_all.dtype))
        va_ref[half] = jnp.where(own, v_all, jnp.ones_like(v_all))
        km = jnp.where(klane_low if half == 0 else ~klane_low, kmean, 0.0).astype(jnp.bfloat16)
        gate = lax.dot_general(km, q_all, (((1,), (1,)), ((), ())),
                               preferred_element_type=jnp.float32)
        gate = jnp.where(eligible, gate, -jnp.inf)
        beaten = jnp.zeros((pad_rows, seq), jnp.int32)
        for jp in range(nblk):
            row = gate[jp:jp + 1, :]
            wins = (row > gate) | ((row == gate) & (jp < j_id))
            beaten = beaten + wins.astype(jnp.int32)
        dropped = eligible & (beaten >= MOBA_TOPK)
        bias = jnp.where(dropped, NEG_BIG, 0.0)
        pieces = [bias, jnp.zeros((LANES - spare - pad_rows, seq), jnp.float32)]
        if spare:
            pieces.insert(0, jnp.zeros((spare, seq), jnp.float32))
        bias_t = jnp.concatenate(pieces, axis=0).T
        qa_ref[half] = jnp.where(own, q_all, bias_t.astype(q_all.dtype))

    qlane_low = lax.broadcasted_iota(jnp.int32, (MOBA_BLOCK, LANES), 1) < HEAD_DIM
    rr = lax.broadcasted_iota(jnp.int32, (MOBA_BLOCK, MOBA_BLOCK), 0)
    cc = lax.broadcasted_iota(jnp.int32, (MOBA_BLOCK, MOBA_BLOCK), 1)
    causal = cc <= rr

    def scores(i, half):
        r0 = i * MOBA_BLOCK
        n_keys = r0 + MOBA_BLOCK
        s = lax.dot_general(qa_ref[half, r0:n_keys, :], ka_ref[half, 0:n_keys, :], (((1,), (1,)), ((), ())),
                            preferred_element_type=jnp.float32)
        own_blk = jnp.where(causal, s[:, r0:n_keys], NEG_BIG)
        return jnp.concatenate([s[:, :r0], own_blk], axis=1) if i else own_blk

    units = [(i, half) for i in range(nblk) for half in range(2)]
    pending = [scores(*u) for u in units[:MOBA_LOOKAHEAD]]
    acc = []
    for n, (i, half) in enumerate(units):
        s = pending.pop(0)
        if n + MOBA_LOOKAHEAD < len(units):
            pending.append(scores(*units[n + MOBA_LOOKAHEAD]))
        n_keys = (i + 1) * MOBA_BLOCK
        m = jnp.max(s, axis=-1, keepdims=True)
        p = jnp.exp2(s - m).astype(jnp.bfloat16)
        acc.append(jnp.dot(p, va_ref[half, 0:n_keys, :], preferred_element_type=jnp.float32))
        if half == 1:
            num = jnp.where(qlane_low, acc[0], acc[1])
            den = pltpu.roll(jnp.where(qlane_low, acc[1], acc[0]), HEAD_DIM, 1)
            o_ref[i * MOBA_BLOCK:n_keys, :] = (num / den).astype(o_ref.dtype)
            acc = []


def _moba(proj, b0, batch, seq):
    pairs = W_MOBA // LANES
    return pl.pallas_call(
        functools.partial(_moba_kernel, seq=seq),
        grid=(batch, pairs),
        in_specs=[
            pl.BlockSpec((seq, LANES), lambda b, p: (b + b0, COL_Q_MOBA + p)),
            pl.BlockSpec((seq, LANES), lambda b, p: (b + b0, COL_K_MOBA + p)),
            pl.BlockSpec((seq, LANES), lambda b, p: (b + b0, COL_V_MOBA + p)),
        ],
        out_specs=pl.BlockSpec((seq, LANES), lambda b, p: (b, p)),
        out_shape=jax.ShapeDtypeStruct((batch * seq, W_MOBA), jnp.bfloat16),
        scratch_shapes=[pltpu.VMEM((2, seq, LANES), jnp.bfloat16)] * 3,
        compiler_params=_params("parallel", "parallel"),
        name="moba_attention",
    )(proj, proj, proj)


def _layer_norm(h, g, b):
    mu = jnp.mean(h, axis=-1, keepdims=True)
    d = h - mu
    var = jnp.mean(d * d, axis=-1, keepdims=True)
    return d * lax.rsqrt(var + LN_EPS) * g + b


def _outproj_kernel(oa_ref, ob_ref, wo_ref, bo_ref, x_ref, g_ref, b_ref, wr_ref, br_ref, tri_ref,
                    x1_ref, x1p_ref, idx_ref, gate_ref, rank_ref, count_ref, running_ref):
    @pl.when(pl.program_id(0) == 0)
    def _():
        running_ref[...] = jnp.zeros(running_ref.shape, running_ref.dtype)

    half_rows = ROW_TILE // 2
    halves = [slice(h * half_rows, (h + 1) * half_rows) for h in range(2)]

    def project(rows):
        heads = jnp.concatenate([oa_ref[rows, :], ob_ref[rows, :]], axis=1)
        return jnp.dot(heads, wo_ref[...], preferred_element_type=jnp.float32) + bo_ref[...]

    def normalise(rows, mix):
        x1 = _layer_norm(DEEPNORM_ALPHA * x_ref[rows, :] + mix, g_ref[...], b_ref[...])
        x1_ref[rows, :] = x1
        x1p_ref[rows, :] = _pack_bf16_pairs(x1)
        return x1.astype(jnp.bfloat16)

    mixes = [project(rows) for rows in halves]
    x1b = jnp.concatenate([normalise(rows, mix) for rows, mix in zip(halves, mixes)], axis=0)

    logits = lax.dot_general(wr_ref[...], x1b, (((1,), (1,)), ((), ())),
                             preferred_element_type=jnp.float32) + br_ref[...]
    expert = lax.broadcasted_iota(jnp.int32, logits.shape, 0)
    idx_rows, val_rows, onehots = [], [], []
    top = None
    total = None
    for k in range(TOP_K):
        m = jnp.max(logits, axis=0, keepdims=True)
        idx = jnp.min(jnp.where(logits == m, expert, N_EXPERTS), axis=0, keepdims=True)
        picked = expert == idx
        onehots.append(picked)
        logits = jnp.where(picked, -jnp.inf, logits)
        if k == 0:
            top = m
        e = jnp.exp(m - top)
        total = e if k == 0 else total + e
        idx_rows.append(idx)
        val_rows.append(e)
    idx_ref[...] = jnp.concatenate(idx_rows, axis=0)
    gates_t = jnp.concatenate(val_rows, axis=0) / total
    gates_t = jnp.concatenate([gates_t, jnp.zeros((LANES - TOP_K, gates_t.shape[1]), jnp.float32)], axis=0)
    gate_ref[...] = gates_t.T[:, :TOP_K]

    picks = jnp.concatenate([p.astype(jnp.bfloat16) for p in onehots], axis=0)
    before = jnp.dot(picks, tri_ref[...], preferred_element_type=jnp.float32)
    base = running_ref[...]
    rank_rows = []
    for k in range(TOP_K):
        pk = onehots[k].astype(jnp.float32)
        here = before[k * N_EXPERTS:(k + 1) * N_EXPERTS, :] + base
        rank_rows.append(jnp.sum(pk * here, axis=0, keepdims=True))
        base = base + jnp.sum(pk, axis=1, keepdims=True)
    running_ref[...] = base
    rank_ref[...] = jnp.concatenate(rank_rows, axis=0).astype(jnp.int32)
    count_ref[...] = base.astype(jnp.int32)


def _outproj(o_a, o_b, w_o, b_out, x2, g, b, w_r, b_r, t0, n_tok):
    tile0 = t0 // ROW_TILE
    src = lambda i: (i + tile0, 0)
    row = lambda i: (i, 0)
    fixed = lambda i: (0, 0)
    r = lax.broadcasted_iota(jnp.int32, (ROW_TILE, ROW_TILE), 0)
    c = lax.broadcasted_iota(jnp.int32, (ROW_TILE, ROW_TILE), 1)
    tri = (r < c).astype(jnp.bfloat16)
    return pl.pallas_call(
        _outproj_kernel,
        grid=(n_tok // ROW_TILE,),
        in_specs=[
            pl.BlockSpec((ROW_TILE, W_Q_SWA), row),
            pl.BlockSpec((ROW_TILE, W_MOBA), row),
            pl.BlockSpec((W_Q_SWA + W_MOBA, D_MODEL), fixed),
            pl.BlockSpec((1, D_MODEL), fixed),
            pl.BlockSpec((ROW_TILE, D_MODEL), src),
            pl.BlockSpec((1, D_MODEL), fixed),
            pl.BlockSpec((1, D_MODEL), fixed),
            pl.BlockSpec((N_EXPERTS, D_MODEL), fixed),
            pl.BlockSpec((N_EXPERTS, 1), fixed),
            pl.BlockSpec((ROW_TILE, ROW_TILE), fixed),
        ],
        out_specs=[
            pl.BlockSpec((ROW_TILE, D_MODEL), row),
            pl.BlockSpec((ROW_TILE, PACKED), row),
            pl.BlockSpec((TOP_K, ROW_TILE), lambda i: (0, i)),
            pl.BlockSpec((ROW_TILE, TOP_K), row),
            pl.BlockSpec((TOP_K, ROW_TILE), lambda i: (0, i)),
            pl.BlockSpec((N_EXPERTS, 1), fixed),
        ],
        out_shape=[
            jax.ShapeDtypeStruct((n_tok, D_MODEL), jnp.float32),
            jax.ShapeDtypeStruct((n_tok, PACKED), jnp.int32),
            jax.ShapeDtypeStruct((TOP_K, n_tok), jnp.int32),
            jax.ShapeDtypeStruct((n_tok, TOP_K), jnp.float32),
            jax.ShapeDtypeStruct((TOP_K, n_tok), jnp.int32),
            jax.ShapeDtypeStruct((N_EXPERTS, 1), jnp.int32),
        ],
        scratch_shapes=[pltpu.VMEM((N_EXPERTS, 1), jnp.float32)],
        compiler_params=_params("arbitrary"),
        name="outproj_ln_router",
    )(o_a, o_b, w_o, b_out, x2, g, b, w_r, b_r, tri)


def _sc_worker_id():
    return lax.axis_index("s") * SC_CORES + lax.axis_index("c")


def _sc_scatter_rows(rows, pos3, n_out):
    n_tok = pos3.shape[0] * SC_ROWS
    steps = n_tok // SC_ROWS // SC_WORKERS
    assert steps * SC_ROWS * SC_WORKERS == n_tok and steps % 2 == 0
    mesh = plsc.VectorSubcoreMesh(core_axis_name="c", subcore_axis_name="s")

    @functools.partial(
        pl.kernel, mesh=mesh,
        out_type=jax.ShapeDtypeStruct((n_out, PACKED), jnp.int32),
        scratch_types=[pltpu.VMEM((2, TOP_K, SC_ROWS), jnp.int32), pltpu.VMEM((2, SC_ROWS, PACKED), jnp.int32),
                       pltpu.SemaphoreType.DMA((2,)), pltpu.SemaphoreType.DMA((2,))],
        name="sc_dispatch_scatter")
    def scatter(x_hbm, pos_hbm, out_hbm, idx_v, rows_v, sem_ld, sem_st):
        base = _sc_worker_id() * steps

        def loads(s, b):
            return (pltpu.make_async_copy(pos_hbm.at[base + s], idx_v.at[b], sem_ld.at[b]),
                    pltpu.make_async_copy(x_hbm.at[pl.ds((base + s) * SC_ROWS, SC_ROWS)], rows_v.at[b],
                                          sem_ld.at[b]))

        def stores(b):
            return [pltpu.make_async_copy(rows_v.at[b], out_hbm.at[idx_v.at[b, k]], sem_st.at[b])
                    for k in range(TOP_K)]

        for c in loads(0, 0):
            c.start()

        @pl.loop(0, steps, step=2)
        def _(s0):
            for b in range(2):
                s = s0 + b
                for c in loads(s, b):
                    c.wait()

                @pl.when(s >= 1)
                def _():
                    for c in stores(1 - b):
                        c.wait()

                @pl.when(s + 1 < steps)
                def _():
                    for c in loads(s + 1, 1 - b):
                        c.start()

                for c in stores(b):
                    c.start()

        for c in stores(1):
            c.wait()

    return scatter(rows, pos3)


def _sc_gather_rows(table, idx2):
    n_blk = idx2.shape[0]
    steps = n_blk // SC_WORKERS
    assert steps * SC_WORKERS == n_blk and steps % 2 == 0 and idx2.shape[1] == SC_ROWS
    mesh = plsc.VectorSubcoreMesh(core_axis_name="c", subcore_axis_name="s")

    @functools.partial(
        pl.kernel, mesh=mesh,
        out_type=jax.ShapeDtypeStruct((n_blk * SC_ROWS, PACKED), jnp.int32),
        scratch_types=[pltpu.VMEM((steps, SC_ROWS), jnp.int32), pltpu.VMEM((2, SC_ROWS, PACKED), jnp.int32),
                       pltpu.SemaphoreType.DMA((2,)), pltpu.SemaphoreType.DMA((2,))],
        name="sc_combine_gather")
    def gather(y_hbm, idx_hbm, out_hbm, idx_v, rows_v, sem_ld, sem_st):
        base = _sc_worker_id() * steps
        pltpu.sync_copy(idx_hbm.at[pl.ds(base, steps)], idx_v)

        def fetch(s, b):
            return pltpu.make_async_copy(y_hbm.at[idx_v.at[s]], rows_v.at[b], sem_ld.at[b])

        def store(s, b):
            return pltpu.make_async_copy(rows_v.at[b], out_hbm.at[pl.ds((base + s) * SC_ROWS, SC_ROWS)],
                                         sem_st.at[b])

        fetch(0, 0).start()

        @pl.loop(0, steps, step=2)
        def _(s0):
            for b in range(2):
                s = s0 + b
                fetch(s, b).wait()

                @pl.when(s >= 1)
                def _():
                    store(s - 1, 1 - b).wait()

                @pl.when(s + 1 < steps)
                def _():
                    fetch(s + 1, 1 - b).start()

                store(s, b).start()

        store(steps - 1, 1).wait()

    return gather(table, idx2)


def _expert_kernel(be_ref, br_ref, slot_ref, next_ref, nv_ref, x_ref, w1_hbm, perm_ref, bg_ref, bl_ref, w2_hbm,
                   b2_ref, y_ref, w1f_ref, w2f_ref, wg_ref, wl_ref, w2b_ref, sem):
    i = pl.program_id(0)
    live = i < nv_ref[0]
    new_expert = (i == 0) | (be_ref[i] != be_ref[jnp.maximum(i - 1, 0)])

    def weight_copies(expert, slot):
        return (pltpu.make_async_copy(w1_hbm.at[expert], w1f_ref.at[slot], sem.at[0, slot]),
                pltpu.make_async_copy(w2_hbm.at[expert], w2f_ref.at[slot], sem.at[1, slot]))

    @pl.when(live & new_expert)
    def _():
        slot = slot_ref[i]

        @pl.when(i == 0)
        def _():
            for copy in weight_copies(be_ref[i], slot):
                copy.start()

        for copy in weight_copies(be_ref[i], slot):
            copy.wait()

        @pl.when(next_ref[i] >= 0)
        def _():
            for copy in weight_copies(next_ref[i], 1 - slot):
                copy.start()

        for c in range(2 * D_FF // 256):
            t = w1f_ref[slot, :, c * 256:(c + 1) * 256].astype(jnp.bfloat16)
            r = jnp.dot(t, perm_ref[...], preferred_element_type=jnp.float32)
            wg_ref[:, c * LANES:(c + 1) * LANES] = r[:, :LANES].astype(wg_ref.dtype)
            wl_ref[:, c * LANES:(c + 1) * LANES] = r[:, LANES:].astype(wl_ref.dtype)
        w2b_ref[...] = w2f_ref[slot].astype(w2b_ref.dtype)

    def mlp(rows):
        valid = lax.broadcasted_iota(jnp.int32, (rows, PACKED), 0) < br_ref[i]
        lo, hi = _unpack_bf16_pairs(jnp.where(valid, x_ref[0:rows, :], 0))
        xb = jnp.concatenate([lo.astype(jnp.bfloat16), hi.astype(jnp.bfloat16)], axis=1)
        hg = jnp.dot(xb, wg_ref[...], preferred_element_type=jnp.float32) + bg_ref[0]
        hl = jnp.dot(xb, wl_ref[...], preferred_element_type=jnp.float32) + bl_ref[0]
        glu = jnp.minimum(hg, SWIGLU_LIMIT)
        lin = jnp.clip(hl, -SWIGLU_LIMIT, SWIGLU_LIMIT)
        act = glu * jax.nn.sigmoid(SWIGLU_ALPHA * glu) * (lin + 1.0)
        y = jnp.dot(act.astype(jnp.bfloat16), w2b_ref[...], preferred_element_type=jnp.float32) + b2_ref[0]
        y_ref[0:rows, :] = _pack_bf16_pairs(y)

    parts = (br_ref[i] + EXPERT_PART - 1) // EXPERT_PART
    for n_parts in range(1, EXPERT_ROWS // EXPERT_PART + 1):
        @pl.when(live & (parts == n_parts))
        def _(rows=n_parts * EXPERT_PART):
            mlp(rows)
            if rows < EXPERT_ROWS:
                y_ref[rows:, :] = jnp.zeros((EXPERT_ROWS - rows, PACKED), y_ref.dtype)

    @pl.when(jnp.logical_not(live))
    def _():
        y_ref[...] = jnp.zeros(y_ref.shape, y_ref.dtype)


def _experts(tables, xg, w1, perm, b1g, b1l, w2, b2):
    n_rows = xg.shape[0]
    n_blocks = n_rows // EXPERT_ROWS

    def row(i, be, br, slot, nxt, nv):
        return (jnp.minimum(i, nv[0] - 1), 0)

    def per_expert(i, be, br, slot, nxt, nv):
        return (be[i], 0, 0)

    grid_spec = pltpu.PrefetchScalarGridSpec(
        num_scalar_prefetch=5,
        grid=(n_blocks,),
        in_specs=[
            pl.BlockSpec((EXPERT_ROWS, PACKED), row),
            pl.BlockSpec(memory_space=pl.ANY),
            pl.BlockSpec((256, 256), lambda i, *_: (0, 0)),
            pl.BlockSpec((1, 1, D_FF), per_expert),
            pl.BlockSpec((1, 1, D_FF), per_expert),
            pl.BlockSpec(memory_space=pl.ANY),
            pl.BlockSpec((1, 1, D_MODEL), per_expert),
        ],
        out_specs=pl.BlockSpec((EXPERT_ROWS, PACKED), lambda i, *_: (i, 0)),
        scratch_shapes=[pltpu.VMEM((2, D_MODEL, 2 * D_FF), jnp.float32), pltpu.VMEM((2, D_FF, D_MODEL), jnp.float32),
                        pltpu.VMEM((D_MODEL, D_FF), jnp.bfloat16), pltpu.VMEM((D_MODEL, D_FF), jnp.bfloat16),
                        pltpu.VMEM((D_FF, D_MODEL), jnp.bfloat16), pltpu.SemaphoreType.DMA((2, 2))],
    )
    return pl.pallas_call(
        _expert_kernel,
        grid_spec=grid_spec,
        out_shape=jax.ShapeDtypeStruct((n_rows, PACKED), jnp.int32),
        compiler_params=pltpu.CompilerParams(dimension_semantics=("arbitrary",),
                                             vmem_limit_bytes=EXPERT_VMEM_LIMIT),
        name="grouped_experts",
    )(*tables, xg, w1, perm, b1g, b1l, w2, b2)


def _split_columns_perm():
    i = lax.broadcasted_iota(jnp.int32, (256, 256), 0)
    o = lax.broadcasted_iota(jnp.int32, (256, 256), 1)
    return (i == jnp.where(o < LANES, 2 * o, 2 * (o - LANES) + 1)).astype(jnp.bfloat16)


def _combine_kernel(y_ref, gate_ref, x1_ref, g_ref, b_ref, *rest):
    o_ref = rest[-1]
    gates = gate_ref[...]
    lo_sum = None
    hi_sum = None
    for k in range(TOP_K):
        lo, hi = _unpack_bf16_pairs(y_ref[k])
        gk = gates[:, k:k + 1]
        lo_sum = gk * lo if k == 0 else lo_sum + gk * lo
        hi_sum = gk * hi if k == 0 else hi_sum + gk * hi
    moe = jnp.concatenate([lo_sum, hi_sum], axis=1)
    o_ref[...] = _layer_norm(DEEPNORM_ALPHA * x1_ref[...] + moe, g_ref[...], b_ref[...])


def _combine(yg, gates, x1, g, b, tile0, n_tok, prev):
    row = lambda i: (i, 0)
    fixed = lambda i: (0, 0)
    in_specs = [
        pl.BlockSpec((TOP_K, ROW_TILE, PACKED), lambda i: (0, i, 0)),
        pl.BlockSpec((ROW_TILE, TOP_K), row),
        pl.BlockSpec((ROW_TILE, D_MODEL), row),
        pl.BlockSpec((1, D_MODEL), fixed),
        pl.BlockSpec((1, D_MODEL), fixed),
    ]
    args = [yg, gates, x1, g, b]
    aliases = {}
    if prev is not None:
        in_specs.append(pl.BlockSpec(memory_space=pl.ANY))
        args.append(prev)
        aliases = {len(args) - 1: 0}
    return pl.pallas_call(
        _combine_kernel,
        grid=(yg.shape[1] // ROW_TILE,),
        in_specs=in_specs,
        out_specs=pl.BlockSpec((ROW_TILE, D_MODEL), lambda i: (i + tile0, 0)),
        out_shape=jax.ShapeDtypeStruct((n_tok, D_MODEL), jnp.float32),
        input_output_aliases=aliases,
        compiler_params=_params("parallel"),
        name="combine_ln",
    )(*args)


def _route(top_idx, rank, counts, n_blocks):
    counts = counts[:, 0]
    blocks_per = (counts + EXPERT_ROWS - 1) // EXPERT_ROWS
    blk_end = jnp.cumsum(blocks_per)
    blk_start = blk_end - blocks_per
    pos = rank
    for e in range(N_EXPERTS):
        pos = pos + jnp.where(top_idx == e, blk_start[e] * EXPERT_ROWS, 0)
    n_valid = blk_end[-1:].astype(jnp.int32)
    experts = jnp.arange(N_EXPERTS, dtype=jnp.int32)
    has_rows = blocks_per > 0
    run_of = jnp.cumsum(has_rows.astype(jnp.int32)) - 1
    later = (experts[None, :] > experts[:, None]) & has_rows[None, :]
    next_of = jnp.min(jnp.where(later, experts[None, :], N_EXPERTS), axis=1)
    next_of = jnp.where(next_of == N_EXPERTS, -1, next_of)
    blk = jnp.arange(n_blocks, dtype=jnp.int32)
    owner = ((blk[:, None] >= blk_start[None, :]) & (blk[:, None] < blk_end[None, :])).astype(jnp.int32)
    pick = lambda per_expert: jnp.sum(owner * per_expert[None, :], axis=1).astype(jnp.int32)
    block_expert = pick(experts)
    block_rows = jnp.clip(pick(counts) - (blk - pick(blk_start)) * EXPERT_ROWS, 0, EXPERT_ROWS).astype(jnp.int32)
    return pos, (block_expert, block_rows, pick(run_of) % 2, pick(next_of), n_valid)


def _moe_groups(n_tok, seq):
    unit = math.lcm(2 * SC_ROWS * SC_WORKERS, seq)
    parts = sum(MOE_SPLIT)
    if n_tok % (parts * unit):
        return ((0, n_tok),)
    groups, t0 = [], 0
    for share in MOE_SPLIT:
        groups.append((t0, n_tok * share // parts))
        t0 += groups[-1][1]
    return tuple(groups)


def _rope_tables(seq):
    inv_freq = 1.0 / (ROPE_THETA ** (jnp.arange(0, HEAD_DIM, 2, dtype=jnp.float32) / HEAD_DIM))
    ang = jnp.arange(seq, dtype=jnp.float32)[:, None] * inv_freq[None, :]
    cos, sin = jnp.cos(ang), jnp.sin(ang)
    cos_t = jnp.tile(jnp.concatenate([cos, cos], axis=1), (1, 256 // HEAD_DIM))
    sin_t = jnp.tile(jnp.concatenate([-sin, sin], axis=1), (1, 256 // HEAD_DIM))
    return cos_t, sin_t


def kernel(x, w_in, b_in, sinks, w_out, b_out, ln1_g, ln1_b, w_router, b_router, w1, b1, w2, b2, ln2_g, ln2_b):
    batch, seq, d = x.shape
    assert d == D_MODEL and seq % ROW_TILE == 0 and seq % MOBA_BLOCK == 0 and w_in.shape[0] == DEPTH == 1
    assert (seq // WINDOW - 1) % SWA_UNROLL == 0
    n_tok = batch * seq
    bf16 = jnp.bfloat16
    x2 = x.reshape(n_tok, d)
    cos_t, sin_t = _rope_tables(seq)

    proj = _inproj(x2, w_in[0].astype(bf16), b_in[0].reshape(1, IN_WIDTH), cos_t, sin_t, seq)

    w_o = w_out[0].astype(bf16)
    w_r = w_router[0].T.astype(bf16)
    b_r = b_router[0].reshape(N_EXPERTS, 1)
    b_o, g1, be1 = b_out[0].reshape(1, d), ln1_g[0].reshape(1, d), ln1_b[0].reshape(1, d)
    perm = _split_columns_perm()
    b1r = b1[0].reshape(N_EXPERTS, 1, D_FF, 2)
    b1g, b1l = b1r[..., 0], b1r[..., 1]
    b2r = b2[0].reshape(N_EXPERTS, 1, d)
    g2, be2 = ln2_g[0].reshape(1, d), ln2_b[0].reshape(1, d)

    def dispatch(stage):
        t0, tok_g, x1, x1p, gates, pos, tables, n_blocks = stage
        pos3 = pos.reshape(TOP_K, tok_g // SC_ROWS, SC_ROWS).transpose(1, 0, 2)
        return _sc_scatter_rows(x1p, pos3, n_blocks * EXPERT_ROWS)

    def experts_and_gather(stage, xg):
        t0, tok_g, x1, x1p, gates, pos, tables, n_blocks = stage
        y = _experts(tables, xg, w1[0], perm, b1g, b1l, w2[0], b2r)
        return _sc_gather_rows(y, pos.reshape(tok_g * TOP_K // SC_ROWS, SC_ROWS))

    stages, gathered, waiting = [], [], None
    for t0, tok_g in _moe_groups(n_tok, seq):
        o_a = _swa(proj, sinks[0], t0 // seq, tok_g // seq, seq)
        o_b = _moba(proj, t0 // seq, tok_g // seq, seq)
        x1, x1p, top_idx, gates, rank, counts = _outproj(o_a, o_b, w_o, b_o, x2, g1, be1, w_r, b_r, t0, tok_g)
        n_blocks = tok_g * TOP_K // EXPERT_ROWS + N_EXPERTS
        pos, tables = _route(top_idx, rank, counts, n_blocks)
        stages.append((t0, tok_g, x1, x1p, gates, pos, tables, n_blocks))
        if waiting is not None:
            gathered.append(experts_and_gather(*waiting))
        waiting = (stages[-1], dispatch(stages[-1]))
    gathered.append(experts_and_gather(*waiting))
    out = None
    for (t0, tok_g, x1, _, gates, _, _, _), yg in zip(stages, gathered):
        out = _combine(yg.reshape(TOP_K, tok_g, PACKED), gates, x1, g2, be2, t0 // ROW_TILE, n_tok, out)
    return out.reshape(batch, seq, d)
```

```python
import functools
import math

import jax
import jax.numpy as jnp
from jax import lax
from jax.experimental import pallas as pl
from jax.experimental.pallas import tpu as pltpu
from jax.experimental.pallas import tpu_sc as plsc

D_MODEL = 1024
HEAD_DIM = 64
N_HEADS_SWA = 8
N_KV_SWA = 2
WINDOW = 128
N_HEADS_MOBA = 8
MOBA_BLOCK = 256
MOBA_TOPK = 3
ROPE_THETA = 10000.0
N_EXPERTS = 32
TOP_K = 4
D_FF = 1024
SWIGLU_LIMIT = 7.0
SWIGLU_ALPHA = 1.702
LN_EPS = 1e-5
DEPTH = 1
DEEPNORM_ALPHA = (2 * DEPTH) ** 0.25

W_Q_SWA = N_HEADS_SWA * HEAD_DIM
W_KV_SWA = N_KV_SWA * HEAD_DIM
W_MOBA = N_HEADS_MOBA * HEAD_DIM
IN_WIDTH = W_Q_SWA + 2 * W_KV_SWA + 3 * W_MOBA
LANES = 128
COL_K_SWA = W_Q_SWA // LANES
COL_V_SWA = COL_K_SWA + 1
COL_Q_MOBA = COL_V_SWA + 1
COL_K_MOBA = COL_Q_MOBA + W_MOBA // LANES
COL_V_MOBA = COL_K_MOBA + W_MOBA // LANES

ROW_TILE = 1024
EXPERT_ROWS = 1024
EXPERT_PART = 256
PACKED = D_MODEL // 2
NEG_BIG = -1e30
LOG2E = 1.4426950408889634
MOBA_LOOKAHEAD = 1
SWA_UNROLL = 3
MOE_SPLIT = (3, 1)
VMEM_LIMIT = 48 * 1024 * 1024
EXPERT_VMEM_LIMIT = 60 * 1024 * 1024

SC_CORES = 2
SC_SUBCORES = 16
SC_WORKERS = SC_CORES * SC_SUBCORES
SC_ROWS = 64

_PROJ_CHUNKS = (
    (0, 256, 256, True), (256, 256, 256, True),
    (512, 256, 128, False),
    (768, 256, 256, True), (1024, 256, 256, True),
    (1280, 256, 256, False), (1536, 256, 256, False),
    (1792, 256, 0, False), (2048, 256, 0, False),
)


def _params(*sem):
    return pltpu.CompilerParams(dimension_semantics=sem, vmem_limit_bytes=VMEM_LIMIT)


def _pack_bf16_pairs(v):
    n = v.shape[1] // 2
    bits = lax.bitcast_convert_type(v.astype(jnp.bfloat16).astype(jnp.float32), jnp.uint32)
    word = (bits[:, :n] >> 16) | (bits[:, n:] & jnp.uint32(0xFFFF0000))
    return lax.bitcast_convert_type(word, jnp.int32)


def _unpack_bf16_pairs(word):
    bits = lax.bitcast_convert_type(word, jnp.uint32)
    lo = lax.bitcast_convert_type(bits << 16, jnp.float32)
    hi = lax.bitcast_convert_type(bits & jnp.uint32(0xFFFF0000), jnp.float32)
    return lo, hi


def _inproj_kernel(x_ref, w_ref, b_ref, cos_ref, sin_ref, o_ref):
    xb = x_ref[...].astype(jnp.bfloat16)
    for start, width, rope, scaled in _PROJ_CHUNKS:
        t = jnp.dot(xb, w_ref[:, start:start + width], preferred_element_type=jnp.float32)
        t = t + b_ref[:, start:start + width]
        if rope:
            lane = lax.broadcasted_iota(jnp.int32, t.shape, 1)
            first_half = (lane % HEAD_DIM) < (HEAD_DIM // 2)
            rot = jnp.where(first_half,
                            pltpu.roll(t, width - HEAD_DIM // 2, 1),
                            pltpu.roll(t, HEAD_DIM // 2, 1))
            roped = t * cos_ref[:, :width] + rot * sin_ref[:, :width]
            t = roped if rope == width else jnp.where(lane < rope, roped, t)
        if scaled:
            t = t * (HEAD_DIM ** -0.5 * LOG2E)
        o_ref[:, start:start + width] = t.astype(o_ref.dtype)


def _inproj(x2, w_in, b_in, cos_t, sin_t, seq):
    n_tok = x2.shape[0]
    per_seq = seq // ROW_TILE
    return pl.pallas_call(
        _inproj_kernel,
        grid=(n_tok // ROW_TILE,),
        in_specs=[
            pl.BlockSpec((ROW_TILE, D_MODEL), lambda i: (i, 0)),
            pl.BlockSpec((D_MODEL, IN_WIDTH), lambda i: (0, 0)),
            pl.BlockSpec((1, IN_WIDTH), lambda i: (0, 0)),
            pl.BlockSpec((ROW_TILE, 256), lambda i: (i % per_seq, 0)),
            pl.BlockSpec((ROW_TILE, 256), lambda i: (i % per_seq, 0)),
        ],
        out_specs=pl.BlockSpec((ROW_TILE, IN_WIDTH), lambda i: (i, 0)),
        out_shape=jax.ShapeDtypeStruct((n_tok, IN_WIDTH), jnp.bfloat16),
        compiler_params=_params("parallel"),
        name="inproj_rope",
    )(x2, w_in, b_in, cos_t, sin_t)


def _swa_kernel(sink_ref, q_ref, k_ref, v_ref, o_ref, kd_ref, vd_ref, *, seq):
    lane = lax.broadcasted_iota(jnp.int32, (seq, LANES), 1)
    low = lane < HEAD_DIM
    k = k_ref[...].astype(jnp.float32)
    kr = pltpu.roll(k, HEAD_DIM, 1)
    kd_ref[0, WINDOW:, :] = jnp.where(low, k, kr).astype(kd_ref.dtype)
    kd_ref[1, WINDOW:, :] = jnp.where(low, kr, k).astype(kd_ref.dtype)
    v = v_ref[...].astype(jnp.float32)
    vd_ref[0, WINDOW:, :] = jnp.where(low, v, 1.0).astype(vd_ref.dtype)
    vd_ref[1, WINDOW:, :] = jnp.where(low, pltpu.roll(v, HEAD_DIM, 1), 1.0).astype(vd_ref.dtype)
    kd_ref[:, :WINDOW, :] = jnp.zeros((N_KV_SWA, WINDOW, LANES), kd_ref.dtype)
    vd_ref[:, :WINDOW, :] = jnp.zeros((N_KV_SWA, WINDOW, LANES), vd_ref.dtype)

    group = N_HEADS_SWA // N_KV_SWA
    rows = group * WINDOW
    r_in = lax.broadcasted_iota(jnp.int32, (rows, 2 * WINDOW), 0) % WINDOW
    c_id = lax.broadcasted_iota(jnp.int32, (rows, 2 * WINDOW), 1)
    band = (c_id > r_in) & (c_id <= r_in + WINDOW)
    head_in_group = lax.broadcasted_iota(jnp.int32, (rows, 1), 0) // WINDOW
    qlane_low = lax.broadcasted_iota(jnp.int32, (WINDOW, LANES), 1) < HEAD_DIM
    sinks = []
    for g in range(N_KV_SWA):
        col = jnp.zeros((rows, 1), jnp.float32)
        for j in range(group):
            col = jnp.where(head_in_group == j, sink_ref[g * group + j] * LOG2E, col)
        sinks.append(col)

    def scores(n, mask):
        r0 = pl.multiple_of(n * WINDOW, WINDOW)
        out = []
        for g in range(N_KV_SWA):
            parts = []
            for c in (2 * g, 2 * g + 1):
                qc = q_ref[pl.ds(r0, WINDOW), c * LANES:(c + 1) * LANES]
                zero = jnp.zeros_like(qc)
                parts.append(jnp.where(qlane_low, qc, zero))
                parts.append(jnp.where(qlane_low, zero, qc))
            qcat = jnp.concatenate(parts, axis=0)
            kd = kd_ref[g, pl.ds(r0, 2 * WINDOW), :]
            s = lax.dot_general(qcat, kd, (((1,), (1,)), ((), ())),
                                preferred_element_type=jnp.float32)
            out.append(jnp.where(mask, s, -jnp.inf))
        return out

    def finish(n, scored):
        r0 = pl.multiple_of(n * WINDOW, WINDOW)
        for g, s in enumerate(scored):
            vd = vd_ref[g, pl.ds(r0, 2 * WINDOW), :]
            m = jnp.maximum(jnp.max(s, axis=-1, keepdims=True), sinks[g])
            p = jnp.exp2(s - m)
            o = jnp.dot(p.astype(vd.dtype), vd, preferred_element_type=jnp.float32)
            sink_term = jnp.exp2(sinks[g] - m)
            for ci, c in enumerate((2 * g, 2 * g + 1)):
                o_lo = o[(2 * ci) * WINDOW:(2 * ci + 1) * WINDOW]
                o_hi = o[(2 * ci + 1) * WINDOW:(2 * ci + 2) * WINDOW]
                e_lo = sink_term[(2 * ci) * WINDOW:(2 * ci + 1) * WINDOW]
                e_hi = sink_term[(2 * ci + 1) * WINDOW:(2 * ci + 2) * WINDOW]
                num = jnp.where(qlane_low, o_lo, pltpu.roll(o_hi, HEAD_DIM, 1))
                den = jnp.where(qlane_low, pltpu.roll(o_lo, HEAD_DIM, 1) + e_lo, o_hi + e_hi)
                o_ref[pl.ds(r0, WINDOW), c * LANES:(c + 1) * LANES] = (num / den).astype(o_ref.dtype)

    finish(0, scores(0, band & (c_id >= WINDOW)))

    def body(it, carry):
        n0 = 1 + it * SWA_UNROLL
        nxt = scores(n0, band)
        for u in range(SWA_UNROLL):
            cur = nxt
            if u + 1 < SWA_UNROLL:
                nxt = scores(n0 + u + 1, band)
            finish(n0 + u, cur)
        return carry

    lax.fori_loop(0, (seq // WINDOW - 1) // SWA_UNROLL, body, 0)


def _swa(proj, sinks, b0, batch, seq):
    grid_spec = pltpu.PrefetchScalarGridSpec(
        num_scalar_prefetch=0,
        grid=(batch,),
        in_specs=[
            pl.BlockSpec(memory_space=pltpu.SMEM),
            pl.BlockSpec((seq, W_Q_SWA), lambda b: (b + b0, 0)),
            pl.BlockSpec((seq, LANES), lambda b: (b + b0, COL_K_SWA)),
            pl.BlockSpec((seq, LANES), lambda b: (b + b0, COL_V_SWA)),
        ],
        out_specs=pl.BlockSpec((seq, W_Q_SWA), lambda b: (b, 0)),
        scratch_shapes=[pltpu.VMEM((N_KV_SWA, WINDOW + seq, LANES), jnp.bfloat16),
                        pltpu.VMEM((N_KV_SWA, WINDOW + seq, LANES), jnp.bfloat16)],
    )
    return pl.pallas_call(
        functools.partial(_swa_kernel, seq=seq),
        grid_spec=grid_spec,
        out_shape=jax.ShapeDtypeStruct((batch * seq, W_Q_SWA), jnp.bfloat16),
        compiler_params=_params("parallel"),
        name="swa_sink_attention",
    )(sinks, proj, proj, proj)


def _moba_kernel(q_ref, k_ref, v_ref, o_ref, qa_ref, ka_ref, va_ref, *, seq):
    nblk = seq // MOBA_BLOCK
    pad_rows = 16
    q_all, k_all, v_all = q_ref[...], k_ref[...], v_ref[...]
    kmean = jnp.sum(k_all.astype(jnp.float32).reshape(nblk, MOBA_BLOCK, LANES), axis=1) / MOBA_BLOCK
    kmean = jnp.concatenate([kmean, jnp.zeros((pad_rows - nblk, LANES), jnp.float32)], axis=0)
    klane_low = lax.broadcasted_iota(jnp.int32, (pad_rows, LANES), 1) < HEAD_DIM
    lane = lax.broadcasted_iota(jnp.int32, (seq, LANES), 1)
    low = lane < HEAD_DIM
    key_blk = lax.broadcasted_iota(jnp.int32, (seq, LANES), 0) // MOBA_BLOCK
    j_id = lax.broadcasted_iota(jnp.int32, (pad_rows, seq), 0)
    q_blk = lax.broadcasted_iota(jnp.int32, (pad_rows, seq), 1) // MOBA_BLOCK
    eligible = j_id < q_blk

    for half in range(2):
        own = low if half == 0 else ~low
        spare = HEAD_DIM if half == 0 else 0
        ka_ref[half] = jnp.where(own, k_all, (lane - spare == key_blk).astype(k_all.dtype))
        va_ref[half] = jnp.where(own, v_all, jnp.ones_like(v_all))
        km = jnp.where(klane_low if half == 0 else ~klane_low, kmean, 0.0).astype(jnp.bfloat16)
        gate = lax.dot_general(km, q_all, (((1,), (1,)), ((), ())),
                               preferred_element_type=jnp.float32)
        gate = jnp.where(eligible, gate, -jnp.inf)
        beaten = jnp.zeros((pad_rows, seq), jnp.int32)
        for jp in range(nblk):
            row = gate[jp:jp + 1, :]
            wins = (row > gate) | ((row == gate) & (jp < j_id))
            beaten = beaten + wins.astype(jnp.int32)
        dropped = eligible & (beaten >= MOBA_TOPK)
        bias = jnp.where(dropped, NEG_BIG, 0.0)
        pieces = [bias, jnp.zeros((LANES - spare - pad_rows, seq), jnp.float32)]
        if spare:
            pieces.insert(0, jnp.zeros((spare, seq), jnp.float32))
        bias_t = jnp.concatenate(pieces, axis=0).T
        qa_ref[half] = jnp.where(own, q_all, bias_t.astype(q_all.dtype))

    qlane_low = lax.broadcasted_iota(jnp.int32, (MOBA_BLOCK, LANES), 1) < HEAD_DIM
    rr = lax.broadcasted_iota(jnp.int32, (MOBA_BLOCK, MOBA_BLOCK), 0)
    cc = lax.broadcasted_iota(jnp.int32, (MOBA_BLOCK, MOBA_BLOCK), 1)
    causal = cc <= rr

    def scores(i, half):
        r0 = i * MOBA_BLOCK
        n_keys = r0 + MOBA_BLOCK
        s = lax.dot_general(qa_ref[half, r0:n_keys, :], ka_ref[half, 0:n_keys, :], (((1,), (1,)), ((), ())),
                            preferred_element_type=jnp.float32)
        own_blk = jnp.where(causal, s[:, r0:n_keys], NEG_BIG)
        return jnp.concatenate([s[:, :r0], own_blk], axis=1) if i else own_blk

    units = [(i, half) for i in range(nblk) for half in range(2)]
    pending = [scores(*u) for u in units[:MOBA_LOOKAHEAD]]
    acc = []
    for n, (i, half) in enumerate(units):
        s = pending.pop(0)
        if n + MOBA_LOOKAHEAD < len(units):
            pending.append(scores(*units[n + MOBA_LOOKAHEAD]))
        n_keys = (i + 1) * MOBA_BLOCK
        m = jnp.max(s, axis=-1, keepdims=True)
        p = jnp.exp2(s - m).astype(jnp.bfloat16)
        acc.append(jnp.dot(p, va_ref[half, 0:n_keys, :], preferred_element_type=jnp.float32))
        if half == 1:
            num = jnp.where(qlane_low, acc[0], acc[1])
            den = pltpu.roll(jnp.where(qlane_low, acc[1], acc[0]), HEAD_DIM, 1)
            o_ref[i * MOBA_BLOCK:n_keys, :] = (num / den).astype(o_ref.dtype)
            acc = []


def _moba(proj, b0, batch, seq):
    pairs = W_MOBA // LANES
    return pl.pallas_call(
        functools.partial(_moba_kernel, seq=seq),
        grid=(batch, pairs),
        in_specs=[
            pl.BlockSpec((seq, LANES), lambda b, p: (b + b0, COL_Q_MOBA + p)),
            pl.BlockSpec((seq, LANES), lambda b, p: (b + b0, COL_K_MOBA + p)),
            pl.BlockSpec((seq, LANES), lambda b, p: (b + b0, COL_V_MOBA + p)),
        ],
        out_specs=pl.BlockSpec((seq, LANES), lambda b, p: (b, p)),
        out_shape=jax.ShapeDtypeStruct((batch * seq, W_MOBA), jnp.bfloat16),
        scratch_shapes=[pltpu.VMEM((2, seq, LANES), jnp.bfloat16)] * 3,
        compiler_params=_params("parallel", "parallel"),
        name="moba_attention",
    )(proj, proj, proj)


def _layer_norm(h, g, b):
    mu = jnp.mean(h, axis=-1, keepdims=True)
    d = h - mu
    var = jnp.mean(d * d, axis=-1, keepdims=True)
    return d * lax.rsqrt(var + LN_EPS) * g + b


def _outproj_kernel(oa_ref, ob_ref, wo_ref, bo_ref, x_ref, g_ref, b_ref, wr_ref, br_ref, tri_ref,
                    x1_ref, x1p_ref, idx_ref, gate_ref, rank_ref, count_ref, running_ref):
    @pl.when(pl.program_id(0) == 0)
    def _():
        running_ref[...] = jnp.zeros(running_ref.shape, running_ref.dtype)

    half_rows = ROW_TILE // 2
    halves = [slice(h * half_rows, (h + 1) * half_rows) for h in range(2)]

    def project(rows):
        heads = jnp.concatenate([oa_ref[rows, :], ob_ref[rows, :]], axis=1)
        return jnp.dot(heads, wo_ref[...], preferred_element_type=jnp.float32) + bo_ref[...]

    def normalise(rows, mix):
        x1 = _layer_norm(DEEPNORM_ALPHA * x_ref[rows, :] + mix, g_ref[...], b_ref[...])
        x1_ref[rows, :] = x1
        x1p_ref[rows, :] = _pack_bf16_pairs(x1)
        return x1.astype(jnp.bfloat16)

    mixes = [project(rows) for rows in halves]
    x1b = jnp.concatenate([normalise(rows, mix) for rows, mix in zip(halves, mixes)], axis=0)

    logits = lax.dot_general(wr_ref[...], x1b, (((1,), (1,)), ((), ())),
                             preferred_element_type=jnp.float32) + br_ref[...]
    expert = lax.broadcasted_iota(jnp.int32, logits.shape, 0)
    idx_rows, val_rows, onehots = [], [], []
    top = None
    total = None
    for k in range(TOP_K):
        m = jnp.max(logits, axis=0, keepdims=True)
        idx = jnp.min(jnp.where(logits == m, expert, N_EXPERTS), axis=0, keepdims=True)
        picked = expert == idx
        onehots.append(picked)
        logits = jnp.where(picked, -jnp.inf, logits)
        if k == 0:
            top = m
        e = jnp.exp(m - top)
        total = e if k == 0 else total + e
        idx_rows.append(idx)
        val_rows.append(e)
    idx_ref[...] = jnp.concatenate(idx_rows, axis=0)
    gates_t = jnp.concatenate(val_rows, axis=0) / total
    gates_t = jnp.concatenate([gates_t, jnp.zeros((LANES - TOP_K, gates_t.shape[1]), jnp.float32)], axis=0)
    gate_ref[...] = gates_t.T[:, :TOP_K]

    picks = jnp.concatenate([p.astype(jnp.bfloat16) for p in onehots], axis=0)
    before = jnp.dot(picks, tri_ref[...], preferred_element_type=jnp.float32)
    base = running_ref[...]
    rank_rows = []
    for k in range(TOP_K):
        pk = onehots[k].astype(jnp.float32)
        here = before[k * N_EXPERTS:(k + 1) * N_EXPERTS, :] + base
        rank_rows.append(jnp.sum(pk * here, axis=0, keepdims=True))
        base = base + jnp.sum(pk, axis=1, keepdims=True)
    running_ref[...] = base
    rank_ref[...] = jnp.concatenate(rank_rows, axis=0).astype(jnp.int32)
    count_ref[...] = base.astype(jnp.int32)


def _outproj(o_a, o_b, w_o, b_out, x2, g, b, w_r, b_r, t0, n_tok):
    tile0 = t0 // ROW_TILE
    src = lambda i: (i + tile0, 0)
    row = lambda i: (i, 0)
    fixed = lambda i: (0, 0)
    r = lax.broadcasted_iota(jnp.int32, (ROW_TILE, ROW_TILE), 0)
    c = lax.broadcasted_iota(jnp.int32, (ROW_TILE, ROW_TILE), 1)
    tri = (r < c).astype(jnp.bfloat16)
    return pl.pallas_call(
        _outproj_kernel,
        grid=(n_tok // ROW_TILE,),
        in_specs=[
            pl.BlockSpec((ROW_TILE, W_Q_SWA), row),
            pl.BlockSpec((ROW_TILE, W_MOBA), row),
            pl.BlockSpec((W_Q_SWA + W_MOBA, D_MODEL), fixed),
            pl.BlockSpec((1, D_MODEL), fixed),
            pl.BlockSpec((ROW_TILE, D_MODEL), src),
            pl.BlockSpec((1, D_MODEL), fixed),
            pl.BlockSpec((1, D_MODEL), fixed),
            pl.BlockSpec((N_EXPERTS, D_MODEL), fixed),
            pl.BlockSpec((N_EXPERTS, 1), fixed),
            pl.BlockSpec((ROW_TILE, ROW_TILE), fixed),
        ],
        out_specs=[
            pl.BlockSpec((ROW_TILE, D_MODEL), row),
            pl.BlockSpec((ROW_TILE, PACKED), row),
            pl.BlockSpec((TOP_K, ROW_TILE), lambda i: (0, i)),
            pl.BlockSpec((ROW_TILE, TOP_K), row),
            pl.BlockSpec((TOP_K, ROW_TILE), lambda i: (0, i)),
            pl.BlockSpec((N_EXPERTS, 1), fixed),
        ],
        out_shape=[
            jax.ShapeDtypeStruct((n_tok, D_MODEL), jnp.float32),
            jax.ShapeDtypeStruct((n_tok, PACKED), jnp.int32),
            jax.ShapeDtypeStruct((TOP_K, n_tok), jnp.int32),
            jax.ShapeDtypeStruct((n_tok, TOP_K), jnp.float32),
            jax.ShapeDtypeStruct((TOP_K, n_tok), jnp.int32),
            jax.ShapeDtypeStruct((N_EXPERTS, 1), jnp.int32),
        ],
        scratch_shapes=[pltpu.VMEM((N_EXPERTS, 1), jnp.float32)],
        compiler_params=_params("arbitrary"),
        name="outproj_ln_router",
    )(o_a, o_b, w_o, b_out, x2, g, b, w_r, b_r, tri)


def _sc_worker_id():
    return lax.axis_index("s") * SC_CORES + lax.axis_index("c")


def _sc_scatter_rows(rows, pos3, n_out):
    n_tok = pos3.shape[0] * SC_ROWS
    steps = n_tok // SC_ROWS // SC_WORKERS
    assert steps * SC_ROWS * SC_WORKERS == n_tok and steps % 2 == 0
    mesh = plsc.VectorSubcoreMesh(core_axis_name="c", subcore_axis_name="s")

    @functools.partial(
        pl.kernel, mesh=mesh,
        out_type=jax.ShapeDtypeStruct((n_out, PACKED), jnp.int32),
        scratch_types=[pltpu.VMEM((2, TOP_K, SC_ROWS), jnp.int32), pltpu.VMEM((2, SC_ROWS, PACKED), jnp.int32),
                       pltpu.SemaphoreType.DMA((2,)), pltpu.SemaphoreType.DMA((2,))],
        name="sc_dispatch_scatter")
    def scatter(x_hbm, pos_hbm, out_hbm, idx_v, rows_v, sem_ld, sem_st):
        base = _sc_worker_id() * steps

        def loads(s, b):
            return (pltpu.make_async_copy(pos_hbm.at[base + s], idx_v.at[b], sem_ld.at[b]),
                    pltpu.make_async_copy(x_hbm.at[pl.ds((base + s) * SC_ROWS, SC_ROWS)], rows_v.at[b],
                                          sem_ld.at[b]))

        def stores(b):
            return [pltpu.make_async_copy(rows_v.at[b], out_hbm.at[idx_v.at[b, k]], sem_st.at[b])
                    for k in range(TOP_K)]

        for c in loads(0, 0):
            c.start()

        @pl.loop(0, steps, step=2)
        def _(s0):
            for b in range(2):
                s = s0 + b
                for c in loads(s, b):
                    c.wait()

                @pl.when(s >= 1)
                def _():
                    for c in stores(1 - b):
                        c.wait()

                @pl.when(s + 1 < steps)
                def _():
                    for c in loads(s + 1, 1 - b):
                        c.start()

                for c in stores(b):
                    c.start()

        for c in stores(1):
            c.wait()

    return scatter(rows, pos3)


def _sc_gather_rows(table, idx2):
    n_blk = idx2.shape[0]
    steps = n_blk // SC_WORKERS
    assert steps * SC_WORKERS == n_blk and steps % 2 == 0 and idx2.shape[1] == SC_ROWS
    mesh = plsc.VectorSubcoreMesh(core_axis_name="c", subcore_axis_name="s")

    @functools.partial(
        pl.kernel, mesh=mesh,
        out_type=jax.ShapeDtypeStruct((n_blk * SC_ROWS, PACKED), jnp.int32),
        scratch_types=[pltpu.VMEM((steps, SC_ROWS), jnp.int32), pltpu.VMEM((2, SC_ROWS, PACKED), jnp.int32),
                       pltpu.SemaphoreType.DMA((2,)), pltpu.SemaphoreType.DMA((2,))],
        name="sc_combine_gather")
    def gather(y_hbm, idx_hbm, out_hbm, idx_v, rows_v, sem_ld, sem_st):
        base = _sc_worker_id() * steps
        pltpu.sync_copy(idx_hbm.at[pl.ds(base, steps)], idx_v)

        def fetch(s, b):
            return pltpu.make_async_copy(y_hbm.at[idx_v.at[s]], rows_v.at[b], sem_ld.at[b])

        def store(s, b):
            return pltpu.make_async_copy(rows_v.at[b], out_hbm.at[pl.ds((base + s) * SC_ROWS, SC_ROWS)],
                                         sem_st.at[b])

        fetch(0, 0).start()

        @pl.loop(0, steps, step=2)
        def _(s0):
            for b in range(2):
                s = s0 + b
                fetch(s, b).wait()

                @pl.when(s >= 1)
                def _():
                    store(s - 1, 1 - b).wait()

                @pl.when(s + 1 < steps)
                def _():
                    fetch(s + 1, 1 - b).start()

                store(s, b).start()

        store(steps - 1, 1).wait()

    return gather(table, idx2)


def _expert_kernel(be_ref, br_ref, slot_ref, next_ref, nv_ref, x_ref, w1_hbm, perm_ref, bg_ref, bl_ref, w2_hbm,
                   b2_ref, y_ref, w1f_ref, w2f_ref, wg_ref, wl_ref, w2b_ref, sem):
    i = pl.program_id(0)
    live = i < nv_ref[0]
    new_expert = (i == 0) | (be_ref[i] != be_ref[jnp.maximum(i - 1, 0)])

    def weight_copies(expert, slot):
        return (pltpu.make_async_copy(w1_hbm.at[expert], w1f_ref.at[slot], sem.at[0, slot]),
                pltpu.make_async_copy(w2_hbm.at[expert], w2f_ref.at[slot], sem.at[1, slot]))

    @pl.when(live & new_expert)
    def _():
        slot = slot_ref[i]

        @pl.when(i == 0)
        def _():
            for copy in weight_copies(be_ref[i], slot):
                copy.start()

        for copy in weight_copies(be_ref[i], slot):
            copy.wait()

        @pl.when(next_ref[i] >= 0)
        def _():
            for copy in weight_copies(next_ref[i], 1 - slot):
                copy.start()

        for c in range(2 * D_FF // 256):
            t = w1f_ref[slot, :, c * 256:(c + 1) * 256].astype(jnp.bfloat16)
            r = jnp.dot(t, perm_ref[...], preferred_element_type=jnp.float32)
            wg_ref[:, c * LANES:(c + 1) * LANES] = r[:, :LANES].astype(wg_ref.dtype)
            wl_ref[:, c * LANES:(c + 1) * LANES] = r[:, LANES:].astype(wl_ref.dtype)
        w2b_ref[...] = w2f_ref[slot].astype(w2b_ref.dtype)

    def mlp(rows):
        valid = lax.broadcasted_iota(jnp.int32, (rows, PACKED), 0) < br_ref[i]
        lo, hi = _unpack_bf16_pairs(jnp.where(valid, x_ref[0:rows, :], 0))
        xb = jnp.concatenate([lo.astype(jnp.bfloat16), hi.astype(jnp.bfloat16)], axis=1)
        hg = jnp.dot(xb, wg_ref[...], preferred_element_type=jnp.float32) + bg_ref[0]
        hl = jnp.dot(xb, wl_ref[...], preferred_element_type=jnp.float32) + bl_ref[0]
        glu = jnp.minimum(hg, SWIGLU_LIMIT)
        lin = jnp.clip(hl, -SWIGLU_LIMIT, SWIGLU_LIMIT)
        act = glu * jax.nn.sigmoid(SWIGLU_ALPHA * glu) * (lin + 1.0)
        y = jnp.dot(act.astype(jnp.bfloat16), w2b_ref[...], preferred_element_type=jnp.float32) + b2_ref[0]
        y_ref[0:rows, :] = _pack_bf16_pairs(y)

    parts = (br_ref[i] + EXPERT_PART - 1) // EXPERT_PART
    for n_parts in range(1, EXPERT_ROWS // EXPERT_PART + 1):
        @pl.when(live & (parts == n_parts))
        def _(rows=n_parts * EXPERT_PART):
            mlp(rows)
            if rows < EXPERT_ROWS:
                y_ref[rows:, :] = jnp.zeros((EXPERT_ROWS - rows, PACKED), y_ref.dtype)

    @pl.when(jnp.logical_not(live))
    def _():
        y_ref[...] = jnp.zeros(y_ref.shape, y_ref.dtype)


def _experts(tables, xg, w1, perm, b1g, b1l, w2, b2):
    n_rows = xg.shape[0]
    n_blocks = n_rows // EXPERT_ROWS

    def row(i, be, br, slot, nxt, nv):
        return (jnp.minimum(i, nv[0] - 1), 0)

    def per_expert(i, be, br, slot, nxt, nv):
        return (be[i], 0, 0)

    grid_spec = pltpu.PrefetchScalarGridSpec(
        num_scalar_prefetch=5,
        grid=(n_blocks,),
        in_specs=[
            pl.BlockSpec((EXPERT_ROWS, PACKED), row),
            pl.BlockSpec(memory_space=pl.ANY),
            pl.BlockSpec((256, 256), lambda i, *_: (0, 0)),
            pl.BlockSpec((1, 1, D_FF), per_expert),
            pl.BlockSpec((1, 1, D_FF), per_expert),
            pl.BlockSpec(memory_space=pl.ANY),
            pl.BlockSpec((1, 1, D_MODEL), per_expert),
        ],
        out_specs=pl.BlockSpec((EXPERT_ROWS, PACKED), lambda i, *_: (i, 0)),
        scratch_shapes=[pltpu.VMEM((2, D_MODEL, 2 * D_FF), jnp.float32), pltpu.VMEM((2, D_FF, D_MODEL), jnp.float32),
                        pltpu.VMEM((D_MODEL, D_FF), jnp.bfloat16), pltpu.VMEM((D_MODEL, D_FF), jnp.bfloat16),
                        pltpu.VMEM((D_FF, D_MODEL), jnp.bfloat16), pltpu.SemaphoreType.DMA((2, 2))],
    )
    return pl.pallas_call(
        _expert_kernel,
        grid_spec=grid_spec,
        out_shape=jax.ShapeDtypeStruct((n_rows, PACKED), jnp.int32),
        compiler_params=pltpu.CompilerParams(dimension_semantics=("arbitrary",),
                                             vmem_limit_bytes=EXPERT_VMEM_LIMIT),
        name="grouped_experts",
    )(*tables, xg, w1, perm, b1g, b1l, w2, b2)


def _split_columns_perm():
    i = lax.broadcasted_iota(jnp.int32, (256, 256), 0)
    o = lax.broadcasted_iota(jnp.int32, (256, 256), 1)
    return (i == jnp.where(o < LANES, 2 * o, 2 * (o - LANES) + 1)).astype(jnp.bfloat16)


def _combine_kernel(y_ref, gate_ref, x1_ref, g_ref, b_ref, *rest):
    o_ref = rest[-1]
    gates = gate_ref[...]
    lo_sum = None
    hi_sum = None
    for k in range(TOP_K):
        lo, hi = _unpack_bf16_pairs(y_ref[k])
        gk = gates[:, k:k + 1]
        lo_sum = gk * lo if k == 0 else lo_sum + gk * lo
        hi_sum = gk * hi if k == 0 else hi_sum + gk * hi
    moe = jnp.concatenate([lo_sum, hi_sum], axis=1)
    o_ref[...] = _layer_norm(DEEPNORM_ALPHA * x1_ref[...] + moe, g_ref[...], b_ref[...])


def _combine(yg, gates, x1, g, b, tile0, n_tok, prev):
    row = lambda i: (i, 0)
    fixed = lambda i: (0, 0)
    in_specs = [
        pl.BlockSpec((TOP_K, ROW_TILE, PACKED), lambda i: (0, i, 0)),
        pl.BlockSpec((ROW_TILE, TOP_K), row),
        pl.BlockSpec((ROW_TILE, D_MODEL), row),
        pl.BlockSpec((1, D_MODEL), fixed),
        pl.BlockSpec((1, D_MODEL), fixed),
    ]
    args = [yg, gates, x1, g, b]
    aliases = {}
    if prev is not None:
        in_specs.append(pl.BlockSpec(memory_space=pl.ANY))
        args.append(prev)
        aliases = {len(args) - 1: 0}
    return pl.pallas_call(
        _combine_kernel,
        grid=(yg.shape[1] // ROW_TILE,),
        in_specs=in_specs,
        out_specs=pl.BlockSpec((ROW_TILE, D_MODEL), lambda i: (i + tile0, 0)),
        out_shape=jax.ShapeDtypeStruct((n_tok, D_MODEL), jnp.float32),
        input_output_aliases=aliases,
        compiler_params=_params("parallel"),
        name="combine_ln",
    )(*args)


def _route(top_idx, rank, counts, n_blocks):
    counts = counts[:, 0]
    experts = jnp.arange(N_EXPERTS, dtype=jnp.int32)
    upto = experts[None, :] <= experts[:, None]
    later = experts[None, :] > experts[:, None]
    prefix = lambda v: jnp.sum(jnp.where(upto, v[None, :], 0), axis=1)
    blocks_per = (counts + EXPERT_ROWS - 1) // EXPERT_ROWS
    blk_end = prefix(blocks_per)
    blk_start = blk_end - blocks_per
    first_row = (blk_start * EXPERT_ROWS)[:, None, None]
    pos = rank + jnp.sum(jnp.where(top_idx[None] == experts[:, None, None], first_row, 0), axis=0)
    n_valid = jnp.sum(blocks_per, keepdims=True).astype(jnp.int32)
    has_rows = blocks_per > 0
    run_of = prefix(has_rows.astype(jnp.int32)) - 1
    later = later & has_rows[None, :]
    next_of = jnp.min(jnp.where(later, experts[None, :], N_EXPERTS), axis=1)
    next_of = jnp.where(next_of == N_EXPERTS, -1, next_of)
    blk = jnp.arange(n_blocks, dtype=jnp.int32)
    owner = ((blk[:, None] >= blk_start[None, :]) & (blk[:, None] < blk_end[None, :])).astype(jnp.int32)
    pick = lambda per_expert: jnp.sum(owner * per_expert[None, :], axis=1).astype(jnp.int32)
    block_expert = pick(experts)
    block_rows = jnp.clip(pick(counts) - (blk - pick(blk_start)) * EXPERT_ROWS, 0, EXPERT_ROWS).astype(jnp.int32)
    return pos, (block_expert, block_rows, pick(run_of) % 2, pick(next_of), n_valid)


def _moe_groups(n_tok, seq):
    unit = math.lcm(2 * SC_ROWS * SC_WORKERS, seq)
    parts = sum(MOE_SPLIT)
    if n_tok % (parts * unit):
        return ((0, n_tok),)
    groups, t0 = [], 0
    for share in MOE_SPLIT:
        groups.append((t0, n_tok * share // parts))
        t0 += groups[-1][1]
    return tuple(groups)


def _rope_tables(seq):
    inv_freq = 1.0 / (ROPE_THETA ** (jnp.arange(0, HEAD_DIM, 2, dtype=jnp.float32) / HEAD_DIM))
    ang = jnp.arange(seq, dtype=jnp.float32)[:, None] * inv_freq[None, :]
    cos, sin = jnp.cos(ang), jnp.sin(ang)
    cos_t = jnp.tile(jnp.concatenate([cos, cos], axis=1), (1, 256 // HEAD_DIM))
    sin_t = jnp.tile(jnp.concatenate([-sin, sin], axis=1), (1, 256 // HEAD_DIM))
    return cos_t, sin_t


def kernel(x, w_in, b_in, sinks, w_out, b_out, ln1_g, ln1_b, w_router, b_router, w1, b1, w2, b2, ln2_g, ln2_b):
    batch, seq, d = x.shape
    assert d == D_MODEL and seq % ROW_TILE == 0 and seq % MOBA_BLOCK == 0 and w_in.shape[0] == DEPTH == 1
    assert (seq // WINDOW - 1) % SWA_UNROLL == 0
    n_tok = batch * seq
    bf16 = jnp.bfloat16
    x2 = x.reshape(n_tok, d)
    cos_t, sin_t = _rope_tables(seq)

    proj = _inproj(x2, w_in[0].astype(bf16), b_in[0].reshape(1, IN_WIDTH), cos_t, sin_t, seq)

    w_o = w_out[0].astype(bf16)
    w_r = w_router[0].T.astype(bf16)
    b_r = b_router[0].reshape(N_EXPERTS, 1)
    b_o, g1, be1 = b_out[0].reshape(1, d), ln1_g[0].reshape(1, d), ln1_b[0].reshape(1, d)
    perm = _split_columns_perm()
    b1r = b1[0].reshape(N_EXPERTS, 1, D_FF, 2)
    b1g, b1l = b1r[..., 0], b1r[..., 1]
    b2r = b2[0].reshape(N_EXPERTS, 1, d)
    g2, be2 = ln2_g[0].reshape(1, d), ln2_b[0].reshape(1, d)

    def dispatch(stage):
        t0, tok_g, x1, x1p, gates, pos, tables, n_blocks = stage
        pos3 = pos.reshape(TOP_K, tok_g // SC_ROWS, SC_ROWS).transpose(1, 0, 2)
        return _sc_scatter_rows(x1p, pos3, n_blocks * EXPERT_ROWS)

    def experts_and_gather(stage, xg):
        t0, tok_g, x1, x1p, gates, pos, tables, n_blocks = stage
        y = _experts(tables, xg, w1[0], perm, b1g, b1l, w2[0], b2r)
        return _sc_gather_rows(y, pos.reshape(tok_g * TOP_K // SC_ROWS, SC_ROWS))

    stages, gathered, waiting = [], [], None
    for t0, tok_g in _moe_groups(n_tok, seq):
        o_a = _swa(proj, sinks[0], t0 // seq, tok_g // seq, seq)
        o_b = _moba(proj, t0 // seq, tok_g // seq, seq)
        x1, x1p, top_idx, gates, rank, counts = _outproj(o_a, o_b, w_o, b_o, x2, g1, be1, w_r, b_r, t0, tok_g)
        n_blocks = tok_g * TOP_K // EXPERT_ROWS + N_EXPERTS
        pos, tables = _route(top_idx, rank, counts, n_blocks)
        stages.append((t0, tok_g, x1, x1p, gates, pos, tables, n_blocks))
        if waiting is not None:
            gathered.append(experts_and_gather(*waiting))
        waiting = (stages[-1], dispatch(stages[-1]))
    gathered.append(experts_and_gather(*waiting))
    out = None
    for (t0, tok_g, x1, _, gates, _, _, _), yg in zip(stages, gathered):
        out = _combine(yg.reshape(TOP_K, tok_g, PACKED), gates, x1, g2, be2, t0 // ROW_TILE, n_tok, out)
    return out.reshape(batch, seq, d)
```

```python
import functools
import math

import jax
import jax.numpy as jnp
from jax import lax
from jax.experimental import pallas as pl
from jax.experimental.pallas import tpu as pltpu
from jax.experimental.pallas import tpu_sc as plsc

D_MODEL = 1024
HEAD_DIM = 64
N_HEADS_SWA = 8
N_KV_SWA = 2
WINDOW = 128
N_HEADS_MOBA = 8
MOBA_BLOCK = 256
MOBA_TOPK = 3
ROPE_THETA = 10000.0
N_EXPERTS = 32
TOP_K = 4
D_FF = 1024
SWIGLU_LIMIT = 7.0
SWIGLU_ALPHA = 1.702
LN_EPS = 1e-5
DEPTH = 1
DEEPNORM_ALPHA = (2 * DEPTH) ** 0.25

W_Q_SWA = N_HEADS_SWA * HEAD_DIM
W_KV_SWA = N_KV_SWA * HEAD_DIM
W_MOBA = N_HEADS_MOBA * HEAD_DIM
IN_WIDTH = W_Q_SWA + 2 * W_KV_SWA + 3 * W_MOBA
LANES = 128
COL_K_SWA = W_Q_SWA // LANES
COL_V_SWA = COL_K_SWA + 1
COL_Q_MOBA = COL_V_SWA + 1
COL_K_MOBA = COL_Q_MOBA + W_MOBA // LANES
COL_V_MOBA = COL_K_MOBA + W_MOBA // LANES

ROW_TILE = 1024
EXPERT_ROWS = 1024
EXPERT_PART = 256
PACKED = D_MODEL // 2
NEG_BIG = -1e30
LOG2E = 1.4426950408889634
MOBA_LOOKAHEAD = 1
SWA_UNROLL = 3
MOE_SPLIT = (3, 1)
VMEM_LIMIT = 48 * 1024 * 1024
EXPERT_VMEM_LIMIT = 60 * 1024 * 1024

SC_CORES = 2
SC_SUBCORES = 16
SC_WORKERS = SC_CORES * SC_SUBCORES
SC_ROWS = 64

_PROJ_CHUNKS = (
    (0, 256, 256, True), (256, 256, 256, True),
    (512, 256, 128, False),
    (768, 256, 256, True), (1024, 256, 256, True),
    (1280, 256, 256, False), (1536, 256, 256, False),
    (1792, 256, 0, False), (2048, 256, 0, False),
)


def _params(*sem):
    return pltpu.CompilerParams(dimension_semantics=sem, vmem_limit_bytes=VMEM_LIMIT)


def _pack_bf16_pairs(v):
    n = v.shape[1] // 2
    bits = lax.bitcast_convert_type(v.astype(jnp.bfloat16).astype(jnp.float32), jnp.uint32)
    word = (bits[:, :n] >> 16) | (bits[:, n:] & jnp.uint32(0xFFFF0000))
    return lax.bitcast_convert_type(word, jnp.int32)


def _unpack_bf16_pairs(word):
    bits = lax.bitcast_convert_type(word, jnp.uint32)
    lo = lax.bitcast_convert_type(bits << 16, jnp.float32)
    hi = lax.bitcast_convert_type(bits & jnp.uint32(0xFFFF0000), jnp.float32)
    return lo, hi


def _inproj_kernel(x_ref, w_ref, b_ref, cos_ref, sin_ref, o_ref):
    xb = x_ref[...].astype(jnp.bfloat16)
    for start, width, rope, scaled in _PROJ_CHUNKS:
        t = jnp.dot(xb, w_ref[:, start:start + width], preferred_element_type=jnp.float32)
        t = t + b_ref[:, start:start + width]
        if rope:
            lane = lax.broadcasted_iota(jnp.int32, t.shape, 1)
            first_half = (lane % HEAD_DIM) < (HEAD_DIM // 2)
            rot = jnp.where(first_half,
                            pltpu.roll(t, width - HEAD_DIM // 2, 1),
                            pltpu.roll(t, HEAD_DIM // 2, 1))
            roped = t * cos_ref[:, :width] + rot * sin_ref[:, :width]
            t = roped if rope == width else jnp.where(lane < rope, roped, t)
        if scaled:
            t = t * (HEAD_DIM ** -0.5 * LOG2E)
        o_ref[:, start:start + width] = t.astype(o_ref.dtype)


def _inproj(x2, w_in, b_in, cos_t, sin_t, seq):
    n_tok = x2.shape[0]
    per_seq = seq // ROW_TILE
    return pl.pallas_call(
        _inproj_kernel,
        grid=(n_tok // ROW_TILE,),
        in_specs=[
            pl.BlockSpec((ROW_TILE, D_MODEL), lambda i: (i, 0)),
            pl.BlockSpec((D_MODEL, IN_WIDTH), lambda i: (0, 0)),
            pl.BlockSpec((1, IN_WIDTH), lambda i: (0, 0)),
            pl.BlockSpec((ROW_TILE, 256), lambda i: (i % per_seq, 0)),
            pl.BlockSpec((ROW_TILE, 256), lambda i: (i % per_seq, 0)),
        ],
        out_specs=pl.BlockSpec((ROW_TILE, IN_WIDTH), lambda i: (i, 0)),
        out_shape=jax.ShapeDtypeStruct((n_tok, IN_WIDTH), jnp.bfloat16),
        compiler_params=_params("parallel"),
        name="inproj_rope",
    )(x2, w_in, b_in, cos_t, sin_t)


def _swa_kernel(sink_ref, q_ref, k_ref, v_ref, after_ref, o_ref, kd_ref, vd_ref, *, seq):
    del after_ref
    lane = lax.broadcasted_iota(jnp.int32, (seq, LANES), 1)
    low = lane < HEAD_DIM
    k = k_ref[...].astype(jnp.float32)
    kr = pltpu.roll(k, HEAD_DIM, 1)
    kd_ref[0, WINDOW:, :] = jnp.where(low, k, kr).astype(kd_ref.dtype)
    kd_ref[1, WINDOW:, :] = jnp.where(low, kr, k).astype(kd_ref.dtype)
    v = v_ref[...].astype(jnp.float32)
    vd_ref[0, WINDOW:, :] = jnp.where(low, v, 1.0).astype(vd_ref.dtype)
    vd_ref[1, WINDOW:, :] = jnp.where(low, pltpu.roll(v, HEAD_DIM, 1), 1.0).astype(vd_ref.dtype)
    kd_ref[:, :WINDOW, :] = jnp.zeros((N_KV_SWA, WINDOW, LANES), kd_ref.dtype)
    vd_ref[:, :WINDOW, :] = jnp.zeros((N_KV_SWA, WINDOW, LANES), vd_ref.dtype)

    group = N_HEADS_SWA // N_KV_SWA
    rows = group * WINDOW
    r_in = lax.broadcasted_iota(jnp.int32, (rows, 2 * WINDOW), 0) % WINDOW
    c_id = lax.broadcasted_iota(jnp.int32, (rows, 2 * WINDOW), 1)
    band = (c_id > r_in) & (c_id <= r_in + WINDOW)
    head_in_group = lax.broadcasted_iota(jnp.int32, (rows, 1), 0) // WINDOW
    qlane_low = lax.broadcasted_iota(jnp.int32, (WINDOW, LANES), 1) < HEAD_DIM
    sinks = []
    for g in range(N_KV_SWA):
        col = jnp.zeros((rows, 1), jnp.float32)
        for j in range(group):
            col = jnp.where(head_in_group == j, sink_ref[g * group + j] * LOG2E, col)
        sinks.append(col)

    def scores(n, mask):
        r0 = pl.multiple_of(n * WINDOW, WINDOW)
        out = []
        for g in range(N_KV_SWA):
            parts = []
            for c in (2 * g, 2 * g + 1):
                qc = q_ref[pl.ds(r0, WINDOW), c * LANES:(c + 1) * LANES]
                zero = jnp.zeros_like(qc)
                parts.append(jnp.where(qlane_low, qc, zero))
                parts.append(jnp.where(qlane_low, zero, qc))
            qcat = jnp.concatenate(parts, axis=0)
            kd = kd_ref[g, pl.ds(r0, 2 * WINDOW), :]
            s = lax.dot_general(qcat, kd, (((1,), (1,)), ((), ())),
                                preferred_element_type=jnp.float32)
            out.append(jnp.where(mask, s, -jnp.inf))
        return out

    def finish(n, scored):
        r0 = pl.multiple_of(n * WINDOW, WINDOW)
        for g, s in enumerate(scored):
            vd = vd_ref[g, pl.ds(r0, 2 * WINDOW), :]
            m = jnp.maximum(jnp.max(s, axis=-1, keepdims=True), sinks[g])
            p = jnp.exp2(s - m)
            o = jnp.dot(p.astype(vd.dtype), vd, preferred_element_type=jnp.float32)
            sink_term = jnp.exp2(sinks[g] - m)
            for ci, c in enumerate((2 * g, 2 * g + 1)):
                o_lo = o[(2 * ci) * WINDOW:(2 * ci + 1) * WINDOW]
                o_hi = o[(2 * ci + 1) * WINDOW:(2 * ci + 2) * WINDOW]
                e_lo = sink_term[(2 * ci) * WINDOW:(2 * ci + 1) * WINDOW]
                e_hi = sink_term[(2 * ci + 1) * WINDOW:(2 * ci + 2) * WINDOW]
                num = jnp.where(qlane_low, o_lo, pltpu.roll(o_hi, HEAD_DIM, 1))
                den = jnp.where(qlane_low, pltpu.roll(o_lo, HEAD_DIM, 1) + e_lo, o_hi + e_hi)
                o_ref[pl.ds(r0, WINDOW), c * LANES:(c + 1) * LANES] = (num / den).astype(o_ref.dtype)

    finish(0, scores(0, band & (c_id >= WINDOW)))

    def body(it, carry):
        n0 = 1 + it * SWA_UNROLL
        nxt = scores(n0, band)
        for u in range(SWA_UNROLL):
            cur = nxt
            if u + 1 < SWA_UNROLL:
                nxt = scores(n0 + u + 1, band)
            finish(n0 + u, cur)
        return carry

    lax.fori_loop(0, (seq // WINDOW - 1) // SWA_UNROLL, body, 0)


def _swa(proj, sinks, b0, batch, seq, after):
    grid_spec = pltpu.PrefetchScalarGridSpec(
        num_scalar_prefetch=0,
        grid=(batch,),
        in_specs=[
            pl.BlockSpec(memory_space=pltpu.SMEM),
            pl.BlockSpec((seq, W_Q_SWA), lambda b: (b + b0, 0)),
            pl.BlockSpec((seq, LANES), lambda b: (b + b0, COL_K_SWA)),
            pl.BlockSpec((seq, LANES), lambda b: (b + b0, COL_V_SWA)),
            pl.BlockSpec(memory_space=pl.ANY),
        ],
        out_specs=pl.BlockSpec((seq, W_Q_SWA), lambda b: (b, 0)),
        scratch_shapes=[pltpu.VMEM((N_KV_SWA, WINDOW + seq, LANES), jnp.bfloat16),
                        pltpu.VMEM((N_KV_SWA, WINDOW + seq, LANES), jnp.bfloat16)],
    )
    return pl.pallas_call(
        functools.partial(_swa_kernel, seq=seq),
        grid_spec=grid_spec,
        out_shape=jax.ShapeDtypeStruct((batch * seq, W_Q_SWA), jnp.bfloat16),
        compiler_params=_params("parallel"),
        name="swa_sink_attention",
    )(sinks, proj, proj, proj, after)


def _moba_kernel(q_ref, k_ref, v_ref, after_ref, o_ref, qa_ref, ka_ref, va_ref, *, seq):
    del after_ref
    nblk = seq // MOBA_BLOCK
    pad_rows = 16
    q_all, k_all, v_all = q_ref[...], k_ref[...], v_ref[...]
    kmean = jnp.sum(k_all.astype(jnp.float32).reshape(nblk, MOBA_BLOCK, LANES), axis=1) / MOBA_BLOCK
    kmean = jnp.concatenate([kmean, jnp.zeros((pad_rows - nblk, LANES), jnp.float32)], axis=0)
    klane_low = lax.broadcasted_iota(jnp.int32, (pad_rows, LANES), 1) < HEAD_DIM
    lane = lax.broadcasted_iota(jnp.int32, (seq, LANES), 1)
    low = lane < HEAD_DIM
    key_blk = lax.broadcasted_iota(jnp.int32, (seq, LANES), 0) // MOBA_BLOCK
    j_id = lax.broadcasted_iota(jnp.int32, (pad_rows, seq), 0)
    q_blk = lax.broadcasted_iota(jnp.int32, (pad_rows, seq), 1) // MOBA_BLOCK
    eligible = j_id < q_blk

    for half in range(2):
        own = low if half == 0 else ~low
        spare = HEAD_DIM if half == 0 else 0
        ka_ref[half] = jnp.where(own, k_all, (lane - spare == key_blk).astype(k_all.dtype))
        va_ref[half] = jnp.where(own, v_all, jnp.ones_like(v_all))
        km = jnp.where(klane_low if half == 0 else ~klane_low, kmean, 0.0).astype(jnp.bfloat16)
        gate = lax.dot_general(km, q_all, (((1,), (1,)), ((), ())),
                               preferred_element_type=jnp.float32)
        gate = jnp.where(eligible, gate, -jnp.inf)
        beaten = jnp.zeros((pad_rows, seq), jnp.int32)
        for jp in range(nblk):
            row = gate[jp:jp + 1, :]
            wins = (row > gate) | ((row == gate) & (jp < j_id))
            beaten = beaten + wins.astype(jnp.int32)
        dropped = eligible & (beaten >= MOBA_TOPK)
        bias = jnp.where(dropped, NEG_BIG, 0.0)
        pieces = [bias, jnp.zeros((LANES - spare - pad_rows, seq), jnp.float32)]
        if spare:
            pieces.insert(0, jnp.zeros((spare, seq), jnp.float32))
        bias_t = jnp.concatenate(pieces, axis=0).T
        qa_ref[half] = jnp.where(own, q_all, bias_t.astype(q_all.dtype))

    qlane_low = lax.broadcasted_iota(jnp.int32, (MOBA_BLOCK, LANES), 1) < HEAD_DIM
    rr = lax.broadcasted_iota(jnp.int32, (MOBA_BLOCK, MOBA_BLOCK), 0)
    cc = lax.broadcasted_iota(jnp.int32, (MOBA_BLOCK, MOBA_BLOCK), 1)
    causal = cc <= rr

    def scores(i, half):
        r0 = i * MOBA_BLOCK
        n_keys = r0 + MOBA_BLOCK
        s = lax.dot_general(qa_ref[half, r0:n_keys, :], ka_ref[half, 0:n_keys, :], (((1,), (1,)), ((), ())),
                            preferred_element_type=jnp.float32)
        own_blk = jnp.where(causal, s[:, r0:n_keys], NEG_BIG)
        return jnp.concatenate([s[:, :r0], own_blk], axis=1) if i else own_blk

    units = [(i, half) for i in range(nblk) for half in range(2)]
    pending = [scores(*u) for u in units[:MOBA_LOOKAHEAD]]
    acc = []
    for n, (i, half) in enumerate(units):
        s = pending.pop(0)
        if n + MOBA_LOOKAHEAD < len(units):
            pending.append(scores(*units[n + MOBA_LOOKAHEAD]))
        n_keys = (i + 1) * MOBA_BLOCK
        m = jnp.max(s, axis=-1, keepdims=True)
        p = jnp.exp2(s - m).astype(jnp.bfloat16)
        acc.append(jnp.dot(p, va_ref[half, 0:n_keys, :], preferred_element_type=jnp.float32))
        if half == 1:
            num = jnp.where(qlane_low, acc[0], acc[1])
            den = pltpu.roll(jnp.where(qlane_low, acc[1], acc[0]), HEAD_DIM, 1)
            o_ref[i * MOBA_BLOCK:n_keys, :] = (num / den).astype(o_ref.dtype)
            acc = []


def _moba(proj, b0, batch, seq, after):
    pairs = W_MOBA // LANES
    return pl.pallas_call(
        functools.partial(_moba_kernel, seq=seq),
        grid=(batch, pairs),
        in_specs=[
            pl.BlockSpec((seq, LANES), lambda b, p: (b + b0, COL_Q_MOBA + p)),
            pl.BlockSpec((seq, LANES), lambda b, p: (b + b0, COL_K_MOBA + p)),
            pl.BlockSpec((seq, LANES), lambda b, p: (b + b0, COL_V_MOBA + p)),
            pl.BlockSpec(memory_space=pl.ANY),
        ],
        out_specs=pl.BlockSpec((seq, LANES), lambda b, p: (b, p)),
        out_shape=jax.ShapeDtypeStruct((batch * seq, W_MOBA), jnp.bfloat16),
        scratch_shapes=[pltpu.VMEM((2, seq, LANES), jnp.bfloat16)] * 3,
        compiler_params=_params("parallel", "parallel"),
        name="moba_attention",
    )(proj, proj, proj, after)


def _layer_norm(h, g, b):
    mu = jnp.mean(h, axis=-1, keepdims=True)
    d = h - mu
    var = jnp.mean(d * d, axis=-1, keepdims=True)
    return d * lax.rsqrt(var + LN_EPS) * g + b


def _outproj_kernel(oa_ref, ob_ref, wo_ref, bo_ref, x_ref, g_ref, b_ref, wr_ref, br_ref, tri_ref,
                    x1_ref, x1p_ref, idx_ref, gate_ref, rank_ref, count_ref, running_ref):
    @pl.when(pl.program_id(0) == 0)
    def _():
        running_ref[...] = jnp.zeros(running_ref.shape, running_ref.dtype)

    half_rows = ROW_TILE // 2
    halves = [slice(h * half_rows, (h + 1) * half_rows) for h in range(2)]

    def project(rows):
        heads = jnp.concatenate([oa_ref[rows, :], ob_ref[rows, :]], axis=1)
        return jnp.dot(heads, wo_ref[...], preferred_element_type=jnp.float32) + bo_ref[...]

    def normalise(rows, mix):
        x1 = _layer_norm(DEEPNORM_ALPHA * x_ref[rows, :] + mix, g_ref[...], b_ref[...])
        x1_ref[rows, :] = x1
        x1p_ref[rows, :] = _pack_bf16_pairs(x1)
        return x1.astype(jnp.bfloat16)

    mixes = [project(rows) for rows in halves]
    x1b = jnp.concatenate([normalise(rows, mix) for rows, mix in zip(halves, mixes)], axis=0)

    logits = lax.dot_general(wr_ref[...], x1b, (((1,), (1,)), ((), ())),
                             preferred_element_type=jnp.float32) + br_ref[...]
    expert = lax.broadcasted_iota(jnp.int32, logits.shape, 0)
    idx_rows, val_rows, onehots = [], [], []
    top = None
    total = None
    for k in range(TOP_K):
        m = jnp.max(logits, axis=0, keepdims=True)
        idx = jnp.min(jnp.where(logits == m, expert, N_EXPERTS), axis=0, keepdims=True)
        picked = expert == idx
        onehots.append(picked)
        logits = jnp.where(picked, -jnp.inf, logits)
        if k == 0:
            top = m
        e = jnp.exp(m - top)
        total = e if k == 0 else total + e
        idx_rows.append(idx)
        val_rows.append(e)
    idx_ref[...] = jnp.concatenate(idx_rows, axis=0)
    gates_t = jnp.concatenate(val_rows, axis=0) / total
    gates_t = jnp.concatenate([gates_t, jnp.zeros((LANES - TOP_K, gates_t.shape[1]), jnp.float32)], axis=0)
    gate_ref[...] = gates_t.T[:, :TOP_K]

    picks = jnp.concatenate([p.astype(jnp.bfloat16) for p in onehots], axis=0)
    before = jnp.dot(picks, tri_ref[...], preferred_element_type=jnp.float32)
    base = running_ref[...]
    rank_rows = []
    for k in range(TOP_K):
        pk = onehots[k].astype(jnp.float32)
        here = before[k * N_EXPERTS:(k + 1) * N_EXPERTS, :] + base
        rank_rows.append(jnp.sum(pk * here, axis=0, keepdims=True))
        base = base + jnp.sum(pk, axis=1, keepdims=True)
    running_ref[...] = base
    rank_ref[...] = jnp.concatenate(rank_rows, axis=0).astype(jnp.int32)
    count_ref[...] = base.astype(jnp.int32)


def _outproj(o_a, o_b, w_o, b_out, x2, g, b, w_r, b_r, t0, n_tok):
    tile0 = t0 // ROW_TILE
    src = lambda i: (i + tile0, 0)
    row = lambda i: (i, 0)
    fixed = lambda i: (0, 0)
    r = lax.broadcasted_iota(jnp.int32, (ROW_TILE, ROW_TILE), 0)
    c = lax.broadcasted_iota(jnp.int32, (ROW_TILE, ROW_TILE), 1)
    tri = (r < c).astype(jnp.bfloat16)
    return pl.pallas_call(
        _outproj_kernel,
        grid=(n_tok // ROW_TILE,),
        in_specs=[
            pl.BlockSpec((ROW_TILE, W_Q_SWA), row),
            pl.BlockSpec((ROW_TILE, W_MOBA), row),
            pl.BlockSpec((W_Q_SWA + W_MOBA, D_MODEL), fixed),
            pl.BlockSpec((1, D_MODEL), fixed),
            pl.BlockSpec((ROW_TILE, D_MODEL), src),
            pl.BlockSpec((1, D_MODEL), fixed),
            pl.BlockSpec((1, D_MODEL), fixed),
            pl.BlockSpec((N_EXPERTS, D_MODEL), fixed),
            pl.BlockSpec((N_EXPERTS, 1), fixed),
            pl.BlockSpec((ROW_TILE, ROW_TILE), fixed),
        ],
        out_specs=[
            pl.BlockSpec((ROW_TILE, D_MODEL), row),
            pl.BlockSpec((ROW_TILE, PACKED), row),
            pl.BlockSpec((TOP_K, ROW_TILE), lambda i: (0, i)),
            pl.BlockSpec((ROW_TILE, TOP_K), row),
            pl.BlockSpec((TOP_K, ROW_TILE), lambda i: (0, i)),
            pl.BlockSpec((N_EXPERTS, 1), fixed),
        ],
        out_shape=[
            jax.ShapeDtypeStruct((n_tok, D_MODEL), jnp.float32),
            jax.ShapeDtypeStruct((n_tok, PACKED), jnp.int32),
            jax.ShapeDtypeStruct((TOP_K, n_tok), jnp.int32),
            jax.ShapeDtypeStruct((n_tok, TOP_K), jnp.float32),
            jax.ShapeDtypeStruct((TOP_K, n_tok), jnp.int32),
            jax.ShapeDtypeStruct((N_EXPERTS, 1), jnp.int32),
        ],
        scratch_shapes=[pltpu.VMEM((N_EXPERTS, 1), jnp.float32)],
        compiler_params=_params("arbitrary"),
        name="outproj_ln_router",
    )(o_a, o_b, w_o, b_out, x2, g, b, w_r, b_r, tri)


def _sc_worker_id():
    return lax.axis_index("s") * SC_CORES + lax.axis_index("c")


def _sc_scatter_rows(rows, pos3, n_out):
    n_tok = pos3.shape[0] * SC_ROWS
    steps = n_tok // SC_ROWS // SC_WORKERS
    assert steps * SC_ROWS * SC_WORKERS == n_tok and steps % 2 == 0
    mesh = plsc.VectorSubcoreMesh(core_axis_name="c", subcore_axis_name="s")

    @functools.partial(
        pl.kernel, mesh=mesh,
        out_type=jax.ShapeDtypeStruct((n_out, PACKED), jnp.int32),
        scratch_types=[pltpu.VMEM((2, TOP_K, SC_ROWS), jnp.int32), pltpu.VMEM((2, SC_ROWS, PACKED), jnp.int32),
                       pltpu.SemaphoreType.DMA((2,)), pltpu.SemaphoreType.DMA((2,))],
        name="sc_dispatch_scatter")
    def scatter(x_hbm, pos_hbm, out_hbm, idx_v, rows_v, sem_ld, sem_st):
        base = _sc_worker_id() * steps

        def loads(s, b):
            return (pltpu.make_async_copy(pos_hbm.at[base + s], idx_v.at[b], sem_ld.at[b]),
                    pltpu.make_async_copy(x_hbm.at[pl.ds((base + s) * SC_ROWS, SC_ROWS)], rows_v.at[b],
                                          sem_ld.at[b]))

        def stores(b):
            return [pltpu.make_async_copy(rows_v.at[b], out_hbm.at[idx_v.at[b, k]], sem_st.at[b])
                    for k in range(TOP_K)]

        for c in loads(0, 0):
            c.start()

        @pl.loop(0, steps, step=2)
        def _(s0):
            for b in range(2):
                s = s0 + b
                for c in loads(s, b):
                    c.wait()

                @pl.when(s >= 1)
                def _():
                    for c in stores(1 - b):
                        c.wait()

                @pl.when(s + 1 < steps)
                def _():
                    for c in loads(s + 1, 1 - b):
                        c.start()

                for c in stores(b):
                    c.start()

        for c in stores(1):
            c.wait()

    return scatter(rows, pos3)


def _sc_gather_rows(table, idx2):
    n_blk = idx2.shape[0]
    steps = n_blk // SC_WORKERS
    assert steps * SC_WORKERS == n_blk and steps % 2 == 0 and idx2.shape[1] == SC_ROWS
    mesh = plsc.VectorSubcoreMesh(core_axis_name="c", subcore_axis_name="s")

    @functools.partial(
        pl.kernel, mesh=mesh,
        out_type=jax.ShapeDtypeStruct((n_blk * SC_ROWS, PACKED), jnp.int32),
        scratch_types=[pltpu.VMEM((steps, SC_ROWS), jnp.int32), pltpu.VMEM((2, SC_ROWS, PACKED), jnp.int32),
                       pltpu.SemaphoreType.DMA((2,)), pltpu.SemaphoreType.DMA((2,))],
        name="sc_combine_gather")
    def gather(y_hbm, idx_hbm, out_hbm, idx_v, rows_v, sem_ld, sem_st):
        base = _sc_worker_id() * steps
        pltpu.sync_copy(idx_hbm.at[pl.ds(base, steps)], idx_v)

        def fetch(s, b):
            return pltpu.make_async_copy(y_hbm.at[idx_v.at[s]], rows_v.at[b], sem_ld.at[b])

        def store(s, b):
            return pltpu.make_async_copy(rows_v.at[b], out_hbm.at[pl.ds((base + s) * SC_ROWS, SC_ROWS)],
                                         sem_st.at[b])

        fetch(0, 0).start()

        @pl.loop(0, steps, step=2)
        def _(s0):
            for b in range(2):
                s = s0 + b
                fetch(s, b).wait()

                @pl.when(s >= 1)
                def _():
                    store(s - 1, 1 - b).wait()

                @pl.when(s + 1 < steps)
                def _():
                    fetch(s + 1, 1 - b).start()

                store(s, b).start()

        store(steps - 1, 1).wait()

    return gather(table, idx2)


def _expert_kernel(be_ref, br_ref, slot_ref, next_ref, nv_ref, x_ref, w1_hbm, perm_ref, bg_ref, bl_ref, w2_hbm,
                   b2_ref, after_ref, y_ref, w1f_ref, w2f_ref, wg_ref, wl_ref, w2b_ref, sem):
    del after_ref
    i = pl.program_id(0)
    live = i < nv_ref[0]
    new_expert = (i == 0) | (be_ref[i] != be_ref[jnp.maximum(i - 1, 0)])

    def weight_copies(expert, slot):
        return (pltpu.make_async_copy(w1_hbm.at[expert], w1f_ref.at[slot], sem.at[0, slot]),
                pltpu.make_async_copy(w2_hbm.at[expert], w2f_ref.at[slot], sem.at[1, slot]))

    @pl.when(live & new_expert)
    def _():
        slot = slot_ref[i]

        @pl.when(i == 0)
        def _():
            for copy in weight_copies(be_ref[i], slot):
                copy.start()

        for copy in weight_copies(be_ref[i], slot):
            copy.wait()

        @pl.when(next_ref[i] >= 0)
        def _():
            for copy in weight_copies(next_ref[i], 1 - slot):
                copy.start()

        for c in range(2 * D_FF // 256):
            t = w1f_ref[slot, :, c * 256:(c + 1) * 256].astype(jnp.bfloat16)
            r = jnp.dot(t, perm_ref[...], preferred_element_type=jnp.float32)
            wg_ref[:, c * LANES:(c + 1) * LANES] = r[:, :LANES].astype(wg_ref.dtype)
            wl_ref[:, c * LANES:(c + 1) * LANES] = r[:, LANES:].astype(wl_ref.dtype)
        w2b_ref[...] = w2f_ref[slot].astype(w2b_ref.dtype)

    def mlp(rows):
        valid = lax.broadcasted_iota(jnp.int32, (rows, PACKED), 0) < br_ref[i]
        lo, hi = _unpack_bf16_pairs(jnp.where(valid, x_ref[0:rows, :], 0))
        xb = jnp.concatenate([lo.astype(jnp.bfloat16), hi.astype(jnp.bfloat16)], axis=1)
        hg = jnp.dot(xb, wg_ref[...], preferred_element_type=jnp.float32) + bg_ref[0]
        hl = jnp.dot(xb, wl_ref[...], preferred_element_type=jnp.float32) + bl_ref[0]
        glu = jnp.minimum(hg, SWIGLU_LIMIT)
        lin = jnp.clip(hl, -SWIGLU_LIMIT, SWIGLU_LIMIT)
        act = glu * jax.nn.sigmoid(SWIGLU_ALPHA * glu) * (lin + 1.0)
        y = jnp.dot(act.astype(jnp.bfloat16), w2b_ref[...], preferred_element_type=jnp.float32) + b2_ref[0]
        y_ref[0:rows, :] = _pack_bf16_pairs(y)

    parts = (br_ref[i] + EXPERT_PART - 1) // EXPERT_PART
    for n_parts in range(1, EXPERT_ROWS // EXPERT_PART + 1):
        @pl.when(live & (parts == n_parts))
        def _(rows=n_parts * EXPERT_PART):
            mlp(rows)
            if rows < EXPERT_ROWS:
                y_ref[rows:, :] = jnp.zeros((EXPERT_ROWS - rows, PACKED), y_ref.dtype)

    @pl.when(jnp.logical_not(live))
    def _():
        y_ref[...] = jnp.zeros(y_ref.shape, y_ref.dtype)


def _experts(tables, xg, w1, perm, b1g, b1l, w2, b2, after):
    n_rows = xg.shape[0]
    n_blocks = n_rows // EXPERT_ROWS

    def row(i, be, br, slot, nxt, nv):
        return (jnp.minimum(i, nv[0] - 1), 0)

    def per_expert(i, be, br, slot, nxt, nv):
        return (be[i], 0, 0)

    grid_spec = pltpu.PrefetchScalarGridSpec(
        num_scalar_prefetch=5,
        grid=(n_blocks,),
        in_specs=[
            pl.BlockSpec((EXPERT_ROWS, PACKED), row),
            pl.BlockSpec(memory_space=pl.ANY),
            pl.BlockSpec((256, 256), lambda i, *_: (0, 0)),
            pl.BlockSpec((1, 1, D_FF), per_expert),
            pl.BlockSpec((1, 1, D_FF), per_expert),
            pl.BlockSpec(memory_space=pl.ANY),
            pl.BlockSpec((1, 1, D_MODEL), per_expert),
            pl.BlockSpec(memory_space=pl.ANY),
        ],
        out_specs=pl.BlockSpec((EXPERT_ROWS, PACKED), lambda i, *_: (i, 0)),
        scratch_shapes=[pltpu.VMEM((2, D_MODEL, 2 * D_FF), jnp.float32), pltpu.VMEM((2, D_FF, D_MODEL), jnp.float32),
                        pltpu.VMEM((D_MODEL, D_FF), jnp.bfloat16), pltpu.VMEM((D_MODEL, D_FF), jnp.bfloat16),
                        pltpu.VMEM((D_FF, D_MODEL), jnp.bfloat16), pltpu.SemaphoreType.DMA((2, 2))],
    )
    return pl.pallas_call(
        _expert_kernel,
        grid_spec=grid_spec,
        out_shape=jax.ShapeDtypeStruct((n_rows, PACKED), jnp.int32),
        compiler_params=pltpu.CompilerParams(dimension_semantics=("arbitrary",),
                                             vmem_limit_bytes=EXPERT_VMEM_LIMIT),
        name="grouped_experts",
    )(*tables, xg, w1, perm, b1g, b1l, w2, b2, after)


def _split_columns_perm():
    i = lax.broadcasted_iota(jnp.int32, (256, 256), 0)
    o = lax.broadcasted_iota(jnp.int32, (256, 256), 1)
    return (i == jnp.where(o < LANES, 2 * o, 2 * (o - LANES) + 1)).astype(jnp.bfloat16)


def _combine_kernel(y_ref, gate_ref, x1_ref, g_ref, b_ref, *rest):
    o_ref = rest[-1]
    gates = gate_ref[...]
    lo_sum = None
    hi_sum = None
    for k in range(TOP_K):
        lo, hi = _unpack_bf16_pairs(y_ref[k])
        gk = gates[:, k:k + 1]
        lo_sum = gk * lo if k == 0 else lo_sum + gk * lo
        hi_sum = gk * hi if k == 0 else hi_sum + gk * hi
    moe = jnp.concatenate([lo_sum, hi_sum], axis=1)
    o_ref[...] = _layer_norm(DEEPNORM_ALPHA * x1_ref[...] + moe, g_ref[...], b_ref[...])


def _combine(yg, gates, x1, g, b, tile0, n_tok, prev):
    row = lambda i: (i, 0)
    fixed = lambda i: (0, 0)
    in_specs = [
        pl.BlockSpec((TOP_K, ROW_TILE, PACKED), lambda i: (0, i, 0)),
        pl.BlockSpec((ROW_TILE, TOP_K), row),
        pl.BlockSpec((ROW_TILE, D_MODEL), row),
        pl.BlockSpec((1, D_MODEL), fixed),
        pl.BlockSpec((1, D_MODEL), fixed),
    ]
    args = [yg, gates, x1, g, b]
    aliases = {}
    if prev is not None:
        in_specs.append(pl.BlockSpec(memory_space=pl.ANY))
        args.append(prev)
        aliases = {len(args) - 1: 0}
    return pl.pallas_call(
        _combine_kernel,
        grid=(yg.shape[1] // ROW_TILE,),
        in_specs=in_specs,
        out_specs=pl.BlockSpec((ROW_TILE, D_MODEL), lambda i: (i + tile0, 0)),
        out_shape=jax.ShapeDtypeStruct((n_tok, D_MODEL), jnp.float32),
        input_output_aliases=aliases,
        compiler_params=_params("parallel"),
        name="combine_ln",
    )(*args)


def _route(top_idx, rank, counts, n_blocks):
    counts = counts[:, 0]
    experts = jnp.arange(N_EXPERTS, dtype=jnp.int32)
    upto = experts[None, :] <= experts[:, None]
    later = experts[None, :] > experts[:, None]
    prefix = lambda v: jnp.sum(jnp.where(upto, v[None, :], 0), axis=1)
    blocks_per = (counts + EXPERT_ROWS - 1) // EXPERT_ROWS
    blk_end = prefix(blocks_per)
    blk_start = blk_end - blocks_per
    first_row = (blk_start * EXPERT_ROWS)[:, None, None]
    pos = rank + jnp.sum(jnp.where(top_idx[None] == experts[:, None, None], first_row, 0), axis=0)
    n_valid = jnp.sum(blocks_per, keepdims=True).astype(jnp.int32)
    has_rows = blocks_per > 0
    run_of = prefix(has_rows.astype(jnp.int32)) - 1
    later = later & has_rows[None, :]
    next_of = jnp.min(jnp.where(later, experts[None, :], N_EXPERTS), axis=1)
    next_of = jnp.where(next_of == N_EXPERTS, -1, next_of)
    blk = jnp.arange(n_blocks, dtype=jnp.int32)
    owner = ((blk[:, None] >= blk_start[None, :]) & (blk[:, None] < blk_end[None, :])).astype(jnp.int32)
    pick = lambda per_expert: jnp.sum(owner * per_expert[None, :], axis=1).astype(jnp.int32)
    block_expert = pick(experts)
    block_rows = jnp.clip(pick(counts) - (blk - pick(blk_start)) * EXPERT_ROWS, 0, EXPERT_ROWS).astype(jnp.int32)
    return pos, (block_expert, block_rows, pick(run_of) % 2, pick(next_of), n_valid)


def _moe_groups(n_tok, seq):
    unit = math.lcm(2 * SC_ROWS * SC_WORKERS, seq)
    parts = sum(MOE_SPLIT)
    if n_tok % (parts * unit):
        return ((0, n_tok),)
    groups, t0 = [], 0
    for share in MOE_SPLIT:
        groups.append((t0, n_tok * share // parts))
        t0 += groups[-1][1]
    return tuple(groups)


def _rope_tables(seq):
    inv_freq = 1.0 / (ROPE_THETA ** (jnp.arange(0, HEAD_DIM, 2, dtype=jnp.float32) / HEAD_DIM))
    ang = jnp.arange(seq, dtype=jnp.float32)[:, None] * inv_freq[None, :]
    cos, sin = jnp.cos(ang), jnp.sin(ang)
    cos_t = jnp.tile(jnp.concatenate([cos, cos], axis=1), (1, 256 // HEAD_DIM))
    sin_t = jnp.tile(jnp.concatenate([-sin, sin], axis=1), (1, 256 // HEAD_DIM))
    return cos_t, sin_t


def kernel(x, w_in, b_in, sinks, w_out, b_out, ln1_g, ln1_b, w_router, b_router, w1, b1, w2, b2, ln2_g, ln2_b):
    batch, seq, d = x.shape
    assert d == D_MODEL and seq % ROW_TILE == 0 and seq % MOBA_BLOCK == 0 and w_in.shape[0] == DEPTH == 1
    assert (seq // WINDOW - 1) % SWA_UNROLL == 0
    n_tok = batch * seq
    bf16 = jnp.bfloat16
    x2 = x.reshape(n_tok, d)
    cos_t, sin_t = _rope_tables(seq)

    proj = _inproj(x2, w_in[0].astype(bf16), b_in[0].reshape(1, IN_WIDTH), cos_t, sin_t, seq)

    w_o = w_out[0].astype(bf16)
    w_r = w_router[0].T.astype(bf16)
    b_r = b_router[0].reshape(N_EXPERTS, 1)
    b_o, g1, be1 = b_out[0].reshape(1, d), ln1_g[0].reshape(1, d), ln1_b[0].reshape(1, d)
    perm = _split_columns_perm()
    b1r = b1[0].reshape(N_EXPERTS, 1, D_FF, 2)
    b1g, b1l = b1r[..., 0], b1r[..., 1]
    b2r = b2[0].reshape(N_EXPERTS, 1, d)
    g2, be2 = ln2_g[0].reshape(1, d), ln2_b[0].reshape(1, d)

    def dispatch(stage):
        t0, tok_g, x1, x1p, gates, pos, tables, n_blocks = stage
        pos3 = pos.reshape(TOP_K, tok_g // SC_ROWS, SC_ROWS).transpose(1, 0, 2)
        return _sc_scatter_rows(x1p, pos3, n_blocks * EXPERT_ROWS)

    def experts_and_gather(stage, xg, y_prev):
        t0, tok_g, x1, x1p, gates, pos, tables, n_blocks = stage
        y = _experts(tables, xg, w1[0], perm, b1g, b1l, w2[0], b2r, y_prev)
        return y, _sc_gather_rows(y, pos.reshape(tok_g * TOP_K // SC_ROWS, SC_ROWS))

    stages, gathered, waiting = [], [], None
    pos = jnp.zeros((TOP_K, SC_ROWS), jnp.int32)
    y_prev = jnp.zeros((SC_ROWS, PACKED), jnp.int32)
    for t0, tok_g in _moe_groups(n_tok, seq):
        o_a = _swa(proj, sinks[0], t0 // seq, tok_g // seq, seq, pos)
        o_b = _moba(proj, t0 // seq, tok_g // seq, seq, pos)
        x1, x1p, top_idx, gates, rank, counts = _outproj(o_a, o_b, w_o, b_o, x2, g1, be1, w_r, b_r, t0, tok_g)
        n_blocks = tok_g * TOP_K // EXPERT_ROWS + N_EXPERTS
        pos, tables = _route(top_idx, rank, counts, n_blocks)
        stages.append((t0, tok_g, x1, x1p, gates, pos, tables, n_blocks))
        if waiting is not None:
            y_prev, yg = experts_and_gather(*waiting, y_prev)
            gathered.append(yg)
        waiting = (stages[-1], dispatch(stages[-1]))
    gathered.append(experts_and_gather(*waiting, y_prev)[1])
    out = None
    for (t0, tok_g, x1, _, gates, _, _, _), yg in zip(stages, gathered):
        out = _combine(yg.reshape(TOP_K, tok_g, PACKED), gates, x1, g2, be2, t0 // ROW_TILE, n_tok, out)
    return out.reshape(batch, seq, d)
```

```python
import functools
import math

import jax
import jax.numpy as jnp
from jax import lax
from jax.experimental import pallas as pl
from jax.experimental.pallas import tpu as pltpu
from jax.experimental.pallas import tpu_sc as plsc

D_MODEL = 1024
HEAD_DIM = 64
N_HEADS_SWA = 8
N_KV_SWA = 2
WINDOW = 128
N_HEADS_MOBA = 8
MOBA_BLOCK = 256
MOBA_TOPK = 3
ROPE_THETA = 10000.0
N_EXPERTS = 32
TOP_K = 4
D_FF = 1024
SWIGLU_LIMIT = 7.0
SWIGLU_ALPHA = 1.702
LN_EPS = 1e-5
DEPTH = 1
DEEPNORM_ALPHA = (2 * DEPTH) ** 0.25

W_Q_SWA = N_HEADS_SWA * HEAD_DIM
W_KV_SWA = N_KV_SWA * HEAD_DIM
W_MOBA = N_HEADS_MOBA * HEAD_DIM
IN_WIDTH = W_Q_SWA + 2 * W_KV_SWA + 3 * W_MOBA
LANES = 128
COL_K_SWA = W_Q_SWA // LANES
COL_V_SWA = COL_K_SWA + 1
COL_Q_MOBA = COL_V_SWA + 1
COL_K_MOBA = COL_Q_MOBA + W_MOBA // LANES
COL_V_MOBA = COL_K_MOBA + W_MOBA // LANES

ROW_TILE = 1024
EXPERT_ROWS = 1024
EXPERT_PART = 256
PACKED = D_MODEL // 2
NEG_BIG = -1e30
LOG2E = 1.4426950408889634
MOBA_LOOKAHEAD = 1
SWA_UNROLL = 3
MOE_SPLIT = (3, 1)
VMEM_LIMIT = 48 * 1024 * 1024
EXPERT_VMEM_LIMIT = 60 * 1024 * 1024

SC_CORES = 2
SC_SUBCORES = 16
SC_WORKERS = SC_CORES * SC_SUBCORES
SC_ROWS = 64

_PROJ_CHUNKS = (
    (0, 256, 256, True), (256, 256, 256, True),
    (512, 256, 128, False),
    (768, 256, 256, True), (1024, 256, 256, True),
    (1280, 256, 256, False), (1536, 256, 256, False),
    (1792, 256, 0, False), (2048, 256, 0, False),
)


def _params(*sem):
    return pltpu.CompilerParams(dimension_semantics=sem, vmem_limit_bytes=VMEM_LIMIT)


def _pack_bf16_pairs(v):
    n = v.shape[1] // 2
    bits = lax.bitcast_convert_type(v.astype(jnp.bfloat16).astype(jnp.float32), jnp.uint32)
    word = (bits[:, :n] >> 16) | (bits[:, n:] & jnp.uint32(0xFFFF0000))
    return lax.bitcast_convert_type(word, jnp.int32)


def _unpack_bf16_pairs(word):
    bits = lax.bitcast_convert_type(word, jnp.uint32)
    lo = lax.bitcast_convert_type(bits << 16, jnp.float32)
    hi = lax.bitcast_convert_type(bits & jnp.uint32(0xFFFF0000), jnp.float32)
    return lo, hi


def _inproj_kernel(x_ref, w_ref, b_ref, cos_ref, sin_ref, o_ref):
    xb = x_ref[...].astype(jnp.bfloat16)
    for start, width, rope, scaled in _PROJ_CHUNKS:
        t = jnp.dot(xb, w_ref[:, start:start + width], preferred_element_type=jnp.float32)
        t = t + b_ref[:, start:start + width]
        if rope:
            lane = lax.broadcasted_iota(jnp.int32, t.shape, 1)
            first_half = (lane % HEAD_DIM) < (HEAD_DIM // 2)
            rot = jnp.where(first_half,
                            pltpu.roll(t, width - HEAD_DIM // 2, 1),
                            pltpu.roll(t, HEAD_DIM // 2, 1))
            roped = t * cos_ref[:, :width] + rot * sin_ref[:, :width]
            t = roped if rope == width else jnp.where(lane < rope, roped, t)
        if scaled:
            t = t * (HEAD_DIM ** -0.5 * LOG2E)
        o_ref[:, start:start + width] = t.astype(o_ref.dtype)


def _inproj(x2, w_in, b_in, cos_t, sin_t, seq):
    n_tok = x2.shape[0]
    per_seq = seq // ROW_TILE
    return pl.pallas_call(
        _inproj_kernel,
        grid=(n_tok // ROW_TILE,),
        in_specs=[
            pl.BlockSpec((ROW_TILE, D_MODEL), lambda i: (i, 0)),
            pl.BlockSpec((D_MODEL, IN_WIDTH), lambda i: (0, 0)),
            pl.BlockSpec((1, IN_WIDTH), lambda i: (0, 0)),
            pl.BlockSpec((ROW_TILE, 256), lambda i: (i % per_seq, 0)),
            pl.BlockSpec((ROW_TILE, 256), lambda i: (i % per_seq, 0)),
        ],
        out_specs=pl.BlockSpec((ROW_TILE, IN_WIDTH), lambda i: (i, 0)),
        out_shape=jax.ShapeDtypeStruct((n_tok, IN_WIDTH), jnp.bfloat16),
        compiler_params=_params("parallel"),
        name="inproj_rope",
    )(x2, w_in, b_in, cos_t, sin_t)


def _swa_kernel(sink_ref, q_ref, k_ref, v_ref, after_ref, o_ref, kd_ref, vd_ref, *, seq):
    del after_ref
    lane = lax.broadcasted_iota(jnp.int32, (seq, LANES), 1)
    low = lane < HEAD_DIM
    k = k_ref[...].astype(jnp.float32)
    kr = pltpu.roll(k, HEAD_DIM, 1)
    kd_ref[0, WINDOW:, :] = jnp.where(low, k, kr).astype(kd_ref.dtype)
    kd_ref[1, WINDOW:, :] = jnp.where(low, kr, k).astype(kd_ref.dtype)
    v = v_ref[...].astype(jnp.float32)
    vd_ref[0, WINDOW:, :] = jnp.where(low, v, 1.0).astype(vd_ref.dtype)
    vd_ref[1, WINDOW:, :] = jnp.where(low, pltpu.roll(v, HEAD_DIM, 1), 1.0).astype(vd_ref.dtype)
    kd_ref[:, :WINDOW, :] = jnp.zeros((N_KV_SWA, WINDOW, LANES), kd_ref.dtype)
    vd_ref[:, :WINDOW, :] = jnp.zeros((N_KV_SWA, WINDOW, LANES), vd_ref.dtype)

    group = N_HEADS_SWA // N_KV_SWA
    rows = group * WINDOW
    r_in = lax.broadcasted_iota(jnp.int32, (rows, 2 * WINDOW), 0) % WINDOW
    c_id = lax.broadcasted_iota(jnp.int32, (rows, 2 * WINDOW), 1)
    band = (c_id > r_in) & (c_id <= r_in + WINDOW)
    head_in_group = lax.broadcasted_iota(jnp.int32, (rows, 1), 0) // WINDOW
    qlane_low = lax.broadcasted_iota(jnp.int32, (WINDOW, LANES), 1) < HEAD_DIM
    sinks = []
    for g in range(N_KV_SWA):
        col = jnp.zeros((rows, 1), jnp.float32)
        for j in range(group):
            col = jnp.where(head_in_group == j, sink_ref[g * group + j] * LOG2E, col)
        sinks.append(col)

    def scores(n, mask):
        r0 = pl.multiple_of(n * WINDOW, WINDOW)
        out = []
        for g in range(N_KV_SWA):
            parts = []
            for c in (2 * g, 2 * g + 1):
                qc = q_ref[pl.ds(r0, WINDOW), c * LANES:(c + 1) * LANES]
                zero = jnp.zeros_like(qc)
                parts.append(jnp.where(qlane_low, qc, zero))
                parts.append(jnp.where(qlane_low, zero, qc))
            qcat = jnp.concatenate(parts, axis=0)
            kd = kd_ref[g, pl.ds(r0, 2 * WINDOW), :]
            s = lax.dot_general(qcat, kd, (((1,), (1,)), ((), ())),
                                preferred_element_type=jnp.float32)
            out.append(jnp.where(mask, s, -jnp.inf))
        return out

    def finish(n, scored):
        r0 = pl.multiple_of(n * WINDOW, WINDOW)
        for g, s in enumerate(scored):
            vd = vd_ref[g, pl.ds(r0, 2 * WINDOW), :]
            m = jnp.maximum(jnp.max(s, axis=-1, keepdims=True), sinks[g])
            p = jnp.exp2(s - m)
            o = jnp.dot(p.astype(vd.dtype), vd, preferred_element_type=jnp.float32)
            sink_term = jnp.exp2(sinks[g] - m)
            for ci, c in enumerate((2 * g, 2 * g + 1)):
                o_lo = o[(2 * ci) * WINDOW:(2 * ci + 1) * WINDOW]
                o_hi = o[(2 * ci + 1) * WINDOW:(2 * ci + 2) * WINDOW]
                e_lo = sink_term[(2 * ci) * WINDOW:(2 * ci + 1) * WINDOW]
                e_hi = sink_term[(2 * ci + 1) * WINDOW:(2 * ci + 2) * WINDOW]
                num = jnp.where(qlane_low, o_lo, pltpu.roll(o_hi, HEAD_DIM, 1))
                den = jnp.where(qlane_low, pltpu.roll(o_lo, HEAD_DIM, 1) + e_lo, o_hi + e_hi)
                o_ref[pl.ds(r0, WINDOW), c * LANES:(c + 1) * LANES] = (num / den).astype(o_ref.dtype)

    finish(0, scores(0, band & (c_id >= WINDOW)))

    def body(it, carry):
        n0 = 1 + it * SWA_UNROLL
        nxt = scores(n0, band)
        for u in range(SWA_UNROLL):
            cur = nxt
            if u + 1 < SWA_UNROLL:
                nxt = scores(n0 + u + 1, band)
            finish(n0 + u, cur)
        return carry

    lax.fori_loop(0, (seq // WINDOW - 1) // SWA_UNROLL, body, 0)


def _swa(proj, sinks, b0, batch, seq, after):
    grid_spec = pltpu.PrefetchScalarGridSpec(
        num_scalar_prefetch=0,
        grid=(batch,),
        in_specs=[
            pl.BlockSpec(memory_space=pltpu.SMEM),
            pl.BlockSpec((seq, W_Q_SWA), lambda b: (b + b0, 0)),
            pl.BlockSpec((seq, LANES), lambda b: (b + b0, COL_K_SWA)),
            pl.BlockSpec((seq, LANES), lambda b: (b + b0, COL_V_SWA)),
            pl.BlockSpec(memory_space=pl.ANY),
        ],
        out_specs=pl.BlockSpec((seq, W_Q_SWA), lambda b: (b, 0)),
        scratch_shapes=[pltpu.VMEM((N_KV_SWA, WINDOW + seq, LANES), jnp.bfloat16),
                        pltpu.VMEM((N_KV_SWA, WINDOW + seq, LANES), jnp.bfloat16)],
    )
    return pl.pallas_call(
        functools.partial(_swa_kernel, seq=seq),
        grid_spec=grid_spec,
        out_shape=jax.ShapeDtypeStruct((batch * seq, W_Q_SWA), jnp.bfloat16),
        compiler_params=_params("parallel"),
        name="swa_sink_attention",
    )(sinks, proj, proj, proj, after)


def _moba_kernel(q_ref, k_ref, v_ref, after_ref, o_ref, qa_ref, ka_ref, va_ref, *, seq):
    del after_ref
    nblk = seq // MOBA_BLOCK
    pad_rows = 16
    q_all, k_all, v_all = q_ref[...], k_ref[...], v_ref[...]
    kmean = jnp.sum(k_all.astype(jnp.float32).reshape(nblk, MOBA_BLOCK, LANES), axis=1) / MOBA_BLOCK
    kmean = jnp.concatenate([kmean, jnp.zeros((pad_rows - nblk, LANES), jnp.float32)], axis=0)
    klane_low = lax.broadcasted_iota(jnp.int32, (pad_rows, LANES), 1) < HEAD_DIM
    lane = lax.broadcasted_iota(jnp.int32, (seq, LANES), 1)
    low = lane < HEAD_DIM
    key_blk = lax.broadcasted_iota(jnp.int32, (seq, LANES), 0) // MOBA_BLOCK
    j_id = lax.broadcasted_iota(jnp.int32, (pad_rows, seq), 0)
    q_blk = lax.broadcasted_iota(jnp.int32, (pad_rows, seq), 1) // MOBA_BLOCK
    eligible = j_id < q_blk

    for half in range(2):
        own = low if half == 0 else ~low
        spare = HEAD_DIM if half == 0 else 0
        ka_ref[half] = jnp.where(own, k_all, (lane - spare == key_blk).astype(k_all.dtype))
        va_ref[half] = jnp.where(own, v_all, jnp.ones_like(v_all))
        km = jnp.where(klane_low if half == 0 else ~klane_low, kmean, 0.0).astype(jnp.bfloat16)
        gate = lax.dot_general(km, q_all, (((1,), (1,)), ((), ())),
                               preferred_element_type=jnp.float32)
        gate = jnp.where(eligible, gate, -jnp.inf)
        beaten = jnp.zeros((pad_rows, seq), jnp.int32)
        for jp in range(nblk):
            row = gate[jp:jp + 1, :]
            wins = (row > gate) | ((row == gate) & (jp < j_id))
            beaten = beaten + wins.astype(jnp.int32)
        dropped = eligible & (beaten >= MOBA_TOPK)
        bias = jnp.where(dropped, NEG_BIG, 0.0)
        pieces = [bias, jnp.zeros((LANES - spare - pad_rows, seq), jnp.float32)]
        if spare:
            pieces.insert(0, jnp.zeros((spare, seq), jnp.float32))
        bias_t = jnp.concatenate(pieces, axis=0).T
        qa_ref[half] = jnp.where(own, q_all, bias_t.astype(q_all.dtype))

    qlane_low = lax.broadcasted_iota(jnp.int32, (MOBA_BLOCK, LANES), 1) < HEAD_DIM
    rr = lax.broadcasted_iota(jnp.int32, (MOBA_BLOCK, MOBA_BLOCK), 0)
    cc = lax.broadcasted_iota(jnp.int32, (MOBA_BLOCK, MOBA_BLOCK), 1)
    causal = cc <= rr

    def scores(i, half):
        r0 = i * MOBA_BLOCK
        n_keys = r0 + MOBA_BLOCK
        s = lax.dot_general(qa_ref[half, r0:n_keys, :], ka_ref[half, 0:n_keys, :], (((1,), (1,)), ((), ())),
                            preferred_element_type=jnp.float32)
        own_blk = jnp.where(causal, s[:, r0:n_keys], NEG_BIG)
        return jnp.concatenate([s[:, :r0], own_blk], axis=1) if i else own_blk

    units = [(i, half) for i in range(nblk) for half in range(2)]
    pending = [scores(*u) for u in units[:MOBA_LOOKAHEAD]]
    acc = []
    for n, (i, half) in enumerate(units):
        s = pending.pop(0)
        if n + MOBA_LOOKAHEAD < len(units):
            pending.append(scores(*units[n + MOBA_LOOKAHEAD]))
        n_keys = (i + 1) * MOBA_BLOCK
        m = jnp.max(s, axis=-1, keepdims=True)
        p = jnp.exp2(s - m).astype(jnp.bfloat16)
        acc.append(jnp.dot(p, va_ref[half, 0:n_keys, :], preferred_element_type=jnp.float32))
        if half == 1:
            num = jnp.where(qlane_low, acc[0], acc[1])
            den = pltpu.roll(jnp.where(qlane_low, acc[1], acc[0]), HEAD_DIM, 1)
            o_ref[i * MOBA_BLOCK:n_keys, :] = (num / den).astype(o_ref.dtype)
            acc = []


def _moba(proj, b0, batch, seq, after):
    pairs = W_MOBA // LANES
    return pl.pallas_call(
        functools.partial(_moba_kernel, seq=seq),
        grid=(batch, pairs),
        in_specs=[
            pl.BlockSpec((seq, LANES), lambda b, p: (b + b0, COL_Q_MOBA + p)),
            pl.BlockSpec((seq, LANES), lambda b, p: (b + b0, COL_K_MOBA + p)),
            pl.BlockSpec((seq, LANES), lambda b, p: (b + b0, COL_V_MOBA + p)),
            pl.BlockSpec(memory_space=pl.ANY),
        ],
        out_specs=pl.BlockSpec((seq, LANES), lambda b, p: (b, p)),
        out_shape=jax.ShapeDtypeStruct((batch * seq, W_MOBA), jnp.bfloat16),
        scratch_shapes=[pltpu.VMEM((2, seq, LANES), jnp.bfloat16)] * 3,
        compiler_params=_params("parallel", "parallel"),
        name="moba_attention",
    )(proj, proj, proj, after)


def _layer_norm(h, g, b):
    mu = jnp.mean(h, axis=-1, keepdims=True)
    d = h - mu
    var = jnp.mean(d * d, axis=-1, keepdims=True)
    return d * lax.rsqrt(var + LN_EPS) * g + b


def _outproj_kernel(oa_ref, ob_ref, wo_ref, bo_ref, x_ref, g_ref, b_ref, wr_ref, br_ref, tri_ref,
                    x1_ref, x1p_ref, idx_ref, gate_ref, rank_ref, count_ref, running_ref):
    @pl.when(pl.program_id(0) == 0)
    def _():
        running_ref[...] = jnp.zeros(running_ref.shape, running_ref.dtype)

    half_rows = ROW_TILE // 2
    halves = [slice(h * half_rows, (h + 1) * half_rows) for h in range(2)]

    def project(rows):
        heads = jnp.concatenate([oa_ref[rows, :], ob_ref[rows, :]], axis=1)
        return jnp.dot(heads, wo_ref[...], preferred_element_type=jnp.float32) + bo_ref[...]

    def normalise(rows, mix):
        x1 = _layer_norm(DEEPNORM_ALPHA * x_ref[rows, :] + mix, g_ref[...], b_ref[...])
        x1_ref[rows, :] = x1
        x1p_ref[rows, :] = _pack_bf16_pairs(x1)
        return x1.astype(jnp.bfloat16)

    mixes = [project(rows) for rows in halves]
    x1b = jnp.concatenate([normalise(rows, mix) for rows, mix in zip(halves, mixes)], axis=0)

    logits = lax.dot_general(wr_ref[...], x1b, (((1,), (1,)), ((), ())),
                             preferred_element_type=jnp.float32) + br_ref[...]
    expert = lax.broadcasted_iota(jnp.int32, logits.shape, 0)
    idx_rows, val_rows, onehots = [], [], []
    top = None
    total = None
    for k in range(TOP_K):
        m = jnp.max(logits, axis=0, keepdims=True)
        idx = jnp.min(jnp.where(logits == m, expert, N_EXPERTS), axis=0, keepdims=True)
        picked = expert == idx
        onehots.append(picked)
        logits = jnp.where(picked, -jnp.inf, logits)
        if k == 0:
            top = m
        e = jnp.exp(m - top)
        total = e if k == 0 else total + e
        idx_rows.append(idx)
        val_rows.append(e)
    idx_ref[...] = jnp.concatenate(idx_rows, axis=0)
    gates_t = jnp.concatenate(val_rows, axis=0) / total
    gates_t = jnp.concatenate([gates_t, jnp.zeros((LANES - TOP_K, gates_t.shape[1]), jnp.float32)], axis=0)
    gate_ref[...] = gates_t.T[:, :TOP_K]

    picks = jnp.concatenate([p.astype(jnp.bfloat16) for p in onehots], axis=0)
    before = jnp.dot(picks, tri_ref[...], preferred_element_type=jnp.float32)
    base = running_ref[...]
    rank_rows = []
    for k in range(TOP_K):
        pk = onehots[k].astype(jnp.float32)
        here = before[k * N_EXPERTS:(k + 1) * N_EXPERTS, :] + base
        rank_rows.append(jnp.sum(pk * here, axis=0, keepdims=True))
        base = base + jnp.sum(pk, axis=1, keepdims=True)
    running_ref[...] = base
    rank_ref[...] = jnp.concatenate(rank_rows, axis=0).astype(jnp.int32)
    count_ref[...] = base.astype(jnp.int32)


def _outproj(o_a, o_b, w_o, b_out, x2, g, b, w_r, b_r, t0, n_tok):
    tile0 = t0 // ROW_TILE
    src = lambda i: (i + tile0, 0)
    row = lambda i: (i, 0)
    fixed = lambda i: (0, 0)
    r = lax.broadcasted_iota(jnp.int32, (ROW_TILE, ROW_TILE), 0)
    c = lax.broadcasted_iota(jnp.int32, (ROW_TILE, ROW_TILE), 1)
    tri = (r < c).astype(jnp.bfloat16)
    return pl.pallas_call(
        _outproj_kernel,
        grid=(n_tok // ROW_TILE,),
        in_specs=[
            pl.BlockSpec((ROW_TILE, W_Q_SWA), row),
            pl.BlockSpec((ROW_TILE, W_MOBA), row),
            pl.BlockSpec((W_Q_SWA + W_MOBA, D_MODEL), fixed),
            pl.BlockSpec((1, D_MODEL), fixed),
            pl.BlockSpec((ROW_TILE, D_MODEL), src),
            pl.BlockSpec((1, D_MODEL), fixed),
            pl.BlockSpec((1, D_MODEL), fixed),
            pl.BlockSpec((N_EXPERTS, D_MODEL), fixed),
            pl.BlockSpec((N_EXPERTS, 1), fixed),
            pl.BlockSpec((ROW_TILE, ROW_TILE), fixed),
        ],
        out_specs=[
            pl.BlockSpec((ROW_TILE, D_MODEL), row),
            pl.BlockSpec((ROW_TILE, PACKED), row),
            pl.BlockSpec((TOP_K, ROW_TILE), lambda i: (0, i)),
            pl.BlockSpec((ROW_TILE, TOP_K), row),
            pl.BlockSpec((TOP_K, ROW_TILE), lambda i: (0, i)),
            pl.BlockSpec((N_EXPERTS, 1), fixed),
        ],
        out_shape=[
            jax.ShapeDtypeStruct((n_tok, D_MODEL), jnp.float32),
            jax.ShapeDtypeStruct((n_tok, PACKED), jnp.int32),
            jax.ShapeDtypeStruct((TOP_K, n_tok), jnp.int32),
            jax.ShapeDtypeStruct((n_tok, TOP_K), jnp.float32),
            jax.ShapeDtypeStruct((TOP_K, n_tok), jnp.int32),
            jax.ShapeDtypeStruct((N_EXPERTS, 1), jnp.int32),
        ],
        scratch_shapes=[pltpu.VMEM((N_EXPERTS, 1), jnp.float32)],
        compiler_params=_params("arbitrary"),
        name="outproj_ln_router",
    )(o_a, o_b, w_o, b_out, x2, g, b, w_r, b_r, tri)


def _sc_worker_id():
    return lax.axis_index("s") * SC_CORES + lax.axis_index("c")


def _sc_scatter_rows(rows, pos3, n_out):
    n_tok = pos3.shape[0] * SC_ROWS
    steps = n_tok // SC_ROWS // SC_WORKERS
    assert steps * SC_ROWS * SC_WORKERS == n_tok and steps % 2 == 0
    mesh = plsc.VectorSubcoreMesh(core_axis_name="c", subcore_axis_name="s")

    @functools.partial(
        pl.kernel, mesh=mesh,
        out_type=jax.ShapeDtypeStruct((n_out, PACKED), jnp.int32),
        scratch_types=[pltpu.VMEM((2, TOP_K, SC_ROWS), jnp.int32), pltpu.VMEM((2, SC_ROWS, PACKED), jnp.int32),
                       pltpu.SemaphoreType.DMA((2,)), pltpu.SemaphoreType.DMA((2,))],
        name="sc_dispatch_scatter")
    def scatter(x_hbm, pos_hbm, out_hbm, idx_v, rows_v, sem_ld, sem_st):
        base = _sc_worker_id() * steps

        def loads(s, b):
            return (pltpu.make_async_copy(pos_hbm.at[base + s], idx_v.at[b], sem_ld.at[b]),
                    pltpu.make_async_copy(x_hbm.at[pl.ds((base + s) * SC_ROWS, SC_ROWS)], rows_v.at[b],
                                          sem_ld.at[b]))

        def stores(b):
            return [pltpu.make_async_copy(rows_v.at[b], out_hbm.at[idx_v.at[b, k]], sem_st.at[b])
                    for k in range(TOP_K)]

        for c in loads(0, 0):
            c.start()

        @pl.loop(0, steps, step=2)
        def _(s0):
            for b in range(2):
                s = s0 + b
                for c in loads(s, b):
                    c.wait()

                @pl.when(s >= 1)
                def _():
                    for c in stores(1 - b):
                        c.wait()

                @pl.when(s + 1 < steps)
                def _():
                    for c in loads(s + 1, 1 - b):
                        c.start()

                for c in stores(b):
                    c.start()

        for c in stores(1):
            c.wait()

    return scatter(rows, pos3)


def _sc_gather_rows(table, idx2):
    n_blk = idx2.shape[0]
    steps = n_blk // SC_WORKERS
    assert steps * SC_WORKERS == n_blk and steps % 2 == 0 and idx2.shape[1] == SC_ROWS
    mesh = plsc.VectorSubcoreMesh(core_axis_name="c", subcore_axis_name="s")

    @functools.partial(
        pl.kernel, mesh=mesh,
        out_type=jax.ShapeDtypeStruct((n_blk * SC_ROWS, PACKED), jnp.int32),
        scratch_types=[pltpu.VMEM((steps, SC_ROWS), jnp.int32), pltpu.VMEM((2, SC_ROWS, PACKED), jnp.int32),
                       pltpu.SemaphoreType.DMA((2,)), pltpu.SemaphoreType.DMA((2,))],
        name="sc_combine_gather")
    def gather(y_hbm, idx_hbm, out_hbm, idx_v, rows_v, sem_ld, sem_st):
        base = _sc_worker_id() * steps
        pltpu.sync_copy(idx_hbm.at[pl.ds(base, steps)], idx_v)

        def fetch(s, b):
            return pltpu.make_async_copy(y_hbm.at[idx_v.at[s]], rows_v.at[b], sem_ld.at[b])

        def store(s, b):
            return pltpu.make_async_copy(rows_v.at[b], out_hbm.at[pl.ds((base + s) * SC_ROWS, SC_ROWS)],
                                         sem_st.at[b])

        fetch(0, 0).start()

        @pl.loop(0, steps, step=2)
        def _(s0):
            for b in range(2):
                s = s0 + b
                fetch(s, b).wait()

                @pl.when(s >= 1)
                def _():
                    store(s - 1, 1 - b).wait()

                @pl.when(s + 1 < steps)
                def _():
                    fetch(s + 1, 1 - b).start()

                store(s, b).start()

        store(steps - 1, 1).wait()

    return gather(table, idx2)


def _expert_blocks(i, live, br_ref, x_ref, wg, wl, w2b, bg_ref, bl_ref, b2_ref, y_ref):
    def mlp(rows):
        valid = lax.broadcasted_iota(jnp.int32, (rows, PACKED), 0) < br_ref[i]
        lo, hi = _unpack_bf16_pairs(jnp.where(valid, x_ref[0:rows, :], 0))
        xb = jnp.concatenate([lo.astype(jnp.bfloat16), hi.astype(jnp.bfloat16)], axis=1)
        hg = jnp.dot(xb, wg[...], preferred_element_type=jnp.float32) + bg_ref[0]
        hl = jnp.dot(xb, wl[...], preferred_element_type=jnp.float32) + bl_ref[0]
        glu = jnp.minimum(hg, SWIGLU_LIMIT)
        lin = jnp.clip(hl, -SWIGLU_LIMIT, SWIGLU_LIMIT)
        act = glu * jax.nn.sigmoid(SWIGLU_ALPHA * glu) * (lin + 1.0)
        y = jnp.dot(act.astype(jnp.bfloat16), w2b[...], preferred_element_type=jnp.float32) + b2_ref[0]
        y_ref[0:rows, :] = _pack_bf16_pairs(y)

    parts = (br_ref[i] + EXPERT_PART - 1) // EXPERT_PART
    for n_parts in range(EXPERT_ROWS // EXPERT_PART + 1):
        @pl.when(live & (parts == n_parts))
        def _(rows=n_parts * EXPERT_PART):
            if rows:
                mlp(rows)
            if rows < EXPERT_ROWS:
                y_ref[rows:, :] = jnp.zeros((EXPERT_ROWS - rows, PACKED), y_ref.dtype)

    @pl.when(jnp.logical_not(live))
    def _():
        y_ref[...] = jnp.zeros(y_ref.shape, y_ref.dtype)


def _expert_prep_kernel(be_ref, br_ref, slot_ref, next_ref, nv_ref, x_ref, w1_hbm, perm_ref, bg_ref, bl_ref, w2_hbm,
                        b2_ref, after_ref, y_ref, wg_ref, wl_ref, w2b_ref, w1f_ref, w2f_ref, sem):
    del after_ref
    i = pl.program_id(0)
    live = i < nv_ref[0]
    new_expert = (i == 0) | (be_ref[i] != be_ref[jnp.maximum(i - 1, 0)])

    def weight_copies(expert, slot):
        return (pltpu.make_async_copy(w1_hbm.at[expert], w1f_ref.at[slot], sem.at[0, slot]),
                pltpu.make_async_copy(w2_hbm.at[expert], w2f_ref.at[slot], sem.at[1, slot]))

    @pl.when(live & new_expert)
    def _():
        slot = slot_ref[i]

        @pl.when(i == 0)
        def _():
            for copy in weight_copies(be_ref[i], slot):
                copy.start()

        for copy in weight_copies(be_ref[i], slot):
            copy.wait()

        @pl.when(next_ref[i] >= 0)
        def _():
            for copy in weight_copies(next_ref[i], 1 - slot):
                copy.start()

        for c in range(2 * D_FF // 256):
            t = w1f_ref[slot, :, c * 256:(c + 1) * 256].astype(jnp.bfloat16)
            r = jnp.dot(t, perm_ref[...], preferred_element_type=jnp.float32)
            wg_ref[0, :, c * LANES:(c + 1) * LANES] = r[:, :LANES].astype(wg_ref.dtype)
            wl_ref[0, :, c * LANES:(c + 1) * LANES] = r[:, LANES:].astype(wl_ref.dtype)
        w2b_ref[0] = w2f_ref[slot].astype(w2b_ref.dtype)

    _expert_blocks(i, live, br_ref, x_ref, wg_ref.at[0], wl_ref.at[0], w2b_ref.at[0], bg_ref, bl_ref, b2_ref, y_ref)


def _expert_cached_kernel(be_ref, br_ref, slot_ref, next_ref, nv_ref, x_ref, wg_ref, wl_ref, w2b_ref, bg_ref, bl_ref,
                          b2_ref, after_ref, y_ref):
    del slot_ref, next_ref, after_ref
    i = pl.program_id(0)
    _expert_blocks(i, i < nv_ref[0], br_ref, x_ref, wg_ref.at[0], wl_ref.at[0], w2b_ref.at[0], bg_ref, bl_ref,
                   b2_ref, y_ref)


def _experts(tables, xg, w1, perm, b1g, b1l, w2, b2, after, prepared):
    n_rows = xg.shape[0]
    n_blocks = n_rows // EXPERT_ROWS

    def row(i, be, br, slot, nxt, nv):
        return (jnp.minimum(i, nv[0] - 1), 0)

    def per_expert(i, be, br, slot, nxt, nv):
        return (be[i], 0, 0)

    x_spec = pl.BlockSpec((EXPERT_ROWS, PACKED), row)
    y_spec = pl.BlockSpec((EXPERT_ROWS, PACKED), lambda i, *_: (i, 0))
    y_shape = jax.ShapeDtypeStruct((n_rows, PACKED), jnp.int32)
    bias_specs = [pl.BlockSpec((1, 1, D_FF), per_expert), pl.BlockSpec((1, 1, D_FF), per_expert)]
    b2_spec = pl.BlockSpec((1, 1, D_MODEL), per_expert)
    weight_specs = [pl.BlockSpec((1, D_MODEL, D_FF), per_expert), pl.BlockSpec((1, D_MODEL, D_FF), per_expert),
                    pl.BlockSpec((1, D_FF, D_MODEL), per_expert)]
    params = pltpu.CompilerParams(dimension_semantics=("arbitrary",), vmem_limit_bytes=EXPERT_VMEM_LIMIT)
    any_spec = pl.BlockSpec(memory_space=pl.ANY)
    if prepared is None:
        grid_spec = pltpu.PrefetchScalarGridSpec(
            num_scalar_prefetch=5,
            grid=(n_blocks,),
            in_specs=[x_spec, any_spec, pl.BlockSpec((256, 256), lambda i, *_: (0, 0)), *bias_specs, any_spec,
                      b2_spec, any_spec],
            out_specs=[y_spec, *weight_specs],
            scratch_shapes=[pltpu.VMEM((2, D_MODEL, 2 * D_FF), jnp.float32),
                            pltpu.VMEM((2, D_FF, D_MODEL), jnp.float32), pltpu.SemaphoreType.DMA((2, 2))],
        )
        y, *prepared = pl.pallas_call(
            _expert_prep_kernel,
            grid_spec=grid_spec,
            out_shape=[y_shape,
                       jax.ShapeDtypeStruct((N_EXPERTS, D_MODEL, D_FF), jnp.bfloat16),
                       jax.ShapeDtypeStruct((N_EXPERTS, D_MODEL, D_FF), jnp.bfloat16),
                       jax.ShapeDtypeStruct((N_EXPERTS, D_FF, D_MODEL), jnp.bfloat16)],
            compiler_params=params,
            name="grouped_experts_prep",
        )(*tables, xg, w1, perm, b1g, b1l, w2, b2, after)
        return y, tuple(prepared)
    grid_spec = pltpu.PrefetchScalarGridSpec(
        num_scalar_prefetch=5,
        grid=(n_blocks,),
        in_specs=[x_spec, *weight_specs, *bias_specs, b2_spec, any_spec],
        out_specs=y_spec,
    )
    y = pl.pallas_call(
        _expert_cached_kernel,
        grid_spec=grid_spec,
        out_shape=y_shape,
        compiler_params=params,
        name="grouped_experts",
    )(*tables, xg, *prepared, b1g, b1l, b2, after)
    return y, prepared


def _split_columns_perm():
    i = lax.broadcasted_iota(jnp.int32, (256, 256), 0)
    o = lax.broadcasted_iota(jnp.int32, (256, 256), 1)
    return (i == jnp.where(o < LANES, 2 * o, 2 * (o - LANES) + 1)).astype(jnp.bfloat16)


def _combine_kernel(y_ref, gate_ref, x1_ref, g_ref, b_ref, *rest):
    o_ref = rest[-1]
    gates = gate_ref[...]
    lo_sum = None
    hi_sum = None
    for k in range(TOP_K):
        lo, hi = _unpack_bf16_pairs(y_ref[k])
        gk = gates[:, k:k + 1]
        lo_sum = gk * lo if k == 0 else lo_sum + gk * lo
        hi_sum = gk * hi if k == 0 else hi_sum + gk * hi
    moe = jnp.concatenate([lo_sum, hi_sum], axis=1)
    o_ref[...] = _layer_norm(DEEPNORM_ALPHA * x1_ref[...] + moe, g_ref[...], b_ref[...])


def _combine(yg, gates, x1, g, b, tile0, n_tok, prev):
    row = lambda i: (i, 0)
    fixed = lambda i: (0, 0)
    in_specs = [
        pl.BlockSpec((TOP_K, ROW_TILE, PACKED), lambda i: (0, i, 0)),
        pl.BlockSpec((ROW_TILE, TOP_K), row),
        pl.BlockSpec((ROW_TILE, D_MODEL), row),
        pl.BlockSpec((1, D_MODEL), fixed),
        pl.BlockSpec((1, D_MODEL), fixed),
    ]
    args = [yg, gates, x1, g, b]
    aliases = {}
    if prev is not None:
        in_specs.append(pl.BlockSpec(memory_space=pl.ANY))
        args.append(prev)
        aliases = {len(args) - 1: 0}
    return pl.pallas_call(
        _combine_kernel,
        grid=(yg.shape[1] // ROW_TILE,),
        in_specs=in_specs,
        out_specs=pl.BlockSpec((ROW_TILE, D_MODEL), lambda i: (i + tile0, 0)),
        out_shape=jax.ShapeDtypeStruct((n_tok, D_MODEL), jnp.float32),
        input_output_aliases=aliases,
        compiler_params=_params("parallel"),
        name="combine_ln",
    )(*args)


def _route(top_idx, rank, counts, n_blocks, every_expert):
    counts = counts[:, 0]
    experts = jnp.arange(N_EXPERTS, dtype=jnp.int32)
    upto = experts[None, :] <= experts[:, None]
    later = experts[None, :] > experts[:, None]
    prefix = lambda v: jnp.sum(jnp.where(upto, v[None, :], 0), axis=1)
    blocks_per = (counts + EXPERT_ROWS - 1) // EXPERT_ROWS
    if every_expert:
        blocks_per = jnp.maximum(blocks_per, 1)
    blk_end = prefix(blocks_per)
    blk_start = blk_end - blocks_per
    first_row = (blk_start * EXPERT_ROWS)[:, None, None]
    pos = rank + jnp.sum(jnp.where(top_idx[None] == experts[:, None, None], first_row, 0), axis=0)
    n_valid = jnp.sum(blocks_per, keepdims=True).astype(jnp.int32)
    has_rows = blocks_per > 0
    run_of = prefix(has_rows.astype(jnp.int32)) - 1
    later = later & has_rows[None, :]
    next_of = jnp.min(jnp.where(later, experts[None, :], N_EXPERTS), axis=1)
    next_of = jnp.where(next_of == N_EXPERTS, -1, next_of)
    blk = jnp.arange(n_blocks, dtype=jnp.int32)
    owner = ((blk[:, None] >= blk_start[None, :]) & (blk[:, None] < blk_end[None, :])).astype(jnp.int32)
    pick = lambda per_expert: jnp.sum(owner * per_expert[None, :], axis=1).astype(jnp.int32)
    block_expert = jnp.where(blk < n_valid[0], pick(experts), jnp.max(jnp.where(has_rows, experts, 0)))
    block_rows = jnp.clip(pick(counts) - (blk - pick(blk_start)) * EXPERT_ROWS, 0, EXPERT_ROWS).astype(jnp.int32)
    return pos, (block_expert, block_rows, pick(run_of) % 2, pick(next_of), n_valid)


def _moe_groups(n_tok, seq):
    unit = math.lcm(2 * SC_ROWS * SC_WORKERS, seq)
    parts = sum(MOE_SPLIT)
    if n_tok % (parts * unit):
        return ((0, n_tok),)
    groups, t0 = [], 0
    for share in MOE_SPLIT:
        groups.append((t0, n_tok * share // parts))
        t0 += groups[-1][1]
    return tuple(groups)


def _rope_tables(seq):
    inv_freq = 1.0 / (ROPE_THETA ** (jnp.arange(0, HEAD_DIM, 2, dtype=jnp.float32) / HEAD_DIM))
    ang = jnp.arange(seq, dtype=jnp.float32)[:, None] * inv_freq[None, :]
    cos, sin = jnp.cos(ang), jnp.sin(ang)
    cos_t = jnp.tile(jnp.concatenate([cos, cos], axis=1), (1, 256 // HEAD_DIM))
    sin_t = jnp.tile(jnp.concatenate([-sin, sin], axis=1), (1, 256 // HEAD_DIM))
    return cos_t, sin_t


def kernel(x, w_in, b_in, sinks, w_out, b_out, ln1_g, ln1_b, w_router, b_router, w1, b1, w2, b2, ln2_g, ln2_b):
    batch, seq, d = x.shape
    assert d == D_MODEL and seq % ROW_TILE == 0 and seq % MOBA_BLOCK == 0 and w_in.shape[0] == DEPTH == 1
    assert (seq // WINDOW - 1) % SWA_UNROLL == 0
    n_tok = batch * seq
    bf16 = jnp.bfloat16
    x2 = x.reshape(n_tok, d)
    cos_t, sin_t = _rope_tables(seq)

    proj = _inproj(x2, w_in[0].astype(bf16), b_in[0].reshape(1, IN_WIDTH), cos_t, sin_t, seq)

    w_o = w_out[0].astype(bf16)
    w_r = w_router[0].T.astype(bf16)
    b_r = b_router[0].reshape(N_EXPERTS, 1)
    b_o, g1, be1 = b_out[0].reshape(1, d), ln1_g[0].reshape(1, d), ln1_b[0].reshape(1, d)
    perm = _split_columns_perm()
    b1r = b1[0].reshape(N_EXPERTS, 1, D_FF, 2)
    b1g, b1l = b1r[..., 0], b1r[..., 1]
    b2r = b2[0].reshape(N_EXPERTS, 1, d)
    g2, be2 = ln2_g[0].reshape(1, d), ln2_b[0].reshape(1, d)

    def dispatch(stage):
        t0, tok_g, x1, x1p, gates, pos, tables, n_blocks = stage
        pos3 = pos.reshape(TOP_K, tok_g // SC_ROWS, SC_ROWS).transpose(1, 0, 2)
        return _sc_scatter_rows(x1p, pos3, n_blocks * EXPERT_ROWS)

    def experts_and_gather(stage, xg, y_prev, prepared):
        t0, tok_g, x1, x1p, gates, pos, tables, n_blocks = stage
        y, prepared = _experts(tables, xg, w1[0], perm, b1g, b1l, w2[0], b2r, y_prev, prepared)
        return y, prepared, _sc_gather_rows(y, pos.reshape(tok_g * TOP_K // SC_ROWS, SC_ROWS))

    stages, gathered, waiting = [], [], None
    pos = jnp.zeros((TOP_K, SC_ROWS), jnp.int32)
    y_prev = jnp.zeros((SC_ROWS, PACKED), jnp.int32)
    prepared = None
    groups = _moe_groups(n_tok, seq)
    for t0, tok_g in groups:
        o_a = _swa(proj, sinks[0], t0 // seq, tok_g // seq, seq, pos)
        o_b = _moba(proj, t0 // seq, tok_g // seq, seq, pos)
        x1, x1p, top_idx, gates, rank, counts = _outproj(o_a, o_b, w_o, b_o, x2, g1, be1, w_r, b_r, t0, tok_g)
        n_blocks = tok_g * TOP_K // EXPERT_ROWS + N_EXPERTS
        pos, tables = _route(top_idx, rank, counts, n_blocks, every_expert=(t0 == 0 and len(groups) > 1))
        stages.append((t0, tok_g, x1, x1p, gates, pos, tables, n_blocks))
        if waiting is not None:
            y_prev, prepared, yg = experts_and_gather(*waiting, y_prev, prepared)
            gathered.append(yg)
        waiting = (stages[-1], dispatch(stages[-1]))
    gathered.append(experts_and_gather(*waiting, y_prev, prepared)[2])
    out = None
    for (t0, tok_g, x1, _, gates, _, _, _), yg in zip(stages, gathered):
        out = _combine(yg.reshape(TOP_K, tok_g, PACKED), gates, x1, g2, be2, t0 // ROW_TILE, n_tok, out)
    return out.reshape(batch, seq, d)
```

```python
import functools
import math

import jax
import jax.numpy as jnp
from jax import lax
from jax.experimental import pallas as pl
from jax.experimental.pallas import tpu as pltpu
from jax.experimental.pallas import tpu_sc as plsc

D_MODEL = 1024
HEAD_DIM = 64
N_HEADS_SWA = 8
N_KV_SWA = 2
WINDOW = 128
N_HEADS_MOBA = 8
MOBA_BLOCK = 256
MOBA_TOPK = 3
ROPE_THETA = 10000.0
N_EXPERTS = 32
TOP_K = 4
D_FF = 1024
SWIGLU_LIMIT = 7.0
SWIGLU_ALPHA = 1.702
LN_EPS = 1e-5
DEPTH = 1
DEEPNORM_ALPHA = (2 * DEPTH) ** 0.25

W_Q_SWA = N_HEADS_SWA * HEAD_DIM
W_KV_SWA = N_KV_SWA * HEAD_DIM
W_MOBA = N_HEADS_MOBA * HEAD_DIM
IN_WIDTH = W_Q_SWA + 2 * W_KV_SWA + 3 * W_MOBA
LANES = 128
COL_K_SWA = W_Q_SWA // LANES
COL_V_SWA = COL_K_SWA + 1
COL_Q_MOBA = COL_V_SWA + 1
COL_K_MOBA = COL_Q_MOBA + W_MOBA // LANES
COL_V_MOBA = COL_K_MOBA + W_MOBA // LANES

ROW_TILE = 1024
EXPERT_ROWS = 1024
EXPERT_PART = 256
PACKED = D_MODEL // 2
NEG_BIG = -1e30
LOG2E = 1.4426950408889634
MOBA_LOOKAHEAD = 1
SWA_UNROLL = 3
MOE_SPLIT = (3, 1)
VMEM_LIMIT = 48 * 1024 * 1024
EXPERT_VMEM_LIMIT = 60 * 1024 * 1024

SC_CORES = 2
SC_SUBCORES = 16
SC_WORKERS = SC_CORES * SC_SUBCORES
SC_ROWS = 64

_PROJ_CHUNKS = (
    (0, 256, 256, True), (256, 256, 256, True),
    (512, 256, 128, False),
    (768, 256, 256, True), (1024, 256, 256, True),
    (1280, 256, 256, False), (1536, 256, 256, False),
    (1792, 256, 0, False), (2048, 256, 0, False),
)


def _params(*sem):
    return pltpu.CompilerParams(dimension_semantics=sem, vmem_limit_bytes=VMEM_LIMIT)


def _pack_bf16_pairs(v):
    n = v.shape[1] // 2
    bits = lax.bitcast_convert_type(v.astype(jnp.bfloat16).astype(jnp.float32), jnp.uint32)
    word = (bits[:, :n] >> 16) | (bits[:, n:] & jnp.uint32(0xFFFF0000))
    return lax.bitcast_convert_type(word, jnp.int32)


def _unpack_bf16_pairs(word):
    bits = lax.bitcast_convert_type(word, jnp.uint32)
    lo = lax.bitcast_convert_type(bits << 16, jnp.float32)
    hi = lax.bitcast_convert_type(bits & jnp.uint32(0xFFFF0000), jnp.float32)
    return lo, hi


def _inproj_kernel(x_ref, w_ref, b_ref, cos_ref, sin_ref, o_ref):
    xb = x_ref[...].astype(jnp.bfloat16)
    for start, width, rope, scaled in _PROJ_CHUNKS:
        t = jnp.dot(xb, w_ref[:, start:start + width], preferred_element_type=jnp.float32)
        t = t + b_ref[:, start:start + width]
        if rope:
            lane = lax.broadcasted_iota(jnp.int32, t.shape, 1)
            first_half = (lane % HEAD_DIM) < (HEAD_DIM // 2)
            rot = jnp.where(first_half,
                            pltpu.roll(t, width - HEAD_DIM // 2, 1),
                            pltpu.roll(t, HEAD_DIM // 2, 1))
            roped = t * cos_ref[:, :width] + rot * sin_ref[:, :width]
            t = roped if rope == width else jnp.where(lane < rope, roped, t)
        if scaled:
            t = t * (HEAD_DIM ** -0.5 * LOG2E)
        o_ref[:, start:start + width] = t.astype(o_ref.dtype)


def _inproj(x2, w_in, b_in, cos_t, sin_t, seq):
    n_tok = x2.shape[0]
    per_seq = seq // ROW_TILE
    return pl.pallas_call(
        _inproj_kernel,
        grid=(n_tok // ROW_TILE,),
        in_specs=[
            pl.BlockSpec((ROW_TILE, D_MODEL), lambda i: (i, 0)),
            pl.BlockSpec((D_MODEL, IN_WIDTH), lambda i: (0, 0)),
            pl.BlockSpec((1, IN_WIDTH), lambda i: (0, 0)),
            pl.BlockSpec((ROW_TILE, 256), lambda i: (i % per_seq, 0)),
            pl.BlockSpec((ROW_TILE, 256), lambda i: (i % per_seq, 0)),
        ],
        out_specs=pl.BlockSpec((ROW_TILE, IN_WIDTH), lambda i: (i, 0)),
        out_shape=jax.ShapeDtypeStruct((n_tok, IN_WIDTH), jnp.bfloat16),
        compiler_params=_params("parallel"),
        name="inproj_rope",
    )(x2, w_in, b_in, cos_t, sin_t)


def _swa_kernel(sink_ref, q_ref, k_ref, v_ref, after_ref, o_ref, kd_ref, vd_ref, *, seq):
    del after_ref
    lane = lax.broadcasted_iota(jnp.int32, (seq, LANES), 1)
    low = lane < HEAD_DIM
    k = k_ref[...].astype(jnp.float32)
    kr = pltpu.roll(k, HEAD_DIM, 1)
    kd_ref[0, WINDOW:, :] = jnp.where(low, k, kr).astype(kd_ref.dtype)
    kd_ref[1, WINDOW:, :] = jnp.where(low, kr, k).astype(kd_ref.dtype)
    v = v_ref[...].astype(jnp.float32)
    vd_ref[0, WINDOW:, :] = jnp.where(low, v, 1.0).astype(vd_ref.dtype)
    vd_ref[1, WINDOW:, :] = jnp.where(low, pltpu.roll(v, HEAD_DIM, 1), 1.0).astype(vd_ref.dtype)
    kd_ref[:, :WINDOW, :] = jnp.zeros((N_KV_SWA, WINDOW, LANES), kd_ref.dtype)
    vd_ref[:, :WINDOW, :] = jnp.zeros((N_KV_SWA, WINDOW, LANES), vd_ref.dtype)

    group = N_HEADS_SWA // N_KV_SWA
    rows = group * WINDOW
    r_in = lax.broadcasted_iota(jnp.int32, (rows, 2 * WINDOW), 0) % WINDOW
    c_id = lax.broadcasted_iota(jnp.int32, (rows, 2 * WINDOW), 1)
    band = (c_id > r_in) & (c_id <= r_in + WINDOW)
    head_in_group = lax.broadcasted_iota(jnp.int32, (rows, 1), 0) // WINDOW
    qlane_low = lax.broadcasted_iota(jnp.int32, (WINDOW, LANES), 1) < HEAD_DIM
    sinks = []
    for g in range(N_KV_SWA):
        col = jnp.zeros((rows, 1), jnp.float32)
        for j in range(group):
            col = jnp.where(head_in_group == j, sink_ref[g * group + j] * LOG2E, col)
        sinks.append(col)

    def scores(n, mask):
        r0 = pl.multiple_of(n * WINDOW, WINDOW)
        out = []
        for g in range(N_KV_SWA):
            parts = []
            for c in (2 * g, 2 * g + 1):
                qc = q_ref[pl.ds(r0, WINDOW), c * LANES:(c + 1) * LANES]
                zero = jnp.zeros_like(qc)
                parts.append(jnp.where(qlane_low, qc, zero))
                parts.append(jnp.where(qlane_low, zero, qc))
            qcat = jnp.concatenate(parts, axis=0)
            kd = kd_ref[g, pl.ds(r0, 2 * WINDOW), :]
            s = lax.dot_general(qcat, kd, (((1,), (1,)), ((), ())),
                                preferred_element_type=jnp.float32)
            out.append(jnp.where(mask, s, -jnp.inf))
        return out

    def finish(n, scored):
        r0 = pl.multiple_of(n * WINDOW, WINDOW)
        for g, s in enumerate(scored):
            vd = vd_ref[g, pl.ds(r0, 2 * WINDOW), :]
            m = jnp.maximum(jnp.max(s, axis=-1, keepdims=True), sinks[g])
            p = jnp.exp2(s - m)
            o = jnp.dot(p.astype(vd.dtype), vd, preferred_element_type=jnp.float32)
            sink_term = jnp.exp2(sinks[g] - m)
            for ci, c in enumerate((2 * g, 2 * g + 1)):
                o_lo = o[(2 * ci) * WINDOW:(2 * ci + 1) * WINDOW]
                o_hi = o[(2 * ci + 1) * WINDOW:(2 * ci + 2) * WINDOW]
                e_lo = sink_term[(2 * ci) * WINDOW:(2 * ci + 1) * WINDOW]
                e_hi = sink_term[(2 * ci + 1) * WINDOW:(2 * ci + 2) * WINDOW]
                num = jnp.where(qlane_low, o_lo, pltpu.roll(o_hi, HEAD_DIM, 1))
                den = jnp.where(qlane_low, pltpu.roll(o_lo, HEAD_DIM, 1) + e_lo, o_hi + e_hi)
                o_ref[pl.ds(r0, WINDOW), c * LANES:(c + 1) * LANES] = (num / den).astype(o_ref.dtype)

    finish(0, scores(0, band & (c_id >= WINDOW)))

    def body(it, carry):
        n0 = 1 + it * SWA_UNROLL
        nxt = scores(n0, band)
        for u in range(SWA_UNROLL):
            cur = nxt
            if u + 1 < SWA_UNROLL:
                nxt = scores(n0 + u + 1, band)
            finish(n0 + u, cur)
        return carry

    lax.fori_loop(0, (seq // WINDOW - 1) // SWA_UNROLL, body, 0)


def _swa(proj, sinks, b0, batch, seq, after):
    grid_spec = pltpu.PrefetchScalarGridSpec(
        num_scalar_prefetch=0,
        grid=(batch,),
        in_specs=[
            pl.BlockSpec(memory_space=pltpu.SMEM),
            pl.BlockSpec((seq, W_Q_SWA), lambda b: (b + b0, 0)),
            pl.BlockSpec((seq, LANES), lambda b: (b + b0, COL_K_SWA)),
            pl.BlockSpec((seq, LANES), lambda b: (b + b0, COL_V_SWA)),
            pl.BlockSpec(memory_space=pl.ANY),
        ],
        out_specs=pl.BlockSpec((seq, W_Q_SWA), lambda b: (b, 0)),
        scratch_shapes=[pltpu.VMEM((N_KV_SWA, WINDOW + seq, LANES), jnp.bfloat16),
                        pltpu.VMEM((N_KV_SWA, WINDOW + seq, LANES), jnp.bfloat16)],
    )
    return pl.pallas_call(
        functools.partial(_swa_kernel, seq=seq),
        grid_spec=grid_spec,
        out_shape=jax.ShapeDtypeStruct((batch * seq, W_Q_SWA), jnp.bfloat16),
        compiler_params=_params("parallel"),
        name="swa_sink_attention",
    )(sinks, proj, proj, proj, after)


def _moba_kernel(q_ref, k_ref, v_ref, after_ref, o_ref, qa_ref, ka_ref, va_ref, *, seq):
    del after_ref
    nblk = seq // MOBA_BLOCK
    pad_rows = 16
    q_all, k_all, v_all = q_ref[...], k_ref[...], v_ref[...]
    kmean = jnp.sum(k_all.astype(jnp.float32).reshape(nblk, MOBA_BLOCK, LANES), axis=1) / MOBA_BLOCK
    kmean = jnp.concatenate([kmean, jnp.zeros((pad_rows - nblk, LANES), jnp.float32)], axis=0)
    klane_low = lax.broadcasted_iota(jnp.int32, (pad_rows, LANES), 1) < HEAD_DIM
    lane = lax.broadcasted_iota(jnp.int32, (seq, LANES), 1)
    low = lane < HEAD_DIM
    key_blk = lax.broadcasted_iota(jnp.int32, (seq, LANES), 0) // MOBA_BLOCK
    j_id = lax.broadcasted_iota(jnp.int32, (pad_rows, seq), 0)
    q_blk = lax.broadcasted_iota(jnp.int32, (pad_rows, seq), 1) // MOBA_BLOCK
    eligible = j_id < q_blk

    for half in range(2):
        own = low if half == 0 else ~low
        spare = HEAD_DIM if half == 0 else 0
        ka_ref[half] = jnp.where(own, k_all, (lane - spare == key_blk).astype(k_all.dtype))
        va_ref[half] = jnp.where(own, v_all, jnp.ones_like(v_all))
        km = jnp.where(klane_low if half == 0 else ~klane_low, kmean, 0.0).astype(jnp.bfloat16)
        gate = lax.dot_general(km, q_all, (((1,), (1,)), ((), ())),
                               preferred_element_type=jnp.float32)
        gate = jnp.where(eligible, gate, -jnp.inf)
        beaten = jnp.zeros((pad_rows, seq), jnp.int32)
        for jp in range(nblk):
            row = gate[jp:jp + 1, :]
            wins = (row > gate) | ((row == gate) & (jp < j_id))
            beaten = beaten + wins.astype(jnp.int32)
        dropped = eligible & (beaten >= MOBA_TOPK)
        bias = jnp.where(dropped, NEG_BIG, 0.0)
        pieces = [bias, jnp.zeros((LANES - spare - pad_rows, seq), jnp.float32)]
        if spare:
            pieces.insert(0, jnp.zeros((spare, seq), jnp.float32))
        bias_t = jnp.concatenate(pieces, axis=0).T
        qa_ref[half] = jnp.where(own, q_all, bias_t.astype(q_all.dtype))

    qlane_low = lax.broadcasted_iota(jnp.int32, (MOBA_BLOCK, LANES), 1) < HEAD_DIM
    rr = lax.broadcasted_iota(jnp.int32, (MOBA_BLOCK, MOBA_BLOCK), 0)
    cc = lax.broadcasted_iota(jnp.int32, (MOBA_BLOCK, MOBA_BLOCK), 1)
    causal = cc <= rr

    def scores(i, half):
        r0 = i * MOBA_BLOCK
        n_keys = r0 + MOBA_BLOCK
        s = lax.dot_general(qa_ref[half, r0:n_keys, :], ka_ref[half, 0:n_keys, :], (((1,), (1,)), ((), ())),
                            preferred_element_type=jnp.float32)
        own_blk = jnp.where(causal, s[:, r0:n_keys], NEG_BIG)
        return jnp.concatenate([s[:, :r0], own_blk], axis=1) if i else own_blk

    pending = [[scores(i, half) for half in range(2)] for i in range(MOBA_LOOKAHEAD)]
    for i in range(nblk):
        cur = pending.pop(0)
        if i + MOBA_LOOKAHEAD < nblk:
            pending.append([scores(i + MOBA_LOOKAHEAD, half) for half in range(2)])
        n_keys = (i + 1) * MOBA_BLOCK
        ms = [jnp.max(s, axis=-1, keepdims=True) for s in cur]
        ps = [jnp.exp2(s - m).astype(jnp.bfloat16) for s, m in zip(cur, ms)]
        acc = [jnp.dot(p, va_ref[half, 0:n_keys, :], preferred_element_type=jnp.float32)
               for half, p in enumerate(ps)]
        num = jnp.where(qlane_low, acc[0], acc[1])
        den = pltpu.roll(jnp.where(qlane_low, acc[1], acc[0]), HEAD_DIM, 1)
        o_ref[i * MOBA_BLOCK:n_keys, :] = (num / den).astype(o_ref.dtype)


def _moba(proj, b0, batch, seq, after):
    pairs = W_MOBA // LANES
    return pl.pallas_call(
        functools.partial(_moba_kernel, seq=seq),
        grid=(batch, pairs),
        in_specs=[
            pl.BlockSpec((seq, LANES), lambda b, p: (b + b0, COL_Q_MOBA + p)),
            pl.BlockSpec((seq, LANES), lambda b, p: (b + b0, COL_K_MOBA + p)),
            pl.BlockSpec((seq, LANES), lambda b, p: (b + b0, COL_V_MOBA + p)),
            pl.BlockSpec(memory_space=pl.ANY),
        ],
        out_specs=pl.BlockSpec((seq, LANES), lambda b, p: (b, p)),
        out_shape=jax.ShapeDtypeStruct((batch * seq, W_MOBA), jnp.bfloat16),
        scratch_shapes=[pltpu.VMEM((2, seq, LANES), jnp.bfloat16)] * 3,
        compiler_params=_params("parallel", "parallel"),
        name="moba_attention",
    )(proj, proj, proj, after)


def _layer_norm(h, g, b):
    mu = jnp.mean(h, axis=-1, keepdims=True)
    d = h - mu
    var = jnp.mean(d * d, axis=-1, keepdims=True)
    return d * lax.rsqrt(var + LN_EPS) * g + b


def _outproj_kernel(oa_ref, ob_ref, wo_ref, bo_ref, x_ref, g_ref, b_ref, wr_ref, br_ref, tri_ref,
                    x1_ref, x1p_ref, idx_ref, gate_ref, rank_ref, count_ref, running_ref):
    @pl.when(pl.program_id(0) == 0)
    def _():
        running_ref[...] = jnp.zeros(running_ref.shape, running_ref.dtype)

    half_rows = ROW_TILE // 2
    halves = [slice(h * half_rows, (h + 1) * half_rows) for h in range(2)]

    def project(rows):
        heads = jnp.concatenate([oa_ref[rows, :], ob_ref[rows, :]], axis=1)
        return jnp.dot(heads, wo_ref[...], preferred_element_type=jnp.float32) + bo_ref[...]

    def normalise(rows, mix):
        x1 = _layer_norm(DEEPNORM_ALPHA * x_ref[rows, :] + mix, g_ref[...], b_ref[...])
        x1_ref[rows, :] = x1
        x1p_ref[rows, :] = _pack_bf16_pairs(x1)
        return x1.astype(jnp.bfloat16)

    mixes = [project(rows) for rows in halves]
    x1b = jnp.concatenate([normalise(rows, mix) for rows, mix in zip(halves, mixes)], axis=0)

    logits = lax.dot_general(wr_ref[...], x1b, (((1,), (1,)), ((), ())),
                             preferred_element_type=jnp.float32) + br_ref[...]
    expert = lax.broadcasted_iota(jnp.int32, logits.shape, 0)
    idx_rows, val_rows, onehots = [], [], []
    top = None
    total = None
    for k in range(TOP_K):
        m = jnp.max(logits, axis=0, keepdims=True)
        idx = jnp.min(jnp.where(logits == m, expert, N_EXPERTS), axis=0, keepdims=True)
        picked = expert == idx
        onehots.append(picked)
        logits = jnp.where(picked, -jnp.inf, logits)
        if k == 0:
            top = m
        e = jnp.exp(m - top)
        total = e if k == 0 else total + e
        idx_rows.append(idx)
        val_rows.append(e)
    idx_ref[...] = jnp.concatenate(idx_rows, axis=0)
    gates_t = jnp.concatenate(val_rows, axis=0) / total
    gates_t = jnp.concatenate([gates_t, jnp.zeros((LANES - TOP_K, gates_t.shape[1]), jnp.float32)], axis=0)
    gate_ref[...] = gates_t.T[:, :TOP_K]

    picks = jnp.concatenate([p.astype(jnp.bfloat16) for p in onehots], axis=0)
    before = jnp.dot(picks, tri_ref[...], preferred_element_type=jnp.float32)
    base = running_ref[...]
    rank_rows = []
    for k in range(TOP_K):
        pk = onehots[k].astype(jnp.float32)
        here = before[k * N_EXPERTS:(k + 1) * N_EXPERTS, :] + base
        rank_rows.append(jnp.sum(pk * here, axis=0, keepdims=True))
        base = base + jnp.sum(pk, axis=1, keepdims=True)
    running_ref[...] = base
    rank_ref[...] = jnp.concatenate(rank_rows, axis=0).astype(jnp.int32)
    count_ref[...] = base.astype(jnp.int32)


def _outproj(o_a, o_b, w_o, b_out, x2, g, b, w_r, b_r, t0, n_tok):
    tile0 = t0 // ROW_TILE
    src = lambda i: (i + tile0, 0)
    row = lambda i: (i, 0)
    fixed = lambda i: (0, 0)
    r = lax.broadcasted_iota(jnp.int32, (ROW_TILE, ROW_TILE), 0)
    c = lax.broadcasted_iota(jnp.int32, (ROW_TILE, ROW_TILE), 1)
    tri = (r < c).astype(jnp.bfloat16)
    return pl.pallas_call(
        _outproj_kernel,
        grid=(n_tok // ROW_TILE,),
        in_specs=[
            pl.BlockSpec((ROW_TILE, W_Q_SWA), row),
            pl.BlockSpec((ROW_TILE, W_MOBA), row),
            pl.BlockSpec((W_Q_SWA + W_MOBA, D_MODEL), fixed),
            pl.BlockSpec((1, D_MODEL), fixed),
            pl.BlockSpec((ROW_TILE, D_MODEL), src),
            pl.BlockSpec((1, D_MODEL), fixed),
            pl.BlockSpec((1, D_MODEL), fixed),
            pl.BlockSpec((N_EXPERTS, D_MODEL), fixed),
            pl.BlockSpec((N_EXPERTS, 1), fixed),
            pl.BlockSpec((ROW_TILE, ROW_TILE), fixed),
        ],
        out_specs=[
            pl.BlockSpec((ROW_TILE, D_MODEL), row),
            pl.BlockSpec((ROW_TILE, PACKED), row),
            pl.BlockSpec((TOP_K, ROW_TILE), lambda i: (0, i)),
            pl.BlockSpec((ROW_TILE, TOP_K), row),
            pl.BlockSpec((TOP_K, ROW_TILE), lambda i: (0, i)),
            pl.BlockSpec((N_EXPERTS, 1), fixed),
        ],
        out_shape=[
            jax.ShapeDtypeStruct((n_tok, D_MODEL), jnp.float32),
            jax.ShapeDtypeStruct((n_tok, PACKED), jnp.int32),
            jax.ShapeDtypeStruct((TOP_K, n_tok), jnp.int32),
            jax.ShapeDtypeStruct((n_tok, TOP_K), jnp.float32),
            jax.ShapeDtypeStruct((TOP_K, n_tok), jnp.int32),
            jax.ShapeDtypeStruct((N_EXPERTS, 1), jnp.int32),
        ],
        scratch_shapes=[pltpu.VMEM((N_EXPERTS, 1), jnp.float32)],
        compiler_params=_params("arbitrary"),
        name="outproj_ln_router",
    )(o_a, o_b, w_o, b_out, x2, g, b, w_r, b_r, tri)


def _sc_worker_id():
    return lax.axis_index("s") * SC_CORES + lax.axis_index("c")


def _sc_scatter_rows(rows, pos3, n_out):
    n_tok = pos3.shape[0] * SC_ROWS
    steps = n_tok // SC_ROWS // SC_WORKERS
    assert steps * SC_ROWS * SC_WORKERS == n_tok and steps % 2 == 0
    mesh = plsc.VectorSubcoreMesh(core_axis_name="c", subcore_axis_name="s")

    @functools.partial(
        pl.kernel, mesh=mesh,
        out_type=jax.ShapeDtypeStruct((n_out, PACKED), jnp.int32),
        scratch_types=[pltpu.VMEM((2, TOP_K, SC_ROWS), jnp.int32), pltpu.VMEM((2, SC_ROWS, PACKED), jnp.int32),
                       pltpu.SemaphoreType.DMA((2,)), pltpu.SemaphoreType.DMA((2,))],
        name="sc_dispatch_scatter")
    def scatter(x_hbm, pos_hbm, out_hbm, idx_v, rows_v, sem_ld, sem_st):
        base = _sc_worker_id() * steps

        def loads(s, b):
            return (pltpu.make_async_copy(pos_hbm.at[base + s], idx_v.at[b], sem_ld.at[b]),
                    pltpu.make_async_copy(x_hbm.at[pl.ds((base + s) * SC_ROWS, SC_ROWS)], rows_v.at[b],
                                          sem_ld.at[b]))

        def stores(b):
            return [pltpu.make_async_copy(rows_v.at[b], out_hbm.at[idx_v.at[b, k]], sem_st.at[b])
                    for k in range(TOP_K)]

        for c in loads(0, 0):
            c.start()

        @pl.loop(0, steps, step=2)
        def _(s0):
            for b in range(2):
                s = s0 + b
                for c in loads(s, b):
                    c.wait()

                @pl.when(s >= 1)
                def _():
                    for c in stores(1 - b):
                        c.wait()

                @pl.when(s + 1 < steps)
                def _():
                    for c in loads(s + 1, 1 - b):
                        c.start()

                for c in stores(b):
                    c.start()

        for c in stores(1):
            c.wait()

    return scatter(rows, pos3)


def _sc_gather_rows(table, idx2):
    n_blk = idx2.shape[0]
    steps = n_blk // SC_WORKERS
    assert steps * SC_WORKERS == n_blk and steps % 2 == 0 and idx2.shape[1] == SC_ROWS
    mesh = plsc.VectorSubcoreMesh(core_axis_name="c", subcore_axis_name="s")

    @functools.partial(
        pl.kernel, mesh=mesh,
        out_type=jax.ShapeDtypeStruct((n_blk * SC_ROWS, PACKED), jnp.int32),
        scratch_types=[pltpu.VMEM((steps, SC_ROWS), jnp.int32), pltpu.VMEM((2, SC_ROWS, PACKED), jnp.int32),
                       pltpu.SemaphoreType.DMA((2,)), pltpu.SemaphoreType.DMA((2,))],
        name="sc_combine_gather")
    def gather(y_hbm, idx_hbm, out_hbm, idx_v, rows_v, sem_ld, sem_st):
        base = _sc_worker_id() * steps
        pltpu.sync_copy(idx_hbm.at[pl.ds(base, steps)], idx_v)

        def fetch(s, b):
            return pltpu.make_async_copy(y_hbm.at[idx_v.at[s]], rows_v.at[b], sem_ld.at[b])

        def store(s, b):
            return pltpu.make_async_copy(rows_v.at[b], out_hbm.at[pl.ds((base + s) * SC_ROWS, SC_ROWS)],
                                         sem_st.at[b])

        fetch(0, 0).start()

        @pl.loop(0, steps, step=2)
        def _(s0):
            for b in range(2):
                s = s0 + b
                fetch(s, b).wait()

                @pl.when(s >= 1)
                def _():
                    store(s - 1, 1 - b).wait()

                @pl.when(s + 1 < steps)
                def _():
                    fetch(s + 1, 1 - b).start()

                store(s, b).start()

        store(steps - 1, 1).wait()

    return gather(table, idx2)


def _expert_kernel(be_ref, br_ref, slot_ref, next_ref, nv_ref, x_ref, w1_hbm, perm_ref, bg_ref, bl_ref, w2_hbm,
                   b2_ref, after_ref, y_ref, w1f_ref, w2f_ref, wg_ref, wl_ref, w2b_ref, sem):
    del after_ref
    i = pl.program_id(0)
    live = i < nv_ref[0]
    new_expert = (i == 0) | (be_ref[i] != be_ref[jnp.maximum(i - 1, 0)])

    def weight_copies(expert, slot):
        return (pltpu.make_async_copy(w1_hbm.at[expert], w1f_ref.at[slot], sem.at[0, slot]),
                pltpu.make_async_copy(w2_hbm.at[expert], w2f_ref.at[slot], sem.at[1, slot]))

    @pl.when(live & new_expert)
    def _():
        slot = slot_ref[i]

        @pl.when(i == 0)
        def _():
            for copy in weight_copies(be_ref[i], slot):
                copy.start()

        for copy in weight_copies(be_ref[i], slot):
            copy.wait()

        @pl.when(next_ref[i] >= 0)
        def _():
            for copy in weight_copies(next_ref[i], 1 - slot):
                copy.start()

        for c in range(2 * D_FF // 256):
            t = w1f_ref[slot, :, c * 256:(c + 1) * 256].astype(jnp.bfloat16)
            r = jnp.dot(t, perm_ref[...], preferred_element_type=jnp.float32)
            wg_ref[:, c * LANES:(c + 1) * LANES] = r[:, :LANES].astype(wg_ref.dtype)
            wl_ref[:, c * LANES:(c + 1) * LANES] = r[:, LANES:].astype(wl_ref.dtype)
        w2b_ref[...] = w2f_ref[slot].astype(w2b_ref.dtype)

    def mlp(rows):
        valid = lax.broadcasted_iota(jnp.int32, (rows, PACKED), 0) < br_ref[i]
        lo, hi = _unpack_bf16_pairs(jnp.where(valid, x_ref[0:rows, :], 0))
        xb = jnp.concatenate([lo.astype(jnp.bfloat16), hi.astype(jnp.bfloat16)], axis=1)
        hg = jnp.dot(xb, wg_ref[...], preferred_element_type=jnp.float32) + bg_ref[0]
        hl = jnp.dot(xb, wl_ref[...], preferred_element_type=jnp.float32) + bl_ref[0]
        glu = jnp.minimum(hg, SWIGLU_LIMIT)
        lin = jnp.clip(hl, -SWIGLU_LIMIT, SWIGLU_LIMIT)
        act = glu * jax.nn.sigmoid(SWIGLU_ALPHA * glu) * (lin + 1.0)
        y = jnp.dot(act.astype(jnp.bfloat16), w2b_ref[...], preferred_element_type=jnp.float32) + b2_ref[0]
        y_ref[0:rows, :] = _pack_bf16_pairs(y)

    parts = (br_ref[i] + EXPERT_PART - 1) // EXPERT_PART
    for n_parts in range(1, EXPERT_ROWS // EXPERT_PART + 1):
        @pl.when(live & (parts == n_parts))
        def _(rows=n_parts * EXPERT_PART):
            mlp(rows)
            if rows < EXPERT_ROWS:
                y_ref[rows:, :] = jnp.zeros((EXPERT_ROWS - rows, PACKED), y_ref.dtype)

    @pl.when(jnp.logical_not(live))
    def _():
        y_ref[...] = jnp.zeros(y_ref.shape, y_ref.dtype)


def _experts(tables, xg, w1, perm, b1g, b1l, w2, b2, after):
    n_rows = xg.shape[0]
    n_blocks = n_rows // EXPERT_ROWS

    def row(i, be, br, slot, nxt, nv):
        return (jnp.minimum(i, nv[0] - 1), 0)

    def per_expert(i, be, br, slot, nxt, nv):
        return (be[i], 0, 0)

    grid_spec = pltpu.PrefetchScalarGridSpec(
        num_scalar_prefetch=5,
        grid=(n_blocks,),
        in_specs=[
            pl.BlockSpec((EXPERT_ROWS, PACKED), row),
            pl.BlockSpec(memory_space=pl.ANY),
            pl.BlockSpec((256, 256), lambda i, *_: (0, 0)),
            pl.BlockSpec((1, 1, D_FF), per_expert),
            pl.BlockSpec((1, 1, D_FF), per_expert),
            pl.BlockSpec(memory_space=pl.ANY),
            pl.BlockSpec((1, 1, D_MODEL), per_expert),
            pl.BlockSpec(memory_space=pl.ANY),
        ],
        out_specs=pl.BlockSpec((EXPERT_ROWS, PACKED), lambda i, *_: (i, 0)),
        scratch_shapes=[pltpu.VMEM((2, D_MODEL, 2 * D_FF), jnp.float32), pltpu.VMEM((2, D_FF, D_MODEL), jnp.float32),
                        pltpu.VMEM((D_MODEL, D_FF), jnp.bfloat16), pltpu.VMEM((D_MODEL, D_FF), jnp.bfloat16),
                        pltpu.VMEM((D_FF, D_MODEL), jnp.bfloat16), pltpu.SemaphoreType.DMA((2, 2))],
    )
    return pl.pallas_call(
        _expert_kernel,
        grid_spec=grid_spec,
        out_shape=jax.ShapeDtypeStruct((n_rows, PACKED), jnp.int32),
        compiler_params=pltpu.CompilerParams(dimension_semantics=("arbitrary",),
                                             vmem_limit_bytes=EXPERT_VMEM_LIMIT),
        name="grouped_experts",
    )(*tables, xg, w1, perm, b1g, b1l, w2, b2, after)


def _split_columns_perm():
    i = lax.broadcasted_iota(jnp.int32, (256, 256), 0)
    o = lax.broadcasted_iota(jnp.int32, (256, 256), 1)
    return (i == jnp.where(o < LANES, 2 * o, 2 * (o - LANES) + 1)).astype(jnp.bfloat16)


def _combine_kernel(y_ref, gate_ref, x1_ref, g_ref, b_ref, *rest):
    o_ref = rest[-1]
    gates = gate_ref[...]
    lo_sum = None
    hi_sum = None
    for k in range(TOP_K):
        lo, hi = _unpack_bf16_pairs(y_ref[k])
        gk = gates[:, k:k + 1]
        lo_sum = gk * lo if k == 0 else lo_sum + gk * lo
        hi_sum = gk * hi if k == 0 else hi_sum + gk * hi
    moe = jnp.concatenate([lo_sum, hi_sum], axis=1)
    o_ref[...] = _layer_norm(DEEPNORM_ALPHA * x1_ref[...] + moe, g_ref[...], b_ref[...])


def _combine(yg, gates, x1, g, b, tile0, n_tok, prev):
    row = lambda i: (i, 0)
    fixed = lambda i: (0, 0)
    in_specs = [
        pl.BlockSpec((TOP_K, ROW_TILE, PACKED), lambda i: (0, i, 0)),
        pl.BlockSpec((ROW_TILE, TOP_K), row),
        pl.BlockSpec((ROW_TILE, D_MODEL), row),
        pl.BlockSpec((1, D_MODEL), fixed),
        pl.BlockSpec((1, D_MODEL), fixed),
    ]
    args = [yg, gates, x1, g, b]
    aliases = {}
    if prev is not None:
        in_specs.append(pl.BlockSpec(memory_space=pl.ANY))
        args.append(prev)
        aliases = {len(args) - 1: 0}
    return pl.pallas_call(
        _combine_kernel,
        grid=(yg.shape[1] // ROW_TILE,),
        in_specs=in_specs,
        out_specs=pl.BlockSpec((ROW_TILE, D_MODEL), lambda i: (i + tile0, 0)),
        out_shape=jax.ShapeDtypeStruct((n_tok, D_MODEL), jnp.float32),
        input_output_aliases=aliases,
        compiler_params=_params("parallel"),
        name="combine_ln",
    )(*args)


def _route(top_idx, rank, counts, n_blocks):
    counts = counts[:, 0]
    experts = jnp.arange(N_EXPERTS, dtype=jnp.int32)
    upto = experts[None, :] <= experts[:, None]
    later = experts[None, :] > experts[:, None]
    prefix = lambda v: jnp.sum(jnp.where(upto, v[None, :], 0), axis=1)
    blocks_per = (counts + EXPERT_ROWS - 1) // EXPERT_ROWS
    blk_end = prefix(blocks_per)
    blk_start = blk_end - blocks_per
    first_row = (blk_start * EXPERT_ROWS)[:, None, None]
    pos = rank + jnp.sum(jnp.where(top_idx[None] == experts[:, None, None], first_row, 0), axis=0)
    n_valid = jnp.sum(blocks_per, keepdims=True).astype(jnp.int32)
    has_rows = blocks_per > 0
    run_of = prefix(has_rows.astype(jnp.int32)) - 1
    later = later & has_rows[None, :]
    next_of = jnp.min(jnp.where(later, experts[None, :], N_EXPERTS), axis=1)
    next_of = jnp.where(next_of == N_EXPERTS, -1, next_of)
    blk = jnp.arange(n_blocks, dtype=jnp.int32)
    owner = ((blk[:, None] >= blk_start[None, :]) & (blk[:, None] < blk_end[None, :])).astype(jnp.int32)
    pick = lambda per_expert: jnp.sum(owner * per_expert[None, :], axis=1).astype(jnp.int32)
    block_expert = pick(experts)
    block_rows = jnp.clip(pick(counts) - (blk - pick(blk_start)) * EXPERT_ROWS, 0, EXPERT_ROWS).astype(jnp.int32)
    return pos, (block_expert, block_rows, pick(run_of) % 2, pick(next_of), n_valid)


def _moe_groups(n_tok, seq):
    unit = math.lcm(2 * SC_ROWS * SC_WORKERS, seq)
    parts = sum(MOE_SPLIT)
    if n_tok % (parts * unit):
        return ((0, n_tok),)
    groups, t0 = [], 0
    for share in MOE_SPLIT:
        groups.append((t0, n_tok * share // parts))
        t0 += groups[-1][1]
    return tuple(groups)


def _rope_tables(seq):
    inv_freq = 1.0 / (ROPE_THETA ** (jnp.arange(0, HEAD_DIM, 2, dtype=jnp.float32) / HEAD_DIM))
    ang = jnp.arange(seq, dtype=jnp.float32)[:, None] * inv_freq[None, :]
    cos, sin = jnp.cos(ang), jnp.sin(ang)
    cos_t = jnp.tile(jnp.concatenate([cos, cos], axis=1), (1, 256 // HEAD_DIM))
    sin_t = jnp.tile(jnp.concatenate([-sin, sin], axis=1), (1, 256 // HEAD_DIM))
    return cos_t, sin_t


def kernel(x, w_in, b_in, sinks, w_out, b_out, ln1_g, ln1_b, w_router, b_router, w1, b1, w2, b2, ln2_g, ln2_b):
    batch, seq, d = x.shape
    assert d == D_MODEL and seq % ROW_TILE == 0 and seq % MOBA_BLOCK == 0 and w_in.shape[0] == DEPTH == 1
    assert (seq // WINDOW - 1) % SWA_UNROLL == 0
    n_tok = batch * seq
    bf16 = jnp.bfloat16
    x2 = x.reshape(n_tok, d)
    cos_t, sin_t = _rope_tables(seq)

    proj = _inproj(x2, w_in[0].astype(bf16), b_in[0].reshape(1, IN_WIDTH), cos_t, sin_t, seq)

    w_o = w_out[0].astype(bf16)
    w_r = w_router[0].T.astype(bf16)
    b_r = b_router[0].reshape(N_EXPERTS, 1)
    b_o, g1, be1 = b_out[0].reshape(1, d), ln1_g[0].reshape(1, d), ln1_b[0].reshape(1, d)
    perm = _split_columns_perm()
    b1r = b1[0].reshape(N_EXPERTS, 1, D_FF, 2)
    b1g, b1l = b1r[..., 0], b1r[..., 1]
    b2r = b2[0].reshape(N_EXPERTS, 1, d)
    g2, be2 = ln2_g[0].reshape(1, d), ln2_b[0].reshape(1, d)

    def dispatch(stage):
        t0, tok_g, x1, x1p, gates, pos, tables, n_blocks = stage
        pos3 = pos.reshape(TOP_K, tok_g // SC_ROWS, SC_ROWS).transpose(1, 0, 2)
        return _sc_scatter_rows(x1p, pos3, n_blocks * EXPERT_ROWS)

    def experts_and_gather(stage, xg, y_prev):
        t0, tok_g, x1, x1p, gates, pos, tables, n_blocks = stage
        y = _experts(tables, xg, w1[0], perm, b1g, b1l, w2[0], b2r, y_prev)
        return y, _sc_gather_rows(y, pos.reshape(tok_g * TOP_K // SC_ROWS, SC_ROWS))

    stages, gathered, waiting = [], [], None
    pos = jnp.zeros((TOP_K, SC_ROWS), jnp.int32)
    y_prev = jnp.zeros((SC_ROWS, PACKED), jnp.int32)
    for t0, tok_g in _moe_groups(n_tok, seq):
        o_a = _swa(proj, sinks[0], t0 // seq, tok_g // seq, seq, pos)
        o_b = _moba(proj, t0 // seq, tok_g // seq, seq, pos)
        x1, x1p, top_idx, gates, rank, counts = _outproj(o_a, o_b, w_o, b_o, x2, g1, be1, w_r, b_r, t0, tok_g)
        n_blocks = tok_g * TOP_K // EXPERT_ROWS + N_EXPERTS
        pos, tables = _route(top_idx, rank, counts, n_blocks)
        stages.append((t0, tok_g, x1, x1p, gates, pos, tables, n_blocks))
        if waiting is not None:
            y_prev, yg = experts_and_gather(*waiting, y_prev)
            gathered.append(yg)
        waiting = (stages[-1], dispatch(stages[-1]))
    gathered.append(experts_and_gather(*waiting, y_prev)[1])
    out = None
    for (t0, tok_g, x1, _, gates, _, _, _), yg in zip(stages, gathered):
        out = _combine(yg.reshape(TOP_K, tok_g, PACKED), gates, x1, g2, be2, t0 // ROW_TILE, n_tok, out)
    return out.reshape(batch, seq, d)
```

```python
import functools
import math

import jax
import jax.numpy as jnp
from jax import lax
from jax.experimental import pallas as pl
from jax.experimental.pallas import tpu as pltpu
from jax.experimental.pallas import tpu_sc as plsc

D_MODEL = 1024
HEAD_DIM = 64
N_HEADS_SWA = 8
N_KV_SWA = 2
WINDOW = 128
N_HEADS_MOBA = 8
MOBA_BLOCK = 256
MOBA_TOPK = 3
ROPE_THETA = 10000.0
N_EXPERTS = 32
TOP_K = 4
D_FF = 1024
SWIGLU_LIMIT = 7.0
SWIGLU_ALPHA = 1.702
LN_EPS = 1e-5
DEPTH = 1
DEEPNORM_ALPHA = (2 * DEPTH) ** 0.25

W_Q_SWA = N_HEADS_SWA * HEAD_DIM
W_KV_SWA = N_KV_SWA * HEAD_DIM
W_MOBA = N_HEADS_MOBA * HEAD_DIM
IN_WIDTH = W_Q_SWA + 2 * W_KV_SWA + 3 * W_MOBA
LANES = 128
COL_K_SWA = W_Q_SWA // LANES
COL_V_SWA = COL_K_SWA + 1
COL_Q_MOBA = COL_V_SWA + 1
COL_K_MOBA = COL_Q_MOBA + W_MOBA // LANES
COL_V_MOBA = COL_K_MOBA + W_MOBA // LANES

ROW_TILE = 1024
EXPERT_ROWS = 1024
EXPERT_PART = 256
PACKED = D_MODEL // 2
NEG_BIG = -1e30
LOG2E = 1.4426950408889634
MOBA_LOOKAHEAD = 1
SWA_UNROLL = 3
MOE_SPLIT = (3, 1)
VMEM_LIMIT = 48 * 1024 * 1024
EXPERT_VMEM_LIMIT = 60 * 1024 * 1024

SC_CORES = 2
SC_SUBCORES = 16
SC_WORKERS = SC_CORES * SC_SUBCORES
SC_ROWS = 64

_PROJ_CHUNKS = (
    (0, 256, 256, True), (256, 256, 256, True),
    (512, 256, 128, False),
    (768, 256, 256, True), (1024, 256, 256, True),
    (1280, 256, 256, False), (1536, 256, 256, False),
    (1792, 256, 0, False), (2048, 256, 0, False),
)


def _params(*sem):
    return pltpu.CompilerParams(dimension_semantics=sem, vmem_limit_bytes=VMEM_LIMIT)


def _pack_bf16_pairs(v):
    n = v.shape[1] // 2
    bits = lax.bitcast_convert_type(v.astype(jnp.bfloat16).astype(jnp.float32), jnp.uint32)
    word = (bits[:, :n] >> 16) | (bits[:, n:] & jnp.uint32(0xFFFF0000))
    return lax.bitcast_convert_type(word, jnp.int32)


def _unpack_bf16_pairs(word):
    bits = lax.bitcast_convert_type(word, jnp.uint32)
    lo = lax.bitcast_convert_type(bits << 16, jnp.float32)
    hi = lax.bitcast_convert_type(bits & jnp.uint32(0xFFFF0000), jnp.float32)
    return lo, hi


def _inproj_kernel(x_ref, w_ref, b_ref, cos_ref, sin_ref, o_ref):
    xb = x_ref[...].astype(jnp.bfloat16)
    for start, width, rope, scaled in _PROJ_CHUNKS:
        t = jnp.dot(xb, w_ref[:, start:start + width], preferred_element_type=jnp.float32)
        t = t + b_ref[:, start:start + width]
        if rope:
            lane = lax.broadcasted_iota(jnp.int32, t.shape, 1)
            first_half = (lane % HEAD_DIM) < (HEAD_DIM // 2)
            rot = jnp.where(first_half,
                            pltpu.roll(t, width - HEAD_DIM // 2, 1),
                            pltpu.roll(t, HEAD_DIM // 2, 1))
            roped = t * cos_ref[:, :width] + rot * sin_ref[:, :width]
            t = roped if rope == width else jnp.where(lane < rope, roped, t)
        if scaled:
            t = t * (HEAD_DIM ** -0.5 * LOG2E)
        o_ref[:, start:start + width] = t.astype(o_ref.dtype)


def _inproj(x2, w_in, b_in, cos_t, sin_t, seq):
    n_tok = x2.shape[0]
    per_seq = seq // ROW_TILE
    return pl.pallas_call(
        _inproj_kernel,
        grid=(n_tok // ROW_TILE,),
        in_specs=[
            pl.BlockSpec((ROW_TILE, D_MODEL), lambda i: (i, 0)),
            pl.BlockSpec((D_MODEL, IN_WIDTH), lambda i: (0, 0)),
            pl.BlockSpec((1, IN_WIDTH), lambda i: (0, 0)),
            pl.BlockSpec((ROW_TILE, 256), lambda i: (i % per_seq, 0)),
            pl.BlockSpec((ROW_TILE, 256), lambda i: (i % per_seq, 0)),
        ],
        out_specs=pl.BlockSpec((ROW_TILE, IN_WIDTH), lambda i: (i, 0)),
        out_shape=jax.ShapeDtypeStruct((n_tok, IN_WIDTH), jnp.bfloat16),
        compiler_params=_params("parallel"),
        name="inproj_rope",
    )(x2, w_in, b_in, cos_t, sin_t)


def _swa_kernel(sink_ref, q_ref, k_ref, v_ref, after_ref, o_ref, kd_ref, vd_ref, *, seq):
    del after_ref
    lane = lax.broadcasted_iota(jnp.int32, (seq, LANES), 1)
    low = lane < HEAD_DIM
    k = k_ref[...].astype(jnp.float32)
    kr = pltpu.roll(k, HEAD_DIM, 1)
    kd_ref[0, WINDOW:, :] = jnp.where(low, k, kr).astype(kd_ref.dtype)
    kd_ref[1, WINDOW:, :] = jnp.where(low, kr, k).astype(kd_ref.dtype)
    v = v_ref[...].astype(jnp.float32)
    vd_ref[0, WINDOW:, :] = jnp.where(low, v, 1.0).astype(vd_ref.dtype)
    vd_ref[1, WINDOW:, :] = jnp.where(low, pltpu.roll(v, HEAD_DIM, 1), 1.0).astype(vd_ref.dtype)
    kd_ref[:, :WINDOW, :] = jnp.zeros((N_KV_SWA, WINDOW, LANES), kd_ref.dtype)
    vd_ref[:, :WINDOW, :] = jnp.zeros((N_KV_SWA, WINDOW, LANES), vd_ref.dtype)

    group = N_HEADS_SWA // N_KV_SWA
    rows = group * WINDOW
    r_in = lax.broadcasted_iota(jnp.int32, (rows, 2 * WINDOW), 0) % WINDOW
    c_id = lax.broadcasted_iota(jnp.int32, (rows, 2 * WINDOW), 1)
    band = (c_id > r_in) & (c_id <= r_in + WINDOW)
    head_in_group = lax.broadcasted_iota(jnp.int32, (rows, 1), 0) // WINDOW
    qlane_low = lax.broadcasted_iota(jnp.int32, (WINDOW, LANES), 1) < HEAD_DIM
    sinks = []
    for g in range(N_KV_SWA):
        col = jnp.zeros((rows, 1), jnp.float32)
        for j in range(group):
            col = jnp.where(head_in_group == j, sink_ref[g * group + j] * LOG2E, col)
        sinks.append(col)

    def scores(n, mask):
        r0 = pl.multiple_of(n * WINDOW, WINDOW)
        out = []
        for g in range(N_KV_SWA):
            parts = []
            for c in (2 * g, 2 * g + 1):
                qc = q_ref[pl.ds(r0, WINDOW), c * LANES:(c + 1) * LANES]
                zero = jnp.zeros_like(qc)
                parts.append(jnp.where(qlane_low, qc, zero))
                parts.append(jnp.where(qlane_low, zero, qc))
            qcat = jnp.concatenate(parts, axis=0)
            kd = kd_ref[g, pl.ds(r0, 2 * WINDOW), :]
            s = lax.dot_general(qcat, kd, (((1,), (1,)), ((), ())),
                                preferred_element_type=jnp.float32)
            out.append(jnp.where(mask, s, -jnp.inf))
        return out

    def finish(n, scored):
        r0 = pl.multiple_of(n * WINDOW, WINDOW)
        for g, s in enumerate(scored):
            vd = vd_ref[g, pl.ds(r0, 2 * WINDOW), :]
            m = jnp.maximum(jnp.max(s, axis=-1, keepdims=True), sinks[g])
            p = jnp.exp2(s - m)
            o = jnp.dot(p.astype(vd.dtype), vd, preferred_element_type=jnp.float32)
            sink_term = jnp.exp2(sinks[g] - m)
            for ci, c in enumerate((2 * g, 2 * g + 1)):
                o_lo = o[(2 * ci) * WINDOW:(2 * ci + 1) * WINDOW]
                o_hi = o[(2 * ci + 1) * WINDOW:(2 * ci + 2) * WINDOW]
                e_lo = sink_term[(2 * ci) * WINDOW:(2 * ci + 1) * WINDOW]
                e_hi = sink_term[(2 * ci + 1) * WINDOW:(2 * ci + 2) * WINDOW]
                num = jnp.where(qlane_low, o_lo, pltpu.roll(o_hi, HEAD_DIM, 1))
                den = jnp.where(qlane_low, pltpu.roll(o_lo, HEAD_DIM, 1) + e_lo, o_hi + e_hi)
                o_ref[pl.ds(r0, WINDOW), c * LANES:(c + 1) * LANES] = (num / den).astype(o_ref.dtype)

    finish(0, scores(0, band & (c_id >= WINDOW)))

    def body(it, carry):
        n0 = 1 + it * SWA_UNROLL
        nxt = scores(n0, band)
        for u in range(SWA_UNROLL):
            cur = nxt
            if u + 1 < SWA_UNROLL:
                nxt = scores(n0 + u + 1, band)
            finish(n0 + u, cur)
        return carry

    lax.fori_loop(0, (seq // WINDOW - 1) // SWA_UNROLL, body, 0)


def _swa(proj, sinks, b0, batch, seq, after):
    grid_spec = pltpu.PrefetchScalarGridSpec(
        num_scalar_prefetch=0,
        grid=(batch,),
        in_specs=[
            pl.BlockSpec(memory_space=pltpu.SMEM),
            pl.BlockSpec((seq, W_Q_SWA), lambda b: (b + b0, 0)),
            pl.BlockSpec((seq, LANES), lambda b: (b + b0, COL_K_SWA)),
            pl.BlockSpec((seq, LANES), lambda b: (b + b0, COL_V_SWA)),
            pl.BlockSpec(memory_space=pl.ANY),
        ],
        out_specs=pl.BlockSpec((seq, W_Q_SWA), lambda b: (b, 0)),
        scratch_shapes=[pltpu.VMEM((N_KV_SWA, WINDOW + seq, LANES), jnp.bfloat16),
                        pltpu.VMEM((N_KV_SWA, WINDOW + seq, LANES), jnp.bfloat16)],
    )
    return pl.pallas_call(
        functools.partial(_swa_kernel, seq=seq),
        grid_spec=grid_spec,
        out_shape=jax.ShapeDtypeStruct((batch * seq, W_Q_SWA), jnp.bfloat16),
        compiler_params=_params("parallel"),
        name="swa_sink_attention",
    )(sinks, proj, proj, proj, after)


def _moba_kernel(q_ref, k_ref, v_ref, after_ref, o_ref, qa_ref, ka_ref, va_ref, *, seq):
    del after_ref
    nblk = seq // MOBA_BLOCK
    pad_rows = 16
    q_all, k_all, v_all = q_ref[...], k_ref[...], v_ref[...]
    kmean = jnp.sum(k_all.astype(jnp.float32).reshape(nblk, MOBA_BLOCK, LANES), axis=1) / MOBA_BLOCK
    kmean = jnp.concatenate([kmean, jnp.zeros((pad_rows - nblk, LANES), jnp.float32)], axis=0)
    klane_low = lax.broadcasted_iota(jnp.int32, (pad_rows, LANES), 1) < HEAD_DIM
    lane = lax.broadcasted_iota(jnp.int32, (seq, LANES), 1)
    low = lane < HEAD_DIM
    key_blk = lax.broadcasted_iota(jnp.int32, (seq, LANES), 0) // MOBA_BLOCK
    j_id = lax.broadcasted_iota(jnp.int32, (pad_rows, seq), 0)
    q_blk = lax.broadcasted_iota(jnp.int32, (pad_rows, seq), 1) // MOBA_BLOCK
    eligible = j_id < q_blk

    for half in range(2):
        own = low if half == 0 else ~low
        spare = HEAD_DIM if half == 0 else 0
        ka_ref[half] = jnp.where(own, k_all, (lane - spare == key_blk).astype(k_all.dtype))
        va_ref[half] = jnp.where(own, v_all, jnp.ones_like(v_all))
        km = jnp.where(klane_low if half == 0 else ~klane_low, kmean, 0.0).astype(jnp.bfloat16)
        gate = lax.dot_general(km, q_all, (((1,), (1,)), ((), ())),
                               preferred_element_type=jnp.float32)
        gate = jnp.where(eligible, gate, -jnp.inf)
        beaten = jnp.zeros((pad_rows, seq), jnp.int32)
        for jp in range(nblk):
            row = gate[jp:jp + 1, :]
            wins = (row > gate) | ((row == gate) & (jp < j_id))
            beaten = beaten + wins.astype(jnp.int32)
        dropped = eligible & (beaten >= MOBA_TOPK)
        bias = jnp.where(dropped, NEG_BIG, 0.0)
        pieces = [bias, jnp.zeros((LANES - spare - pad_rows, seq), jnp.float32)]
        if spare:
            pieces.insert(0, jnp.zeros((spare, seq), jnp.float32))
        bias_t = jnp.concatenate(pieces, axis=0).T
        qa_ref[half] = jnp.where(own, q_all, bias_t.astype(q_all.dtype))

    qlane_low = lax.broadcasted_iota(jnp.int32, (MOBA_BLOCK, LANES), 1) < HEAD_DIM
    rr = lax.broadcasted_iota(jnp.int32, (MOBA_BLOCK, MOBA_BLOCK), 0)
    cc = lax.broadcasted_iota(jnp.int32, (MOBA_BLOCK, MOBA_BLOCK), 1)
    causal = cc <= rr

    def scores(i, half):
        r0 = i * MOBA_BLOCK
        n_keys = r0 + MOBA_BLOCK
        s = lax.dot_general(qa_ref[half, r0:n_keys, :], ka_ref[half, 0:n_keys, :], (((1,), (1,)), ((), ())),
                            preferred_element_type=jnp.float32)
        own_blk = jnp.where(causal, s[:, r0:n_keys], NEG_BIG)
        return jnp.concatenate([s[:, :r0], own_blk], axis=1) if i else own_blk

    pending = [[scores(i, half) for half in range(2)] for i in range(MOBA_LOOKAHEAD)]
    for i in range(nblk):
        cur = pending.pop(0)
        if i + MOBA_LOOKAHEAD < nblk:
            pending.append([scores(i + MOBA_LOOKAHEAD, half) for half in range(2)])
        n_keys = (i + 1) * MOBA_BLOCK
        ms = [jnp.max(s, axis=-1, keepdims=True) for s in cur]
        ps = [jnp.exp2(s - m).astype(jnp.bfloat16) for s, m in zip(cur, ms)]
        acc = [jnp.dot(p, va_ref[half, 0:n_keys, :], preferred_element_type=jnp.float32)
               for half, p in enumerate(ps)]
        num = jnp.where(qlane_low, acc[0], acc[1])
        den = pltpu.roll(jnp.where(qlane_low, acc[1], acc[0]), HEAD_DIM, 1)
        o_ref[i * MOBA_BLOCK:n_keys, :] = (num / den).astype(o_ref.dtype)


def _moba(proj, b0, batch, seq, after):
    pairs = W_MOBA // LANES
    return pl.pallas_call(
        functools.partial(_moba_kernel, seq=seq),
        grid=(batch, pairs),
        in_specs=[
            pl.BlockSpec((seq, LANES), lambda b, p: (b + b0, COL_Q_MOBA + p)),
            pl.BlockSpec((seq, LANES), lambda b, p: (b + b0, COL_K_MOBA + p)),
            pl.BlockSpec((seq, LANES), lambda b, p: (b + b0, COL_V_MOBA + p)),
            pl.BlockSpec(memory_space=pl.ANY),
        ],
        out_specs=pl.BlockSpec((seq, LANES), lambda b, p: (b, p)),
        out_shape=jax.ShapeDtypeStruct((batch * seq, W_MOBA), jnp.bfloat16),
        scratch_shapes=[pltpu.VMEM((2, seq, LANES), jnp.bfloat16)] * 3,
        compiler_params=_params("parallel", "parallel"),
        name="moba_attention",
    )(proj, proj, proj, after)


SWA_PER_STEP = 4


def _attention_kernel(sink_ref, sq_ref, sk_ref, sv_ref, q_ref, k_ref, v_ref, after_ref, so_ref, o_ref,
                      kd_ref, vd_ref, qa_ref, ka_ref, va_ref, *, seq):
    del after_ref
    pair = pl.program_id(1)

    @pl.when(pair == 0)
    def _():
        lane = lax.broadcasted_iota(jnp.int32, (seq, LANES), 1)
        low = lane < HEAD_DIM
        k = sk_ref[...].astype(jnp.float32)
        kr = pltpu.roll(k, HEAD_DIM, 1)
        kd_ref[0, WINDOW:, :] = jnp.where(low, k, kr).astype(kd_ref.dtype)
        kd_ref[1, WINDOW:, :] = jnp.where(low, kr, k).astype(kd_ref.dtype)
        v = sv_ref[...].astype(jnp.float32)
        vd_ref[0, WINDOW:, :] = jnp.where(low, v, 1.0).astype(vd_ref.dtype)
        vd_ref[1, WINDOW:, :] = jnp.where(low, pltpu.roll(v, HEAD_DIM, 1), 1.0).astype(vd_ref.dtype)
        kd_ref[:, :WINDOW, :] = jnp.zeros((N_KV_SWA, WINDOW, LANES), kd_ref.dtype)
        vd_ref[:, :WINDOW, :] = jnp.zeros((N_KV_SWA, WINDOW, LANES), vd_ref.dtype)

    group = N_HEADS_SWA // N_KV_SWA
    rows = group * WINDOW
    r_in = lax.broadcasted_iota(jnp.int32, (rows, 2 * WINDOW), 0) % WINDOW
    c_id = lax.broadcasted_iota(jnp.int32, (rows, 2 * WINDOW), 1)
    band = (c_id > r_in) & (c_id <= r_in + WINDOW)
    head_in_group = lax.broadcasted_iota(jnp.int32, (rows, 1), 0) // WINDOW
    wlane_low = lax.broadcasted_iota(jnp.int32, (WINDOW, LANES), 1) < HEAD_DIM
    sinks = []
    for g in range(N_KV_SWA):
        col = jnp.zeros((rows, 1), jnp.float32)
        for j in range(group):
            col = jnp.where(head_in_group == j, sink_ref[g * group + j] * LOG2E, col)
        sinks.append(col)

    def swa_scores(u):
        n = pair * SWA_PER_STEP + u
        r0 = pl.multiple_of(n * WINDOW, WINDOW)
        mask = band & ((c_id >= WINDOW) | (pair > 0)) if u == 0 else band
        out = []
        for g in range(N_KV_SWA):
            parts = []
            for c in (2 * g, 2 * g + 1):
                qc = sq_ref[pl.ds(r0, WINDOW), c * LANES:(c + 1) * LANES]
                zero = jnp.zeros_like(qc)
                parts.append(jnp.where(wlane_low, qc, zero))
                parts.append(jnp.where(wlane_low, zero, qc))
            qcat = jnp.concatenate(parts, axis=0)
            kd = kd_ref[g, pl.ds(r0, 2 * WINDOW), :]
            s = lax.dot_general(qcat, kd, (((1,), (1,)), ((), ())), preferred_element_type=jnp.float32)
            out.append(jnp.where(mask, s, -jnp.inf))
        return out

    def swa_finish(u, scored):
        n = pair * SWA_PER_STEP + u
        r0 = pl.multiple_of(n * WINDOW, WINDOW)
        for g, s in enumerate(scored):
            vd = vd_ref[g, pl.ds(r0, 2 * WINDOW), :]
            m = jnp.maximum(jnp.max(s, axis=-1, keepdims=True), sinks[g])
            p = jnp.exp2(s - m)
            o = jnp.dot(p.astype(vd.dtype), vd, preferred_element_type=jnp.float32)
            sink_term = jnp.exp2(sinks[g] - m)
            for ci, c in enumerate((2 * g, 2 * g + 1)):
                o_lo = o[(2 * ci) * WINDOW:(2 * ci + 1) * WINDOW]
                o_hi = o[(2 * ci + 1) * WINDOW:(2 * ci + 2) * WINDOW]
                e_lo = sink_term[(2 * ci) * WINDOW:(2 * ci + 1) * WINDOW]
                e_hi = sink_term[(2 * ci + 1) * WINDOW:(2 * ci + 2) * WINDOW]
                num = jnp.where(wlane_low, o_lo, pltpu.roll(o_hi, HEAD_DIM, 1))
                den = jnp.where(wlane_low, pltpu.roll(o_lo, HEAD_DIM, 1) + e_lo, o_hi + e_hi)
                so_ref[pl.ds(r0, WINDOW), c * LANES:(c + 1) * LANES] = (num / den).astype(so_ref.dtype)

    nblk = seq // MOBA_BLOCK
    pad_rows = 16
    q_all, k_all, v_all = q_ref[...], k_ref[...], v_ref[...]
    kmean = jnp.sum(k_all.astype(jnp.float32).reshape(nblk, MOBA_BLOCK, LANES), axis=1) / MOBA_BLOCK
    kmean = jnp.concatenate([kmean, jnp.zeros((pad_rows - nblk, LANES), jnp.float32)], axis=0)
    klane_low = lax.broadcasted_iota(jnp.int32, (pad_rows, LANES), 1) < HEAD_DIM
    lane = lax.broadcasted_iota(jnp.int32, (seq, LANES), 1)
    low = lane < HEAD_DIM
    key_blk = lax.broadcasted_iota(jnp.int32, (seq, LANES), 0) // MOBA_BLOCK
    j_id = lax.broadcasted_iota(jnp.int32, (pad_rows, seq), 0)
    q_blk = lax.broadcasted_iota(jnp.int32, (pad_rows, seq), 1) // MOBA_BLOCK
    eligible = j_id < q_blk
    for half in range(2):
        own = low if half == 0 else ~low
        spare = HEAD_DIM if half == 0 else 0
        ka_ref[half] = jnp.where(own, k_all, (lane - spare == key_blk).astype(k_all.dtype))
        va_ref[half] = jnp.where(own, v_all, jnp.ones_like(v_all))
        km = jnp.where(klane_low if half == 0 else ~klane_low, kmean, 0.0).astype(jnp.bfloat16)
        gate = lax.dot_general(km, q_all, (((1,), (1,)), ((), ())), preferred_element_type=jnp.float32)
        gate = jnp.where(eligible, gate, -jnp.inf)
        beaten = jnp.zeros((pad_rows, seq), jnp.int32)
        for jp in range(nblk):
            row = gate[jp:jp + 1, :]
            wins = (row > gate) | ((row == gate) & (jp < j_id))
            beaten = beaten + wins.astype(jnp.int32)
        dropped = eligible & (beaten >= MOBA_TOPK)
        bias = jnp.where(dropped, NEG_BIG, 0.0)
        pieces = [bias, jnp.zeros((LANES - spare - pad_rows, seq), jnp.float32)]
        if spare:
            pieces.insert(0, jnp.zeros((spare, seq), jnp.float32))
        bias_t = jnp.concatenate(pieces, axis=0).T
        qa_ref[half] = jnp.where(own, q_all, bias_t.astype(q_all.dtype))

    qlane_low = lax.broadcasted_iota(jnp.int32, (MOBA_BLOCK, LANES), 1) < HEAD_DIM
    rr = lax.broadcasted_iota(jnp.int32, (MOBA_BLOCK, MOBA_BLOCK), 0)
    cc = lax.broadcasted_iota(jnp.int32, (MOBA_BLOCK, MOBA_BLOCK), 1)
    causal = cc <= rr

    def scores(i, half):
        r0 = i * MOBA_BLOCK
        n_keys = r0 + MOBA_BLOCK
        s = lax.dot_general(qa_ref[half, r0:n_keys, :], ka_ref[half, 0:n_keys, :], (((1,), (1,)), ((), ())),
                            preferred_element_type=jnp.float32)
        own_blk = jnp.where(causal, s[:, r0:n_keys], NEG_BIG)
        return jnp.concatenate([s[:, :r0], own_blk], axis=1) if i else own_blk

    swa_every = nblk // SWA_PER_STEP
    swa_next = swa_scores(0)
    pending = [[scores(i, half) for half in range(2)] for i in range(MOBA_LOOKAHEAD)]
    for i in range(nblk):
        cur = pending.pop(0)
        if i + MOBA_LOOKAHEAD < nblk:
            pending.append([scores(i + MOBA_LOOKAHEAD, half) for half in range(2)])
        n_keys = (i + 1) * MOBA_BLOCK
        ms = [jnp.max(s, axis=-1, keepdims=True) for s in cur]
        ps = [jnp.exp2(s - m).astype(jnp.bfloat16) for s, m in zip(cur, ms)]
        acc = [jnp.dot(p, va_ref[half, 0:n_keys, :], preferred_element_type=jnp.float32)
               for half, p in enumerate(ps)]
        num = jnp.where(qlane_low, acc[0], acc[1])
        den = pltpu.roll(jnp.where(qlane_low, acc[1], acc[0]), HEAD_DIM, 1)
        o_ref[i * MOBA_BLOCK:n_keys, :] = (num / den).astype(o_ref.dtype)
        if i % swa_every == swa_every - 1:
            u = i // swa_every
            swa_cur = swa_next
            if u + 1 < SWA_PER_STEP:
                swa_next = swa_scores(u + 1)
            swa_finish(u, swa_cur)


def _attention(proj, sinks, b0, batch, seq, after):
    pairs = W_MOBA // LANES
    assert seq // WINDOW == pairs * SWA_PER_STEP and (seq // MOBA_BLOCK) % SWA_PER_STEP == 0
    grid_spec = pltpu.PrefetchScalarGridSpec(
        num_scalar_prefetch=0,
        grid=(batch, pairs),
        in_specs=[
            pl.BlockSpec(memory_space=pltpu.SMEM),
            pl.BlockSpec((seq, W_Q_SWA), lambda b, p: (b + b0, 0)),
            pl.BlockSpec((seq, LANES), lambda b, p: (b + b0, COL_K_SWA)),
            pl.BlockSpec((seq, LANES), lambda b, p: (b + b0, COL_V_SWA)),
            pl.BlockSpec((seq, LANES), lambda b, p: (b + b0, COL_Q_MOBA + p)),
            pl.BlockSpec((seq, LANES), lambda b, p: (b + b0, COL_K_MOBA + p)),
            pl.BlockSpec((seq, LANES), lambda b, p: (b + b0, COL_V_MOBA + p)),
            pl.BlockSpec(memory_space=pl.ANY),
        ],
        out_specs=[pl.BlockSpec((seq, W_Q_SWA), lambda b, p: (b, 0)),
                   pl.BlockSpec((seq, LANES), lambda b, p: (b, p))],
        scratch_shapes=[pltpu.VMEM((N_KV_SWA, WINDOW + seq, LANES), jnp.bfloat16),
                        pltpu.VMEM((N_KV_SWA, WINDOW + seq, LANES), jnp.bfloat16)]
                       + [pltpu.VMEM((2, seq, LANES), jnp.bfloat16)] * 3,
    )
    return pl.pallas_call(
        functools.partial(_attention_kernel, seq=seq),
        grid_spec=grid_spec,
        out_shape=[jax.ShapeDtypeStruct((batch * seq, W_Q_SWA), jnp.bfloat16),
                   jax.ShapeDtypeStruct((batch * seq, W_MOBA), jnp.bfloat16)],
        compiler_params=_params("arbitrary", "arbitrary"),
        name="fused_attention",
    )(sinks, proj, proj, proj, proj, proj, proj, after)


def _layer_norm(h, g, b):
    mu = jnp.mean(h, axis=-1, keepdims=True)
    d = h - mu
    var = jnp.mean(d * d, axis=-1, keepdims=True)
    return d * lax.rsqrt(var + LN_EPS) * g + b


def _outproj_kernel(oa_ref, ob_ref, wo_ref, bo_ref, x_ref, g_ref, b_ref, wr_ref, br_ref, tri_ref,
                    x1_ref, x1p_ref, idx_ref, gate_ref, rank_ref, count_ref, running_ref):
    @pl.when(pl.program_id(0) == 0)
    def _():
        running_ref[...] = jnp.zeros(running_ref.shape, running_ref.dtype)

    half_rows = ROW_TILE // 2
    halves = [slice(h * half_rows, (h + 1) * half_rows) for h in range(2)]

    def project(rows):
        heads = jnp.concatenate([oa_ref[rows, :], ob_ref[rows, :]], axis=1)
        return jnp.dot(heads, wo_ref[...], preferred_element_type=jnp.float32) + bo_ref[...]

    def normalise(rows, mix):
        x1 = _layer_norm(DEEPNORM_ALPHA * x_ref[rows, :] + mix, g_ref[...], b_ref[...])
        x1_ref[rows, :] = x1
        x1p_ref[rows, :] = _pack_bf16_pairs(x1)
        return x1.astype(jnp.bfloat16)

    mixes = [project(rows) for rows in halves]
    x1b = jnp.concatenate([normalise(rows, mix) for rows, mix in zip(halves, mixes)], axis=0)

    logits = lax.dot_general(wr_ref[...], x1b, (((1,), (1,)), ((), ())),
                             preferred_element_type=jnp.float32) + br_ref[...]
    expert = lax.broadcasted_iota(jnp.int32, logits.shape, 0)
    idx_rows, val_rows, onehots = [], [], []
    top = None
    total = None
    for k in range(TOP_K):
        m = jnp.max(logits, axis=0, keepdims=True)
        idx = jnp.min(jnp.where(logits == m, expert, N_EXPERTS), axis=0, keepdims=True)
        picked = expert == idx
        onehots.append(picked)
        logits = jnp.where(picked, -jnp.inf, logits)
        if k == 0:
            top = m
        e = jnp.exp(m - top)
        total = e if k == 0 else total + e
        idx_rows.append(idx)
        val_rows.append(e)
    idx_ref[...] = jnp.concatenate(idx_rows, axis=0)
    gates_t = jnp.concatenate(val_rows, axis=0) / total
    gates_t = jnp.concatenate([gates_t, jnp.zeros((LANES - TOP_K, gates_t.shape[1]), jnp.float32)], axis=0)
    gate_ref[...] = gates_t.T[:, :TOP_K]

    picks = jnp.concatenate([p.astype(jnp.bfloat16) for p in onehots], axis=0)
    before = jnp.dot(picks, tri_ref[...], preferred_element_type=jnp.float32)
    base = running_ref[...]
    rank_rows = []
    for k in range(TOP_K):
        pk = onehots[k].astype(jnp.float32)
        here = before[k * N_EXPERTS:(k + 1) * N_EXPERTS, :] + base
        rank_rows.append(jnp.sum(pk * here, axis=0, keepdims=True))
        base = base + jnp.sum(pk, axis=1, keepdims=True)
    running_ref[...] = base
    rank_ref[...] = jnp.concatenate(rank_rows, axis=0).astype(jnp.int32)
    count_ref[...] = base.astype(jnp.int32)


def _outproj(o_a, o_b, w_o, b_out, x2, g, b, w_r, b_r, t0, n_tok):
    tile0 = t0 // ROW_TILE
    src = lambda i: (i + tile0, 0)
    row = lambda i: (i, 0)
    fixed = lambda i: (0, 0)
    r = lax.broadcasted_iota(jnp.int32, (ROW_TILE, ROW_TILE), 0)
    c = lax.broadcasted_iota(jnp.int32, (ROW_TILE, ROW_TILE), 1)
    tri = (r < c).astype(jnp.bfloat16)
    return pl.pallas_call(
        _outproj_kernel,
        grid=(n_tok // ROW_TILE,),
        in_specs=[
            pl.BlockSpec((ROW_TILE, W_Q_SWA), row),
            pl.BlockSpec((ROW_TILE, W_MOBA), row),
            pl.BlockSpec((W_Q_SWA + W_MOBA, D_MODEL), fixed),
            pl.BlockSpec((1, D_MODEL), fixed),
            pl.BlockSpec((ROW_TILE, D_MODEL), src),
            pl.BlockSpec((1, D_MODEL), fixed),
            pl.BlockSpec((1, D_MODEL), fixed),
            pl.BlockSpec((N_EXPERTS, D_MODEL), fixed),
            pl.BlockSpec((N_EXPERTS, 1), fixed),
            pl.BlockSpec((ROW_TILE, ROW_TILE), fixed),
        ],
        out_specs=[
            pl.BlockSpec((ROW_TILE, D_MODEL), row),
            pl.BlockSpec((ROW_TILE, PACKED), row),
            pl.BlockSpec((TOP_K, ROW_TILE), lambda i: (0, i)),
            pl.BlockSpec((ROW_TILE, TOP_K), row),
            pl.BlockSpec((TOP_K, ROW_TILE), lambda i: (0, i)),
            pl.BlockSpec((N_EXPERTS, 1), fixed),
        ],
        out_shape=[
            jax.ShapeDtypeStruct((n_tok, D_MODEL), jnp.float32),
            jax.ShapeDtypeStruct((n_tok, PACKED), jnp.int32),
            jax.ShapeDtypeStruct((TOP_K, n_tok), jnp.int32),
            jax.ShapeDtypeStruct((n_tok, TOP_K), jnp.float32),
            jax.ShapeDtypeStruct((TOP_K, n_tok), jnp.int32),
            jax.ShapeDtypeStruct((N_EXPERTS, 1), jnp.int32),
        ],
        scratch_shapes=[pltpu.VMEM((N_EXPERTS, 1), jnp.float32)],
        compiler_params=_params("arbitrary"),
        name="outproj_ln_router",
    )(o_a, o_b, w_o, b_out, x2, g, b, w_r, b_r, tri)


def _sc_worker_id():
    return lax.axis_index("s") * SC_CORES + lax.axis_index("c")


def _sc_scatter_rows(rows, pos3, n_out):
    n_tok = pos3.shape[0] * SC_ROWS
    steps = n_tok // SC_ROWS // SC_WORKERS
    assert steps * SC_ROWS * SC_WORKERS == n_tok and steps % 2 == 0
    mesh = plsc.VectorSubcoreMesh(core_axis_name="c", subcore_axis_name="s")

    @functools.partial(
        pl.kernel, mesh=mesh,
        out_type=jax.ShapeDtypeStruct((n_out, PACKED), jnp.int32),
        scratch_types=[pltpu.VMEM((2, TOP_K, SC_ROWS), jnp.int32), pltpu.VMEM((2, SC_ROWS, PACKED), jnp.int32),
                       pltpu.SemaphoreType.DMA((2,)), pltpu.SemaphoreType.DMA((2,))],
        name="sc_dispatch_scatter")
    def scatter(x_hbm, pos_hbm, out_hbm, idx_v, rows_v, sem_ld, sem_st):
        base = _sc_worker_id() * steps

        def loads(s, b):
            return (pltpu.make_async_copy(pos_hbm.at[base + s], idx_v.at[b], sem_ld.at[b]),
                    pltpu.make_async_copy(x_hbm.at[pl.ds((base + s) * SC_ROWS, SC_ROWS)], rows_v.at[b],
                                          sem_ld.at[b]))

        def stores(b):
            return [pltpu.make_async_copy(rows_v.at[b], out_hbm.at[idx_v.at[b, k]], sem_st.at[b])
                    for k in range(TOP_K)]

        for c in loads(0, 0):
            c.start()

        @pl.loop(0, steps, step=2)
        def _(s0):
            for b in range(2):
                s = s0 + b
                for c in loads(s, b):
                    c.wait()

                @pl.when(s >= 1)
                def _():
                    for c in stores(1 - b):
                        c.wait()

                @pl.when(s + 1 < steps)
                def _():
                    for c in loads(s + 1, 1 - b):
                        c.start()

                for c in stores(b):
                    c.start()

        for c in stores(1):
            c.wait()

    return scatter(rows, pos3)


def _sc_gather_rows(table, idx2):
    n_blk = idx2.shape[0]
    steps = n_blk // SC_WORKERS
    assert steps * SC_WORKERS == n_blk and steps % 2 == 0 and idx2.shape[1] == SC_ROWS
    mesh = plsc.VectorSubcoreMesh(core_axis_name="c", subcore_axis_name="s")

    @functools.partial(
        pl.kernel, mesh=mesh,
        out_type=jax.ShapeDtypeStruct((n_blk * SC_ROWS, PACKED), jnp.int32),
        scratch_types=[pltpu.VMEM((steps, SC_ROWS), jnp.int32), pltpu.VMEM((2, SC_ROWS, PACKED), jnp.int32),
                       pltpu.SemaphoreType.DMA((2,)), pltpu.SemaphoreType.DMA((2,))],
        name="sc_combine_gather")
    def gather(y_hbm, idx_hbm, out_hbm, idx_v, rows_v, sem_ld, sem_st):
        base = _sc_worker_id() * steps
        pltpu.sync_copy(idx_hbm.at[pl.ds(base, steps)], idx_v)

        def fetch(s, b):
            return pltpu.make_async_copy(y_hbm.at[idx_v.at[s]], rows_v.at[b], sem_ld.at[b])

        def store(s, b):
            return pltpu.make_async_copy(rows_v.at[b], out_hbm.at[pl.ds((base + s) * SC_ROWS, SC_ROWS)],
                                         sem_st.at[b])

        fetch(0, 0).start()

        @pl.loop(0, steps, step=2)
        def _(s0):
            for b in range(2):
                s = s0 + b
                fetch(s, b).wait()

                @pl.when(s >= 1)
                def _():
                    store(s - 1, 1 - b).wait()

                @pl.when(s + 1 < steps)
                def _():
                    fetch(s + 1, 1 - b).start()

                store(s, b).start()

        store(steps - 1, 1).wait()

    return gather(table, idx2)


def _expert_kernel(be_ref, br_ref, slot_ref, next_ref, nv_ref, x_ref, w1_hbm, perm_ref, bg_ref, bl_ref, w2_hbm,
                   b2_ref, after_ref, y_ref, w1f_ref, w2f_ref, wg_ref, wl_ref, w2b_ref, sem):
    del after_ref
    i = pl.program_id(0)
    live = i < nv_ref[0]
    new_expert = (i == 0) | (be_ref[i] != be_ref[jnp.maximum(i - 1, 0)])

    def weight_copies(expert, slot):
        return (pltpu.make_async_copy(w1_hbm.at[expert], w1f_ref.at[slot], sem.at[0, slot]),
                pltpu.make_async_copy(w2_hbm.at[expert], w2f_ref.at[slot], sem.at[1, slot]))

    @pl.when(live & new_expert)
    def _():
        slot = slot_ref[i]

        @pl.when(i == 0)
        def _():
            for copy in weight_copies(be_ref[i], slot):
                copy.start()

        for copy in weight_copies(be_ref[i], slot):
            copy.wait()

        @pl.when(next_ref[i] >= 0)
        def _():
            for copy in weight_copies(next_ref[i], 1 - slot):
                copy.start()

        for c in range(2 * D_FF // 256):
            t = w1f_ref[slot, :, c * 256:(c + 1) * 256].astype(jnp.bfloat16)
            r = jnp.dot(t, perm_ref[...], preferred_element_type=jnp.float32)
            wg_ref[:, c * LANES:(c + 1) * LANES] = r[:, :LANES].astype(wg_ref.dtype)
            wl_ref[:, c * LANES:(c + 1) * LANES] = r[:, LANES:].astype(wl_ref.dtype)
        w2b_ref[...] = w2f_ref[slot].astype(w2b_ref.dtype)

    def mlp(rows):
        valid = lax.broadcasted_iota(jnp.int32, (rows, PACKED), 0) < br_ref[i]
        lo, hi = _unpack_bf16_pairs(jnp.where(valid, x_ref[0:rows, :], 0))
        xb = jnp.concatenate([lo.astype(jnp.bfloat16), hi.astype(jnp.bfloat16)], axis=1)
        hg = jnp.dot(xb, wg_ref[...], preferred_element_type=jnp.float32) + bg_ref[0]
        hl = jnp.dot(xb, wl_ref[...], preferred_element_type=jnp.float32) + bl_ref[0]
        glu = jnp.minimum(hg, SWIGLU_LIMIT)
        lin = jnp.clip(hl, -SWIGLU_LIMIT, SWIGLU_LIMIT)
        act = glu * jax.nn.sigmoid(SWIGLU_ALPHA * glu) * (lin + 1.0)
        y = jnp.dot(act.astype(jnp.bfloat16), w2b_ref[...], preferred_element_type=jnp.float32) + b2_ref[0]
        y_ref[0:rows, :] = _pack_bf16_pairs(y)

    parts = (br_ref[i] + EXPERT_PART - 1) // EXPERT_PART
    for n_parts in range(1, EXPERT_ROWS // EXPERT_PART + 1):
        @pl.when(live & (parts == n_parts))
        def _(rows=n_parts * EXPERT_PART):
            mlp(rows)
            if rows < EXPERT_ROWS:
                y_ref[rows:, :] = jnp.zeros((EXPERT_ROWS - rows, PACKED), y_ref.dtype)

    @pl.when(jnp.logical_not(live))
    def _():
        y_ref[...] = jnp.zeros(y_ref.shape, y_ref.dtype)


def _experts(tables, xg, w1, perm, b1g, b1l, w2, b2, after):
    n_rows = xg.shape[0]
    n_blocks = n_rows // EXPERT_ROWS

    def row(i, be, br, slot, nxt, nv):
        return (jnp.minimum(i, nv[0] - 1), 0)

    def per_expert(i, be, br, slot, nxt, nv):
        return (be[i], 0, 0)

    grid_spec = pltpu.PrefetchScalarGridSpec(
        num_scalar_prefetch=5,
        grid=(n_blocks,),
        in_specs=[
            pl.BlockSpec((EXPERT_ROWS, PACKED), row),
            pl.BlockSpec(memory_space=pl.ANY),
            pl.BlockSpec((256, 256), lambda i, *_: (0, 0)),
            pl.BlockSpec((1, 1, D_FF), per_expert),
            pl.BlockSpec((1, 1, D_FF), per_expert),
            pl.BlockSpec(memory_space=pl.ANY),
            pl.BlockSpec((1, 1, D_MODEL), per_expert),
            pl.BlockSpec(memory_space=pl.ANY),
        ],
        out_specs=pl.BlockSpec((EXPERT_ROWS, PACKED), lambda i, *_: (i, 0)),
        scratch_shapes=[pltpu.VMEM((2, D_MODEL, 2 * D_FF), jnp.float32), pltpu.VMEM((2, D_FF, D_MODEL), jnp.float32),
                        pltpu.VMEM((D_MODEL, D_FF), jnp.bfloat16), pltpu.VMEM((D_MODEL, D_FF), jnp.bfloat16),
                        pltpu.VMEM((D_FF, D_MODEL), jnp.bfloat16), pltpu.SemaphoreType.DMA((2, 2))],
    )
    return pl.pallas_call(
        _expert_kernel,
        grid_spec=grid_spec,
        out_shape=jax.ShapeDtypeStruct((n_rows, PACKED), jnp.int32),
        compiler_params=pltpu.CompilerParams(dimension_semantics=("arbitrary",),
                                             vmem_limit_bytes=EXPERT_VMEM_LIMIT),
        name="grouped_experts",
    )(*tables, xg, w1, perm, b1g, b1l, w2, b2, after)


def _split_columns_perm():
    i = lax.broadcasted_iota(jnp.int32, (256, 256), 0)
    o = lax.broadcasted_iota(jnp.int32, (256, 256), 1)
    return (i == jnp.where(o < LANES, 2 * o, 2 * (o - LANES) + 1)).astype(jnp.bfloat16)


def _combine_kernel(y_ref, gate_ref, x1_ref, g_ref, b_ref, *rest):
    o_ref = rest[-1]
    gates = gate_ref[...]
    lo_sum = None
    hi_sum = None
    for k in range(TOP_K):
        lo, hi = _unpack_bf16_pairs(y_ref[k])
        gk = gates[:, k:k + 1]
        lo_sum = gk * lo if k == 0 else lo_sum + gk * lo
        hi_sum = gk * hi if k == 0 else hi_sum + gk * hi
    moe = jnp.concatenate([lo_sum, hi_sum], axis=1)
    o_ref[...] = _layer_norm(DEEPNORM_ALPHA * x1_ref[...] + moe, g_ref[...], b_ref[...])


def _combine(yg, gates, x1, g, b, tile0, n_tok, prev):
    row = lambda i: (i, 0)
    fixed = lambda i: (0, 0)
    in_specs = [
        pl.BlockSpec((TOP_K, ROW_TILE, PACKED), lambda i: (0, i, 0)),
        pl.BlockSpec((ROW_TILE, TOP_K), row),
        pl.BlockSpec((ROW_TILE, D_MODEL), row),
        pl.BlockSpec((1, D_MODEL), fixed),
        pl.BlockSpec((1, D_MODEL), fixed),
    ]
    args = [yg, gates, x1, g, b]
    aliases = {}
    if prev is not None:
        in_specs.append(pl.BlockSpec(memory_space=pl.ANY))
        args.append(prev)
        aliases = {len(args) - 1: 0}
    return pl.pallas_call(
        _combine_kernel,
        grid=(yg.shape[1] // ROW_TILE,),
        in_specs=in_specs,
        out_specs=pl.BlockSpec((ROW_TILE, D_MODEL), lambda i: (i + tile0, 0)),
        out_shape=jax.ShapeDtypeStruct((n_tok, D_MODEL), jnp.float32),
        input_output_aliases=aliases,
        compiler_params=_params("parallel"),
        name="combine_ln",
    )(*args)


def _route(top_idx, rank, counts, n_blocks):
    counts = counts[:, 0]
    experts = jnp.arange(N_EXPERTS, dtype=jnp.int32)
    upto = experts[None, :] <= experts[:, None]
    later = experts[None, :] > experts[:, None]
    prefix = lambda v: jnp.sum(jnp.where(upto, v[None, :], 0), axis=1)
    blocks_per = (counts + EXPERT_ROWS - 1) // EXPERT_ROWS
    blk_end = prefix(blocks_per)
    blk_start = blk_end - blocks_per
    first_row = (blk_start * EXPERT_ROWS)[:, None, None]
    pos = rank + jnp.sum(jnp.where(top_idx[None] == experts[:, None, None], first_row, 0), axis=0)
    n_valid = jnp.sum(blocks_per, keepdims=True).astype(jnp.int32)
    has_rows = blocks_per > 0
    run_of = prefix(has_rows.astype(jnp.int32)) - 1
    later = later & has_rows[None, :]
    next_of = jnp.min(jnp.where(later, experts[None, :], N_EXPERTS), axis=1)
    next_of = jnp.where(next_of == N_EXPERTS, -1, next_of)
    blk = jnp.arange(n_blocks, dtype=jnp.int32)
    owner = ((blk[:, None] >= blk_start[None, :]) & (blk[:, None] < blk_end[None, :])).astype(jnp.int32)
    pick = lambda per_expert: jnp.sum(owner * per_expert[None, :], axis=1).astype(jnp.int32)
    block_expert = pick(experts)
    block_rows = jnp.clip(pick(counts) - (blk - pick(blk_start)) * EXPERT_ROWS, 0, EXPERT_ROWS).astype(jnp.int32)
    return pos, (block_expert, block_rows, pick(run_of) % 2, pick(next_of), n_valid)


def _moe_groups(n_tok, seq):
    unit = math.lcm(2 * SC_ROWS * SC_WORKERS, seq)
    parts = sum(MOE_SPLIT)
    if n_tok % (parts * unit):
        return ((0, n_tok),)
    groups, t0 = [], 0
    for share in MOE_SPLIT:
        groups.append((t0, n_tok * share // parts))
        t0 += groups[-1][1]
    return tuple(groups)


def _rope_tables(seq):
    inv_freq = 1.0 / (ROPE_THETA ** (jnp.arange(0, HEAD_DIM, 2, dtype=jnp.float32) / HEAD_DIM))
    ang = jnp.arange(seq, dtype=jnp.float32)[:, None] * inv_freq[None, :]
    cos, sin = jnp.cos(ang), jnp.sin(ang)
    cos_t = jnp.tile(jnp.concatenate([cos, cos], axis=1), (1, 256 // HEAD_DIM))
    sin_t = jnp.tile(jnp.concatenate([-sin, sin], axis=1), (1, 256 // HEAD_DIM))
    return cos_t, sin_t


def kernel(x, w_in, b_in, sinks, w_out, b_out, ln1_g, ln1_b, w_router, b_router, w1, b1, w2, b2, ln2_g, ln2_b):
    batch, seq, d = x.shape
    assert d == D_MODEL and seq % ROW_TILE == 0 and seq % MOBA_BLOCK == 0 and w_in.shape[0] == DEPTH == 1
    assert (seq // WINDOW - 1) % SWA_UNROLL == 0
    n_tok = batch * seq
    bf16 = jnp.bfloat16
    x2 = x.reshape(n_tok, d)
    cos_t, sin_t = _rope_tables(seq)

    proj = _inproj(x2, w_in[0].astype(bf16), b_in[0].reshape(1, IN_WIDTH), cos_t, sin_t, seq)

    w_o = w_out[0].astype(bf16)
    w_r = w_router[0].T.astype(bf16)
    b_r = b_router[0].reshape(N_EXPERTS, 1)
    b_o, g1, be1 = b_out[0].reshape(1, d), ln1_g[0].reshape(1, d), ln1_b[0].reshape(1, d)
    perm = _split_columns_perm()
    b1r = b1[0].reshape(N_EXPERTS, 1, D_FF, 2)
    b1g, b1l = b1r[..., 0], b1r[..., 1]
    b2r = b2[0].reshape(N_EXPERTS, 1, d)
    g2, be2 = ln2_g[0].reshape(1, d), ln2_b[0].reshape(1, d)

    def dispatch(stage):
        t0, tok_g, x1, x1p, gates, pos, tables, n_blocks = stage
        pos3 = pos.reshape(TOP_K, tok_g // SC_ROWS, SC_ROWS).transpose(1, 0, 2)
        return _sc_scatter_rows(x1p, pos3, n_blocks * EXPERT_ROWS)

    def experts_and_gather(stage, xg, y_prev):
        t0, tok_g, x1, x1p, gates, pos, tables, n_blocks = stage
        y = _experts(tables, xg, w1[0], perm, b1g, b1l, w2[0], b2r, y_prev)
        return y, _sc_gather_rows(y, pos.reshape(tok_g * TOP_K // SC_ROWS, SC_ROWS))

    stages, gathered, waiting = [], [], None
    pos = jnp.zeros((TOP_K, SC_ROWS), jnp.int32)
    y_prev = jnp.zeros((SC_ROWS, PACKED), jnp.int32)
    for t0, tok_g in _moe_groups(n_tok, seq):
        o_a, o_b = _attention(proj, sinks[0], t0 // seq, tok_g // seq, seq, pos)
        x1, x1p, top_idx, gates, rank, counts = _outproj(o_a, o_b, w_o, b_o, x2, g1, be1, w_r, b_r, t0, tok_g)
        n_blocks = tok_g * TOP_K // EXPERT_ROWS + N_EXPERTS
        pos, tables = _route(top_idx, rank, counts, n_blocks)
        stages.append((t0, tok_g, x1, x1p, gates, pos, tables, n_blocks))
        if waiting is not None:
            y_prev, yg = experts_and_gather(*waiting, y_prev)
            gathered.append(yg)
        waiting = (stages[-1], dispatch(stages[-1]))
    gathered.append(experts_and_gather(*waiting, y_prev)[1])
    out = None
    for (t0, tok_g, x1, _, gates, _, _, _), yg in zip(stages, gathered):
        out = _combine(yg.reshape(TOP_K, tok_g, PACKED), gates, x1, g2, be2, t0 // ROW_TILE, n_tok, out)
    return out.reshape(batch, seq, d)
```
